```python
import jax, jax.numpy as jnp
from jax import lax
import numpy as np

D_MODEL = 2048
BATCH = 8
SEQ = 4096
DEPTH = 2

N_MIXERS = 2
EPS = 1e-6
D_FF = 5632
GM_WIDTH = D_MODEL
CHUNK = 128
GM_GROUPS = 16
GM_GROUP_DIM = GM_WIDTH // GM_GROUPS
CONV_WIDTH = D_MODEL
CONV_K = 31
N_SUB = 3
N_MOD = 3
N_A = (DEPTH + 1) // 2
N_B = DEPTH // 2

kernel_name = "macaron_gmlp_conformer_hybrid"


def rms_norm(x, g):
    xf = x.astype(jnp.float32)
    y = xf * lax.rsqrt(jnp.mean(xf * xf, axis=-1, keepdims=True) + EPS)
    return (y * g.astype(jnp.float32)).astype(x.dtype)


def layer_norm(x, g, b):
    xf = x.astype(jnp.float32)
    mu = jnp.mean(xf, axis=-1, keepdims=True)
    var = jnp.mean(jnp.square(xf - mu), axis=-1, keepdims=True)
    y = (xf - mu) * lax.rsqrt(var + EPS)
    return (y * g.astype(jnp.float32) + b.astype(jnp.float32)).astype(x.dtype)


def modulate(h, shift, scale):
    return h * (1 + scale[:, None, :]) + shift[:, None, :]


def swiglu_ffn(h, w_in, w_out):
    gate, up = jnp.split(h @ w_in, 2, axis=-1)
    return (jax.nn.silu(gate) * up) @ w_out


def gmlp_mixer(h, w_in, ln_g, ln_b, ws, bs, w_out):
    b, t, _ = h.shape
    z = jax.nn.gelu(h @ w_in, approximate=False)
    u, v = jnp.split(z, 2, axis=-1)
    v = layer_norm(v, ln_g, ln_b)
    v = v.reshape(b, t // CHUNK, CHUNK, GM_GROUPS, GM_GROUP_DIM)
    causal = jnp.tril(jnp.ones((CHUNK, CHUNK), dtype=bool))
    ws_c = jnp.where(causal[None], ws, jnp.zeros_like(ws))
    v = jnp.einsum("hts,bcshd->bcthd", ws_c, v) + bs.T[None, None, :, :, None]
    s = u * v.reshape(b, t, GM_WIDTH)
    return s @ w_out


def conv_mixer(h, w_in, b_in, dw_w, dw_b, ln_g, ln_b, w_out, b_out):
    a, g = jnp.split(h @ w_in + b_in, 2, axis=-1)
    y = a * jax.nn.sigmoid(g)
    y = lax.conv_general_dilated(
        y, dw_w[:, None, :].astype(y.dtype),
        window_strides=(1,), padding=[(CONV_K - 1, 0)],
        dimension_numbers=("NWC", "WIO", "NWC"),
        feature_group_count=CONV_WIDTH) + dw_b
    y = jax.nn.silu(layer_norm(y, ln_g, ln_b))
    return y @ w_out + b_out


def _fwd_setup_inputs(seed: int = 0) -> dict:
    key = jax.random.key(seed)
    ks = jax.random.split(key, 24)
    D, F, E, C, H, L = D_MODEL, D_FF, GM_WIDTH, CONV_WIDTH, GM_GROUPS, CHUNK
    nrm = lambda k, shape, s: (jax.random.normal(k, shape, jnp.float32) * s).astype(jnp.float32)

    x = nrm(ks[0], (BATCH, SEQ, D), 1.0)
    c = nrm(ks[1], (BATCH, D), 1.0)
    ada_w = nrm(ks[2], (DEPTH, D, N_SUB * N_MOD * D), 0.1 * D ** -0.5)
    ada_b = nrm(ks[3], (DEPTH, N_SUB, N_MOD, D), 0.02).at[:, :, 2].add(1.0).reshape(DEPTH, N_SUB * N_MOD * D)
    norm_g = 1.0 + nrm(ks[4], (DEPTH, N_SUB, D), 0.02)
    ffn_w_in = nrm(ks[5], (DEPTH, 2, D, 2 * F), D ** -0.5)
    ffn_w_out = nrm(ks[6], (DEPTH, 2, F, D), F ** -0.5)

    gm_w_in = nrm(ks[7], (N_A, D, 2 * E), D ** -0.5)
    gm_ln_g = 1.0 + nrm(ks[8], (N_A, E), 0.02)
    gm_ln_b = nrm(ks[9], (N_A, E), 0.02)
    gm_ws = nrm(ks[10], (N_A, H, L, L), L ** -0.5)
    gm_bs = 1.0 + nrm(ks[11], (N_A, H, L), 0.02)
    gm_w_out = nrm(ks[12], (N_A, E, D), E ** -0.5)

    cv_w_in = nrm(ks[13], (N_B, D, 2 * C), D ** -0.5)
    cv_b_in = nrm(ks[14], (N_B, 2 * C), 0.02)
    cv_dw_w = nrm(ks[15], (N_B, CONV_K, C), CONV_K ** -0.5)
    cv_dw_b = nrm(ks[16], (N_B, C), 0.02)
    cv_ln_g = 1.0 + nrm(ks[17], (N_B, C), 0.02)
    cv_ln_b = nrm(ks[18], (N_B, C), 0.02)
    cv_w_out = nrm(ks[19], (N_B, C, D), C ** -0.5)
    cv_b_out = nrm(ks[20], (N_B, D), 0.02)

    final_g = 1.0 + nrm(ks[21], (D,), 0.02)
    return {"x": x, "c": c, "ada_w": ada_w, "ada_b": ada_b, "norm_g": norm_g,
            "ffn_w_in": ffn_w_in, "ffn_w_out": ffn_w_out,
            "gm_w_in": gm_w_in, "gm_ln_g": gm_ln_g, "gm_ln_b": gm_ln_b,
            "gm_ws": gm_ws, "gm_bs": gm_bs, "gm_w_out": gm_w_out,
            "cv_w_in": cv_w_in, "cv_b_in": cv_b_in, "cv_dw_w": cv_dw_w, "cv_dw_b": cv_dw_b,
            "cv_ln_g": cv_ln_g, "cv_ln_b": cv_ln_b, "cv_w_out": cv_w_out, "cv_b_out": cv_b_out,
            "final_g": final_g}


def _fwd_reference(x, c, ada_w, ada_b, norm_g, ffn_w_in, ffn_w_out,
              gm_w_in, gm_ln_g, gm_ln_b, gm_ws, gm_bs, gm_w_out,
              cv_w_in, cv_b_in, cv_dw_w, cv_dw_b, cv_ln_g, cv_ln_b, cv_w_out, cv_b_out,
              final_g):
    bsz = x.shape[0]
    cond = jax.nn.silu(c)
    for i in range(DEPTH):
        mod = (cond @ ada_w[i] + ada_b[i]).reshape(bsz, N_SUB, N_MOD, D_MODEL)
        shift, scale, gate = mod[:, :, 0], mod[:, :, 1], mod[:, :, 2]

        h = modulate(rms_norm(x, norm_g[i, 0]), shift[:, 0], scale[:, 0])
        x = x + 0.5 * gate[:, 0, None, :] * swiglu_ffn(h, ffn_w_in[i, 0], ffn_w_out[i, 0])

        h = modulate(rms_norm(x, norm_g[i, 1]), shift[:, 1], scale[:, 1])
        j = i // N_MIXERS
        if i % N_MIXERS == 0:
            y = gmlp_mixer(h, gm_w_in[j], gm_ln_g[j], gm_ln_b[j], gm_ws[j], gm_bs[j], gm_w_out[j])
        else:
            y = conv_mixer(h, cv_w_in[j], cv_b_in[j], cv_dw_w[j], cv_dw_b[j],
                           cv_ln_g[j], cv_ln_b[j], cv_w_out[j], cv_b_out[j])
        x = x + gate[:, 1, None, :] * y

        h = modulate(rms_norm(x, norm_g[i, 2]), shift[:, 2], scale[:, 2])
        x = x + 0.5 * gate[:, 2, None, :] * swiglu_ffn(h, ffn_w_in[i, 1], ffn_w_out[i, 1])
    return rms_norm(x, final_g)


import jax as _jax
import jax.numpy as _jnp

TWIN_FORMAT = 'train_step'
FWD_PARAMS = ['x', 'c', 'ada_w', 'ada_b', 'norm_g', 'ffn_w_in', 'ffn_w_out', 'gm_w_in', 'gm_ln_g', 'gm_ln_b', 'gm_ws', 'gm_bs', 'gm_w_out', 'cv_w_in', 'cv_b_in', 'cv_dw_w', 'cv_dw_b', 'cv_ln_g', 'cv_ln_b', 'cv_w_out', 'cv_b_out', 'final_g']
TWIN_WEIGHTS = ['ada_w', 'ada_b', 'norm_g', 'ffn_w_in', 'ffn_w_out', 'gm_w_in', 'gm_ln_g', 'gm_ln_b', 'gm_ws', 'gm_bs', 'gm_w_out', 'cv_w_in', 'cv_b_in', 'cv_dw_w', 'cv_dw_b', 'cv_ln_g', 'cv_ln_b', 'cv_w_out', 'cv_b_out', 'final_g']
TWIN_DIFF_INPUT = 'x'
TWIN_INPUTS = ['x', 'c', 'ada_w', 'ada_b', 'norm_g', 'ffn_w_in', 'ffn_w_out', 'gm_w_in', 'gm_ln_g', 'gm_ln_b', 'gm_ws', 'gm_bs', 'gm_w_out', 'cv_w_in', 'cv_b_in', 'cv_dw_w', 'cv_dw_b', 'cv_ln_g', 'cv_ln_b', 'cv_w_out', 'cv_b_out', 'final_g', 'loss_target', 'm_ada_w', 'm_ada_b', 'm_norm_g', 'm_ffn_w_in', 'm_ffn_w_out', 'm_gm_w_in', 'm_gm_ln_g', 'm_gm_ln_b', 'm_gm_ws', 'm_gm_bs', 'm_gm_w_out', 'm_cv_w_in', 'm_cv_b_in', 'm_cv_dw_w', 'm_cv_dw_b', 'm_cv_ln_g', 'm_cv_ln_b', 'm_cv_w_out', 'm_cv_b_out', 'm_final_g', 'v_ada_w', 'v_ada_b', 'v_norm_g', 'v_ffn_w_in', 'v_ffn_w_out', 'v_gm_w_in', 'v_gm_ln_g', 'v_gm_ln_b', 'v_gm_ws', 'v_gm_bs', 'v_gm_w_out', 'v_cv_w_in', 'v_cv_b_in', 'v_cv_dw_w', 'v_cv_dw_b', 'v_cv_ln_g', 'v_cv_ln_b', 'v_cv_w_out', 'v_cv_b_out', 'v_final_g']
TWIN_OUTPUTS = ['loss', 'grad_x', 'grad_ada_w', 'grad_ada_b', 'grad_norm_g', 'grad_ffn_w_in', 'grad_ffn_w_out', 'grad_gm_w_in', 'grad_gm_ln_g', 'grad_gm_ln_b', 'grad_gm_ws', 'grad_gm_bs', 'grad_gm_w_out', 'grad_cv_w_in', 'grad_cv_b_in', 'grad_cv_dw_w', 'grad_cv_dw_b', 'grad_cv_ln_g', 'grad_cv_ln_b', 'grad_cv_w_out', 'grad_cv_b_out', 'grad_final_g', 'delta_ada_w', 'delta_ada_b', 'delta_norm_g', 'delta_ffn_w_in', 'delta_ffn_w_out', 'delta_gm_w_in', 'delta_gm_ln_g', 'delta_gm_ln_b', 'delta_gm_ws', 'delta_gm_bs', 'delta_gm_w_out', 'delta_cv_w_in', 'delta_cv_b_in', 'delta_cv_dw_w', 'delta_cv_dw_b', 'delta_cv_ln_g', 'delta_cv_ln_b', 'delta_cv_w_out', 'delta_cv_b_out', 'delta_final_g', 'new_m_ada_w', 'new_m_ada_b', 'new_m_norm_g', 'new_m_ffn_w_in', 'new_m_ffn_w_out', 'new_m_gm_w_in', 'new_m_gm_ln_g', 'new_m_gm_ln_b', 'new_m_gm_ws', 'new_m_gm_bs', 'new_m_gm_w_out', 'new_m_cv_w_in', 'new_m_cv_b_in', 'new_m_cv_dw_w', 'new_m_cv_dw_b', 'new_m_cv_ln_g', 'new_m_cv_ln_b', 'new_m_cv_w_out', 'new_m_cv_b_out', 'new_m_final_g', 'new_v_ada_w', 'new_v_ada_b', 'new_v_norm_g', 'new_v_ffn_w_in', 'new_v_ffn_w_out', 'new_v_gm_w_in', 'new_v_gm_ln_g', 'new_v_gm_ln_b', 'new_v_gm_ws', 'new_v_gm_bs', 'new_v_gm_w_out', 'new_v_cv_w_in', 'new_v_cv_b_in', 'new_v_cv_dw_w', 'new_v_cv_dw_b', 'new_v_cv_ln_g', 'new_v_cv_ln_b', 'new_v_cv_w_out', 'new_v_cv_b_out', 'new_v_final_g']
TWIN_LEAF_KINDS = {'loss': 'loss', 'grad_x': 'grad_x', 'grad_ada_w': 'grad_w', 'grad_ada_b': 'grad_w', 'grad_norm_g': 'grad_w', 'grad_ffn_w_in': 'grad_w', 'grad_ffn_w_out': 'grad_w', 'grad_gm_w_in': 'grad_w', 'grad_gm_ln_g': 'grad_w', 'grad_gm_ln_b': 'grad_w', 'grad_gm_ws': 'grad_w', 'grad_gm_bs': 'grad_w', 'grad_gm_w_out': 'grad_w', 'grad_cv_w_in': 'grad_w', 'grad_cv_b_in': 'grad_w', 'grad_cv_dw_w': 'grad_w', 'grad_cv_dw_b': 'grad_w', 'grad_cv_ln_g': 'grad_w', 'grad_cv_ln_b': 'grad_w', 'grad_cv_w_out': 'grad_w', 'grad_cv_b_out': 'grad_w', 'grad_final_g': 'grad_w', 'delta_ada_w': 'delta_w', 'delta_ada_b': 'delta_w', 'delta_norm_g': 'delta_w', 'delta_ffn_w_in': 'delta_w', 'delta_ffn_w_out': 'delta_w', 'delta_gm_w_in': 'delta_w', 'delta_gm_ln_g': 'delta_w', 'delta_gm_ln_b': 'delta_w', 'delta_gm_ws': 'delta_w', 'delta_gm_bs': 'delta_w', 'delta_gm_w_out': 'delta_w', 'delta_cv_w_in': 'delta_w', 'delta_cv_b_in': 'delta_w', 'delta_cv_dw_w': 'delta_w', 'delta_cv_dw_b': 'delta_w', 'delta_cv_ln_g': 'delta_w', 'delta_cv_ln_b': 'delta_w', 'delta_cv_w_out': 'delta_w', 'delta_cv_b_out': 'delta_w', 'delta_final_g': 'delta_w', 'new_m_ada_w': 'new_m', 'new_m_ada_b': 'new_m', 'new_m_norm_g': 'new_m', 'new_m_ffn_w_in': 'new_m', 'new_m_ffn_w_out': 'new_m', 'new_m_gm_w_in': 'new_m', 'new_m_gm_ln_g': 'new_m', 'new_m_gm_ln_b': 'new_m', 'new_m_gm_ws': 'new_m', 'new_m_gm_bs': 'new_m', 'new_m_gm_w_out': 'new_m', 'new_m_cv_w_in': 'new_m', 'new_m_cv_b_in': 'new_m', 'new_m_cv_dw_w': 'new_m', 'new_m_cv_dw_b': 'new_m', 'new_m_cv_ln_g': 'new_m', 'new_m_cv_ln_b': 'new_m', 'new_m_cv_w_out': 'new_m', 'new_m_cv_b_out': 'new_m', 'new_m_final_g': 'new_m', 'new_v_ada_w': 'new_v', 'new_v_ada_b': 'new_v', 'new_v_norm_g': 'new_v', 'new_v_ffn_w_in': 'new_v', 'new_v_ffn_w_out': 'new_v', 'new_v_gm_w_in': 'new_v', 'new_v_gm_ln_g': 'new_v', 'new_v_gm_ln_b': 'new_v', 'new_v_gm_ws': 'new_v', 'new_v_gm_bs': 'new_v', 'new_v_gm_w_out': 'new_v', 'new_v_cv_w_in': 'new_v', 'new_v_cv_b_in': 'new_v', 'new_v_cv_dw_w': 'new_v', 'new_v_cv_dw_b': 'new_v', 'new_v_cv_ln_g': 'new_v', 'new_v_cv_ln_b': 'new_v', 'new_v_cv_w_out': 'new_v', 'new_v_cv_b_out': 'new_v', 'new_v_final_g': 'new_v'}


def _forward(args):
    return _fwd_reference(*[args[k] for k in FWD_PARAMS])


def _output_shape():
    def fwd():
        inp = _fwd_setup_inputs(0)
        return _fwd_reference(*[inp[k] for k in FWD_PARAMS])
    out = _jax.eval_shape(fwd)
    return out.shape, out.dtype

N_MICROBATCH = 1
ADAM_LR = 0.001
ADAM_B1 = 0.9
ADAM_B2 = 0.999
ADAM_EPS = 1e-08
ADAM_WD = 0.01
ADAM_STEP = 10
PER_EXAMPLE_BATCH_AXIS = {'x': 0, 'c': 0, 'loss_target': 0}
SHARED_INPUTS = []
_WEIGHT_DTYPES = {'ada_w': _jnp.float32, 'ada_b': _jnp.float32, 'norm_g': _jnp.float32, 'ffn_w_in': _jnp.float32, 'ffn_w_out': _jnp.float32, 'gm_w_in': _jnp.float32, 'gm_ln_g': _jnp.float32, 'gm_ln_b': _jnp.float32, 'gm_ws': _jnp.float32, 'gm_bs': _jnp.float32, 'gm_w_out': _jnp.float32, 'cv_w_in': _jnp.float32, 'cv_b_in': _jnp.float32, 'cv_dw_w': _jnp.float32, 'cv_dw_b': _jnp.float32, 'cv_ln_g': _jnp.float32, 'cv_ln_b': _jnp.float32, 'cv_w_out': _jnp.float32, 'cv_b_out': _jnp.float32, 'final_g': _jnp.float32}
MOMENT_SCALE = {'ada_w': 2.613139e-02, 'ada_b': 5.637519e-02, 'norm_g': 4.409112e-02, 'ffn_w_in': 1.399870e-02, 'ffn_w_out': 2.286628e-02, 'gm_w_in': 5.116781e-02, 'gm_ln_g': 3.383726e-02, 'gm_ln_b': 3.241798e-02, 'gm_ws': 3.375699e-02, 'gm_bs': 4.824778e-02, 'gm_w_out': 6.314949e-02, 'cv_w_in': 2.877219e-02, 'cv_b_in': 4.024317e-02, 'cv_dw_w': 3.816044e-02, 'cv_dw_b': 9.161506e-02, 'cv_ln_g': 5.191103e-02, 'cv_ln_b': 5.333928e-02, 'cv_w_out': 3.950057e-02, 'cv_b_out': 9.537440e-02, 'final_g': 1.606582e+01}


def _to_microbatches(a, axis):
    t = _jnp.moveaxis(a, axis, 0)
    t = t.reshape((N_MICROBATCH, t.shape[0] // N_MICROBATCH) + t.shape[1:])
    return _jnp.moveaxis(t, 1, axis + 1)


def setup_inputs(seed: int = 0) -> dict:
    inp = _fwd_setup_inputs(seed)
    key = _jax.random.fold_in(_jax.random.key(seed), 7919)
    shape, _ = _output_shape()
    out = dict(inp)
    out["loss_target"] = _jax.random.normal(_jax.random.fold_in(key, 0), shape, _jnp.float32)
    for i, name in enumerate(TWIN_WEIGHTS):
        w = inp[name].astype(_jnp.float32)
        if MOMENT_SCALE is None:
            s = _jnp.sqrt(_jnp.mean(_jnp.square(w)) + 1e-30)
        else:
            s = MOMENT_SCALE[name]
        km, kv = _jax.random.split(_jax.random.fold_in(key, i + 1))
        out[name] = w
        out["m_" + name] = s * _jax.random.normal(km, w.shape, _jnp.float32)
        out["v_" + name] = (s * s) * _jax.random.uniform(kv, w.shape, _jnp.float32, 0.5, 1.5)
    if N_MICROBATCH > 1:
        for name, axis in PER_EXAMPLE_BATCH_AXIS.items():
            out[name] = _to_microbatches(out[name], axis)
    return {'x': out['x'], 'c': out['c'], 'ada_w': out['ada_w'], 'ada_b': out['ada_b'], 'norm_g': out['norm_g'], 'ffn_w_in': out['ffn_w_in'], 'ffn_w_out': out['ffn_w_out'], 'gm_w_in': out['gm_w_in'], 'gm_ln_g': out['gm_ln_g'], 'gm_ln_b': out['gm_ln_b'], 'gm_ws': out['gm_ws'], 'gm_bs': out['gm_bs'], 'gm_w_out': out['gm_w_out'], 'cv_w_in': out['cv_w_in'], 'cv_b_in': out['cv_b_in'], 'cv_dw_w': out['cv_dw_w'], 'cv_dw_b': out['cv_dw_b'], 'cv_ln_g': out['cv_ln_g'], 'cv_ln_b': out['cv_ln_b'], 'cv_w_out': out['cv_w_out'], 'cv_b_out': out['cv_b_out'], 'final_g': out['final_g'], 'loss_target': out['loss_target'], 'm_ada_w': out['m_ada_w'], 'm_ada_b': out['m_ada_b'], 'm_norm_g': out['m_norm_g'], 'm_ffn_w_in': out['m_ffn_w_in'], 'm_ffn_w_out': out['m_ffn_w_out'], 'm_gm_w_in': out['m_gm_w_in'], 'm_gm_ln_g': out['m_gm_ln_g'], 'm_gm_ln_b': out['m_gm_ln_b'], 'm_gm_ws': out['m_gm_ws'], 'm_gm_bs': out['m_gm_bs'], 'm_gm_w_out': out['m_gm_w_out'], 'm_cv_w_in': out['m_cv_w_in'], 'm_cv_b_in': out['m_cv_b_in'], 'm_cv_dw_w': out['m_cv_dw_w'], 'm_cv_dw_b': out['m_cv_dw_b'], 'm_cv_ln_g': out['m_cv_ln_g'], 'm_cv_ln_b': out['m_cv_ln_b'], 'm_cv_w_out': out['m_cv_w_out'], 'm_cv_b_out': out['m_cv_b_out'], 'm_final_g': out['m_final_g'], 'v_ada_w': out['v_ada_w'], 'v_ada_b': out['v_ada_b'], 'v_norm_g': out['v_norm_g'], 'v_ffn_w_in': out['v_ffn_w_in'], 'v_ffn_w_out': out['v_ffn_w_out'], 'v_gm_w_in': out['v_gm_w_in'], 'v_gm_ln_g': out['v_gm_ln_g'], 'v_gm_ln_b': out['v_gm_ln_b'], 'v_gm_ws': out['v_gm_ws'], 'v_gm_bs': out['v_gm_bs'], 'v_gm_w_out': out['v_gm_w_out'], 'v_cv_w_in': out['v_cv_w_in'], 'v_cv_b_in': out['v_cv_b_in'], 'v_cv_dw_w': out['v_cv_dw_w'], 'v_cv_dw_b': out['v_cv_dw_b'], 'v_cv_ln_g': out['v_cv_ln_g'], 'v_cv_ln_b': out['v_cv_ln_b'], 'v_cv_w_out': out['v_cv_w_out'], 'v_cv_b_out': out['v_cv_b_out'], 'v_final_g': out['v_final_g']}


def _loss(weights, diff, rest, loss_target):
    with _jax.named_scope("forward"):
        args = {**rest, TWIN_DIFF_INPUT: diff, **{k: w.astype(_WEIGHT_DTYPES[k]) for k, w in weights.items()}}
        y = _forward(args)
    with _jax.named_scope("loss_head"):
        err = _jnp.square(y.astype(_jnp.float32) - loss_target)
        return 0.5 * _jnp.sum(_jnp.mean(err, axis=-1)) if err.ndim else 0.5 * err


def _adamw(w, g, m, v):
    m = ADAM_B1 * m + (1.0 - ADAM_B1) * g
    v = ADAM_B2 * v + (1.0 - ADAM_B2) * _jnp.square(g)
    m_hat = m / (1.0 - ADAM_B1 ** ADAM_STEP)
    v_hat = v / (1.0 - ADAM_B2 ** ADAM_STEP)
    delta = -ADAM_LR * (m_hat / (_jnp.sqrt(v_hat) + ADAM_EPS) + ADAM_WD * w)
    return delta, m, v


def reference(x, c, ada_w, ada_b, norm_g, ffn_w_in, ffn_w_out, gm_w_in, gm_ln_g, gm_ln_b, gm_ws, gm_bs, gm_w_out, cv_w_in, cv_b_in, cv_dw_w, cv_dw_b, cv_ln_g, cv_ln_b, cv_w_out, cv_b_out, final_g, loss_target, m_ada_w, m_ada_b, m_norm_g, m_ffn_w_in, m_ffn_w_out, m_gm_w_in, m_gm_ln_g, m_gm_ln_b, m_gm_ws, m_gm_bs, m_gm_w_out, m_cv_w_in, m_cv_b_in, m_cv_dw_w, m_cv_dw_b, m_cv_ln_g, m_cv_ln_b, m_cv_w_out, m_cv_b_out, m_final_g, v_ada_w, v_ada_b, v_norm_g, v_ffn_w_in, v_ffn_w_out, v_gm_w_in, v_gm_ln_g, v_gm_ln_b, v_gm_ws, v_gm_bs, v_gm_w_out, v_cv_w_in, v_cv_b_in, v_cv_dw_w, v_cv_dw_b, v_cv_ln_g, v_cv_ln_b, v_cv_w_out, v_cv_b_out, v_final_g):
    given = dict(x=x, c=c, ada_w=ada_w, ada_b=ada_b, norm_g=norm_g, ffn_w_in=ffn_w_in, ffn_w_out=ffn_w_out, gm_w_in=gm_w_in, gm_ln_g=gm_ln_g, gm_ln_b=gm_ln_b, gm_ws=gm_ws, gm_bs=gm_bs, gm_w_out=gm_w_out, cv_w_in=cv_w_in, cv_b_in=cv_b_in, cv_dw_w=cv_dw_w, cv_dw_b=cv_dw_b, cv_ln_g=cv_ln_g, cv_ln_b=cv_ln_b, cv_w_out=cv_w_out, cv_b_out=cv_b_out, final_g=final_g, loss_target=loss_target, m_ada_w=m_ada_w, m_ada_b=m_ada_b, m_norm_g=m_norm_g, m_ffn_w_in=m_ffn_w_in, m_ffn_w_out=m_ffn_w_out, m_gm_w_in=m_gm_w_in, m_gm_ln_g=m_gm_ln_g, m_gm_ln_b=m_gm_ln_b, m_gm_ws=m_gm_ws, m_gm_bs=m_gm_bs, m_gm_w_out=m_gm_w_out, m_cv_w_in=m_cv_w_in, m_cv_b_in=m_cv_b_in, m_cv_dw_w=m_cv_dw_w, m_cv_dw_b=m_cv_dw_b, m_cv_ln_g=m_cv_ln_g, m_cv_ln_b=m_cv_ln_b, m_cv_w_out=m_cv_w_out, m_cv_b_out=m_cv_b_out, m_final_g=m_final_g, v_ada_w=v_ada_w, v_ada_b=v_ada_b, v_norm_g=v_norm_g, v_ffn_w_in=v_ffn_w_in, v_ffn_w_out=v_ffn_w_out, v_gm_w_in=v_gm_w_in, v_gm_ln_g=v_gm_ln_g, v_gm_ln_b=v_gm_ln_b, v_gm_ws=v_gm_ws, v_gm_bs=v_gm_bs, v_gm_w_out=v_gm_w_out, v_cv_w_in=v_cv_w_in, v_cv_b_in=v_cv_b_in, v_cv_dw_w=v_cv_dw_w, v_cv_dw_b=v_cv_dw_b, v_cv_ln_g=v_cv_ln_g, v_cv_ln_b=v_cv_ln_b, v_cv_w_out=v_cv_w_out, v_cv_b_out=v_cv_b_out, v_final_g=v_final_g)
    weights = {n: given[n] for n in TWIN_WEIGHTS}
    shared = {n: given[n] for n in SHARED_INPUTS}
    per_example = {n: given[n] for n in ['x', 'c']}
    grad_fn = _jax.value_and_grad(_loss, argnums=(0, 1))

    def one_microbatch(ex, loss_target):
        ex = dict(ex)
        diff = ex.pop(TWIN_DIFF_INPUT)
        return grad_fn(weights, diff, {**shared, **ex}, loss_target)

    if N_MICROBATCH == 1:
        loss, (grad_w, grad_x) = one_microbatch(per_example, given["loss_target"])
    else:
        def body(carry, xs):
            loss_sum, grad_sum = carry
            l_k, (gw_k, gx_k) = one_microbatch(xs[0], xs[1])
            with _jax.named_scope("update"):
                return (loss_sum + l_k, _jax.tree.map(_jnp.add, grad_sum, gw_k)), gx_k

        init = (_jnp.zeros((), _jnp.float32), _jax.tree.map(_jnp.zeros_like, weights))
        (loss, grad_w), grad_x = _jax.lax.scan(body, init, (per_example, given["loss_target"]))
    with _jax.named_scope("update"):
        delta_w, new_m, new_v = {}, {}, {}
        for n in TWIN_WEIGHTS:
            delta_w[n], new_m[n], new_v[n] = _adamw(weights[n], grad_w[n], given["m_" + n], given["v_" + n])
    return (loss, grad_x, *[grad_w[n] for n in TWIN_WEIGHTS], *[delta_w[n] for n in TWIN_WEIGHTS],
            *[new_m[n] for n in TWIN_WEIGHTS], *[new_v[n] for n in TWIN_WEIGHTS])
```

```python
import functools

import jax
import jax.numpy as jnp
from jax import lax
from jax.experimental import pallas as pl
from jax.experimental.pallas import tpu as pltpu

F32 = jnp.float32
BF = jnp.bfloat16
MESH = pl.DeviceIdType.MESH

N_DEV = 8
N_CHIP = 4
NORM_EPS = 1e-6
ADAM_LR = 0.001
ADAM_B1 = 0.9
ADAM_B2 = 0.999
ADAM_EPS = 1e-08
ADAM_WD = 0.01
ADAM_STEP = 10

V7X_SUBLANES = 8
V7X_LANES = 128
PACK_ALIGN = V7X_SUBLANES * V7X_LANES
V7X_VMEM_LIMIT = 56 * 1024 * 1024


def _tile(n, pref, align):
    if n <= pref:
        return n
    t = pref - pref % align
    while t >= align:
        if n % t == 0:
            return t
        t -= align
    return n


def _call(body, *, name, grid, in_specs, out_specs, out_shape, scratch=(), sem=None):
    return pl.pallas_call(
        body, out_shape=out_shape, grid=grid, in_specs=in_specs, out_specs=out_specs,
        scratch_shapes=scratch, name=name,
        compiler_params=pltpu.CompilerParams(dimension_semantics=sem, vmem_limit_bytes=V7X_VMEM_LIMIT))


def _sds(shape, dtype):
    return jax.ShapeDtypeStruct(tuple(shape), dtype)


def _sigmoid(v):
    return 1.0 / (1.0 + jnp.exp(-v))


def _erf(v):
    a = jnp.abs(v)
    t = 1.0 / (1.0 + 0.3275911 * a)
    poly = t * (0.254829592 + t * (-0.284496736 + t * (1.421413741 + t * (-1.453152027 + t * 1.061405429))))
    r = 1.0 - poly * jnp.exp(-a * a)
    return jnp.where(v < 0, -r, r)


def _gelu(v):
    return 0.5 * v * (1.0 + _erf(v * 0.7071067811865476))


def _gelu_grad(v):
    cdf = 0.5 * (1.0 + _erf(v * 0.7071067811865476))
    pdf = 0.3989422804014327 * jnp.exp(-0.5 * v * v)
    return cdf + v * pdf


def _fold_rows(val):
    rows, w = val.shape
    return val.reshape(rows // V7X_SUBLANES, V7X_SUBLANES, w).sum(axis=0)


def _rowwise(fn, name, rows_in, vecs_in, rows_out, acc_widths, tm=256):
    t = rows_in[0].shape[0]
    tm = _tile(t, tm, V7X_SUBLANES)
    steps = t // tm
    nr, nv, no, na = len(rows_in), len(vecs_in), len(rows_out), len(acc_widths)

    def body(*refs):
        rin, vin = refs[:nr], refs[nr:nr + nv]
        rout = refs[nr + nv:nr + nv + no]
        aout = refs[nr + nv + no:nr + nv + no + na]
        accs = refs[nr + nv + no + na:]
        i = pl.program_id(0)
        outs, acc_vals = fn(*[r[...] for r in rin], *[v[...] for v in vin])
        for r, o in zip(rout, outs):
            r[...] = o.astype(r.dtype)
        if na:
            @pl.when(i == 0)
            def _():
                for a in accs:
                    a[...] = jnp.zeros_like(a)

            for a, val in zip(accs, acc_vals):
                a[...] += _fold_rows(val)

            @pl.when(i == steps - 1)
            def _():
                for o, a in zip(aout, accs):
                    o[...] = jnp.sum(a[...], axis=0, keepdims=True)

    in_specs = [pl.BlockSpec((tm, r.shape[1]), lambda i: (i, 0)) for r in rows_in]
    in_specs += [pl.BlockSpec(v.shape, functools.partial(lambda nd, i: (0,) * nd, v.ndim)) for v in vecs_in]
    out_specs = [pl.BlockSpec((tm, r.shape[1]), lambda i: (i, 0)) for r in rows_out]
    out_specs += [pl.BlockSpec((1, w), lambda i: (0, 0)) for w in acc_widths]
    out_shape = list(rows_out) + [_sds((1, w), F32) for w in acc_widths]
    scratch = [pltpu.VMEM((V7X_SUBLANES, w), F32) for w in acc_widths]
    res = _call(body, name=name, grid=(steps,), in_specs=in_specs, out_specs=out_specs, out_shape=out_shape,
                scratch=scratch, sem=("arbitrary",) if na else ("parallel",))(*rows_in, *vecs_in)
    return res[:no], res[no:]


def _norm_mod(x, g, scale, shift, name):
    def fn(xv, gv, sc, sh):
        r = lax.rsqrt(jnp.mean(xv * xv, axis=-1, keepdims=True) + NORM_EPS)
        return ((xv * r * gv) * (1.0 + sc) + sh,), ()
    (h,), _ = _rowwise(fn, name, [x], [g, scale, shift], [_sds(x.shape, BF)], [])
    return h


def _residual_bwd(dxp, y, gate, coef, with_colsum, name):
    def fn(dv, yv, gt):
        dy = (coef * gt) * dv
        accs = (coef * dv * yv.astype(F32),)
        if with_colsum:
            accs += (dy,)
        return (dy,), accs
    d = dxp.shape[1]
    (dy,), accs = _rowwise(fn, name, [dxp, y], [gate], [_sds(dxp.shape, BF)], [d, d] if with_colsum else [d])
    return dy, accs


def _norm_mod_bwd(dh, x, dxp, g, scale, name):
    def fn(dhv, xv, dpv, gv, sc):
        r = lax.rsqrt(jnp.mean(xv * xv, axis=-1, keepdims=True) + NORM_EPS)
        xhat = xv * r
        dn = dhv * (1.0 + sc)
        dxhat = dn * gv
        dx = r * (dxhat - xhat * jnp.mean(dxhat * xhat, axis=-1, keepdims=True)) + dpv
        return (dx,), (dhv * (xhat * gv), dhv, dn * xhat)
    d = x.shape[1]
    (dx,), (dscale, dshift, dg) = _rowwise(fn, name, [dh, x, dxp], [g, scale], [_sds(x.shape, F32)], [d, d, d])
    return dx, dscale, dshift, dg


def _final_loss(x, target, g, name):
    d = x.shape[1]

    def fn(xv, tv, gv):
        r = lax.rsqrt(jnp.mean(xv * xv, axis=-1, keepdims=True) + NORM_EPS)
        xhat = xv * r
        err = xhat * gv - tv
        dy = err * (1.0 / d)
        dxhat = dy * gv
        dx = r * (dxhat - xhat * jnp.mean(dxhat * xhat, axis=-1, keepdims=True))
        return (dx,), (err * err, dy * xhat)
    (dx,), (sq, dg) = _rowwise(fn, name, [x, target], [g], [_sds(x.shape, F32)], [d, d])
    return sq, dx, dg


def _gm_act(pre, ln_g, ln_b, name):
    e = pre.shape[1] // 2

    def fn(pv, gv, bv):
        p = pv.astype(F32)
        u = _gelu(p[:, :e])
        v = _gelu(p[:, e:])
        mu = jnp.mean(v, axis=-1, keepdims=True)
        vc = v - mu
        rstd = lax.rsqrt(jnp.mean(vc * vc, axis=-1, keepdims=True) + NORM_EPS)
        return (u, vc * rstd * gv + bv), ()
    t = pre.shape[0]
    (u, vn), _ = _rowwise(fn, name, [pre], [ln_g, ln_b], [_sds((t, e), BF), _sds((t, e), BF)], [])
    return u, vn


def _gm_act_bwd(pre, du, dvn, ln_g, name):
    e = pre.shape[1] // 2

    def fn(pv, duv, dvv, gv):
        p = pv.astype(F32)
        pu, pvv = p[:, :e], p[:, e:]
        v = _gelu(pvv)
        mu = jnp.mean(v, axis=-1, keepdims=True)
        vc = v - mu
        rstd = lax.rsqrt(jnp.mean(vc * vc, axis=-1, keepdims=True) + NORM_EPS)
        vhat = vc * rstd
        dvn_f = dvv.astype(F32)
        dvhat = dvn_f * gv
        dv = rstd * (dvhat - jnp.mean(dvhat, axis=-1, keepdims=True)
                     - vhat * jnp.mean(dvhat * vhat, axis=-1, keepdims=True))
        dpu = duv.astype(F32) * _gelu_grad(pu)
        dpv = dv * _gelu_grad(pvv)
        return (jnp.concatenate([dpu, dpv], axis=1),), (dvn_f * vhat, dvn_f)
    (dpre,), (dg, db) = _rowwise(fn, name, [pre, du, dvn], [ln_g], [_sds(pre.shape, BF)], [e, e], tm=128)
    return dpre, dg, db


def _cv_act(yc, ln_g, ln_b, name):
    def fn(yv, gv, bv):
        mu = jnp.mean(yv, axis=-1, keepdims=True)
        c = yv - mu
        rstd = lax.rsqrt(jnp.mean(c * c, axis=-1, keepdims=True) + NORM_EPS)
        yn = c * rstd * gv + bv
        return (yn * _sigmoid(yn),), ()
    (ys,), _ = _rowwise(fn, name, [yc], [ln_g, ln_b], [_sds(yc.shape, BF)], [])
    return ys


def _cv_act_bwd(dys, yc, ln_g, ln_b, name):
    def fn(dv, yv, gv, bv):
        mu = jnp.mean(yv, axis=-1, keepdims=True)
        c = yv - mu
        rstd = lax.rsqrt(jnp.mean(c * c, axis=-1, keepdims=True) + NORM_EPS)
        yhat = c * rstd
        yn = yhat * gv + bv
        sig = _sigmoid(yn)
        dyn = dv.astype(F32) * (sig * (1.0 + yn * (1.0 - sig)))
        dyhat = dyn * gv
        dyc = rstd * (dyhat - jnp.mean(dyhat, axis=-1, keepdims=True)
                      - yhat * jnp.mean(dyhat * yhat, axis=-1, keepdims=True))
        return (dyc,), (dyn * yhat, dyn, dyc)
    cw = yc.shape[1]
    (dyc,), (dg, db, dbias) = _rowwise(fn, name, [dys, yc], [ln_g, ln_b], [_sds(yc.shape, F32)], [cw, cw, cw])
    return dyc, dg, db, dbias


def _ffn_in(h, w_blk, blk0, name):
    t, d = h.shape
    bn = w_blk.shape[2]
    half = N_DEV // 2
    f = half * bn
    tm = _tile(t, 512, V7X_SUBLANES)

    def body(h_ref, wg_ref, wu_ref, g_ref, u_ref, a_ref):
        hv = h_ref[...]
        g = jnp.dot(hv, wg_ref[...], preferred_element_type=F32)
        u = jnp.dot(hv, wu_ref[...], preferred_element_type=F32)
        g_ref[...] = g.astype(BF)
        u_ref[...] = u.astype(BF)
        a_ref[...] = (g * _sigmoid(g) * u).astype(BF)

    out = _sds((t, f), BF)
    tile = pl.BlockSpec((tm, bn), lambda j, i: (i, j))
    return _call(
        body, name=name, grid=(half, t // tm),
        in_specs=[pl.BlockSpec((tm, d), lambda j, i: (i, 0)),
                  pl.BlockSpec((None, d, bn), lambda j, i: (blk0 + j, 0, 0)),
                  pl.BlockSpec((None, d, bn), lambda j, i: (blk0 + half + j, 0, 0))],
        out_specs=[tile, tile, tile], out_shape=[out, out, out], sem=("parallel", "parallel"))(h, w_blk, w_blk)


def _in_proj(h, w_blk, bias, name):
    t, d = h.shape
    bn = w_blk.shape[2]
    tm = _tile(t, 1024, V7X_SUBLANES)

    def body(*refs):
        if bias is None:
            h_ref, w_ref, o_ref = refs
            o_ref[...] = jnp.dot(h_ref[...], w_ref[...], preferred_element_type=F32).astype(BF)
        else:
            h_ref, w_ref, b_ref, o_ref = refs
            o_ref[...] = (jnp.dot(h_ref[...], w_ref[...], preferred_element_type=F32) + b_ref[...]).astype(BF)

    in_specs = [pl.BlockSpec((tm, d), lambda j, i: (i, 0)), pl.BlockSpec((None, d, bn), lambda j, i: (j, 0, 0))]
    args = [h, w_blk]
    if bias is not None:
        in_specs.append(pl.BlockSpec((1, bn), lambda j, i: (0, j)))
        args.append(bias)
    return _call(body, name=name, grid=(N_DEV, t // tm), in_specs=in_specs,
                 out_specs=pl.BlockSpec((tm, bn), lambda j, i: (i, j)), out_shape=_sds((t, N_DEV * bn), BF),
                 sem=("parallel", "parallel"))(*args)


def _out_proj(a, w3, widx, x, gate, bias, coef, name):
    t, k = a.shape
    d = w3.shape[2]
    tm = _tile(t, 512, V7X_SUBLANES)
    tn = _tile(d, 512, V7X_LANES)

    def body(*refs):
        if bias is None:
            a_ref, w_ref, x_ref, g_ref, xo_ref, y_ref = refs
            y = jnp.dot(a_ref[...], w_ref[...], preferred_element_type=F32)
        else:
            a_ref, w_ref, x_ref, g_ref, b_ref, xo_ref, y_ref = refs
            y = jnp.dot(a_ref[...], w_ref[...], preferred_element_type=F32) + b_ref[...]
        y_ref[...] = y.astype(BF)
        xo_ref[...] = x_ref[...] + (coef * g_ref[...]) * y

    tile = pl.BlockSpec((tm, tn), lambda j, i: (i, j))
    vec = pl.BlockSpec((1, tn), lambda j, i: (0, j))
    in_specs = [pl.BlockSpec((tm, k), lambda j, i: (i, 0)),
                pl.BlockSpec((None, k, tn), lambda j, i: (widx, 0, j)), tile, vec]
    args = [a, w3, x, gate]
    if bias is not None:
        in_specs.append(vec)
        args.append(bias)
    return _call(body, name=name, grid=(d // tn, t // tm), in_specs=in_specs, out_specs=[tile, tile],
                 out_shape=[_sds((t, d), F32), _sds((t, d), BF)], sem=("parallel", "parallel"))(*args)


def _ffn_da(dy, w3, widx, g, u, name):
    t, d = dy.shape
    f = w3.shape[1]
    bn = f // (N_DEV // 2)
    tm = _tile(t, 512, V7X_SUBLANES)

    def body(dy_ref, w_ref, g_ref, u_ref, dgu_ref, a_ref):
        da = lax.dot_general(dy_ref[...], w_ref[...], (((1,), (1,)), ((), ())), preferred_element_type=F32)
        gv = g_ref[...].astype(F32)
        uv = u_ref[...].astype(F32)
        sig = _sigmoid(gv)
        sl = gv * sig
        dgu_ref[0] = (da * uv * (sig * (1.0 + gv * (1.0 - sig)))).astype(BF)
        dgu_ref[1] = (da * sl).astype(BF)
        a_ref[...] = (sl * uv).astype(BF)

    tile = pl.BlockSpec((tm, bn), lambda j, i: (i, j))
    return _call(
        body, name=name, grid=(f // bn, t // tm),
        in_specs=[pl.BlockSpec((tm, d), lambda j, i: (i, 0)),
                  pl.BlockSpec((None, bn, d), lambda j, i: (widx, j, 0)), tile, tile],
        out_specs=[pl.BlockSpec((2, tm, bn), lambda j, i: (0, i, j)), tile],
        out_shape=[_sds((2, t, f), BF), _sds((t, f), BF)], sem=("parallel", "parallel"))(dy, w3, g, u)


def _mm_nt(dy, w3, widx, name):
    t, k = dy.shape
    n = w3.shape[1]
    tm = _tile(t, 512, V7X_SUBLANES)
    tn = _tile(n, 1024, V7X_LANES)

    def body(dy_ref, w_ref, o_ref):
        o_ref[...] = lax.dot_general(dy_ref[...], w_ref[...], (((1,), (1,)), ((), ())),
                                     preferred_element_type=F32).astype(BF)

    return _call(body, name=name, grid=(n // tn, t // tm),
                 in_specs=[pl.BlockSpec((tm, k), lambda j, i: (i, 0)),
                           pl.BlockSpec((None, tn, k), lambda j, i: (widx, j, 0))],
                 out_specs=pl.BlockSpec((tm, tn), lambda j, i: (i, j)), out_shape=_sds((t, n), BF),
                 sem=("parallel", "parallel"))(dy, w3)


def _mm_nt_blocks(z3, w_blk, blk0, name):
    lead, t, _ = z3.shape
    d, bn = w_blk.shape[1], w_blk.shape[2]
    per = N_DEV // lead
    tm = _tile(t, 512, V7X_SUBLANES)

    def body(z_ref, w_ref, o_ref, acc_ref):
        k = pl.program_id(1)

        @pl.when(k == 0)
        def _():
            acc_ref[...] = jnp.zeros_like(acc_ref)

        acc_ref[...] += lax.dot_general(z_ref[...], w_ref[...], (((1,), (1,)), ((), ())),
                                        preferred_element_type=F32)

        @pl.when(k == N_DEV - 1)
        def _():
            o_ref[...] = acc_ref[...]

    return _call(body, name=name, grid=(t // tm, N_DEV),
                 in_specs=[pl.BlockSpec((None, tm, bn), lambda i, k: (k // per, i, k % per)),
                           pl.BlockSpec((None, d, bn), lambda i, k: (blk0 + k, 0, 0))],
                 out_specs=pl.BlockSpec((tm, d), lambda i, k: (i, 0)), out_shape=_sds((t, d), F32),
                 scratch=[pltpu.VMEM((tm, d), F32)], sem=("parallel", "arbitrary"))(z3, w_blk)


def _mm_tn(a, b3, ta, tb, blocked, name):
    t, ka = a.shape
    lead, _, w = b3.shape
    per = w // tb
    nj = lead * per
    tk = _tile(t, 512, V7X_SUBLANES)
    nk = t // tk

    def body(a_ref, b_ref, o_ref, acc_ref):
        k = pl.program_id(2)

        @pl.when(k == 0)
        def _():
            acc_ref[...] = jnp.zeros_like(acc_ref)

        acc_ref[...] += lax.dot_general(a_ref[...], b_ref[...], (((0,), (0,)), ((), ())),
                                        preferred_element_type=F32)

        @pl.when(k == nk - 1)
        def _():
            o_ref[...] = acc_ref[...].astype(BF)

    if blocked:
        out_shape = _sds((nj, ka, tb), BF)
        out_spec = pl.BlockSpec((None, ta, tb), lambda i, j, k: (j, i, 0))
    else:
        out_shape = _sds((1, ka, w), BF)
        out_spec = pl.BlockSpec((None, ta, tb), lambda i, j, k: (0, i, j))
    return _call(body, name=name, grid=(ka // ta, nj, nk),
                 in_specs=[pl.BlockSpec((tk, ta), lambda i, j, k: (k, i)),
                           pl.BlockSpec((None, tk, tb), lambda i, j, k: (j // per, k, j % per))],
                 out_specs=out_spec, out_shape=out_shape, scratch=[pltpu.VMEM((ta, tb), F32)],
                 sem=("parallel", "parallel", "arbitrary"))(a, b3)


def _causal(ws):
    l = ws.shape[0]
    row = lax.broadcasted_iota(jnp.int32, (l, l), 0)
    col = lax.broadcasted_iota(jnp.int32, (l, l), 1)
    return jnp.where(col <= row, ws, 0.0)


def _sgu_fwd(u, vn, ws, bsb, name):
    t, e = u.shape
    hn, l, _ = ws.shape
    dh = e // hn
    nc = t // l

    def body(u_ref, v_ref, ws_ref, bs_ref, s_ref):
        wsc = _causal(ws_ref[...]).astype(BF)
        bias = bs_ref[...]

        def chunk(c, carry):
            rows = pl.ds(pl.multiple_of(c * l, l), l)
            vo = jnp.dot(wsc, v_ref[rows, :], preferred_element_type=F32) + bias
            s_ref[rows, :] = (u_ref[rows, :].astype(F32) * vo).astype(BF)
            return carry
        lax.fori_loop(0, nc, chunk, 0)

    col = pl.BlockSpec((t, dh), lambda h: (0, h))
    return _call(body, name=name, grid=(hn,),
                 in_specs=[col, col, pl.BlockSpec((None, l, l), lambda h: (h, 0, 0)),
                           pl.BlockSpec((None, l, dh), lambda h: (h, 0, 0))],
                 out_specs=col, out_shape=_sds((t, e), BF), sem=("parallel",))(u, vn, ws, bsb)


def _sgu_bwd(ds, u, vn, ws, bsb, name):
    t, e = u.shape
    hn, l, _ = ws.shape
    dh = e // hn
    nc = t // l

    def body(ds_ref, u_ref, v_ref, ws_ref, bs_ref, du_ref, dv_ref, dws_ref, dbs_ref, accw_ref, accb_ref):
        wsc = _causal(ws_ref[...]).astype(BF)
        bias = bs_ref[...]
        accw_ref[...] = jnp.zeros_like(accw_ref)
        accb_ref[...] = jnp.zeros_like(accb_ref)

        def chunk(c, carry):
            rows = pl.ds(pl.multiple_of(c * l, l), l)
            vc = v_ref[rows, :]
            dsv = ds_ref[rows, :].astype(F32)
            vo = jnp.dot(wsc, vc, preferred_element_type=F32) + bias
            du_ref[rows, :] = (dsv * vo).astype(BF)
            dvo = dsv * u_ref[rows, :].astype(F32)
            dvo_b = dvo.astype(BF)
            accb_ref[...] += dvo
            accw_ref[...] += lax.dot_general(dvo_b, vc, (((1,), (1,)), ((), ())), preferred_element_type=F32)
            dv_ref[rows, :] = lax.dot_general(wsc, dvo_b, (((0,), (0,)), ((), ())),
                                              preferred_element_type=F32).astype(BF)
            return carry
        lax.fori_loop(0, nc, chunk, 0)
        dws_ref[...] = _causal(accw_ref[...])
        dbs_ref[...] = jnp.broadcast_to(jnp.sum(accb_ref[...], axis=1, keepdims=True), (l, dh))

    col = pl.BlockSpec((t, dh), lambda h: (0, h))
    return _call(body, name=name, grid=(hn,),
                 in_specs=[col, col, col, pl.BlockSpec((None, l, l), lambda h: (h, 0, 0)),
                           pl.BlockSpec((None, l, dh), lambda h: (h, 0, 0))],
                 out_specs=[col, col, pl.BlockSpec((None, l, l), lambda h: (h, 0, 0)),
                            pl.BlockSpec((None, l, dh), lambda h: (h, 0, 0))],
                 out_shape=[_sds((t, e), BF), _sds((t, e), BF), _sds((hn, l, l), F32), _sds((hn, l, dh), F32)],
                 scratch=[pltpu.VMEM((l, l), F32), pltpu.VMEM((l, dh), F32)],
                 sem=("parallel",))(ds, u, vn, ws, bsb)


CONV_HALO = 32


def _dwconv_fwd(p, dw_w, dw_b, name):
    t, c2 = p.shape
    cw = c2 // 2
    kw = dw_w.shape[0]
    cb = _tile(cw, 512, V7X_LANES)
    ncb = cw // cb
    tm = _tile(t, 512, CONV_HALO)
    off = CONV_HALO - (kw - 1)

    def body(a_ref, g_ref, ap_ref, gp_ref, w_ref, b_ref, o_ref, win_ref):
        i = pl.program_id(1)
        prev = ap_ref[...].astype(F32) * _sigmoid(gp_ref[...].astype(F32))
        win_ref[0:CONV_HALO, :] = jnp.where(i > 0, prev, 0.0)
        win_ref[CONV_HALO:, :] = a_ref[...].astype(F32) * _sigmoid(g_ref[...].astype(F32))
        acc = jnp.zeros((tm, cb), F32) + b_ref[...]
        for k in range(kw):
            acc = acc + w_ref[k:k + 1, :] * win_ref[off + k:off + k + tm, :]
        o_ref[...] = acc

    hpt = tm // CONV_HALO
    cur_a = pl.BlockSpec((tm, cb), lambda j, i: (i, j))
    cur_g = pl.BlockSpec((tm, cb), lambda j, i: (i, ncb + j))
    prev_a = pl.BlockSpec((CONV_HALO, cb), lambda j, i: (jnp.maximum(i * hpt - 1, 0), j))
    prev_g = pl.BlockSpec((CONV_HALO, cb), lambda j, i: (jnp.maximum(i * hpt - 1, 0), ncb + j))
    return _call(body, name=name, grid=(ncb, t // tm),
                 in_specs=[cur_a, cur_g, prev_a, prev_g, pl.BlockSpec((kw, cb), lambda j, i: (0, j)),
                           pl.BlockSpec((1, cb), lambda j, i: (0, j))],
                 out_specs=pl.BlockSpec((tm, cb), lambda j, i: (i, j)), out_shape=_sds((t, cw), F32),
                 scratch=[pltpu.VMEM((tm + CONV_HALO, cb), F32)],
                 sem=("parallel", "parallel"))(p, p, p, p, dw_w, dw_b)


def _dwconv_bwd(dyc, p, dw_w, name):
    t, c2 = p.shape
    cw = c2 // 2
    kw = dw_w.shape[0]
    cb = _tile(cw, 512, V7X_LANES)
    ncb = cw // cb
    tm = _tile(t, 512, CONV_HALO)
    nt = t // tm
    off = CONV_HALO - (kw - 1)
    kpad = -(-kw // V7X_SUBLANES) * V7X_SUBLANES

    def body(d_ref, dn_ref, a_ref, g_ref, ap_ref, gp_ref, w_ref,
             dp_ref, dw_ref, dba_ref, dbg_ref, dwin_ref, ywin_ref, accw_ref, acca_ref, accg_ref):
        i = pl.program_id(1)

        @pl.when(i == 0)
        def _():
            accw_ref[...] = jnp.zeros_like(accw_ref)
            acca_ref[...] = jnp.zeros_like(acca_ref)
            accg_ref[...] = jnp.zeros_like(accg_ref)

        av = a_ref[...].astype(F32)
        sig = _sigmoid(g_ref[...].astype(F32))
        prev = ap_ref[...].astype(F32) * _sigmoid(gp_ref[...].astype(F32))
        ywin_ref[0:CONV_HALO, :] = jnp.where(i > 0, prev, 0.0)
        ywin_ref[CONV_HALO:, :] = av * sig
        dcur = d_ref[...]
        dwin_ref[0:tm, :] = dcur
        dwin_ref[tm:, :] = jnp.where(i < nt - 1, dn_ref[...], 0.0)

        dyg = jnp.zeros((tm, cb), F32)
        for k in range(kw):
            dyg = dyg + w_ref[k:k + 1, :] * dwin_ref[kw - 1 - k:kw - 1 - k + tm, :]
            accw_ref[k] += _fold_rows(dcur * ywin_ref[off + k:off + k + tm, :])
        da = dyg * sig
        dg = dyg * av * sig * (1.0 - sig)
        dp_ref[0] = da.astype(BF)
        dp_ref[1] = dg.astype(BF)
        acca_ref[...] += _fold_rows(da)
        accg_ref[...] += _fold_rows(dg)

        @pl.when(i == nt - 1)
        def _():
            dw_ref[...] = jnp.sum(accw_ref[...], axis=1)
            dba_ref[...] = jnp.sum(acca_ref[...], axis=0, keepdims=True)
            dbg_ref[...] = jnp.sum(accg_ref[...], axis=0, keepdims=True)

    hpt = tm // CONV_HALO
    last_halo = t // CONV_HALO - 1
    tile = pl.BlockSpec((tm, cb), lambda j, i: (i, j))
    cur_g = pl.BlockSpec((tm, cb), lambda j, i: (i, ncb + j))
    nxt = pl.BlockSpec((CONV_HALO, cb), lambda j, i: (jnp.minimum((i + 1) * hpt, last_halo), j))
    prev_a = pl.BlockSpec((CONV_HALO, cb), lambda j, i: (jnp.maximum(i * hpt - 1, 0), j))
    prev_g = pl.BlockSpec((CONV_HALO, cb), lambda j, i: (jnp.maximum(i * hpt - 1, 0), ncb + j))
    vec = pl.BlockSpec((1, cb), lambda j, i: (0, j))
    dp, ddw, dba, dbg = _call(
        body, name=name, grid=(ncb, nt),
        in_specs=[tile, nxt, tile, cur_g, prev_a, prev_g, pl.BlockSpec((kw, cb), lambda j, i: (0, j))],
        out_specs=[pl.BlockSpec((2, tm, cb), lambda j, i: (0, i, j)), pl.BlockSpec((kpad, cb), lambda j, i: (0, j)),
                   vec, vec],
        out_shape=[_sds((2, t, cw), BF), _sds((kpad, cw), F32), _sds((1, cw), F32), _sds((1, cw), F32)],
        scratch=[pltpu.VMEM((tm + CONV_HALO, cb), F32), pltpu.VMEM((tm + CONV_HALO, cb), F32),
                 pltpu.VMEM((kpad, V7X_SUBLANES, cb), F32), pltpu.VMEM((V7X_SUBLANES, cb), F32),
                 pltpu.VMEM((V7X_SUBLANES, cb), F32)],
        sem=("parallel", "arbitrary"))(dyc, dyc, p, p, p, p, dw_w)
    return dp, ddw[:kw], dba, dbg


def _adam_math(g, w, m, v):
    m2 = ADAM_B1 * m + (1.0 - ADAM_B1) * g
    v2 = ADAM_B2 * v + (1.0 - ADAM_B2) * (g * g)
    m_hat = m2 / (1.0 - ADAM_B1 ** ADAM_STEP)
    v_hat = v2 / (1.0 - ADAM_B2 ** ADAM_STEP)
    delta = -ADAM_LR * (m_hat / (jnp.sqrt(v_hat) + ADAM_EPS) + ADAM_WD * w)
    return delta, m2, v2


def _adamw(g_parts, w, m, v, name):
    r, c = w.shape
    tr = _tile(r, 256, V7X_SUBLANES)
    ng = len(g_parts)

    def body(*refs):
        g = refs[0][...].astype(F32)
        for s in refs[1:ng]:
            g = g + s[...].astype(F32)
        w_ref, m_ref, v_ref, go_ref, d_ref, mo_ref, vo_ref = refs[ng:]
        delta, m2, v2 = _adam_math(g, w_ref[...], m_ref[...], v_ref[...])
        go_ref[...] = g
        d_ref[...] = delta
        mo_ref[...] = m2
        vo_ref[...] = v2

    tile = pl.BlockSpec((tr, c), lambda i: (i, 0))
    in_specs = [pl.BlockSpec((None, tr, c), functools.partial(lambda s, i: (s, i, 0), s)) for _, s in g_parts]
    out = _sds((r, c), F32)
    return _call(body, name=name, grid=(r // tr,), in_specs=in_specs + [tile, tile, tile],
                 out_specs=[tile] * 4, out_shape=[out] * 4, sem=("parallel",))(*[a for a, _ in g_parts], w, m, v)


def _ada_fwd(c_pad, ada_w, ada_b, name):
    nl, d, cl = ada_w.shape
    rows = c_pad.shape[0]
    tn = _tile(cl, 256, V7X_LANES)

    def body(c_ref, w_ref, b_ref, o_ref):
        cv = c_ref[...]
        cond = (cv * _sigmoid(cv)).astype(BF)
        o_ref[...] = jnp.dot(cond, w_ref[...].astype(BF), preferred_element_type=F32) + b_ref[...]

    return _call(body, name=name, grid=(nl, cl // tn),
                 in_specs=[pl.BlockSpec((rows, d), lambda l, j: (0, 0)),
                           pl.BlockSpec((None, d, tn), lambda l, j: (l, 0, j)),
                           pl.BlockSpec((None, 1, tn), lambda l, j: (l, 0, j))],
                 out_specs=pl.BlockSpec((None, rows, tn), lambda l, j: (l, 0, j)),
                 out_shape=_sds((nl, rows, cl), F32), sem=("parallel", "parallel"))(c_pad, ada_w, ada_b)


def _ada_bwd(c_pad, dmod, w, m, v, name):
    nl, d, cl = w.shape
    rows = c_pad.shape[0]
    tn = _tile(cl, 256, V7X_LANES)

    def body(c_ref, dm_ref, w_ref, m_ref, v_ref, go_ref, d_ref, mo_ref, vo_ref):
        cv = c_ref[...]
        cond = (cv * _sigmoid(cv)).astype(BF)
        g = lax.dot_general(cond, dm_ref[...].astype(BF), (((0,), (0,)), ((), ())), preferred_element_type=F32)
        delta, m2, v2 = _adam_math(g, w_ref[...], m_ref[...], v_ref[...])
        go_ref[...] = g
        d_ref[...] = delta
        mo_ref[...] = m2
        vo_ref[...] = v2

    tile = pl.BlockSpec((None, d, tn), lambda l, j: (l, 0, j))
    out = _sds((nl, d, cl), F32)
    return _call(body, name=name, grid=(nl, cl // tn),
                 in_specs=[pl.BlockSpec((rows, d), lambda l, j: (0, 0)),
                           pl.BlockSpec((None, rows, tn), lambda l, j: (l, 0, j)), tile, tile, tile],
                 out_specs=[tile] * 4, out_shape=[out] * 4, sem=("parallel", "parallel"))(c_pad, dmod, w, m, v)


def _sum_devices(parts, name):
    n, r, c = parts.shape
    tr = _tile(r, 512, V7X_SUBLANES)

    def body(p_ref, o_ref):
        acc = p_ref[0]
        for k in range(1, n):
            acc = acc + p_ref[k]
        o_ref[...] = acc

    return _call(body, name=name, grid=(r // tr,), in_specs=[pl.BlockSpec((n, tr, c), lambda i: (0, i, 0))],
                 out_specs=pl.BlockSpec((tr, c), lambda i: (i, 0)), out_shape=_sds((r, c), F32),
                 sem=("parallel",))(parts)


def _add_pairs(own, recv, name):
    n, r, c = own.shape
    tr = _tile(r, 512, V7X_SUBLANES)

    def body(a_ref, b_ref, o_ref):
        o_ref[...] = (a_ref[...].astype(F32) + b_ref[...].astype(F32)).astype(BF)

    tile = pl.BlockSpec((None, tr, c), lambda p, i: (p, i, 0))
    return _call(body, name=name, grid=(n, r // tr), in_specs=[tile, tile], out_specs=tile,
                 out_shape=_sds((n, r, c), BF), sem=("parallel", "parallel"))(own, recv)


ANY = pl.BlockSpec(memory_space=pl.ANY)


def _mesh_pos():
    return lax.axis_index("x"), lax.axis_index("y"), lax.axis_index("c")


def _other_chips(x, y):
    return [(1 - x, y), (x, 1 - y), (1 - x, 1 - y)]


def _all_gather(arrs, name):
    n = len(arrs)

    def body(*refs):
        ins, outs = refs[:n], refs[n:2 * n]
        send_sems, recv_sems, local_sems = refs[2 * n:]
        x, y, c = _mesh_pos()
        me, sibling = (x, y, c), (x, y, 1 - c)
        chips = _other_chips(x, y)

        def slot(a, pos):
            px, py, pc = pos
            return outs[a].at[:, pl.ds(4 * px + 2 * py + pc, 1)]

        def copy(a, k, block, to, src=None):
            return pltpu.make_async_remote_copy(
                src_ref=slot(a, block) if src is None else src, dst_ref=slot(a, block),
                send_sem=send_sems.at[a, k], recv_sem=recv_sems.at[a, k], device_id=to, device_id_type=MESH)

        mine = [pltpu.make_async_copy(ins[a], slot(a, me), local_sems.at[a]) for a in range(n)]
        for cp in mine:
            cp.start()
        first = []
        for a in range(n):
            first.append(copy(a, 0, me, sibling, src=ins[a]))
            first += [copy(a, 1 + j, me, (*chip, c), src=ins[a]) for j, chip in enumerate(chips)]
        for cp in first:
            cp.start()
        passed = []
        for a in range(n):
            for j, chip in enumerate(chips):
                copy(a, 1 + j, (*chip, c), me).wait_recv()
                fwd = copy(a, 4 + j, (*chip, c), sibling)
                fwd.start()
                passed.append(fwd)
        for a in range(n):
            copy(a, 0, sibling, me).wait_recv()
            for j, chip in enumerate(chips):
                copy(a, 4 + j, (*chip, 1 - c), me).wait_recv()
        for cp in first + passed:
            cp.wait_send()
        for cp in mine:
            cp.wait()

    out_shape = [_sds((a.shape[0], N_DEV) + a.shape[2:], a.dtype) for a in arrs]
    return pl.pallas_call(
        body, out_shape=out_shape, in_specs=[ANY] * n, out_specs=[ANY] * n, name=name,
        scratch_shapes=[pltpu.SemaphoreType.DMA((n, N_DEV - 1)), pltpu.SemaphoreType.DMA((n, N_DEV - 1)),
                        pltpu.SemaphoreType.DMA((n,))])(*arrs)


def _exchange_sibling(grads, name):
    n = len(grads)

    def body(*refs):
        ins, owns, recvs = refs[:n], refs[n:2 * n], refs[2 * n:3 * n]
        send_sems, recv_sems, local_sems = refs[3 * n:]
        x, y, c = _mesh_pos()
        sibling = (x, y, 1 - c)
        local = [pltpu.make_async_copy(ins[a].at[:, pl.ds(c, 1)], owns[a], local_sems.at[a]) for a in range(n)]
        remote = [pltpu.make_async_remote_copy(
            src_ref=ins[a].at[:, pl.ds(1 - c, 1)], dst_ref=recvs[a], send_sem=send_sems.at[a],
            recv_sem=recv_sems.at[a], device_id=sibling, device_id_type=MESH) for a in range(n)]
        for cp in remote + local:
            cp.start()
        for cp in remote + local:
            cp.wait()

    shapes = [_sds((N_CHIP, 1) + g.shape[2:], g.dtype) for g in grads]
    res = pl.pallas_call(
        body, out_shape=shapes + shapes, in_specs=[ANY] * n, out_specs=[ANY] * (2 * n), name=name,
        scratch_shapes=[pltpu.SemaphoreType.DMA((n,)), pltpu.SemaphoreType.DMA((n,)),
                        pltpu.SemaphoreType.DMA((n,))])(*grads)
    return res[:n], res[n:]


def _exchange_chips(sums, name):
    n = len(sums)

    def body(*refs):
        ins, owns, recvs = refs[:n], refs[n:2 * n], refs[2 * n:3 * n]
        send_sems, recv_sems, local_sems = refs[3 * n:]
        x, y, c = _mesh_pos()
        chips = _other_chips(x, y)
        local = [pltpu.make_async_copy(ins[a].at[pl.ds(2 * x + y, 1)], owns[a], local_sems.at[a]) for a in range(n)]
        remote = []
        for a in range(n):
            for j, (px, py) in enumerate(chips):
                remote.append(pltpu.make_async_remote_copy(
                    src_ref=ins[a].at[pl.ds(2 * px + py, 1)], dst_ref=recvs[a].at[pl.ds(j, 1)],
                    send_sem=send_sems.at[a, j], recv_sem=recv_sems.at[a, j],
                    device_id=(px, py, c), device_id_type=MESH))
        for cp in remote + local:
            cp.start()
        for cp in remote + local:
            cp.wait()

    own_shapes = [_sds((1,) + s.shape[1:], s.dtype) for s in sums]
    recv_shapes = [_sds((N_CHIP - 1,) + s.shape[1:], s.dtype) for s in sums]
    res = pl.pallas_call(
        body, out_shape=own_shapes + recv_shapes, in_specs=[ANY] * n, out_specs=[ANY] * (2 * n), name=name,
        scratch_shapes=[pltpu.SemaphoreType.DMA((n, N_CHIP - 1)), pltpu.SemaphoreType.DMA((n, N_CHIP - 1)),
                        pltpu.SemaphoreType.DMA((n,))])(*sums)
    return res[:n], res[n:]


def _pack(parts):
    flat = []
    for p in parts:
        v = p.reshape(-1).astype(F32)
        pad = -v.shape[0] % PACK_ALIGN
        flat.append(jnp.pad(v, (0, pad)) if pad else v)
    return jnp.concatenate(flat).reshape(-1, V7X_LANES)


def _unpack(buf, shapes):
    lead = buf.shape[:-2]
    flat = buf.reshape(lead + (-1,))
    out, pos = [], 0
    for s in shapes:
        size = 1
        for d in s:
            size *= d
        out.append(flat[..., pos:pos + size].reshape(lead + tuple(s)))
        pos += size + (-size % PACK_ALIGN)
    return out


def _reduce_scatter_update(grads, params, tag):
    g4 = [g.reshape((N_CHIP, 2) + g.shape[1:]) for g in grads]
    owns, recvs = _exchange_sibling(g4, name=f"rs_sibling_{tag}")
    sums = [_add_pairs(o.reshape((N_CHIP,) + o.shape[2:]), r.reshape((N_CHIP,) + r.shape[2:]),
                       name=f"rs_add_{tag}_{i}") for i, (o, r) in enumerate(zip(owns, recvs))]
    owns2, recvs2 = _exchange_chips(sums, name=f"rs_chips_{tag}")
    return [_adamw([(o, 0), (r, 0), (r, 1), (r, 2)], w, m, v, name=f"adamw_{tag}_{i}")
            for i, (o, r, (w, m, v)) in enumerate(zip(owns2, recvs2, params))]


def kernel(x, c, ada_w, ada_b, norm_g, ffn_w_in, ffn_w_out, gm_w_in, gm_ln_g, gm_ln_b, gm_ws, gm_bs, gm_w_out, cv_w_in, cv_b_in, cv_dw_w, cv_dw_b, cv_ln_g, cv_ln_b, cv_w_out, cv_b_out, final_g, loss_target, m_ada_w, m_ada_b, m_norm_g, m_ffn_w_in, m_ffn_w_out, m_gm_w_in, m_gm_ln_g, m_gm_ln_b, m_gm_ws, m_gm_bs, m_gm_w_out, m_cv_w_in, m_cv_b_in, m_cv_dw_w, m_cv_dw_b, m_cv_ln_g, m_cv_ln_b, m_cv_w_out, m_cv_b_out, m_final_g, v_ada_w, v_ada_b, v_norm_g, v_ffn_w_in, v_ffn_w_out, v_gm_w_in, v_gm_ln_g, v_gm_ln_b, v_gm_ws, v_gm_bs, v_gm_w_out, v_cv_w_in, v_cv_b_in, v_cv_dw_w, v_cv_dw_b, v_cv_ln_g, v_cv_ln_b, v_cv_w_out, v_cv_b_out, v_final_g):
    t, d = x.shape[1], x.shape[2]
    depth = ada_w.shape[0]
    assert depth == 2 and ffn_w_in.shape[:2] == (2, 2) and gm_w_in.shape[0] == 1 and cv_w_in.shape[0] == 1
    dl = d // N_DEV
    bn = ffn_w_in.shape[3]
    fl = ffn_w_out.shape[2]
    f = fl * N_DEV
    el = gm_w_in.shape[2]
    e = el * N_DEV // 2
    hn, l = gm_ws.shape[1], gm_ws.shape[2]
    kw = cv_dw_w.shape[1]
    cl = ada_w.shape[2]
    me = 4 * lax.axis_index("x") + 2 * lax.axis_index("y") + lax.axis_index("c")

    xs = x[0]
    tgt = loss_target[0]

    small_in = [c, norm_g, cv_b_in, cv_dw_w, cv_dw_b, cv_ln_g, cv_ln_b, cv_b_out]
    pack1 = _pack(small_in)
    (pack1_all,) = _all_gather([pack1[None, None]], name="ag_small")
    parts = _unpack(pack1_all[0], [s.shape for s in small_in])
    c_all = parts[0].reshape(N_DEV, d)
    ng_full = jnp.moveaxis(parts[1], 0, 2).reshape(depth, 3, d)
    cvb_in_full = parts[2].reshape(1, 2 * e)
    dww_full = jnp.moveaxis(parts[3][:, 0], 0, 1).reshape(kw, e)
    dwb_full, cln_g_full, cln_b_full, cvb_out_full = [p.reshape(1, d) for p in parts[4:8]]

    big_in = [ffn_w_in.astype(BF).reshape(4, 1, d, bn), ffn_w_out.astype(BF).reshape(4, 1, fl, d),
              gm_w_in.astype(BF).reshape(1, 1, d, el), gm_w_out.astype(BF).reshape(1, 1, dl, d),
              cv_w_in.astype(BF).reshape(1, 1, d, el), cv_w_out.astype(BF).reshape(1, 1, dl, d)]
    w_in_all, w_out_all, gm_in_all, gm_out_all, cv_in_all, cv_out_all = _all_gather(big_in, name="ag_weights")
    w_in_blk = w_in_all.reshape(4 * N_DEV, d, bn)
    w_out3 = w_out_all.reshape(4, f, d)
    gm_in_blk = gm_in_all.reshape(N_DEV, d, el)
    gm_out3 = gm_out_all.reshape(1, e, d)
    cv_in_blk = cv_in_all.reshape(N_DEV, d, el)
    cv_out3 = cv_out_all.reshape(1, e, d)

    c_pad = jnp.pad(c_all, ((0, 16 - N_DEV), (0, 0)))
    ada_b_loc = lax.dynamic_slice_in_dim(ada_b, me * cl, cl, axis=1).reshape(depth, 1, cl)
    mod_part = _ada_fwd(c_pad, ada_w, ada_b_loc, name="ada_fwd")[:, :N_DEV]
    (mod_all,) = _all_gather([_pack([mod_part])[None, None]], name="ag_mod")
    mod_all = _unpack(mod_all[0], [mod_part.shape])[0]
    mod_mine = lax.dynamic_index_in_dim(mod_all, me, axis=2, keepdims=False)
    mod = jnp.moveaxis(mod_mine, 0, 1).reshape(depth, 3, 3, 1, d)

    ws = gm_ws[0]
    bsb = jnp.broadcast_to(gm_bs[0][:, :, None], (hn, l, e // hn))
    gm_g, gm_b = gm_ln_g, gm_ln_b

    saved = []
    xcur = xs
    for i in range(depth):
        for s in range(3):
            shift, scale, gate = mod[i, s, 0], mod[i, s, 1], mod[i, s, 2]
            g_norm = ng_full[i, s][None]
            tag = f"l{i}s{s}"
            h = _norm_mod(xcur, g_norm, scale, shift, name=f"norm_mod_{tag}")
            if s != 1:
                widx = 2 * i + s // 2
                gg, uu, act = _ffn_in(h, w_in_blk, widx * N_DEV, name=f"ffn_in_{tag}")
                xnext, yv = _out_proj(act, w_out3, widx, xcur, gate, None, 0.5, name=f"ffn_out_{tag}")
                saved.append(dict(x=xcur, h=h, g=gg, u=uu, y=yv, widx=widx))
            elif i % 2 == 0:
                pre = _in_proj(h, gm_in_blk, None, name=f"gm_in_{tag}")
                uu, vn = _gm_act(pre, gm_g, gm_b, name=f"gm_act_{tag}")
                sg = _sgu_fwd(uu, vn, ws, bsb, name=f"sgu_fwd_{tag}")
                xnext, yv = _out_proj(sg, gm_out3, 0, xcur, gate, None, 1.0, name=f"gm_out_{tag}")
                saved.append(dict(x=xcur, h=h, pre=pre, u=uu, vn=vn, sg=sg, y=yv))
            else:
                p = _in_proj(h, cv_in_blk, cvb_in_full, name=f"cv_in_{tag}")
                yc = _dwconv_fwd(p, dww_full, dwb_full, name=f"dwconv_fwd_{tag}")
                ys = _cv_act(yc, cln_g_full, cln_b_full, name=f"cv_act_{tag}")
                xnext, yv = _out_proj(ys, cv_out3, 0, xcur, gate, cvb_out_full, 1.0, name=f"cv_out_{tag}")
                saved.append(dict(x=xcur, h=h, p=p, yc=yc, ys=ys, y=yv))
            xcur = xnext

    sq, dx, d_final_g = _final_loss(xcur, tgt, final_g[None], name="final_loss")
    loss = lax.psum(0.5 / d * jnp.sum(sq), ("x", "y", "c"))

    dmod = [[[None] * 3 for _ in range(3)] for _ in range(depth)]
    d_norm_g = [[None] * 3 for _ in range(depth)]
    big_grads = {}
    small = {}
    for i in reversed(range(depth)):
        for s in reversed(range(3)):
            sv = saved[3 * i + s]
            shift, scale, gate = mod[i, s, 0], mod[i, s, 1], mod[i, s, 2]
            g_norm = ng_full[i, s][None]
            tag = f"l{i}s{s}"
            if s != 1:
                widx = sv["widx"]
                dy, (dgate,) = _residual_bwd(dx, sv["y"], gate, 0.5, False, name=f"res_bwd_{tag}")
                dgu, act = _ffn_da(dy, w_out3, widx, sv["g"], sv["u"], name=f"ffn_da_{tag}")
                big_grads[("w_out", widx)] = _mm_tn(act, dy[None], bn, d, False, name=f"ffn_dwout_{tag}")
                big_grads[("w_in", widx)] = _mm_tn(sv["h"], dgu, d, bn, True, name=f"ffn_dwin_{tag}")
                dh = _mm_nt_blocks(dgu, w_in_blk, widx * N_DEV, name=f"ffn_dh_{tag}")
            elif i % 2 == 0:
                dy, (dgate,) = _residual_bwd(dx, sv["y"], gate, 1.0, False, name=f"res_bwd_{tag}")
                ds = _mm_nt(dy, gm_out3, 0, name=f"gm_ds_{tag}")
                big_grads[("gm_out", 0)] = _mm_tn(sv["sg"], dy[None], _tile(e, 1024, V7X_LANES), d, False,
                                                  name=f"gm_dwout_{tag}")
                du, dvn, dws, dbs = _sgu_bwd(ds, sv["u"], sv["vn"], ws, bsb, name=f"sgu_bwd_{tag}")
                dpre, dlng, dlnb = _gm_act_bwd(sv["pre"], du, dvn, gm_g, name=f"gm_act_bwd_{tag}")
                small["gm_ln_g"], small["gm_ln_b"] = dlng, dlnb
                small["gm_ws"], small["gm_bs"] = dws, dbs[:, :, 0]
                big_grads[("gm_in", 0)] = _mm_tn(sv["h"], dpre[None], d, el, True, name=f"gm_dwin_{tag}")
                dh = _mm_nt_blocks(dpre[None], gm_in_blk, 0, name=f"gm_dh_{tag}")
            else:
                dy, (dgate, dbout) = _residual_bwd(dx, sv["y"], gate, 1.0, True, name=f"res_bwd_{tag}")
                dys = _mm_nt(dy, cv_out3, 0, name=f"cv_dys_{tag}")
                big_grads[("cv_out", 0)] = _mm_tn(sv["ys"], dy[None], _tile(e, 1024, V7X_LANES), d, False,
                                                  name=f"cv_dwout_{tag}")
                dyc, dlng, dlnb, ddwb = _cv_act_bwd(dys, sv["yc"], cln_g_full, cln_b_full, name=f"cv_act_bwd_{tag}")
                dp, ddww, dba, dbg = _dwconv_bwd(dyc, sv["p"], dww_full, name=f"dwconv_bwd_{tag}")
                small["cv_b_out"], small["cv_ln_g"], small["cv_ln_b"], small["cv_dw_b"] = dbout, dlng, dlnb, ddwb
                small["cv_dw_w"] = ddww
                small["cv_b_in"] = jnp.concatenate([dba, dbg], axis=1)
                big_grads[("cv_in", 0)] = _mm_tn(sv["h"], dp, d, el, True, name=f"cv_dwin_{tag}")
                dh = _mm_nt_blocks(dp, cv_in_blk, 0, name=f"cv_dh_{tag}")
            dx, dscale, dshift, dgn = _norm_mod_bwd(dh, sv["x"], dx, g_norm, scale, name=f"norm_mod_bwd_{tag}")
            dmod[i][s] = [dshift, dscale, dgate]
            d_norm_g[i][s] = dgn
    grad_x = dx[None]

    dmod_mine = jnp.concatenate([v for per_l in dmod for per_s in per_l for v in per_s], axis=1)
    dng_mine = jnp.concatenate([v for per_l in d_norm_g for v in per_l], axis=1)
    small_out = [dmod_mine, dng_mine, small["gm_ln_g"], small["gm_ln_b"], small["gm_ws"], small["gm_bs"],
                 small["cv_b_in"], small["cv_dw_w"], small["cv_dw_b"], small["cv_ln_g"], small["cv_ln_b"],
                 small["cv_b_out"], d_final_g]
    shapes2 = [s.shape for s in small_out]
    (pack2_all,) = _all_gather([_pack(small_out)[None, None]], name="ag_small_grads")
    summed = _unpack(_sum_devices(pack2_all[0], name="sum_small_grads"), shapes2)
    dmod_all = _unpack(pack2_all[0], shapes2)[0].reshape(N_DEV, depth, 9 * d)

    def my_cols(full, width):
        return lax.dynamic_slice_in_dim(full, me * width, width, axis=full.ndim - 1)

    g_ada_b = summed[0].reshape(depth, 9 * d)
    g_norm_g = my_cols(summed[1].reshape(depth, 3, d), dl)
    g_small = {
        "ada_b": g_ada_b, "norm_g": g_norm_g,
        "gm_ln_g": summed[2], "gm_ln_b": summed[3], "gm_ws": summed[4][None], "gm_bs": summed[5][None],
        "cv_b_in": my_cols(summed[6], el), "cv_dw_w": my_cols(summed[7], dl)[None],
        "cv_dw_b": my_cols(summed[8], dl), "cv_ln_g": my_cols(summed[9], dl), "cv_ln_b": my_cols(summed[10], dl),
        "cv_b_out": my_cols(summed[11], dl), "final_g": summed[12].reshape(d),
    }

    dm_loc = jnp.moveaxis(my_cols(dmod_all, cl), 0, 1)
    dm_loc = jnp.pad(dm_loc, ((0, 0), (0, 16 - N_DEV), (0, 0)))
    res_ada_w = _ada_bwd(c_pad, dm_loc, ada_w, m_ada_w, v_ada_w, name="ada_bwd_adamw")

    def flat2(a):
        return a.reshape(-1, a.shape[-1])

    res_big = {}
    for i in range(depth):
        grads, params, keys = [], [], []
        for s2 in range(2):
            widx = 2 * i + s2
            grads.append(big_grads[("w_in", widx)])
            params.append((ffn_w_in[i, s2], m_ffn_w_in[i, s2], v_ffn_w_in[i, s2]))
            keys.append(("w_in", widx))
            grads.append(big_grads[("w_out", widx)].reshape(N_DEV, fl, d))
            params.append((ffn_w_out[i, s2], m_ffn_w_out[i, s2], v_ffn_w_out[i, s2]))
            keys.append(("w_out", widx))
        if i % 2 == 0:
            grads += [big_grads[("gm_in", 0)], big_grads[("gm_out", 0)].reshape(N_DEV, dl, d)]
            params += [(gm_w_in[0], m_gm_w_in[0], v_gm_w_in[0]), (gm_w_out[0], m_gm_w_out[0], v_gm_w_out[0])]
            keys += [("gm_in", 0), ("gm_out", 0)]
        else:
            grads += [big_grads[("cv_in", 0)], big_grads[("cv_out", 0)].reshape(N_DEV, dl, d)]
            params += [(cv_w_in[0], m_cv_w_in[0], v_cv_w_in[0]), (cv_w_out[0], m_cv_w_out[0], v_cv_w_out[0])]
            keys += [("cv_in", 0), ("cv_out", 0)]
        for key, r in zip(keys, _reduce_scatter_update(grads, params, tag=f"l{i}")):
            res_big[key] = r

    def stack_ffn(name, k):
        return jnp.stack([jnp.stack([res_big[(name, 2 * i + s2)][k] for s2 in range(2)]) for i in range(depth)])

    small_params = {
        "ada_b": (ada_b, m_ada_b, v_ada_b), "norm_g": (norm_g, m_norm_g, v_norm_g),
        "gm_ln_g": (gm_ln_g, m_gm_ln_g, v_gm_ln_g), "gm_ln_b": (gm_ln_b, m_gm_ln_b, v_gm_ln_b),
        "gm_ws": (gm_ws, m_gm_ws, v_gm_ws), "gm_bs": (gm_bs, m_gm_bs, v_gm_bs),
        "cv_b_in": (cv_b_in, m_cv_b_in, v_cv_b_in), "cv_dw_w": (cv_dw_w, m_cv_dw_w, v_cv_dw_w),
        "cv_dw_b": (cv_dw_b, m_cv_dw_b, v_cv_dw_b), "cv_ln_g": (cv_ln_g, m_cv_ln_g, v_cv_ln_g),
        "cv_ln_b": (cv_ln_b, m_cv_ln_b, v_cv_ln_b), "cv_b_out": (cv_b_out, m_cv_b_out, v_cv_b_out),
        "final_g": (final_g, m_final_g, v_final_g),
    }
    res_small = {}
    for key, (w, m, v) in small_params.items():
        g2 = flat2(g_small[key].reshape(w.shape)) if w.ndim > 1 else g_small[key].reshape(1, -1)
        w2, m2, v2 = [flat2(a) if a.ndim > 1 else a.reshape(1, -1) for a in (w, m, v)]
        res_small[key] = [o.reshape(w.shape) for o in _adamw([(g2[None], 0)], w2, m2, v2, name=f"adamw_{key}")]

    def big(name, k):
        if name == "ada_w":
            return res_ada_w[k]
        if name == "ffn_w_in":
            return stack_ffn("w_in", k)
        if name == "ffn_w_out":
            return stack_ffn("w_out", k)
        key = {"gm_w_in": "gm_in", "gm_w_out": "gm_out", "cv_w_in": "cv_in", "cv_w_out": "cv_out"}[name]
        return res_big[(key, 0)][k][None]

    order = ["ada_w", "ada_b", "norm_g", "ffn_w_in", "ffn_w_out", "gm_w_in", "gm_ln_g", "gm_ln_b", "gm_ws", "gm_bs",
             "gm_w_out", "cv_w_in", "cv_b_in", "cv_dw_w", "cv_dw_b", "cv_ln_g", "cv_ln_b", "cv_w_out", "cv_b_out",
             "final_g"]
    outs = [loss, grad_x]
    for k in range(4):
        for name in order:
            outs.append(res_small[name][k] if name in res_small else big(name, k))
    return tuple(outs)
```

```python
import functools

import jax
import jax.numpy as jnp
from jax import lax
from jax.experimental import pallas as pl
from jax.experimental.pallas import tpu as pltpu

F32 = jnp.float32
BF = jnp.bfloat16
MESH = pl.DeviceIdType.MESH

N_DEV = 8
N_CHIP = 4
NORM_EPS = 1e-6
ADAM_LR = 0.001
ADAM_B1 = 0.9
ADAM_B2 = 0.999
ADAM_EPS = 1e-08
ADAM_WD = 0.01
ADAM_STEP = 10

V7X_SUBLANES = 8
V7X_LANES = 128
PACK_ALIGN = V7X_SUBLANES * V7X_LANES
V7X_VMEM_LIMIT = 56 * 1024 * 1024


def _tile(n, pref, align):
    if n <= pref:
        return n
    t = pref - pref % align
    while t >= align:
        if n % t == 0:
            return t
        t -= align
    return n


ANY = pl.BlockSpec(memory_space=pl.ANY)


class _Seq:
    last = None
    tokens = []


def _call(body, *, name, grid, in_specs, out_specs, out_shape, scratch=(), sem=None, prefetch=None, aliases=None):
    def run(*args):
        tokens, _Seq.tokens = _Seq.tokens, []
        lead = 0 if prefetch is None else 1
        n_in, n_tok = lead + len(args), len(tokens)

        def wrapped(*refs):
            body(*refs[:n_in], *refs[n_in + n_tok:])

        specs = list(in_specs) + [ANY] * n_tok
        params = pltpu.CompilerParams(dimension_semantics=sem, vmem_limit_bytes=V7X_VMEM_LIMIT)
        if prefetch is None:
            res = pl.pallas_call(wrapped, out_shape=out_shape, grid=grid, in_specs=specs, out_specs=out_specs,
                                 scratch_shapes=scratch, name=name, compiler_params=params,
                                 input_output_aliases=aliases or {})(*args, *tokens)
        else:
            grid_spec = pltpu.PrefetchScalarGridSpec(num_scalar_prefetch=1, grid=grid, in_specs=specs,
                                                     out_specs=out_specs, scratch_shapes=scratch)
            res = pl.pallas_call(wrapped, out_shape=out_shape, grid_spec=grid_spec, name=name,
                                 compiler_params=params, input_output_aliases=aliases or {})(prefetch, *args, *tokens)
        _Seq.last = res[0] if isinstance(res, (list, tuple)) else res
        return res
    return run


def _sds(shape, dtype):
    return jax.ShapeDtypeStruct(tuple(shape), dtype)


def _sigmoid(v):
    return 1.0 / (1.0 + jnp.exp(-v))


def _erf(v):
    a = jnp.abs(v)
    t = 1.0 / (1.0 + 0.3275911 * a)
    poly = t * (0.254829592 + t * (-0.284496736 + t * (1.421413741 + t * (-1.453152027 + t * 1.061405429))))
    r = 1.0 - poly * jnp.exp(-a * a)
    return jnp.where(v < 0, -r, r)


def _gelu(v):
    return 0.5 * v * (1.0 + _erf(v * 0.7071067811865476))


def _gelu_grad(v):
    cdf = 0.5 * (1.0 + _erf(v * 0.7071067811865476))
    pdf = 0.3989422804014327 * jnp.exp(-0.5 * v * v)
    return cdf + v * pdf


def _fold_rows(val):
    rows, w = val.shape
    return val.reshape(rows // V7X_SUBLANES, V7X_SUBLANES, w).sum(axis=0)


def _rowwise(fn, name, rows_in, vecs_in, rows_out, acc_widths, tm=256):
    t = rows_in[0].shape[0]
    tm = _tile(t, tm, V7X_SUBLANES)
    steps = t // tm
    nr, nv, no, na = len(rows_in), len(vecs_in), len(rows_out), len(acc_widths)

    def body(*refs):
        rin, vin = refs[:nr], refs[nr:nr + nv]
        rout = refs[nr + nv:nr + nv + no]
        aout = refs[nr + nv + no:nr + nv + no + na]
        accs = refs[nr + nv + no + na:]
        i = pl.program_id(0)
        outs, acc_vals = fn(*[r[...] for r in rin], *[v[...] for v in vin])
        for r, o in zip(rout, outs):
            r[...] = o.astype(r.dtype)
        if na:
            @pl.when(i == 0)
            def _():
                for a in accs:
                    a[...] = jnp.zeros_like(a)

            for a, val in zip(accs, acc_vals):
                a[...] += _fold_rows(val)

            @pl.when(i == steps - 1)
            def _():
                for o, a in zip(aout, accs):
                    o[...] = jnp.sum(a[...], axis=0, keepdims=True)

    in_specs = [pl.BlockSpec((tm, r.shape[1]), lambda i: (i, 0)) for r in rows_in]
    in_specs += [pl.BlockSpec(v.shape, functools.partial(lambda nd, i: (0,) * nd, v.ndim)) for v in vecs_in]
    out_specs = [pl.BlockSpec((tm, r.shape[1]), lambda i: (i, 0)) for r in rows_out]
    out_specs += [pl.BlockSpec((1, w), lambda i: (0, 0)) for w in acc_widths]
    out_shape = list(rows_out) + [_sds((1, w), F32) for w in acc_widths]
    scratch = [pltpu.VMEM((V7X_SUBLANES, w), F32) for w in acc_widths]
    res = _call(body, name=name, grid=(steps,), in_specs=in_specs, out_specs=out_specs, out_shape=out_shape,
                scratch=scratch, sem=("arbitrary",) if na else ("parallel",))(*rows_in, *vecs_in)
    return res[:no], res[no:]


def _norm_mod(x, g, scale, shift, name):
    def fn(xv, gv, sc, sh):
        r = lax.rsqrt(jnp.mean(xv * xv, axis=-1, keepdims=True) + NORM_EPS)
        return ((xv * r * gv) * (1.0 + sc) + sh,), ()
    (h,), _ = _rowwise(fn, name, [x], [g, scale, shift], [_sds(x.shape, BF)], [])
    return h


def _residual_bwd(dxp, y, gate, coef, with_colsum, name):
    def fn(dv, yv, gt):
        dy = (coef * gt) * dv
        accs = (coef * dv * yv.astype(F32),)
        if with_colsum:
            accs += (dy,)
        return (dy,), accs
    d = dxp.shape[1]
    (dy,), accs = _rowwise(fn, name, [dxp, y], [gate], [_sds(dxp.shape, BF)], [d, d] if with_colsum else [d])
    return dy, accs


def _norm_mod_bwd(dh, x, dxp, g, scale, name):
    def fn(dhv, xv, dpv, gv, sc):
        r = lax.rsqrt(jnp.mean(xv * xv, axis=-1, keepdims=True) + NORM_EPS)
        xhat = xv * r
        dn = dhv * (1.0 + sc)
        dxhat = dn * gv
        dx = r * (dxhat - xhat * jnp.mean(dxhat * xhat, axis=-1, keepdims=True)) + dpv
        return (dx,), (dhv * (xhat * gv), dhv, dn * xhat)
    d = x.shape[1]
    (dx,), (dscale, dshift, dg) = _rowwise(fn, name, [dh, x, dxp], [g, scale], [_sds(x.shape, F32)], [d, d, d])
    return dx, dscale, dshift, dg


def _final_loss(x, target, g, name):
    d = x.shape[1]

    def fn(xv, tv, gv):
        r = lax.rsqrt(jnp.mean(xv * xv, axis=-1, keepdims=True) + NORM_EPS)
        xhat = xv * r
        err = xhat * gv - tv
        dy = err * (1.0 / d)
        dxhat = dy * gv
        dx = r * (dxhat - xhat * jnp.mean(dxhat * xhat, axis=-1, keepdims=True))
        return (dx,), (err * err, dy * xhat)
    (dx,), (sq, dg) = _rowwise(fn, name, [x, target], [g], [_sds(x.shape, F32)], [d, d])
    return sq, dx, dg


def _gm_act(pre, ln_g, ln_b, name):
    e = pre.shape[1] // 2

    def fn(pv, gv, bv):
        p = pv.astype(F32)
        u = _gelu(p[:, :e])
        v = _gelu(p[:, e:])
        mu = jnp.mean(v, axis=-1, keepdims=True)
        vc = v - mu
        rstd = lax.rsqrt(jnp.mean(vc * vc, axis=-1, keepdims=True) + NORM_EPS)
        return (u, vc * rstd * gv + bv), ()
    t = pre.shape[0]
    (u, vn), _ = _rowwise(fn, name, [pre], [ln_g, ln_b], [_sds((t, e), BF), _sds((t, e), BF)], [])
    return u, vn


def _gm_act_bwd(pre, du, dvn, ln_g, name):
    e = pre.shape[1] // 2

    def fn(pv, duv, dvv, gv):
        p = pv.astype(F32)
        pu, pvv = p[:, :e], p[:, e:]
        v = _gelu(pvv)
        mu = jnp.mean(v, axis=-1, keepdims=True)
        vc = v - mu
        rstd = lax.rsqrt(jnp.mean(vc * vc, axis=-1, keepdims=True) + NORM_EPS)
        vhat = vc * rstd
        dvn_f = dvv.astype(F32)
        dvhat = dvn_f * gv
        dv = rstd * (dvhat - jnp.mean(dvhat, axis=-1, keepdims=True)
                     - vhat * jnp.mean(dvhat * vhat, axis=-1, keepdims=True))
        dpu = duv.astype(F32) * _gelu_grad(pu)
        dpv = dv * _gelu_grad(pvv)
        return (jnp.concatenate([dpu, dpv], axis=1),), (dvn_f * vhat, dvn_f)
    (dpre,), (dg, db) = _rowwise(fn, name, [pre, du, dvn], [ln_g], [_sds(pre.shape, BF)], [e, e], tm=128)
    return dpre, dg, db


def _cv_act(yc, ln_g, ln_b, name):
    def fn(yv, gv, bv):
        mu = jnp.mean(yv, axis=-1, keepdims=True)
        c = yv - mu
        rstd = lax.rsqrt(jnp.mean(c * c, axis=-1, keepdims=True) + NORM_EPS)
        yn = c * rstd * gv + bv
        return (yn * _sigmoid(yn),), ()
    (ys,), _ = _rowwise(fn, name, [yc], [ln_g, ln_b], [_sds(yc.shape, BF)], [])
    return ys


def _cv_act_bwd(dys, yc, ln_g, ln_b, name):
    def fn(dv, yv, gv, bv):
        mu = jnp.mean(yv, axis=-1, keepdims=True)
        c = yv - mu
        rstd = lax.rsqrt(jnp.mean(c * c, axis=-1, keepdims=True) + NORM_EPS)
        yhat = c * rstd
        yn = yhat * gv + bv
        sig = _sigmoid(yn)
        dyn = dv.astype(F32) * (sig * (1.0 + yn * (1.0 - sig)))
        dyhat = dyn * gv
        dyc = rstd * (dyhat - jnp.mean(dyhat, axis=-1, keepdims=True)
                      - yhat * jnp.mean(dyhat * yhat, axis=-1, keepdims=True))
        return (dyc,), (dyn * yhat, dyn, dyc)
    cw = yc.shape[1]
    (dyc,), (dg, db, dbias) = _rowwise(fn, name, [dys, yc], [ln_g, ln_b], [_sds(yc.shape, F32)], [cw, cw, cw])
    return dyc, dg, db, dbias


def _ffn_in(h, w_blk, blk0, name):
    t, d = h.shape
    bn = w_blk.shape[2]
    half = N_DEV // 2
    f = half * bn
    tm = _tile(t, 512, V7X_SUBLANES)

    def body(h_ref, wg_ref, wu_ref, g_ref, u_ref, a_ref):
        hv = h_ref[...]
        g = jnp.dot(hv, wg_ref[...], preferred_element_type=F32)
        u = jnp.dot(hv, wu_ref[...], preferred_element_type=F32)
        g_ref[...] = g.astype(BF)
        u_ref[...] = u.astype(BF)
        a_ref[...] = (g * _sigmoid(g) * u).astype(BF)

    out = _sds((t, f), BF)
    tile = pl.BlockSpec((tm, bn), lambda j, i: (i, j))
    return _call(
        body, name=name, grid=(half, t // tm),
        in_specs=[pl.BlockSpec((tm, d), lambda j, i: (i, 0)),
                  pl.BlockSpec((None, d, bn), lambda j, i: (blk0 + j, 0, 0)),
                  pl.BlockSpec((None, d, bn), lambda j, i: (blk0 + half + j, 0, 0))],
        out_specs=[tile, tile, tile], out_shape=[out, out, out], sem=("parallel", "parallel"))(h, w_blk, w_blk)


def _in_proj(h, w_blk, bias, name):
    t, d = h.shape
    bn = w_blk.shape[2]
    tm = _tile(t, 1024, V7X_SUBLANES)

    def body(*refs):
        if bias is None:
            h_ref, w_ref, o_ref = refs
            o_ref[...] = jnp.dot(h_ref[...], w_ref[...], preferred_element_type=F32).astype(BF)
        else:
            h_ref, w_ref, b_ref, o_ref = refs
            o_ref[...] = (jnp.dot(h_ref[...], w_ref[...], preferred_element_type=F32) + b_ref[...]).astype(BF)

    in_specs = [pl.BlockSpec((tm, d), lambda j, i: (i, 0)), pl.BlockSpec((None, d, bn), lambda j, i: (j, 0, 0))]
    args = [h, w_blk]
    if bias is not None:
        in_specs.append(pl.BlockSpec((1, bn), lambda j, i: (0, j)))
        args.append(bias)
    return _call(body, name=name, grid=(N_DEV, t // tm), in_specs=in_specs,
                 out_specs=pl.BlockSpec((tm, bn), lambda j, i: (i, j)), out_shape=_sds((t, N_DEV * bn), BF),
                 sem=("parallel", "parallel"))(*args)


def _out_proj(a, w3, widx, x, gate, bias, coef, name):
    t, k = a.shape
    d = w3.shape[2]
    tm = _tile(t, 512, V7X_SUBLANES)
    tn = _tile(d, 512, V7X_LANES)

    def body(*refs):
        if bias is None:
            a_ref, w_ref, x_ref, g_ref, xo_ref, y_ref = refs
            y = jnp.dot(a_ref[...], w_ref[...], preferred_element_type=F32)
        else:
            a_ref, w_ref, x_ref, g_ref, b_ref, xo_ref, y_ref = refs
            y = jnp.dot(a_ref[...], w_ref[...], preferred_element_type=F32) + b_ref[...]
        y_ref[...] = y.astype(BF)
        xo_ref[...] = x_ref[...] + (coef * g_ref[...]) * y

    tile = pl.BlockSpec((tm, tn), lambda j, i: (i, j))
    vec = pl.BlockSpec((1, tn), lambda j, i: (0, j))
    in_specs = [pl.BlockSpec((tm, k), lambda j, i: (i, 0)),
                pl.BlockSpec((None, k, tn), lambda j, i: (widx, 0, j)), tile, vec]
    args = [a, w3, x, gate]
    if bias is not None:
        in_specs.append(vec)
        args.append(bias)
    return _call(body, name=name, grid=(d // tn, t // tm), in_specs=in_specs, out_specs=[tile, tile],
                 out_shape=[_sds((t, d), F32), _sds((t, d), BF)], sem=("parallel", "parallel"))(*args)


def _ffn_da(dy, w3, widx, g, u, name):
    t, d = dy.shape
    f = w3.shape[1]
    bn = f // (N_DEV // 2)
    tm = _tile(t, 512, V7X_SUBLANES)

    def body(dy_ref, w_ref, g_ref, u_ref, dgu_ref, a_ref):
        da = lax.dot_general(dy_ref[...], w_ref[...], (((1,), (1,)), ((), ())), preferred_element_type=F32)
        gv = g_ref[...].astype(F32)
        uv = u_ref[...].astype(F32)
        sig = _sigmoid(gv)
        sl = gv * sig
        dgu_ref[0] = (da * uv * (sig * (1.0 + gv * (1.0 - sig)))).astype(BF)
        dgu_ref[1] = (da * sl).astype(BF)
        a_ref[...] = (sl * uv).astype(BF)

    tile = pl.BlockSpec((tm, bn), lambda j, i: (i, j))
    return _call(
        body, name=name, grid=(f // bn, t // tm),
        in_specs=[pl.BlockSpec((tm, d), lambda j, i: (i, 0)),
                  pl.BlockSpec((None, bn, d), lambda j, i: (widx, j, 0)), tile, tile],
        out_specs=[pl.BlockSpec((2, tm, bn), lambda j, i: (0, i, j)), tile],
        out_shape=[_sds((2, t, f), BF), _sds((t, f), BF)], sem=("parallel", "parallel"))(dy, w3, g, u)


def _mm_nt(dy, w3, widx, name):
    t, k = dy.shape
    n = w3.shape[1]
    tm = _tile(t, 512, V7X_SUBLANES)
    tn = _tile(n, 1024, V7X_LANES)

    def body(dy_ref, w_ref, o_ref):
        o_ref[...] = lax.dot_general(dy_ref[...], w_ref[...], (((1,), (1,)), ((), ())),
                                     preferred_element_type=F32).astype(BF)

    return _call(body, name=name, grid=(n // tn, t // tm),
                 in_specs=[pl.BlockSpec((tm, k), lambda j, i: (i, 0)),
                           pl.BlockSpec((None, tn, k), lambda j, i: (widx, j, 0))],
                 out_specs=pl.BlockSpec((tm, tn), lambda j, i: (i, j)), out_shape=_sds((t, n), BF),
                 sem=("parallel", "parallel"))(dy, w3)


def _mm_nt_blocks(z3, w_blk, blk0, name):
    lead, t, _ = z3.shape
    d, bn = w_blk.shape[1], w_blk.shape[2]
    per = N_DEV // lead
    tm = _tile(t, 512, V7X_SUBLANES)

    def body(z_ref, w_ref, o_ref, acc_ref):
        k = pl.program_id(1)

        @pl.when(k == 0)
        def _():
            acc_ref[...] = jnp.zeros_like(acc_ref)

        acc_ref[...] += lax.dot_general(z_ref[...], w_ref[...], (((1,), (1,)), ((), ())),
                                        preferred_element_type=F32)

        @pl.when(k == N_DEV - 1)
        def _():
            o_ref[...] = acc_ref[...]

    return _call(body, name=name, grid=(t // tm, N_DEV),
                 in_specs=[pl.BlockSpec((None, tm, bn), lambda i, k: (k // per, i, k % per)),
                           pl.BlockSpec((None, d, bn), lambda i, k: (blk0 + k, 0, 0))],
                 out_specs=pl.BlockSpec((tm, d), lambda i, k: (i, 0)), out_shape=_sds((t, d), F32),
                 scratch=[pltpu.VMEM((tm, d), F32)], sem=("parallel", "arbitrary"))(z3, w_blk)


def _mm_tn(a, b3, ta, tb, blocked, name):
    t, ka = a.shape
    lead, _, w = b3.shape
    per = w // tb
    nj = lead * per
    tk = _tile(t, 512, V7X_SUBLANES)
    nk = t // tk

    def body(a_ref, b_ref, o_ref, acc_ref):
        k = pl.program_id(2)

        @pl.when(k == 0)
        def _():
            acc_ref[...] = jnp.zeros_like(acc_ref)

        acc_ref[...] += lax.dot_general(a_ref[...], b_ref[...], (((0,), (0,)), ((), ())),
                                        preferred_element_type=F32)

        @pl.when(k == nk - 1)
        def _():
            o_ref[...] = acc_ref[...].astype(BF)

    if blocked:
        out_shape = _sds((nj, ka, tb), BF)
        out_spec = pl.BlockSpec((None, ta, tb), lambda i, j, k: (j, i, 0))
    else:
        out_shape = _sds((1, ka, w), BF)
        out_spec = pl.BlockSpec((None, ta, tb), lambda i, j, k: (0, i, j))
    return _call(body, name=name, grid=(ka // ta, nj, nk),
                 in_specs=[pl.BlockSpec((tk, ta), lambda i, j, k: (k, i)),
                           pl.BlockSpec((None, tk, tb), lambda i, j, k: (j // per, k, j % per))],
                 out_specs=out_spec, out_shape=out_shape, scratch=[pltpu.VMEM((ta, tb), F32)],
                 sem=("parallel", "parallel", "arbitrary"))(a, b3)


def _causal(ws):
    l = ws.shape[0]
    row = lax.broadcasted_iota(jnp.int32, (l, l), 0)
    col = lax.broadcasted_iota(jnp.int32, (l, l), 1)
    return jnp.where(col <= row, ws, 0.0)


def _sgu_fwd(u, vn, ws, bsb, name):
    t, e = u.shape
    hn, l, _ = ws.shape
    dh = e // hn
    nc = t // l

    def body(u_ref, v_ref, ws_ref, bs_ref, s_ref):
        wsc = _causal(ws_ref[...]).astype(BF)
        bias = bs_ref[...]

        def chunk(c, carry):
            rows = pl.ds(pl.multiple_of(c * l, l), l)
            vo = jnp.dot(wsc, v_ref[rows, :], preferred_element_type=F32) + bias
            s_ref[rows, :] = (u_ref[rows, :].astype(F32) * vo).astype(BF)
            return carry
        lax.fori_loop(0, nc, chunk, 0)

    col = pl.BlockSpec((t, dh), lambda h: (0, h))
    return _call(body, name=name, grid=(hn,),
                 in_specs=[col, col, pl.BlockSpec((None, l, l), lambda h: (h, 0, 0)),
                           pl.BlockSpec((None, l, dh), lambda h: (h, 0, 0))],
                 out_specs=col, out_shape=_sds((t, e), BF), sem=("parallel",))(u, vn, ws, bsb)


def _sgu_bwd(ds, u, vn, ws, bsb, name):
    t, e = u.shape
    hn, l, _ = ws.shape
    dh = e // hn
    nc = t // l

    def body(ds_ref, u_ref, v_ref, ws_ref, bs_ref, du_ref, dv_ref, dws_ref, dbs_ref, accw_ref, accb_ref):
        wsc = _causal(ws_ref[...]).astype(BF)
        bias = bs_ref[...]
        accw_ref[...] = jnp.zeros_like(accw_ref)
        accb_ref[...] = jnp.zeros_like(accb_ref)

        def chunk(c, carry):
            rows = pl.ds(pl.multiple_of(c * l, l), l)
            vc = v_ref[rows, :]
            dsv = ds_ref[rows, :].astype(F32)
            vo = jnp.dot(wsc, vc, preferred_element_type=F32) + bias
            du_ref[rows, :] = (dsv * vo).astype(BF)
            dvo = dsv * u_ref[rows, :].astype(F32)
            dvo_b = dvo.astype(BF)
            accb_ref[...] += dvo
            accw_ref[...] += lax.dot_general(dvo_b, vc, (((1,), (1,)), ((), ())), preferred_element_type=F32)
            dv_ref[rows, :] = lax.dot_general(wsc, dvo_b, (((0,), (0,)), ((), ())),
                                              preferred_element_type=F32).astype(BF)
            return carry
        lax.fori_loop(0, nc, chunk, 0)
        dws_ref[...] = _causal(accw_ref[...])
        dbs_ref[...] = jnp.broadcast_to(jnp.sum(accb_ref[...], axis=1, keepdims=True), (l, dh))

    col = pl.BlockSpec((t, dh), lambda h: (0, h))
    return _call(body, name=name, grid=(hn,),
                 in_specs=[col, col, col, pl.BlockSpec((None, l, l), lambda h: (h, 0, 0)),
                           pl.BlockSpec((None, l, dh), lambda h: (h, 0, 0))],
                 out_specs=[col, col, pl.BlockSpec((None, l, l), lambda h: (h, 0, 0)),
                            pl.BlockSpec((None, l, dh), lambda h: (h, 0, 0))],
                 out_shape=[_sds((t, e), BF), _sds((t, e), BF), _sds((hn, l, l), F32), _sds((hn, l, dh), F32)],
                 scratch=[pltpu.VMEM((l, l), F32), pltpu.VMEM((l, dh), F32)],
                 sem=("parallel",))(ds, u, vn, ws, bsb)


CONV_HALO = 32


def _dwconv_fwd(p, dw_w, dw_b, name):
    t, c2 = p.shape
    cw = c2 // 2
    kw = dw_w.shape[0]
    cb = _tile(cw, 512, V7X_LANES)
    ncb = cw // cb
    tm = _tile(t, 512, CONV_HALO)
    off = CONV_HALO - (kw - 1)

    def body(a_ref, g_ref, ap_ref, gp_ref, w_ref, b_ref, o_ref, win_ref):
        i = pl.program_id(1)
        prev = ap_ref[...].astype(F32) * _sigmoid(gp_ref[...].astype(F32))
        win_ref[0:CONV_HALO, :] = jnp.where(i > 0, prev, 0.0)
        win_ref[CONV_HALO:, :] = a_ref[...].astype(F32) * _sigmoid(g_ref[...].astype(F32))
        acc = jnp.zeros((tm, cb), F32) + b_ref[...]
        for k in range(kw):
            acc = acc + w_ref[k:k + 1, :] * win_ref[off + k:off + k + tm, :]
        o_ref[...] = acc

    hpt = tm // CONV_HALO
    cur_a = pl.BlockSpec((tm, cb), lambda j, i: (i, j))
    cur_g = pl.BlockSpec((tm, cb), lambda j, i: (i, ncb + j))
    prev_a = pl.BlockSpec((CONV_HALO, cb), lambda j, i: (jnp.maximum(i * hpt - 1, 0), j))
    prev_g = pl.BlockSpec((CONV_HALO, cb), lambda j, i: (jnp.maximum(i * hpt - 1, 0), ncb + j))
    return _call(body, name=name, grid=(ncb, t // tm),
                 in_specs=[cur_a, cur_g, prev_a, prev_g, pl.BlockSpec((kw, cb), lambda j, i: (0, j)),
                           pl.BlockSpec((1, cb), lambda j, i: (0, j))],
                 out_specs=pl.BlockSpec((tm, cb), lambda j, i: (i, j)), out_shape=_sds((t, cw), F32),
                 scratch=[pltpu.VMEM((tm + CONV_HALO, cb), F32)],
                 sem=("parallel", "parallel"))(p, p, p, p, dw_w, dw_b)


def _dwconv_bwd(dyc, p, dw_w, name):
    t, c2 = p.shape
    cw = c2 // 2
    kw = dw_w.shape[0]
    cb = _tile(cw, 512, V7X_LANES)
    ncb = cw // cb
    tm = _tile(t, 512, CONV_HALO)
    nt = t // tm
    off = CONV_HALO - (kw - 1)
    kpad = -(-kw // V7X_SUBLANES) * V7X_SUBLANES

    def body(d_ref, dn_ref, a_ref, g_ref, ap_ref, gp_ref, w_ref,
             dp_ref, dw_ref, dba_ref, dbg_ref, dwin_ref, ywin_ref, accw_ref, acca_ref, accg_ref):
        i = pl.program_id(1)

        @pl.when(i == 0)
        def _():
            accw_ref[...] = jnp.zeros_like(accw_ref)
            acca_ref[...] = jnp.zeros_like(acca_ref)
            accg_ref[...] = jnp.zeros_like(accg_ref)

        av = a_ref[...].astype(F32)
        sig = _sigmoid(g_ref[...].astype(F32))
        prev = ap_ref[...].astype(F32) * _sigmoid(gp_ref[...].astype(F32))
        ywin_ref[0:CONV_HALO, :] = jnp.where(i > 0, prev, 0.0)
        ywin_ref[CONV_HALO:, :] = av * sig
        dcur = d_ref[...]
        dwin_ref[0:tm, :] = dcur
        dwin_ref[tm:, :] = jnp.where(i < nt - 1, dn_ref[...], 0.0)

        dyg = jnp.zeros((tm, cb), F32)
        for k in range(kw):
            dyg = dyg + w_ref[k:k + 1, :] * dwin_ref[kw - 1 - k:kw - 1 - k + tm, :]
            accw_ref[k] += _fold_rows(dcur * ywin_ref[off + k:off + k + tm, :])
        da = dyg * sig
        dg = dyg * av * sig * (1.0 - sig)
        dp_ref[0] = da.astype(BF)
        dp_ref[1] = dg.astype(BF)
        acca_ref[...] += _fold_rows(da)
        accg_ref[...] += _fold_rows(dg)

        @pl.when(i == nt - 1)
        def _():
            dw_ref[...] = jnp.sum(accw_ref[...], axis=1)
            dba_ref[...] = jnp.sum(acca_ref[...], axis=0, keepdims=True)
            dbg_ref[...] = jnp.sum(accg_ref[...], axis=0, keepdims=True)

    hpt = tm // CONV_HALO
    last_halo = t // CONV_HALO - 1
    tile = pl.BlockSpec((tm, cb), lambda j, i: (i, j))
    cur_g = pl.BlockSpec((tm, cb), lambda j, i: (i, ncb + j))
    nxt = pl.BlockSpec((CONV_HALO, cb), lambda j, i: (jnp.minimum((i + 1) * hpt, last_halo), j))
    prev_a = pl.BlockSpec((CONV_HALO, cb), lambda j, i: (jnp.maximum(i * hpt - 1, 0), j))
    prev_g = pl.BlockSpec((CONV_HALO, cb), lambda j, i: (jnp.maximum(i * hpt - 1, 0), ncb + j))
    vec = pl.BlockSpec((1, cb), lambda j, i: (0, j))
    dp, ddw, dba, dbg = _call(
        body, name=name, grid=(ncb, nt),
        in_specs=[tile, nxt, tile, cur_g, prev_a, prev_g, pl.BlockSpec((kw, cb), lambda j, i: (0, j))],
        out_specs=[pl.BlockSpec((2, tm, cb), lambda j, i: (0, i, j)), pl.BlockSpec((kpad, cb), lambda j, i: (0, j)),
                   vec, vec],
        out_shape=[_sds((2, t, cw), BF), _sds((kpad, cw), F32), _sds((1, cw), F32), _sds((1, cw), F32)],
        scratch=[pltpu.VMEM((tm + CONV_HALO, cb), F32), pltpu.VMEM((tm + CONV_HALO, cb), F32),
                 pltpu.VMEM((kpad, V7X_SUBLANES, cb), F32), pltpu.VMEM((V7X_SUBLANES, cb), F32),
                 pltpu.VMEM((V7X_SUBLANES, cb), F32)],
        sem=("parallel", "arbitrary"))(dyc, dyc, p, p, p, p, dw_w)
    return dp, ddw[:kw], dba, dbg


def _adam_math(g, w, m, v):
    m2 = ADAM_B1 * m + (1.0 - ADAM_B1) * g
    v2 = ADAM_B2 * v + (1.0 - ADAM_B2) * (g * g)
    m_hat = m2 / (1.0 - ADAM_B1 ** ADAM_STEP)
    v_hat = v2 / (1.0 - ADAM_B2 ** ADAM_STEP)
    delta = -ADAM_LR * (m_hat / (jnp.sqrt(v_hat) + ADAM_EPS) + ADAM_WD * w)
    return delta, m2, v2


def _adamw(g_parts, w, m, v, name):
    r, c = w.shape
    tr = _tile(r, 256, V7X_SUBLANES)
    ng = len(g_parts)

    def body(*refs):
        g = refs[0][...].astype(F32)
        for s in refs[1:ng]:
            g = g + s[...].astype(F32)
        w_ref, m_ref, v_ref, go_ref, d_ref, mo_ref, vo_ref = refs[ng:]
        delta, m2, v2 = _adam_math(g, w_ref[...], m_ref[...], v_ref[...])
        go_ref[...] = g
        d_ref[...] = delta
        mo_ref[...] = m2
        vo_ref[...] = v2

    tile = pl.BlockSpec((tr, c), lambda i: (i, 0))
    in_specs = [pl.BlockSpec((None, tr, c), functools.partial(lambda s, i: (s, i, 0), s)) for _, s in g_parts]
    out = _sds((r, c), F32)
    return _call(body, name=name, grid=(r // tr,), in_specs=in_specs + [tile, tile, tile],
                 out_specs=[tile] * 4, out_shape=[out] * 4, sem=("parallel",))(*[a for a, _ in g_parts], w, m, v)


def _adamw_stacked(h, recv, chip, w_st, m_st, v_st, k, prev, name):
    kk, r, c = w_st.shape
    tr = _tile(r, 256, V7X_SUBLANES)
    if prev is None:
        prev = [lax.empty((kk, r, c), F32) for _ in range(4)]

    def body(chip_ref, h_ref, r0_ref, r1_ref, r2_ref, w_ref, m_ref, v_ref, pg, pd, pm, pv,
             go_ref, d_ref, mo_ref, vo_ref):
        g = (h_ref[...].astype(F32) + r0_ref[...].astype(F32)) + (r1_ref[...].astype(F32) + r2_ref[...].astype(F32))
        delta, m2, v2 = _adam_math(g, w_ref[...], m_ref[...], v_ref[...])
        go_ref[...] = g
        d_ref[...] = delta
        mo_ref[...] = m2
        vo_ref[...] = v2

    own = pl.BlockSpec((None, tr, c), lambda i, chip_ref: (chip_ref[0], i, 0))
    rcv = [pl.BlockSpec((None, tr, c), functools.partial(lambda s, i, chip_ref: (s, i, 0), s)) for s in range(3)]
    blk = pl.BlockSpec((None, tr, c), lambda i, chip_ref: (k, i, 0))
    out = _sds((kk, r, c), F32)
    return _call(body, name=name, grid=(r // tr,), in_specs=[own] + rcv + [blk, blk, blk] + [ANY] * 4,
                 out_specs=[blk] * 4, out_shape=[out] * 4, sem=("parallel",), prefetch=chip,
                 aliases={8: 0, 9: 1, 10: 2, 11: 3})(h, recv, recv, recv, w_st, m_st, v_st, *prev)


def _add_sibling(g4, land, core, name):
    n, _, r, c = g4.shape
    tr = _tile(r, 512, V7X_SUBLANES)

    def body(core_ref, a_ref, b_ref, o_ref):
        o_ref[...] = (a_ref[...].astype(F32) + b_ref[...].astype(F32)).astype(BF)

    return _call(body, name=name, grid=(n, r // tr),
                 in_specs=[pl.BlockSpec((None, None, tr, c), lambda p, i, core_ref: (p, core_ref[0], i, 0)),
                           pl.BlockSpec((None, None, tr, c), lambda p, i, core_ref: (p, 0, i, 0))],
                 out_specs=pl.BlockSpec((None, tr, c), lambda p, i, core_ref: (p, i, 0)),
                 out_shape=_sds((n, r, c), BF), sem=("parallel", "parallel"), prefetch=core)(g4, land)


def _cast_to_slot(w, lead, me, name):
    r, c = w.shape[-2:]
    nl = len(lead)
    tr = _tile(r, 512, 2 * V7X_SUBLANES)

    def body(me_ref, w_ref, o_ref):
        o_ref[...] = w_ref[...].astype(BF)

    return _call(body, name=name, grid=(r // tr,),
                 in_specs=[pl.BlockSpec((None,) * nl + (tr, c), lambda i, me_ref: tuple(lead) + (i, 0))],
                 out_specs=pl.BlockSpec((None, None, tr, c), lambda i, me_ref: (0, me_ref[0], i, 0)),
                 out_shape=_sds((1, N_DEV, r, c), BF), sem=("parallel",), prefetch=me)(w)


def _ada_fwd(c_pad, ada_w, ada_b, name):
    nl, d, cl = ada_w.shape
    rows = c_pad.shape[0]
    tn = _tile(cl, 256, V7X_LANES)

    def body(c_ref, w_ref, b_ref, o_ref):
        cv = c_ref[...]
        cond = (cv * _sigmoid(cv)).astype(BF)
        o_ref[...] = jnp.dot(cond, w_ref[...].astype(BF), preferred_element_type=F32) + b_ref[...]

    return _call(body, name=name, grid=(nl, cl // tn),
                 in_specs=[pl.BlockSpec((rows, d), lambda l, j: (0, 0)),
                           pl.BlockSpec((None, d, tn), lambda l, j: (l, 0, j)),
                           pl.BlockSpec((None, 1, tn), lambda l, j: (l, 0, j))],
                 out_specs=pl.BlockSpec((None, rows, tn), lambda l, j: (l, 0, j)),
                 out_shape=_sds((nl, rows, cl), F32), sem=("parallel", "parallel"))(c_pad, ada_w, ada_b)


def _ada_bwd(c_pad, dmod, w, m, v, name):
    nl, d, cl = w.shape
    rows = c_pad.shape[0]
    tn = _tile(cl, 256, V7X_LANES)

    def body(c_ref, dm_ref, w_ref, m_ref, v_ref, go_ref, d_ref, mo_ref, vo_ref):
        cv = c_ref[...]
        cond = (cv * _sigmoid(cv)).astype(BF)
        g = lax.dot_general(cond, dm_ref[...].astype(BF), (((0,), (0,)), ((), ())), preferred_element_type=F32)
        delta, m2, v2 = _adam_math(g, w_ref[...], m_ref[...], v_ref[...])
        go_ref[...] = g
        d_ref[...] = delta
        mo_ref[...] = m2
        vo_ref[...] = v2

    tile = pl.BlockSpec((None, d, tn), lambda l, j: (l, 0, j))
    out = _sds((nl, d, cl), F32)
    return _call(body, name=name, grid=(nl, cl // tn),
                 in_specs=[pl.BlockSpec((rows, d), lambda l, j: (0, 0)),
                           pl.BlockSpec((None, rows, tn), lambda l, j: (l, 0, j)), tile, tile, tile],
                 out_specs=[tile] * 4, out_shape=[out] * 4, sem=("parallel", "parallel"))(c_pad, dmod, w, m, v)


def _sum_devices(parts, name):
    n, r, c = parts.shape
    tr = _tile(r, 512, V7X_SUBLANES)

    def body(p_ref, o_ref):
        acc = p_ref[0]
        for k in range(1, n):
            acc = acc + p_ref[k]
        o_ref[...] = acc

    return _call(body, name=name, grid=(r // tr,), in_specs=[pl.BlockSpec((n, tr, c), lambda i: (0, i, 0))],
                 out_specs=pl.BlockSpec((tr, c), lambda i: (i, 0)), out_shape=_sds((r, c), F32),
                 sem=("parallel",))(parts)


def _mesh_pos():
    return lax.axis_index("x"), lax.axis_index("y"), lax.axis_index("c")


def _other_chips(x, y):
    return [(1 - x, y), (x, 1 - y), (1 - x, 1 - y)]


def _all_gather(arrs, name):
    n = len(arrs)

    def body(*refs):
        ins, outs = refs[:n], refs[n:2 * n]
        send_sems, recv_sems, local_sems = refs[2 * n:]
        x, y, c = _mesh_pos()
        me, sibling = (x, y, c), (x, y, 1 - c)
        chips = _other_chips(x, y)

        def slot(a, pos):
            px, py, pc = pos
            return outs[a].at[:, pl.ds(4 * px + 2 * py + pc, 1)]

        def copy(a, k, block, to, src=None):
            return pltpu.make_async_remote_copy(
                src_ref=slot(a, block) if src is None else src, dst_ref=slot(a, block),
                send_sem=send_sems.at[a, k], recv_sem=recv_sems.at[a, k], device_id=to, device_id_type=MESH)

        mine = [pltpu.make_async_copy(ins[a], slot(a, me), local_sems.at[a]) for a in range(n)]
        for cp in mine:
            cp.start()
        first = []
        for a in range(n):
            first.append(copy(a, 0, me, sibling, src=ins[a]))
            first += [copy(a, 1 + j, me, (*chip, c), src=ins[a]) for j, chip in enumerate(chips)]
        for cp in first:
            cp.start()
        passed = []
        for a in range(n):
            for j, chip in enumerate(chips):
                copy(a, 1 + j, (*chip, c), me).wait_recv()
                fwd = copy(a, 4 + j, (*chip, c), sibling)
                fwd.start()
                passed.append(fwd)
        for a in range(n):
            copy(a, 0, sibling, me).wait_recv()
            for j, chip in enumerate(chips):
                copy(a, 4 + j, (*chip, 1 - c), me).wait_recv()
        for cp in first + passed:
            cp.wait_send()
        for cp in mine:
            cp.wait()

    out_shape = [_sds((a.shape[0], N_DEV) + a.shape[2:], a.dtype) for a in arrs]
    return pl.pallas_call(
        body, out_shape=out_shape, in_specs=[ANY] * n, out_specs=[ANY] * n, name=name,
        scratch_shapes=[pltpu.SemaphoreType.DMA((n, N_DEV - 1)), pltpu.SemaphoreType.DMA((n, N_DEV - 1)),
                        pltpu.SemaphoreType.DMA((n,))])(*arrs)


HBM = pl.BlockSpec(memory_space=pltpu.HBM)
SEM = pl.BlockSpec(memory_space=pltpu.SEMAPHORE)


def _hbm(v):
    return pltpu.with_memory_space_constraint(v, pltpu.HBM)


def _comm_call(body, name, bufs, sems_in, sems_out):
    after = [] if _Seq.last is None or any(_Seq.last is b for b in bufs) else [_Seq.last]
    nb, ni, na, no = len(bufs), len(sems_in), len(after), len(sems_out)

    def wrapped(*refs):
        body(refs[:nb], refs[nb:nb + ni], refs[nb + ni + na:nb + ni + na + no])
        if no:
            refs[-1][...] = jnp.zeros_like(refs[-1])

    out_shape = [pltpu.SemaphoreType.DMA(s) for s in sems_out] + [pltpu.HBM(b.shape, b.dtype) for b in bufs]
    out_specs = [SEM] * no + [HBM] * nb
    if no:
        out_shape.append(_sds((V7X_SUBLANES, V7X_LANES), F32))
        out_specs.append(pl.BlockSpec(memory_space=pltpu.VMEM))
    res = pl.pallas_call(
        wrapped, name=name, out_shape=out_shape, in_specs=[HBM] * nb + [SEM] * ni + [ANY] * na, out_specs=out_specs,
        input_output_aliases={i: no + i for i in range(nb)},
        compiler_params=pltpu.CompilerParams(has_side_effects=pltpu.SideEffectType.DATAFLOW_SIDE_EFFECTING),
    )(*bufs, *sems_in, *after)
    out_bufs = list(res[no:no + nb])
    if no:
        _Seq.tokens.append(res[-1])
    _Seq.last = out_bufs[0]
    return list(res[:no]), out_bufs


def _remote(src, dst, send_sem, recv_sem, to):
    return pltpu.make_async_remote_copy(src_ref=src, dst_ref=dst, send_sem=send_sem, recv_sem=recv_sem,
                                        device_id=to, device_id_type=MESH)


def _slot(ref, pos):
    px, py, pc = pos
    return ref.at[:, pl.ds(4 * px + 2 * py + pc, 1)]


def _ag_start(bufs, name):
    n = len(bufs)

    def body(b, _, sems):
        send_sib, recv_sib, send_ici, recv_ici = sems
        x, y, c = _mesh_pos()
        for a in range(n):
            mine = _slot(b[a], (x, y, c))
            _remote(mine, mine, send_sib.at[a], recv_sib.at[a], (x, y, 1 - c)).start()
            for j, (px, py) in enumerate(_other_chips(x, y)):
                _remote(mine, mine, send_ici.at[3 * a + j], recv_ici.at[3 * a + j], (px, py, c)).start()

    sems, bufs = _comm_call(body, name, [_hbm(b) for b in bufs], [], [(n,), (n,), (3 * n,), (3 * n,)])
    return dict(bufs=bufs, send_sib=sems[0], recv_sib=sems[1], send_ici=sems[2], recv_ici=sems[3])


def _ag_mid(st, name):
    n = len(st["bufs"])

    def body(b, sems_in, sems):
        (recv_ici,) = sems_in
        send_fwd, recv_fwd = sems
        x, y, c = _mesh_pos()
        for a in range(n):
            for j, (px, py) in enumerate(_other_chips(x, y)):
                blk = _slot(b[a], (px, py, c))
                _remote(blk, blk, send_fwd.at[3 * a + j], recv_ici.at[3 * a + j], (x, y, 1 - c)).wait_recv()
                _remote(blk, blk, send_fwd.at[3 * a + j], recv_fwd.at[3 * a + j], (x, y, 1 - c)).start()

    sems, bufs = _comm_call(body, name, st["bufs"], [st["recv_ici"]], [(3 * n,), (3 * n,)])
    return dict(st, bufs=bufs, send_fwd=sems[0], recv_fwd=sems[1])


def _ag_end(st, name):
    n = len(st["bufs"])

    def body(b, sems_in, _):
        send_sib, recv_sib, send_ici, send_fwd, recv_fwd = sems_in
        x, y, c = _mesh_pos()
        sibling = (x, y, 1 - c)
        for a in range(n):
            mine, sib_blk = _slot(b[a], (x, y, c)), _slot(b[a], sibling)
            _remote(mine, mine, send_sib.at[a], recv_sib.at[a], sibling).wait_send()
            _remote(sib_blk, sib_blk, send_sib.at[a], recv_sib.at[a], sibling).wait_recv()
            for j, (px, py) in enumerate(_other_chips(x, y)):
                blk, sib_got = _slot(b[a], (px, py, c)), _slot(b[a], (px, py, 1 - c))
                _remote(mine, mine, send_ici.at[3 * a + j], recv_sib.at[a], (px, py, c)).wait_send()
                _remote(blk, blk, send_fwd.at[3 * a + j], recv_fwd.at[3 * a + j], sibling).wait_send()
                _remote(sib_got, sib_got, send_fwd.at[3 * a + j], recv_fwd.at[3 * a + j], sibling).wait_recv()

    _, bufs = _comm_call(body, name, st["bufs"],
                         [st[k] for k in ("send_sib", "recv_sib", "send_ici", "send_fwd", "recv_fwd")], [])
    return bufs


def _rs_start(g4s, name):
    n = len(g4s)
    lands = [lax.empty((N_CHIP, 1) + g.shape[2:], g.dtype) for g in g4s]

    def body(b, _, sems):
        send, recv = sems
        x, y, c = _mesh_pos()
        for a in range(n):
            _remote(b[a].at[:, pl.ds(1 - c, 1)], b[n + a], send.at[a], recv.at[a], (x, y, 1 - c)).start()

    sems, bufs = _comm_call(body, name, [_hbm(v) for v in list(g4s) + lands], [], [(n,), (n,)])
    return dict(bufs=bufs, send=sems[0], recv=sems[1])


def _rs_mid(st, name):
    n = len(st["bufs"]) // 2

    def body(b, sems_in, _):
        send, recv = sems_in
        x, y, c = _mesh_pos()
        for a in range(n):
            cp = _remote(b[a].at[:, pl.ds(1 - c, 1)], b[n + a], send.at[a], recv.at[a], (x, y, 1 - c))
            cp.wait_send()
            cp.wait_recv()

    _, bufs = _comm_call(body, name, st["bufs"], [st["send"], st["recv"]], [])
    return bufs[:n], bufs[n:]


def _rs_start2(sums, name):
    n = len(sums)
    lands = [lax.empty((N_CHIP - 1,) + s.shape[1:], s.dtype) for s in sums]

    def body(b, _, sems):
        send, recv = sems
        x, y, c = _mesh_pos()
        for a in range(n):
            for j, (px, py) in enumerate(_other_chips(x, y)):
                _remote(b[a].at[pl.ds(2 * px + py, 1)], b[n + a].at[pl.ds(j, 1)], send.at[3 * a + j], recv.at[3 * a + j],
                        (px, py, c)).start()

    sems, bufs = _comm_call(body, name, [_hbm(v) for v in list(sums) + lands], [], [(3 * n,), (3 * n,)])
    return dict(bufs=bufs, send=sems[0], recv=sems[1])


def _rs_end(st, name):
    n = len(st["bufs"]) // 2

    def body(b, sems_in, _):
        send, recv = sems_in
        x, y, c = _mesh_pos()
        for a in range(n):
            for j, (px, py) in enumerate(_other_chips(x, y)):
                cp = _remote(b[a].at[pl.ds(2 * px + py, 1)], b[n + a].at[pl.ds(j, 1)], send.at[3 * a + j], recv.at[3 * a + j],
                             (px, py, c))
                cp.wait_send()
                cp.wait_recv()

    _, bufs = _comm_call(body, name, st["bufs"], [st["send"], st["recv"]], [])
    return bufs[:n], bufs[n:]


def _pack(parts):
    flat = []
    for p in parts:
        v = p.reshape(-1).astype(F32)
        pad = -v.shape[0] % PACK_ALIGN
        flat.append(jnp.pad(v, (0, pad)) if pad else v)
    return jnp.concatenate(flat).reshape(-1, V7X_LANES)


def _unpack(buf, shapes):
    lead = buf.shape[:-2]
    flat = buf.reshape(lead + (-1,))
    out, pos = [], 0
    for s in shapes:
        size = 1
        for d in s:
            size *= d
        out.append(flat[..., pos:pos + size].reshape(lead + tuple(s)))
        pos += size + (-size % PACK_ALIGN)
    return out


def kernel(x, c, ada_w, ada_b, norm_g, ffn_w_in, ffn_w_out, gm_w_in, gm_ln_g, gm_ln_b, gm_ws, gm_bs, gm_w_out, cv_w_in, cv_b_in, cv_dw_w, cv_dw_b, cv_ln_g, cv_ln_b, cv_w_out, cv_b_out, final_g, loss_target, m_ada_w, m_ada_b, m_norm_g, m_ffn_w_in, m_ffn_w_out, m_gm_w_in, m_gm_ln_g, m_gm_ln_b, m_gm_ws, m_gm_bs, m_gm_w_out, m_cv_w_in, m_cv_b_in, m_cv_dw_w, m_cv_dw_b, m_cv_ln_g, m_cv_ln_b, m_cv_w_out, m_cv_b_out, m_final_g, v_ada_w, v_ada_b, v_norm_g, v_ffn_w_in, v_ffn_w_out, v_gm_w_in, v_gm_ln_g, v_gm_ln_b, v_gm_ws, v_gm_bs, v_gm_w_out, v_cv_w_in, v_cv_b_in, v_cv_dw_w, v_cv_dw_b, v_cv_ln_g, v_cv_ln_b, v_cv_w_out, v_cv_b_out, v_final_g):
    t, d = x.shape[1], x.shape[2]
    depth = ada_w.shape[0]
    assert depth == 2 and ffn_w_in.shape[:2] == (2, 2) and gm_w_in.shape[0] == 1 and cv_w_in.shape[0] == 1
    dl = d // N_DEV
    bn = ffn_w_in.shape[3]
    fl = ffn_w_out.shape[2]
    f = fl * N_DEV
    el = gm_w_in.shape[2]
    e = el * N_DEV // 2
    hn, l = gm_ws.shape[1], gm_ws.shape[2]
    kw = cv_dw_w.shape[1]
    cl = ada_w.shape[2]
    me = 4 * lax.axis_index("x") + 2 * lax.axis_index("y") + lax.axis_index("c")
    me1 = me.astype(jnp.int32).reshape(1)
    chip1 = (2 * lax.axis_index("x") + lax.axis_index("y")).astype(jnp.int32).reshape(1)
    core1 = lax.axis_index("c").astype(jnp.int32).reshape(1)
    _Seq.last, _Seq.tokens = None, []

    xs = x[0]
    tgt = loss_target[0]

    ag_groups = [("win00", [(ffn_w_in, (0, 0))]), ("wout00", [(ffn_w_out, (0, 0))]),
                 ("gm", [(gm_w_in, (0,)), (gm_w_out, (0,))]),
                 ("win01", [(ffn_w_in, (0, 1))]), ("wout01", [(ffn_w_out, (0, 1))]),
                 ("win10", [(ffn_w_in, (1, 0))]), ("wout10", [(ffn_w_out, (1, 0))]),
                 ("cv", [(cv_w_in, (0,)), (cv_w_out, (0,))]),
                 ("win11", [(ffn_w_in, (1, 1))]), ("wout11", [(ffn_w_out, (1, 1))])]
    ag_flight = {}

    def ag_start(gi):
        gname, members = ag_groups[gi]
        bufs = [_cast_to_slot(w, lead, me1, name=f"cast_{gname}_{k}") for k, (w, lead) in enumerate(members)]
        ag_flight[gi] = _ag_start(bufs, name=f"ag_start_{gname}")

    def ag_take(gi):
        gname = ag_groups[gi][0]
        bufs = _ag_end(_ag_mid(ag_flight.pop(gi), name=f"ag_mid_{gname}"), name=f"ag_end_{gname}")
        if gi + 2 < len(ag_groups):
            ag_start(gi + 2)
        return [b[0] for b in bufs]

    ag_start(0)
    ag_start(1)

    small_in = [c, norm_g, cv_b_in, cv_dw_w, cv_dw_b, cv_ln_g, cv_ln_b, cv_b_out]
    pack1 = _pack(small_in)
    (pack1_all,) = _all_gather([pack1[None, None]], name="ag_small")
    parts = _unpack(pack1_all[0], [s.shape for s in small_in])
    c_all = parts[0].reshape(N_DEV, d)
    ng_full = jnp.moveaxis(parts[1], 0, 2).reshape(depth, 3, d)
    cvb_in_full = parts[2].reshape(1, 2 * e)
    dww_full = jnp.moveaxis(parts[3][:, 0], 0, 1).reshape(kw, e)
    dwb_full, cln_g_full, cln_b_full, cvb_out_full = [p.reshape(1, d) for p in parts[4:8]]

    c_pad = jnp.pad(c_all, ((0, 16 - N_DEV), (0, 0)))
    ada_b_loc = lax.dynamic_slice_in_dim(ada_b, me * cl, cl, axis=1).reshape(depth, 1, cl)
    mod_part = _ada_fwd(c_pad, ada_w, ada_b_loc, name="ada_fwd")[:, :N_DEV]
    (mod_all,) = _all_gather([_pack([mod_part])[None, None]], name="ag_mod")
    mod_all = _unpack(mod_all[0], [mod_part.shape])[0]
    mod_mine = lax.dynamic_index_in_dim(mod_all, me, axis=2, keepdims=False)
    mod = jnp.moveaxis(mod_mine, 0, 1).reshape(depth, 3, 3, 1, d)

    ws = gm_ws[0]
    bsb = jnp.broadcast_to(gm_bs[0][:, :, None], (hn, l, e // hn))
    gm_g, gm_b = gm_ln_g, gm_ln_b

    saved = []
    xcur = xs
    next_group = 0
    for i in range(depth):
        for s in range(3):
            shift, scale, gate = mod[i, s, 0], mod[i, s, 1], mod[i, s, 2]
            g_norm = ng_full[i, s][None]
            tag = f"l{i}s{s}"
            h = _norm_mod(xcur, g_norm, scale, shift, name=f"norm_mod_{tag}")
            if s != 1:
                (w_in_blk,) = ag_take(next_group)
                gg, uu, act = _ffn_in(h, w_in_blk, 0, name=f"ffn_in_{tag}")
                w_out3 = ag_take(next_group + 1)[0].reshape(1, f, d)
                next_group += 2
                xnext, yv = _out_proj(act, w_out3, 0, xcur, gate, None, 0.5, name=f"ffn_out_{tag}")
                saved.append(dict(x=xcur, h=h, g=gg, u=uu, y=yv, w_in=w_in_blk, w_out=w_out3))
            elif i % 2 == 0:
                gm_in_blk, gm_out = ag_take(next_group)
                gm_out3 = gm_out.reshape(1, e, d)
                next_group += 1
                pre = _in_proj(h, gm_in_blk, None, name=f"gm_in_{tag}")
                uu, vn = _gm_act(pre, gm_g, gm_b, name=f"gm_act_{tag}")
                sg = _sgu_fwd(uu, vn, ws, bsb, name=f"sgu_fwd_{tag}")
                xnext, yv = _out_proj(sg, gm_out3, 0, xcur, gate, None, 1.0, name=f"gm_out_{tag}")
                saved.append(dict(x=xcur, h=h, pre=pre, u=uu, vn=vn, sg=sg, y=yv, w_in=gm_in_blk, w_out=gm_out3))
            else:
                cv_in_blk, cv_out = ag_take(next_group)
                cv_out3 = cv_out.reshape(1, e, d)
                next_group += 1
                p = _in_proj(h, cv_in_blk, cvb_in_full, name=f"cv_in_{tag}")
                yc = _dwconv_fwd(p, dww_full, dwb_full, name=f"dwconv_fwd_{tag}")
                ys = _cv_act(yc, cln_g_full, cln_b_full, name=f"cv_act_{tag}")
                xnext, yv = _out_proj(ys, cv_out3, 0, xcur, gate, cvb_out_full, 1.0, name=f"cv_out_{tag}")
                saved.append(dict(x=xcur, h=h, p=p, yc=yc, ys=ys, y=yv, w_in=cv_in_blk, w_out=cv_out3))
            xcur = xnext

    sq, dx, d_final_g = _final_loss(xcur, tgt, final_g[None], name="final_loss")
    loss = lax.psum(0.5 / d * jnp.sum(sq), ("x", "y", "c"))

    dmod = [[[None] * 3 for _ in range(3)] for _ in range(depth)]
    d_norm_g = [[None] * 3 for _ in range(depth)]
    small = {}

    stacked = {
        "ffn_w_in": [a.reshape(4, d, bn) for a in (ffn_w_in, m_ffn_w_in, v_ffn_w_in)],
        "ffn_w_out": [a.reshape(4, fl, d) for a in (ffn_w_out, m_ffn_w_out, v_ffn_w_out)],
        "gm_w_in": [gm_w_in, m_gm_w_in, v_gm_w_in], "gm_w_out": [gm_w_out, m_gm_w_out, v_gm_w_out],
        "cv_w_in": [cv_w_in, m_cv_w_in, v_cv_w_in], "cv_w_out": [cv_w_out, m_cv_w_out, v_cv_w_out],
    }
    res_big = {}

    def rs_finish(flight):
        st, targets, tag = flight
        sums, recvs = _rs_end(st, name=f"rs_end_{tag}")
        for (pname, k), hsum, recv in zip(targets, sums, recvs):
            w_st, m_st, v_st = stacked[pname]
            res_big[pname] = _adamw_stacked(hsum, recv, chip1, w_st, m_st, v_st, k, res_big.get(pname),
                                            name=f"adamw_{pname}_{k}")

    pending = None
    for i in reversed(range(depth)):
        for s in reversed(range(3)):
            sv = saved[3 * i + s]
            shift, scale, gate = mod[i, s, 0], mod[i, s, 1], mod[i, s, 2]
            g_norm = ng_full[i, s][None]
            tag = f"l{i}s{s}"
            if s != 1:
                widx = 2 * i + s // 2
                dy, (dgate,) = _residual_bwd(dx, sv["y"], gate, 0.5, False, name=f"res_bwd_{tag}")
                dgu, act = _ffn_da(dy, sv["w_out"], 0, sv["g"], sv["u"], name=f"ffn_da_{tag}")
                g_out = _mm_tn(act, dy[None], bn, d, False, name=f"ffn_dwout_{tag}").reshape(N_CHIP, 2, fl, d)
                g_in = _mm_tn(sv["h"], dgu, d, bn, True, name=f"ffn_dwin_{tag}").reshape(N_CHIP, 2, d, bn)
                targets = [("ffn_w_in", widx), ("ffn_w_out", widx)]
                st = _rs_start([g_in, g_out], name=f"rs_start_{tag}")
                dh = _mm_nt_blocks(dgu, sv["w_in"], 0, name=f"ffn_dh_{tag}")
            elif i % 2 == 0:
                dy, (dgate,) = _residual_bwd(dx, sv["y"], gate, 1.0, False, name=f"res_bwd_{tag}")
                ds = _mm_nt(dy, sv["w_out"], 0, name=f"gm_ds_{tag}")
                g_out = _mm_tn(sv["sg"], dy[None], _tile(e, 1024, V7X_LANES), d, False,
                               name=f"gm_dwout_{tag}").reshape(N_CHIP, 2, dl, d)
                du, dvn, dws, dbs = _sgu_bwd(ds, sv["u"], sv["vn"], ws, bsb, name=f"sgu_bwd_{tag}")
                dpre, dlng, dlnb = _gm_act_bwd(sv["pre"], du, dvn, gm_g, name=f"gm_act_bwd_{tag}")
                small["gm_ln_g"], small["gm_ln_b"] = dlng, dlnb
                small["gm_ws"], small["gm_bs"] = dws, dbs[:, :, 0]
                g_in = _mm_tn(sv["h"], dpre[None], d, el, True, name=f"gm_dwin_{tag}").reshape(N_CHIP, 2, d, el)
                targets = [("gm_w_in", 0), ("gm_w_out", 0)]
                st = _rs_start([g_in, g_out], name=f"rs_start_{tag}")
                dh = _mm_nt_blocks(dpre[None], sv["w_in"], 0, name=f"gm_dh_{tag}")
            else:
                dy, (dgate, dbout) = _residual_bwd(dx, sv["y"], gate, 1.0, True, name=f"res_bwd_{tag}")
                dys = _mm_nt(dy, sv["w_out"], 0, name=f"cv_dys_{tag}")
                g_out = _mm_tn(sv["ys"], dy[None], _tile(e, 1024, V7X_LANES), d, False,
                               name=f"cv_dwout_{tag}").reshape(N_CHIP, 2, dl, d)
                dyc, dlng, dlnb, ddwb = _cv_act_bwd(dys, sv["yc"], cln_g_full, cln_b_full, name=f"cv_act_bwd_{tag}")
                dp, ddww, dba, dbg = _dwconv_bwd(dyc, sv["p"], dww_full, name=f"dwconv_bwd_{tag}")
                small["cv_b_out"], small["cv_ln_g"], small["cv_ln_b"], small["cv_dw_b"] = dbout, dlng, dlnb, ddwb
                small["cv_dw_w"] = ddww
                small["cv_b_in"] = jnp.concatenate([dba, dbg], axis=1)
                g_in = _mm_tn(sv["h"], dp, d, el, True, name=f"cv_dwin_{tag}").reshape(N_CHIP, 2, d, el)
                targets = [("cv_w_in", 0), ("cv_w_out", 0)]
                st = _rs_start([g_in, g_out], name=f"rs_start_{tag}")
                dh = _mm_nt_blocks(dp, sv["w_in"], 0, name=f"cv_dh_{tag}")
            g4s, lands = _rs_mid(st, name=f"rs_mid_{tag}")
            sums = [_add_sibling(g4, land, core1, name=f"rs_add_{tag}_{k}") for k, (g4, land) in enumerate(zip(g4s, lands))]
            st2 = _rs_start2(sums, name=f"rs_start2_{tag}")
            dx, dscale, dshift, dgn = _norm_mod_bwd(dh, sv["x"], dx, g_norm, scale, name=f"norm_mod_bwd_{tag}")
            dmod[i][s] = [dshift, dscale, dgate]
            d_norm_g[i][s] = dgn
            if pending is not None:
                rs_finish(pending)
            pending = (st2, targets, tag)
    grad_x = dx[None]

    dmod_mine = jnp.concatenate([v for per_l in dmod for per_s in per_l for v in per_s], axis=1)
    dng_mine = jnp.concatenate([v for per_l in d_norm_g for v in per_l], axis=1)
    small_out = [dmod_mine, dng_mine, small["gm_ln_g"], small["gm_ln_b"], small["gm_ws"], small["gm_bs"],
                 small["cv_b_in"], small["cv_dw_w"], small["cv_dw_b"], small["cv_ln_g"], small["cv_ln_b"],
                 small["cv_b_out"], d_final_g]
    shapes2 = [s.shape for s in small_out]
    (pack2_all,) = _all_gather([_pack(small_out)[None, None]], name="ag_small_grads")
    summed = _unpack(_sum_devices(pack2_all[0], name="sum_small_grads"), shapes2)
    dmod_all = _unpack(pack2_all[0], shapes2)[0].reshape(N_DEV, depth, 9 * d)

    def my_cols(full, width):
        return lax.dynamic_slice_in_dim(full, me * width, width, axis=full.ndim - 1)

    g_ada_b = summed[0].reshape(depth, 9 * d)
    g_norm_g = my_cols(summed[1].reshape(depth, 3, d), dl)
    g_small = {
        "ada_b": g_ada_b, "norm_g": g_norm_g,
        "gm_ln_g": summed[2], "gm_ln_b": summed[3], "gm_ws": summed[4][None], "gm_bs": summed[5][None],
        "cv_b_in": my_cols(summed[6], el), "cv_dw_w": my_cols(summed[7], dl)[None],
        "cv_dw_b": my_cols(summed[8], dl), "cv_ln_g": my_cols(summed[9], dl), "cv_ln_b": my_cols(summed[10], dl),
        "cv_b_out": my_cols(summed[11], dl), "final_g": summed[12].reshape(d),
    }

    dm_loc = jnp.moveaxis(my_cols(dmod_all, cl), 0, 1)
    dm_loc = jnp.pad(dm_loc, ((0, 0), (0, 16 - N_DEV), (0, 0)))
    res_ada_w = _ada_bwd(c_pad, dm_loc, ada_w, m_ada_w, v_ada_w, name="ada_bwd_adamw")

    def flat2(a):
        return a.reshape(-1, a.shape[-1])

    small_params = {
        "ada_b": (ada_b, m_ada_b, v_ada_b), "norm_g": (norm_g, m_norm_g, v_norm_g),
        "gm_ln_g": (gm_ln_g, m_gm_ln_g, v_gm_ln_g), "gm_ln_b": (gm_ln_b, m_gm_ln_b, v_gm_ln_b),
        "gm_ws": (gm_ws, m_gm_ws, v_gm_ws), "gm_bs": (gm_bs, m_gm_bs, v_gm_bs),
        "cv_b_in": (cv_b_in, m_cv_b_in, v_cv_b_in), "cv_dw_w": (cv_dw_w, m_cv_dw_w, v_cv_dw_w),
        "cv_dw_b": (cv_dw_b, m_cv_dw_b, v_cv_dw_b), "cv_ln_g": (cv_ln_g, m_cv_ln_g, v_cv_ln_g),
        "cv_ln_b": (cv_ln_b, m_cv_ln_b, v_cv_ln_b), "cv_b_out": (cv_b_out, m_cv_b_out, v_cv_b_out),
        "final_g": (final_g, m_final_g, v_final_g),
    }
    res_small = {}
    for key, (w, m, v) in small_params.items():
        g2 = flat2(g_small[key].reshape(w.shape)) if w.ndim > 1 else g_small[key].reshape(1, -1)
        w2, m2, v2 = [flat2(a) if a.ndim > 1 else a.reshape(1, -1) for a in (w, m, v)]
        res_small[key] = [o.reshape(w.shape) for o in _adamw([(g2[None], 0)], w2, m2, v2, name=f"adamw_{key}")]

    rs_finish(pending)

    def big(name, k):
        if name == "ada_w":
            return res_ada_w[k]
        return res_big[name][k].reshape(stacked_shape[name])

    stacked_shape = {"ffn_w_in": ffn_w_in.shape, "ffn_w_out": ffn_w_out.shape, "gm_w_in": gm_w_in.shape,
                     "gm_w_out": gm_w_out.shape, "cv_w_in": cv_w_in.shape, "cv_w_out": cv_w_out.shape}

    order = ["ada_w", "ada_b", "norm_g", "ffn_w_in", "ffn_w_out", "gm_w_in", "gm_ln_g", "gm_ln_b", "gm_ws", "gm_bs",
             "gm_w_out", "cv_w_in", "cv_b_in", "cv_dw_w", "cv_dw_b", "cv_ln_g", "cv_ln_b", "cv_w_out", "cv_b_out",
             "final_g"]
    outs = [loss, grad_x]
    for k in range(4):
        for name in order:
            outs.append(res_small[name][k] if name in res_small else big(name, k))
    return tuple(outs)
```

```python
import functools

import jax
import jax.numpy as jnp
from jax import lax
from jax.experimental import pallas as pl
from jax.experimental.pallas import tpu as pltpu

F32 = jnp.float32
BF = jnp.bfloat16
MESH = pl.DeviceIdType.MESH

N_DEV = 8
N_CHIP = 4
NORM_EPS = 1e-6
ADAM_LR = 0.001
ADAM_B1 = 0.9
ADAM_B2 = 0.999
ADAM_EPS = 1e-08
ADAM_WD = 0.01
ADAM_STEP = 10

V7X_SUBLANES = 8
V7X_LANES = 128
PACK_ALIGN = V7X_SUBLANES * V7X_LANES
V7X_VMEM_LIMIT = 56 * 1024 * 1024


def _tile(n, pref, align):
    if n <= pref:
        return n
    t = pref - pref % align
    while t >= align:
        if n % t == 0:
            return t
        t -= align
    return n


ANY = pl.BlockSpec(memory_space=pl.ANY)


class _Seq:
    last = None
    tokens = []


def _call(body, *, name, grid, in_specs, out_specs, out_shape, scratch=(), sem=None, prefetch=None, aliases=None,
          after=()):
    def run(*args):
        tokens, _Seq.tokens = _Seq.tokens + list(after), []
        lead = 0 if prefetch is None else 1
        n_in, n_tok = lead + len(args), len(tokens)

        def wrapped(*refs):
            body(*refs[:n_in], *refs[n_in + n_tok:])

        specs = list(in_specs) + [ANY] * n_tok
        params = pltpu.CompilerParams(dimension_semantics=sem, vmem_limit_bytes=V7X_VMEM_LIMIT)
        if prefetch is None:
            res = pl.pallas_call(wrapped, out_shape=out_shape, grid=grid, in_specs=specs, out_specs=out_specs,
                                 scratch_shapes=scratch, name=name, compiler_params=params,
                                 input_output_aliases=aliases or {})(*args, *tokens)
        else:
            grid_spec = pltpu.PrefetchScalarGridSpec(num_scalar_prefetch=1, grid=grid, in_specs=specs,
                                                     out_specs=out_specs, scratch_shapes=scratch)
            res = pl.pallas_call(wrapped, out_shape=out_shape, grid_spec=grid_spec, name=name,
                                 compiler_params=params, input_output_aliases=aliases or {})(prefetch, *args, *tokens)
        _Seq.last = res[0] if isinstance(res, (list, tuple)) else res
        return res
    return run


def _sds(shape, dtype):
    return jax.ShapeDtypeStruct(tuple(shape), dtype)


def _sigmoid(v):
    return 1.0 / (1.0 + jnp.exp(-v))


def _erf(v):
    a = jnp.abs(v)
    t = 1.0 / (1.0 + 0.3275911 * a)
    poly = t * (0.254829592 + t * (-0.284496736 + t * (1.421413741 + t * (-1.453152027 + t * 1.061405429))))
    r = 1.0 - poly * jnp.exp(-a * a)
    return jnp.where(v < 0, -r, r)


def _gelu(v):
    return 0.5 * v * (1.0 + _erf(v * 0.7071067811865476))


def _gelu_grad(v):
    cdf = 0.5 * (1.0 + _erf(v * 0.7071067811865476))
    pdf = 0.3989422804014327 * jnp.exp(-0.5 * v * v)
    return cdf + v * pdf


def _fold_rows(val):
    rows, w = val.shape
    return val.reshape(rows // V7X_SUBLANES, V7X_SUBLANES, w).sum(axis=0)


def _rowwise(fn, name, rows_in, vecs_in, rows_out, acc_widths, tm=256):
    t = rows_in[0].shape[0]
    tm = _tile(t, tm, V7X_SUBLANES)
    steps = t // tm
    nr, nv, no, na = len(rows_in), len(vecs_in), len(rows_out), len(acc_widths)

    def body(*refs):
        rin, vin = refs[:nr], refs[nr:nr + nv]
        rout = refs[nr + nv:nr + nv + no]
        aout = refs[nr + nv + no:nr + nv + no + na]
        accs = refs[nr + nv + no + na:]
        i = pl.program_id(0)
        outs, acc_vals = fn(*[r[...] for r in rin], *[v[...] for v in vin])
        for r, o in zip(rout, outs):
            r[...] = o.astype(r.dtype)
        if na:
            @pl.when(i == 0)
            def _():
                for a in accs:
                    a[...] = jnp.zeros_like(a)

            for a, val in zip(accs, acc_vals):
                a[...] += _fold_rows(val)

            @pl.when(i == steps - 1)
            def _():
                for o, a in zip(aout, accs):
                    o[...] = jnp.sum(a[...], axis=0, keepdims=True)

    in_specs = [pl.BlockSpec((tm, r.shape[1]), lambda i: (i, 0)) for r in rows_in]
    in_specs += [pl.BlockSpec(v.shape, functools.partial(lambda nd, i: (0,) * nd, v.ndim)) for v in vecs_in]
    out_specs = [pl.BlockSpec((tm, r.shape[1]), lambda i: (i, 0)) for r in rows_out]
    out_specs += [pl.BlockSpec((1, w), lambda i: (0, 0)) for w in acc_widths]
    out_shape = list(rows_out) + [_sds((1, w), F32) for w in acc_widths]
    scratch = [pltpu.VMEM((V7X_SUBLANES, w), F32) for w in acc_widths]
    res = _call(body, name=name, grid=(steps,), in_specs=in_specs, out_specs=out_specs, out_shape=out_shape,
                scratch=scratch, sem=("arbitrary",) if na else ("parallel",))(*rows_in, *vecs_in)
    return res[:no], res[no:]


def _norm_mod(x, g, scale, shift, name):
    def fn(xv, gv, sc, sh):
        r = lax.rsqrt(jnp.mean(xv * xv, axis=-1, keepdims=True) + NORM_EPS)
        return ((xv * r * gv) * (1.0 + sc) + sh,), ()
    (h,), _ = _rowwise(fn, name, [x], [g, scale, shift], [_sds(x.shape, BF)], [])
    return h


def _residual_bwd(dxp, y, gate, coef, with_colsum, name):
    def fn(dv, yv, gt):
        dy = (coef * gt) * dv
        accs = (coef * dv * yv.astype(F32),)
        if with_colsum:
            accs += (dy,)
        return (dy,), accs
    d = dxp.shape[1]
    (dy,), accs = _rowwise(fn, name, [dxp, y], [gate], [_sds(dxp.shape, BF)], [d, d] if with_colsum else [d])
    return dy, accs


def _norm_mod_bwd(dh, x, dxp, g, scale, name):
    def fn(dhv, xv, dpv, gv, sc):
        r = lax.rsqrt(jnp.mean(xv * xv, axis=-1, keepdims=True) + NORM_EPS)
        xhat = xv * r
        dn = dhv * (1.0 + sc)
        dxhat = dn * gv
        dx = r * (dxhat - xhat * jnp.mean(dxhat * xhat, axis=-1, keepdims=True)) + dpv
        return (dx,), (dhv * (xhat * gv), dhv, dn * xhat)
    d = x.shape[1]
    (dx,), (dscale, dshift, dg) = _rowwise(fn, name, [dh, x, dxp], [g, scale], [_sds(x.shape, F32)], [d, d, d])
    return dx, dscale, dshift, dg


def _final_loss(x, target, g, name):
    d = x.shape[1]

    def fn(xv, tv, gv):
        r = lax.rsqrt(jnp.mean(xv * xv, axis=-1, keepdims=True) + NORM_EPS)
        xhat = xv * r
        err = xhat * gv - tv
        dy = err * (1.0 / d)
        dxhat = dy * gv
        dx = r * (dxhat - xhat * jnp.mean(dxhat * xhat, axis=-1, keepdims=True))
        return (dx,), (err * err, dy * xhat)
    (dx,), (sq, dg) = _rowwise(fn, name, [x, target], [g], [_sds(x.shape, F32)], [d, d])
    return sq, dx, dg


def _gm_act(pre, ln_g, ln_b, name):
    e = pre.shape[1] // 2

    def fn(pv, gv, bv):
        p = pv.astype(F32)
        u = _gelu(p[:, :e])
        v = _gelu(p[:, e:])
        mu = jnp.mean(v, axis=-1, keepdims=True)
        vc = v - mu
        rstd = lax.rsqrt(jnp.mean(vc * vc, axis=-1, keepdims=True) + NORM_EPS)
        return (u, vc * rstd * gv + bv), ()
    t = pre.shape[0]
    (u, vn), _ = _rowwise(fn, name, [pre], [ln_g, ln_b], [_sds((t, e), BF), _sds((t, e), BF)], [])
    return u, vn


def _gm_act_bwd(pre, du, dvn, ln_g, name):
    e = pre.shape[1] // 2

    def fn(pv, duv, dvv, gv):
        p = pv.astype(F32)
        pu, pvv = p[:, :e], p[:, e:]
        v = _gelu(pvv)
        mu = jnp.mean(v, axis=-1, keepdims=True)
        vc = v - mu
        rstd = lax.rsqrt(jnp.mean(vc * vc, axis=-1, keepdims=True) + NORM_EPS)
        vhat = vc * rstd
        dvn_f = dvv.astype(F32)
        dvhat = dvn_f * gv
        dv = rstd * (dvhat - jnp.mean(dvhat, axis=-1, keepdims=True)
                     - vhat * jnp.mean(dvhat * vhat, axis=-1, keepdims=True))
        dpu = duv.astype(F32) * _gelu_grad(pu)
        dpv = dv * _gelu_grad(pvv)
        return (jnp.concatenate([dpu, dpv], axis=1),), (dvn_f * vhat, dvn_f)
    (dpre,), (dg, db) = _rowwise(fn, name, [pre, du, dvn], [ln_g], [_sds(pre.shape, BF)], [e, e], tm=128)
    return dpre, dg, db


def _cv_act(yc, ln_g, ln_b, name):
    def fn(yv, gv, bv):
        mu = jnp.mean(yv, axis=-1, keepdims=True)
        c = yv - mu
        rstd = lax.rsqrt(jnp.mean(c * c, axis=-1, keepdims=True) + NORM_EPS)
        yn = c * rstd * gv + bv
        return (yn * _sigmoid(yn),), ()
    (ys,), _ = _rowwise(fn, name, [yc], [ln_g, ln_b], [_sds(yc.shape, BF)], [])
    return ys


def _cv_act_bwd(dys, yc, ln_g, ln_b, name):
    def fn(dv, yv, gv, bv):
        mu = jnp.mean(yv, axis=-1, keepdims=True)
        c = yv - mu
        rstd = lax.rsqrt(jnp.mean(c * c, axis=-1, keepdims=True) + NORM_EPS)
        yhat = c * rstd
        yn = yhat * gv + bv
        sig = _sigmoid(yn)
        dyn = dv.astype(F32) * (sig * (1.0 + yn * (1.0 - sig)))
        dyhat = dyn * gv
        dyc = rstd * (dyhat - jnp.mean(dyhat, axis=-1, keepdims=True)
                      - yhat * jnp.mean(dyhat * yhat, axis=-1, keepdims=True))
        return (dyc,), (dyn * yhat, dyn, dyc)
    cw = yc.shape[1]
    (dyc,), (dg, db, dbias) = _rowwise(fn, name, [dys, yc], [ln_g, ln_b], [_sds(yc.shape, F32)], [cw, cw, cw])
    return dyc, dg, db, dbias


def _ffn_in(h, w_blk, blk0, name):
    t, d = h.shape
    bn = w_blk.shape[2]
    half = N_DEV // 2
    f = half * bn
    tm = _tile(t, 512, V7X_SUBLANES)

    def body(h_ref, wg_ref, wu_ref, g_ref, u_ref, a_ref):
        hv = h_ref[...]
        g = jnp.dot(hv, wg_ref[...], preferred_element_type=F32)
        u = jnp.dot(hv, wu_ref[...], preferred_element_type=F32)
        g_ref[...] = g.astype(BF)
        u_ref[...] = u.astype(BF)
        a_ref[...] = (g * _sigmoid(g) * u).astype(BF)

    out = _sds((t, f), BF)
    tile = pl.BlockSpec((tm, bn), lambda j, i: (i, j))
    return _call(
        body, name=name, grid=(half, t // tm),
        in_specs=[pl.BlockSpec((tm, d), lambda j, i: (i, 0)),
                  pl.BlockSpec((None, d, bn), lambda j, i: (blk0 + j, 0, 0)),
                  pl.BlockSpec((None, d, bn), lambda j, i: (blk0 + half + j, 0, 0))],
        out_specs=[tile, tile, tile], out_shape=[out, out, out], sem=("parallel", "parallel"))(h, w_blk, w_blk)


def _in_proj(h, w_blk, bias, name):
    t, d = h.shape
    bn = w_blk.shape[2]
    tm = _tile(t, 1024, V7X_SUBLANES)

    def body(*refs):
        if bias is None:
            h_ref, w_ref, o_ref = refs
            o_ref[...] = jnp.dot(h_ref[...], w_ref[...], preferred_element_type=F32).astype(BF)
        else:
            h_ref, w_ref, b_ref, o_ref = refs
            o_ref[...] = (jnp.dot(h_ref[...], w_ref[...], preferred_element_type=F32) + b_ref[...]).astype(BF)

    in_specs = [pl.BlockSpec((tm, d), lambda j, i: (i, 0)), pl.BlockSpec((None, d, bn), lambda j, i: (j, 0, 0))]
    args = [h, w_blk]
    if bias is not None:
        in_specs.append(pl.BlockSpec((1, bn), lambda j, i: (0, j)))
        args.append(bias)
    return _call(body, name=name, grid=(N_DEV, t // tm), in_specs=in_specs,
                 out_specs=pl.BlockSpec((tm, bn), lambda j, i: (i, j)), out_shape=_sds((t, N_DEV * bn), BF),
                 sem=("parallel", "parallel"))(*args)


def _out_proj(a, w3, widx, x, gate, bias, coef, name):
    t, k = a.shape
    d = w3.shape[2]
    tm = _tile(t, 512, V7X_SUBLANES)
    tn = _tile(d, 512, V7X_LANES)

    def body(*refs):
        if bias is None:
            a_ref, w_ref, x_ref, g_ref, xo_ref, y_ref = refs
            y = jnp.dot(a_ref[...], w_ref[...], preferred_element_type=F32)
        else:
            a_ref, w_ref, x_ref, g_ref, b_ref, xo_ref, y_ref = refs
            y = jnp.dot(a_ref[...], w_ref[...], preferred_element_type=F32) + b_ref[...]
        y_ref[...] = y.astype(BF)
        xo_ref[...] = x_ref[...] + (coef * g_ref[...]) * y

    tile = pl.BlockSpec((tm, tn), lambda j, i: (i, j))
    vec = pl.BlockSpec((1, tn), lambda j, i: (0, j))
    in_specs = [pl.BlockSpec((tm, k), lambda j, i: (i, 0)),
                pl.BlockSpec((None, k, tn), lambda j, i: (widx, 0, j)), tile, vec]
    args = [a, w3, x, gate]
    if bias is not None:
        in_specs.append(vec)
        args.append(bias)
    return _call(body, name=name, grid=(d // tn, t // tm), in_specs=in_specs, out_specs=[tile, tile],
                 out_shape=[_sds((t, d), F32), _sds((t, d), BF)], sem=("parallel", "parallel"))(*args)


def _ffn_da(dy, w3, widx, g, u, name):
    t, d = dy.shape
    f = w3.shape[1]
    bn = f // (N_DEV // 2)
    tm = _tile(t, 512, V7X_SUBLANES)

    def body(dy_ref, w_ref, g_ref, u_ref, dgu_ref, a_ref):
        da = lax.dot_general(dy_ref[...], w_ref[...], (((1,), (1,)), ((), ())), preferred_element_type=F32)
        gv = g_ref[...].astype(F32)
        uv = u_ref[...].astype(F32)
        sig = _sigmoid(gv)
        sl = gv * sig
        dgu_ref[0] = (da * uv * (sig * (1.0 + gv * (1.0 - sig)))).astype(BF)
        dgu_ref[1] = (da * sl).astype(BF)
        a_ref[...] = (sl * uv).astype(BF)

    tile = pl.BlockSpec((tm, bn), lambda j, i: (i, j))
    return _call(
        body, name=name, grid=(f // bn, t // tm),
        in_specs=[pl.BlockSpec((tm, d), lambda j, i: (i, 0)),
                  pl.BlockSpec((None, bn, d), lambda j, i: (widx, j, 0)), tile, tile],
        out_specs=[pl.BlockSpec((2, tm, bn), lambda j, i: (0, i, j)), tile],
        out_shape=[_sds((2, t, f), BF), _sds((t, f), BF)], sem=("parallel", "parallel"))(dy, w3, g, u)


def _mm_nt(dy, w3, widx, name):
    t, k = dy.shape
    n = w3.shape[1]
    tm = _tile(t, 512, V7X_SUBLANES)
    tn = _tile(n, 1024, V7X_LANES)

    def body(dy_ref, w_ref, o_ref):
        o_ref[...] = lax.dot_general(dy_ref[...], w_ref[...], (((1,), (1,)), ((), ())),
                                     preferred_element_type=F32).astype(BF)

    return _call(body, name=name, grid=(n // tn, t // tm),
                 in_specs=[pl.BlockSpec((tm, k), lambda j, i: (i, 0)),
                           pl.BlockSpec((None, tn, k), lambda j, i: (widx, j, 0))],
                 out_specs=pl.BlockSpec((tm, tn), lambda j, i: (i, j)), out_shape=_sds((t, n), BF),
                 sem=("parallel", "parallel"))(dy, w3)


def _mm_nt_blocks(z3, w_blk, blk0, name):
    lead, t, _ = z3.shape
    d, bn = w_blk.shape[1], w_blk.shape[2]
    per = N_DEV // lead
    tm = _tile(t, 512, V7X_SUBLANES)

    def body(z_ref, w_ref, o_ref, acc_ref):
        k = pl.program_id(1)

        @pl.when(k == 0)
        def _():
            acc_ref[...] = jnp.zeros_like(acc_ref)

        acc_ref[...] += lax.dot_general(z_ref[...], w_ref[...], (((1,), (1,)), ((), ())),
                                        preferred_element_type=F32)

        @pl.when(k == N_DEV - 1)
        def _():
            o_ref[...] = acc_ref[...]

    return _call(body, name=name, grid=(t // tm, N_DEV),
                 in_specs=[pl.BlockSpec((None, tm, bn), lambda i, k: (k // per, i, k % per)),
                           pl.BlockSpec((None, d, bn), lambda i, k: (blk0 + k, 0, 0))],
                 out_specs=pl.BlockSpec((tm, d), lambda i, k: (i, 0)), out_shape=_sds((t, d), F32),
                 scratch=[pltpu.VMEM((tm, d), F32)], sem=("parallel", "arbitrary"))(z3, w_blk)


def _mm_tn(a, b3, ta, tb, blocked, name):
    t, ka = a.shape
    lead, _, w = b3.shape
    per = w // tb
    nj = lead * per
    tk = _tile(t, 512, V7X_SUBLANES)
    nk = t // tk

    def body(a_ref, b_ref, o_ref, acc_ref):
        k = pl.program_id(2)

        @pl.when(k == 0)
        def _():
            acc_ref[...] = jnp.zeros_like(acc_ref)

        acc_ref[...] += lax.dot_general(a_ref[...], b_ref[...], (((0,), (0,)), ((), ())),
                                        preferred_element_type=F32)

        @pl.when(k == nk - 1)
        def _():
            o_ref[...] = acc_ref[...].astype(BF)

    if blocked:
        out_shape = _sds((nj, ka, tb), BF)
        out_spec = pl.BlockSpec((None, ta, tb), lambda i, j, k: (j, i, 0))
    else:
        out_shape = _sds((1, ka, w), BF)
        out_spec = pl.BlockSpec((None, ta, tb), lambda i, j, k: (0, i, j))
    return _call(body, name=name, grid=(ka // ta, nj, nk),
                 in_specs=[pl.BlockSpec((tk, ta), lambda i, j, k: (k, i)),
                           pl.BlockSpec((None, tk, tb), lambda i, j, k: (j // per, k, j % per))],
                 out_specs=out_spec, out_shape=out_shape, scratch=[pltpu.VMEM((ta, tb), F32)],
                 sem=("parallel", "parallel", "arbitrary"))(a, b3)


def _causal(ws):
    l = ws.shape[0]
    row = lax.broadcasted_iota(jnp.int32, (l, l), 0)
    col = lax.broadcasted_iota(jnp.int32, (l, l), 1)
    return jnp.where(col <= row, ws, 0.0)


def _sgu_fwd(u, vn, ws, bsb, name):
    t, e = u.shape
    hn, l, _ = ws.shape
    dh = e // hn
    nc = t // l

    def body(u_ref, v_ref, ws_ref, bs_ref, s_ref):
        wsc = _causal(ws_ref[...]).astype(BF)
        bias = bs_ref[...]

        def chunk(c, carry):
            rows = pl.ds(pl.multiple_of(c * l, l), l)
            vo = jnp.dot(wsc, v_ref[rows, :], preferred_element_type=F32) + bias
            s_ref[rows, :] = (u_ref[rows, :].astype(F32) * vo).astype(BF)
            return carry
        lax.fori_loop(0, nc, chunk, 0)

    col = pl.BlockSpec((t, dh), lambda h: (0, h))
    return _call(body, name=name, grid=(hn,),
                 in_specs=[col, col, pl.BlockSpec((None, l, l), lambda h: (h, 0, 0)),
                           pl.BlockSpec((None, l, dh), lambda h: (h, 0, 0))],
                 out_specs=col, out_shape=_sds((t, e), BF), sem=("parallel",))(u, vn, ws, bsb)


def _sgu_bwd(ds, u, vn, ws, bsb, name):
    t, e = u.shape
    hn, l, _ = ws.shape
    dh = e // hn
    nc = t // l

    def body(ds_ref, u_ref, v_ref, ws_ref, bs_ref, du_ref, dv_ref, dws_ref, dbs_ref, accw_ref, accb_ref):
        wsc = _causal(ws_ref[...]).astype(BF)
        bias = bs_ref[...]
        accw_ref[...] = jnp.zeros_like(accw_ref)
        accb_ref[...] = jnp.zeros_like(accb_ref)

        def chunk(c, carry):
            rows = pl.ds(pl.multiple_of(c * l, l), l)
            vc = v_ref[rows, :]
            dsv = ds_ref[rows, :].astype(F32)
            vo = jnp.dot(wsc, vc, preferred_element_type=F32) + bias
            du_ref[rows, :] = (dsv * vo).astype(BF)
            dvo = dsv * u_ref[rows, :].astype(F32)
            dvo_b = dvo.astype(BF)
            accb_ref[...] += dvo
            accw_ref[...] += lax.dot_general(dvo_b, vc, (((1,), (1,)), ((), ())), preferred_element_type=F32)
            dv_ref[rows, :] = lax.dot_general(wsc, dvo_b, (((0,), (0,)), ((), ())),
                                              preferred_element_type=F32).astype(BF)
            return carry
        lax.fori_loop(0, nc, chunk, 0)
        dws_ref[...] = _causal(accw_ref[...])
        dbs_ref[...] = jnp.broadcast_to(jnp.sum(accb_ref[...], axis=1, keepdims=True), (l, dh))

    col = pl.BlockSpec((t, dh), lambda h: (0, h))
    return _call(body, name=name, grid=(hn,),
                 in_specs=[col, col, col, pl.BlockSpec((None, l, l), lambda h: (h, 0, 0)),
                           pl.BlockSpec((None, l, dh), lambda h: (h, 0, 0))],
                 out_specs=[col, col, pl.BlockSpec((None, l, l), lambda h: (h, 0, 0)),
                            pl.BlockSpec((None, l, dh), lambda h: (h, 0, 0))],
                 out_shape=[_sds((t, e), BF), _sds((t, e), BF), _sds((hn, l, l), F32), _sds((hn, l, dh), F32)],
                 scratch=[pltpu.VMEM((l, l), F32), pltpu.VMEM((l, dh), F32)],
                 sem=("parallel",))(ds, u, vn, ws, bsb)


CONV_HALO = 32
CONV_ROWS = 64
CONV_LANES = 256


def _shifted_windows(win_ref, sh_ref, rows):
    for b in range(1, V7X_SUBLANES):
        sh_ref[b - 1, 0:rows, :] = win_ref[b:b + rows, :]


def _window_rows(win_ref, sh_ref, shift, r0, rows):
    a, b = divmod(shift, V7X_SUBLANES)
    start = pl.multiple_of(r0 + V7X_SUBLANES * a, V7X_SUBLANES)
    if b == 0:
        return win_ref[pl.ds(start, rows), :]
    return sh_ref[b - 1, pl.ds(start, rows), :]


def _dwconv_fwd(p, dw_w, dw_b, name):
    t, c2 = p.shape
    cw = c2 // 2
    kw = dw_w.shape[0]
    cb = _tile(cw, CONV_LANES, V7X_LANES)
    ncb = cw // cb
    tm = _tile(t, 512, CONV_ROWS)
    off = CONV_HALO - (kw - 1)

    def body(a_ref, g_ref, ap_ref, gp_ref, w_ref, b_ref, o_ref, win_ref, sh_ref):
        i = pl.program_id(1)
        prev = ap_ref[...].astype(F32) * _sigmoid(gp_ref[...].astype(F32))
        win_ref[0:CONV_HALO, :] = jnp.where(i > 0, prev, 0.0)
        win_ref[CONV_HALO:, :] = a_ref[...].astype(F32) * _sigmoid(g_ref[...].astype(F32))
        _shifted_windows(win_ref, sh_ref, tm + CONV_HALO - V7X_SUBLANES)

        def chunk(ci, carry):
            r0 = ci * CONV_ROWS
            acc = jnp.zeros((CONV_ROWS, cb), F32) + b_ref[...]
            for k in range(kw):
                acc = acc + w_ref[k:k + 1, :] * _window_rows(win_ref, sh_ref, off + k, r0, CONV_ROWS)
            o_ref[pl.ds(pl.multiple_of(r0, CONV_ROWS), CONV_ROWS), :] = acc
            return carry
        lax.fori_loop(0, tm // CONV_ROWS, chunk, 0)

    hpt = tm // CONV_HALO
    cur_a = pl.BlockSpec((tm, cb), lambda j, i: (i, j))
    cur_g = pl.BlockSpec((tm, cb), lambda j, i: (i, ncb + j))
    prev_a = pl.BlockSpec((CONV_HALO, cb), lambda j, i: (jnp.maximum(i * hpt - 1, 0), j))
    prev_g = pl.BlockSpec((CONV_HALO, cb), lambda j, i: (jnp.maximum(i * hpt - 1, 0), ncb + j))
    return _call(body, name=name, grid=(ncb, t // tm),
                 in_specs=[cur_a, cur_g, prev_a, prev_g, pl.BlockSpec((kw, cb), lambda j, i: (0, j)),
                           pl.BlockSpec((1, cb), lambda j, i: (0, j))],
                 out_specs=pl.BlockSpec((tm, cb), lambda j, i: (i, j)), out_shape=_sds((t, cw), F32),
                 scratch=[pltpu.VMEM((tm + CONV_HALO, cb), F32),
                          pltpu.VMEM((V7X_SUBLANES - 1, tm + CONV_HALO - V7X_SUBLANES, cb), F32)],
                 sem=("parallel", "parallel"))(p, p, p, p, dw_w, dw_b)


def _dwconv_bwd(dyc, p, dw_w, name):
    t, c2 = p.shape
    cw = c2 // 2
    kw = dw_w.shape[0]
    cb = _tile(cw, CONV_LANES, V7X_LANES)
    ncb = cw // cb
    tm = _tile(t, 512, CONV_ROWS)
    nt = t // tm
    off = CONV_HALO - (kw - 1)
    kpad = -(-kw // V7X_SUBLANES) * V7X_SUBLANES
    sh_rows = tm + CONV_HALO - V7X_SUBLANES

    def body(d_ref, dn_ref, a_ref, g_ref, ap_ref, gp_ref, w_ref,
             dp_ref, dw_ref, dba_ref, dbg_ref, dwin_ref, ywin_ref, dsh_ref, ysh_ref, accw_ref, acca_ref, accg_ref):
        i = pl.program_id(1)

        @pl.when(i == 0)
        def _():
            accw_ref[...] = jnp.zeros_like(accw_ref)
            acca_ref[...] = jnp.zeros_like(acca_ref)
            accg_ref[...] = jnp.zeros_like(accg_ref)

        prev = ap_ref[...].astype(F32) * _sigmoid(gp_ref[...].astype(F32))
        ywin_ref[0:CONV_HALO, :] = jnp.where(i > 0, prev, 0.0)
        ywin_ref[CONV_HALO:, :] = a_ref[...].astype(F32) * _sigmoid(g_ref[...].astype(F32))
        dwin_ref[0:tm, :] = d_ref[...]
        dwin_ref[tm:, :] = jnp.where(i < nt - 1, dn_ref[...], 0.0)
        _shifted_windows(ywin_ref, ysh_ref, sh_rows)
        _shifted_windows(dwin_ref, dsh_ref, sh_rows)

        def chunk(ci, carry):
            r0 = ci * CONV_ROWS
            rows = pl.ds(pl.multiple_of(r0, CONV_ROWS), CONV_ROWS)
            dcur = d_ref[rows, :]
            dyg = jnp.zeros((CONV_ROWS, cb), F32)
            for k in range(kw):
                dyg = dyg + w_ref[k:k + 1, :] * _window_rows(dwin_ref, dsh_ref, kw - 1 - k, r0, CONV_ROWS)
                accw_ref[k] += _fold_rows(dcur * _window_rows(ywin_ref, ysh_ref, off + k, r0, CONV_ROWS))
            av = a_ref[rows, :].astype(F32)
            sig = _sigmoid(g_ref[rows, :].astype(F32))
            da = dyg * sig
            dg = dyg * av * sig * (1.0 - sig)
            dp_ref[0, rows, :] = da.astype(BF)
            dp_ref[1, rows, :] = dg.astype(BF)
            acca_ref[...] += _fold_rows(da)
            accg_ref[...] += _fold_rows(dg)
            return carry
        lax.fori_loop(0, tm // CONV_ROWS, chunk, 0)

        @pl.when(i == nt - 1)
        def _():
            dw_ref[...] = jnp.sum(accw_ref[...], axis=1)
            dba_ref[...] = jnp.sum(acca_ref[...], axis=0, keepdims=True)
            dbg_ref[...] = jnp.sum(accg_ref[...], axis=0, keepdims=True)

    hpt = tm // CONV_HALO
    last_halo = t // CONV_HALO - 1
    tile = pl.BlockSpec((tm, cb), lambda j, i: (i, j))
    cur_g = pl.BlockSpec((tm, cb), lambda j, i: (i, ncb + j))
    nxt = pl.BlockSpec((CONV_HALO, cb), lambda j, i: (jnp.minimum((i + 1) * hpt, last_halo), j))
    prev_a = pl.BlockSpec((CONV_HALO, cb), lambda j, i: (jnp.maximum(i * hpt - 1, 0), j))
    prev_g = pl.BlockSpec((CONV_HALO, cb), lambda j, i: (jnp.maximum(i * hpt - 1, 0), ncb + j))
    vec = pl.BlockSpec((1, cb), lambda j, i: (0, j))
    dp, ddw, dba, dbg = _call(
        body, name=name, grid=(ncb, nt),
        in_specs=[tile, nxt, tile, cur_g, prev_a, prev_g, pl.BlockSpec((kw, cb), lambda j, i: (0, j))],
        out_specs=[pl.BlockSpec((2, tm, cb), lambda j, i: (0, i, j)), pl.BlockSpec((kpad, cb), lambda j, i: (0, j)),
                   vec, vec],
        out_shape=[_sds((2, t, cw), BF), _sds((kpad, cw), F32), _sds((1, cw), F32), _sds((1, cw), F32)],
        scratch=[pltpu.VMEM((tm + CONV_HALO, cb), F32), pltpu.VMEM((tm + CONV_HALO, cb), F32),
                 pltpu.VMEM((V7X_SUBLANES - 1, sh_rows, cb), F32), pltpu.VMEM((V7X_SUBLANES - 1, sh_rows, cb), F32),
                 pltpu.VMEM((kpad, V7X_SUBLANES, cb), F32), pltpu.VMEM((V7X_SUBLANES, cb), F32),
                 pltpu.VMEM((V7X_SUBLANES, cb), F32)],
        sem=("parallel", "arbitrary"))(dyc, dyc, p, p, p, p, dw_w)
    return dp, ddw[:kw], dba, dbg


def _adam_math(g, w, m, v):
    m2 = ADAM_B1 * m + (1.0 - ADAM_B1) * g
    v2 = ADAM_B2 * v + (1.0 - ADAM_B2) * (g * g)
    m_hat = m2 / (1.0 - ADAM_B1 ** ADAM_STEP)
    v_hat = v2 / (1.0 - ADAM_B2 ** ADAM_STEP)
    delta = -ADAM_LR * (m_hat / (jnp.sqrt(v_hat) + ADAM_EPS) + ADAM_WD * w)
    return delta, m2, v2


def _adamw(g_parts, w, m, v, name):
    r, c = w.shape
    tr = _tile(r, 256, V7X_SUBLANES)
    ng = len(g_parts)

    def body(*refs):
        g = refs[0][...].astype(F32)
        for s in refs[1:ng]:
            g = g + s[...].astype(F32)
        w_ref, m_ref, v_ref, go_ref, d_ref, mo_ref, vo_ref = refs[ng:]
        delta, m2, v2 = _adam_math(g, w_ref[...], m_ref[...], v_ref[...])
        go_ref[...] = g
        d_ref[...] = delta
        mo_ref[...] = m2
        vo_ref[...] = v2

    tile = pl.BlockSpec((tr, c), lambda i: (i, 0))
    in_specs = [pl.BlockSpec((None, tr, c), functools.partial(lambda s, i: (s, i, 0), s)) for _, s in g_parts]
    out = _sds((r, c), F32)
    return _call(body, name=name, grid=(r // tr,), in_specs=in_specs + [tile, tile, tile],
                 out_specs=[tile] * 4, out_shape=[out] * 4, sem=("parallel",))(*[a for a, _ in g_parts], w, m, v)


def _adamw_stacked(h, recv, chip, w_st, m_st, v_st, k, prev, name):
    kk, r, c = w_st.shape
    tr = _tile(r, 256, V7X_SUBLANES)
    if prev is None:
        prev = [lax.empty((kk, r, c), F32) for _ in range(4)]

    def body(chip_ref, h_ref, r0_ref, r1_ref, r2_ref, w_ref, m_ref, v_ref, pg, pd, pm, pv,
             go_ref, d_ref, mo_ref, vo_ref):
        g = (h_ref[...].astype(F32) + r0_ref[...].astype(F32)) + (r1_ref[...].astype(F32) + r2_ref[...].astype(F32))
        delta, m2, v2 = _adam_math(g, w_ref[...], m_ref[...], v_ref[...])
        go_ref[...] = g
        d_ref[...] = delta
        mo_ref[...] = m2
        vo_ref[...] = v2

    own = pl.BlockSpec((None, tr, c), lambda i, chip_ref: (chip_ref[0], i, 0))
    rcv = [pl.BlockSpec((None, tr, c), functools.partial(lambda s, i, chip_ref: (s, i, 0), s)) for s in range(3)]
    blk = pl.BlockSpec((None, tr, c), lambda i, chip_ref: (k, i, 0))
    out = _sds((kk, r, c), F32)
    return _call(body, name=name, grid=(r // tr,), in_specs=[own] + rcv + [blk, blk, blk] + [ANY] * 4,
                 out_specs=[blk] * 4, out_shape=[out] * 4, sem=("parallel",), prefetch=chip,
                 aliases={8: 0, 9: 1, 10: 2, 11: 3})(h, recv, recv, recv, w_st, m_st, v_st, *prev)


def _add_sibling(g4, land, core, name):
    n, _, r, c = g4.shape
    tr = _tile(r, 512, V7X_SUBLANES)

    def body(core_ref, a_ref, b_ref, o_ref):
        o_ref[...] = (a_ref[...].astype(F32) + b_ref[...].astype(F32)).astype(BF)

    return _call(body, name=name, grid=(n, r // tr),
                 in_specs=[pl.BlockSpec((None, None, tr, c), lambda p, i, core_ref: (p, core_ref[0], i, 0)),
                           pl.BlockSpec((None, None, tr, c), lambda p, i, core_ref: (p, 0, i, 0))],
                 out_specs=pl.BlockSpec((None, tr, c), lambda p, i, core_ref: (p, i, 0)),
                 out_shape=_sds((n, r, c), BF), sem=("parallel", "parallel"), prefetch=core)(g4, land)


def _cast_to_slot(w, lead, me, name, after=()):
    r, c = w.shape[-2:]
    nl = len(lead)
    tr = _tile(r, 512, 2 * V7X_SUBLANES)

    def body(me_ref, w_ref, o_ref):
        o_ref[...] = w_ref[...].astype(BF)

    return _call(body, name=name, grid=(r // tr,),
                 in_specs=[pl.BlockSpec((None,) * nl + (tr, c), lambda i, me_ref: tuple(lead) + (i, 0))],
                 out_specs=pl.BlockSpec((None, None, tr, c), lambda i, me_ref: (0, me_ref[0], i, 0)),
                 out_shape=_sds((1, N_DEV, r, c), BF), sem=("parallel",), prefetch=me, after=after)(w)


def _ada_fwd(c_pad, ada_w, ada_b, name):
    nl, d, cl = ada_w.shape
    rows = c_pad.shape[0]
    tn = _tile(cl, 256, V7X_LANES)

    def body(c_ref, w_ref, b_ref, o_ref):
        cv = c_ref[...]
        cond = (cv * _sigmoid(cv)).astype(BF)
        o_ref[...] = jnp.dot(cond, w_ref[...].astype(BF), preferred_element_type=F32) + b_ref[...]

    return _call(body, name=name, grid=(nl, cl // tn),
                 in_specs=[pl.BlockSpec((rows, d), lambda l, j: (0, 0)),
                           pl.BlockSpec((None, d, tn), lambda l, j: (l, 0, j)),
                           pl.BlockSpec((None, 1, tn), lambda l, j: (l, 0, j))],
                 out_specs=pl.BlockSpec((None, rows, tn), lambda l, j: (l, 0, j)),
                 out_shape=_sds((nl, rows, cl), F32), sem=("parallel", "parallel"))(c_pad, ada_w, ada_b)


def _ada_bwd(c_pad, dmod, w, m, v, name):
    nl, d, cl = w.shape
    rows = c_pad.shape[0]
    tn = _tile(cl, 256, V7X_LANES)

    def body(c_ref, dm_ref, w_ref, m_ref, v_ref, go_ref, d_ref, mo_ref, vo_ref):
        cv = c_ref[...]
        cond = (cv * _sigmoid(cv)).astype(BF)
        g = lax.dot_general(cond, dm_ref[...].astype(BF), (((0,), (0,)), ((), ())), preferred_element_type=F32)
        delta, m2, v2 = _adam_math(g, w_ref[...], m_ref[...], v_ref[...])
        go_ref[...] = g
        d_ref[...] = delta
        mo_ref[...] = m2
        vo_ref[...] = v2

    tile = pl.BlockSpec((None, d, tn), lambda l, j: (l, 0, j))
    out = _sds((nl, d, cl), F32)
    return _call(body, name=name, grid=(nl, cl // tn),
                 in_specs=[pl.BlockSpec((rows, d), lambda l, j: (0, 0)),
                           pl.BlockSpec((None, rows, tn), lambda l, j: (l, 0, j)), tile, tile, tile],
                 out_specs=[tile] * 4, out_shape=[out] * 4, sem=("parallel", "parallel"))(c_pad, dmod, w, m, v)


def _sum_devices(parts, name):
    n, r, c = parts.shape
    tr = _tile(r, 512, V7X_SUBLANES)

    def body(p_ref, o_ref):
        acc = p_ref[0]
        for k in range(1, n):
            acc = acc + p_ref[k]
        o_ref[...] = acc

    return _call(body, name=name, grid=(r // tr,), in_specs=[pl.BlockSpec((n, tr, c), lambda i: (0, i, 0))],
                 out_specs=pl.BlockSpec((tr, c), lambda i: (i, 0)), out_shape=_sds((r, c), F32),
                 sem=("parallel",))(parts)


def _mesh_pos():
    return lax.axis_index("x"), lax.axis_index("y"), lax.axis_index("c")


def _other_chips(x, y):
    return [(1 - x, y), (x, 1 - y), (1 - x, 1 - y)]


def _all_gather(arrs, name):
    n = len(arrs)

    def body(*refs):
        ins, outs = refs[:n], refs[n:2 * n]
        send_sems, recv_sems, local_sems = refs[2 * n:]
        x, y, c = _mesh_pos()
        me, sibling = (x, y, c), (x, y, 1 - c)
        chips = _other_chips(x, y)

        def slot(a, pos):
            px, py, pc = pos
            return outs[a].at[:, pl.ds(4 * px + 2 * py + pc, 1)]

        def copy(a, k, block, to, src=None):
            return pltpu.make_async_remote_copy(
                src_ref=slot(a, block) if src is None else src, dst_ref=slot(a, block),
                send_sem=send_sems.at[a, k], recv_sem=recv_sems.at[a, k], device_id=to, device_id_type=MESH)

        mine = [pltpu.make_async_copy(ins[a], slot(a, me), local_sems.at[a]) for a in range(n)]
        for cp in mine:
            cp.start()
        first = []
        for a in range(n):
            first.append(copy(a, 0, me, sibling, src=ins[a]))
            first += [copy(a, 1 + j, me, (*chip, c), src=ins[a]) for j, chip in enumerate(chips)]
        for cp in first:
            cp.start()
        passed = []
        for a in range(n):
            for j, chip in enumerate(chips):
                copy(a, 1 + j, (*chip, c), me).wait_recv()
                fwd = copy(a, 4 + j, (*chip, c), sibling)
                fwd.start()
                passed.append(fwd)
        for a in range(n):
            copy(a, 0, sibling, me).wait_recv()
            for j, chip in enumerate(chips):
                copy(a, 4 + j, (*chip, 1 - c), me).wait_recv()
        for cp in first + passed:
            cp.wait_send()
        for cp in mine:
            cp.wait()

    out_shape = [_sds((a.shape[0], N_DEV) + a.shape[2:], a.dtype) for a in arrs]
    return pl.pallas_call(
        body, out_shape=out_shape, in_specs=[ANY] * n, out_specs=[ANY] * n, name=name,
        scratch_shapes=[pltpu.SemaphoreType.DMA((n, N_DEV - 1)), pltpu.SemaphoreType.DMA((n, N_DEV - 1)),
                        pltpu.SemaphoreType.DMA((n,))])(*arrs)


HBM = pl.BlockSpec(memory_space=pltpu.HBM)
SEM = pl.BlockSpec(memory_space=pltpu.SEMAPHORE)


def _hbm(v):
    return pltpu.with_memory_space_constraint(v, pltpu.HBM)


def _comm_call(body, name, bufs, sems_in, sems_out):
    after = [] if not sems_in or _Seq.last is None or any(_Seq.last is b for b in bufs) else [_Seq.last]
    nb, ni, na, no = len(bufs), len(sems_in), len(after), len(sems_out)

    def wrapped(*refs):
        body(refs[:nb], refs[nb:nb + ni], refs[nb + ni + na:nb + ni + na + no])
        if no:
            refs[-1][...] = jnp.zeros_like(refs[-1])

    out_shape = [pltpu.SemaphoreType.DMA(s) for s in sems_out] + [pltpu.HBM(b.shape, b.dtype) for b in bufs]
    out_specs = [SEM] * no + [HBM] * nb
    if no:
        out_shape.append(_sds((V7X_SUBLANES, V7X_LANES), F32))
        out_specs.append(pl.BlockSpec(memory_space=pltpu.VMEM))
    res = pl.pallas_call(
        wrapped, name=name, out_shape=out_shape, in_specs=[HBM] * nb + [SEM] * ni + [ANY] * na, out_specs=out_specs,
        input_output_aliases={i: no + i for i in range(nb)},
        compiler_params=pltpu.CompilerParams(has_side_effects=pltpu.SideEffectType.DATAFLOW_SIDE_EFFECTING),
    )(*bufs, *sems_in, *after)
    out_bufs = list(res[no:no + nb])
    if no:
        _Seq.tokens.append(res[-1])
    _Seq.last = out_bufs[0]
    return list(res[:no]), out_bufs


def _remote(src, dst, send_sem, recv_sem, to):
    return pltpu.make_async_remote_copy(src_ref=src, dst_ref=dst, send_sem=send_sem, recv_sem=recv_sem,
                                        device_id=to, device_id_type=MESH)


def _slot(ref, pos):
    px, py, pc = pos
    return ref.at[:, pl.ds(4 * px + 2 * py + pc, 1)]


def _ag_start(bufs, name):
    n = len(bufs)

    def body(b, _, sems):
        send_sib, recv_sib, send_ici, recv_ici = sems
        x, y, c = _mesh_pos()
        for a in range(n):
            mine = _slot(b[a], (x, y, c))
            _remote(mine, mine, send_sib.at[a], recv_sib.at[a], (x, y, 1 - c)).start()
            for j, (px, py) in enumerate(_other_chips(x, y)):
                _remote(mine, mine, send_ici.at[3 * a + j], recv_ici.at[3 * a + j], (px, py, c)).start()

    sems, bufs = _comm_call(body, name, [_hbm(b) for b in bufs], [], [(n,), (n,), (3 * n,), (3 * n,)])
    return dict(bufs=bufs, send_sib=sems[0], recv_sib=sems[1], send_ici=sems[2], recv_ici=sems[3])


def _ag_mid(st, name):
    n = len(st["bufs"])

    def body(b, sems_in, sems):
        (recv_ici,) = sems_in
        send_fwd, recv_fwd = sems
        x, y, c = _mesh_pos()
        for a in range(n):
            for j, (px, py) in enumerate(_other_chips(x, y)):
                blk = _slot(b[a], (px, py, c))
                _remote(blk, blk, send_fwd.at[3 * a + j], recv_ici.at[3 * a + j], (x, y, 1 - c)).wait_recv()
                _remote(blk, blk, send_fwd.at[3 * a + j], recv_fwd.at[3 * a + j], (x, y, 1 - c)).start()

    sems, bufs = _comm_call(body, name, st["bufs"], [st["recv_ici"]], [(3 * n,), (3 * n,)])
    return dict(st, bufs=bufs, send_fwd=sems[0], recv_fwd=sems[1])


def _ag_end(st, name):
    n = len(st["bufs"])

    def body(b, sems_in, _):
        send_sib, recv_sib, send_ici, send_fwd, recv_fwd = sems_in
        x, y, c = _mesh_pos()
        sibling = (x, y, 1 - c)
        for a in range(n):
            mine, sib_blk = _slot(b[a], (x, y, c)), _slot(b[a], sibling)
            _remote(mine, mine, send_sib.at[a], recv_sib.at[a], sibling).wait_send()
            _remote(sib_blk, sib_blk, send_sib.at[a], recv_sib.at[a], sibling).wait_recv()
            for j, (px, py) in enumerate(_other_chips(x, y)):
                blk, sib_got = _slot(b[a], (px, py, c)), _slot(b[a], (px, py, 1 - c))
                _remote(mine, mine, send_ici.at[3 * a + j], recv_sib.at[a], (px, py, c)).wait_send()
                _remote(blk, blk, send_fwd.at[3 * a + j], recv_fwd.at[3 * a + j], sibling).wait_send()
                _remote(sib_got, sib_got, send_fwd.at[3 * a + j], recv_fwd.at[3 * a + j], sibling).wait_recv()

    _, bufs = _comm_call(body, name, st["bufs"],
                         [st[k] for k in ("send_sib", "recv_sib", "send_ici", "send_fwd", "recv_fwd")], [])
    return bufs


def _rs_start(g4s, name):
    n = len(g4s)
    lands = [lax.empty((N_CHIP, 1) + g.shape[2:], g.dtype) for g in g4s]

    def body(b, _, sems):
        send, recv = sems
        x, y, c = _mesh_pos()
        for a in range(n):
            _remote(b[a].at[:, pl.ds(1 - c, 1)], b[n + a], send.at[a], recv.at[a], (x, y, 1 - c)).start()

    sems, bufs = _comm_call(body, name, [_hbm(v) for v in list(g4s) + lands], [], [(n,), (n,)])
    return dict(bufs=bufs, send=sems[0], recv=sems[1])


def _rs_mid(st, name):
    n = len(st["bufs"]) // 2

    def body(b, sems_in, _):
        send, recv = sems_in
        x, y, c = _mesh_pos()
        for a in range(n):
            cp = _remote(b[a].at[:, pl.ds(1 - c, 1)], b[n + a], send.at[a], recv.at[a], (x, y, 1 - c))
            cp.wait_send()
            cp.wait_recv()

    _, bufs = _comm_call(body, name, st["bufs"], [st["send"], st["recv"]], [])
    return bufs[:n], bufs[n:]


def _rs_start2(sums, name):
    n = len(sums)
    lands = [lax.empty((N_CHIP - 1,) + s.shape[1:], s.dtype) for s in sums]

    def body(b, _, sems):
        send, recv = sems
        x, y, c = _mesh_pos()
        for a in range(n):
            for j, (px, py) in enumerate(_other_chips(x, y)):
                _remote(b[a].at[pl.ds(2 * px + py, 1)], b[n + a].at[pl.ds(j, 1)], send.at[3 * a + j], recv.at[3 * a + j],
                        (px, py, c)).start()

    sems, bufs = _comm_call(body, name, [_hbm(v) for v in list(sums) + lands], [], [(3 * n,), (3 * n,)])
    return dict(bufs=bufs, send=sems[0], recv=sems[1])


def _rs_end(st, name):
    n = len(st["bufs"]) // 2

    def body(b, sems_in, _):
        send, recv = sems_in
        x, y, c = _mesh_pos()
        for a in range(n):
            for j, (px, py) in enumerate(_other_chips(x, y)):
                cp = _remote(b[a].at[pl.ds(2 * px + py, 1)], b[n + a].at[pl.ds(j, 1)], send.at[3 * a + j], recv.at[3 * a + j],
                             (px, py, c))
                cp.wait_send()
                cp.wait_recv()

    _, bufs = _comm_call(body, name, st["bufs"], [st["send"], st["recv"]], [])
    return bufs[:n], bufs[n:]


def _pack(parts, rows_align=V7X_SUBLANES):
    flat, total = [], 0
    for p in parts:
        v = p.reshape(-1).astype(F32)
        pad = -v.shape[0] % PACK_ALIGN
        flat.append(jnp.pad(v, (0, pad)) if pad else v)
        total += v.shape[0] + pad
    tail = -total % (rows_align * V7X_LANES)
    if tail:
        flat.append(jnp.zeros((tail,), F32))
    return jnp.concatenate(flat).reshape(-1, V7X_LANES)


def _unpack(buf, shapes):
    lead = buf.shape[:-2]
    flat = buf.reshape(lead + (-1,))
    out, pos = [], 0
    for s in shapes:
        size = 1
        for d in s:
            size *= d
        out.append(flat[..., pos:pos + size].reshape(lead + tuple(s)))
        pos += size + (-size % PACK_ALIGN)
    return out


def kernel(x, c, ada_w, ada_b, norm_g, ffn_w_in, ffn_w_out, gm_w_in, gm_ln_g, gm_ln_b, gm_ws, gm_bs, gm_w_out, cv_w_in, cv_b_in, cv_dw_w, cv_dw_b, cv_ln_g, cv_ln_b, cv_w_out, cv_b_out, final_g, loss_target, m_ada_w, m_ada_b, m_norm_g, m_ffn_w_in, m_ffn_w_out, m_gm_w_in, m_gm_ln_g, m_gm_ln_b, m_gm_ws, m_gm_bs, m_gm_w_out, m_cv_w_in, m_cv_b_in, m_cv_dw_w, m_cv_dw_b, m_cv_ln_g, m_cv_ln_b, m_cv_w_out, m_cv_b_out, m_final_g, v_ada_w, v_ada_b, v_norm_g, v_ffn_w_in, v_ffn_w_out, v_gm_w_in, v_gm_ln_g, v_gm_ln_b, v_gm_ws, v_gm_bs, v_gm_w_out, v_cv_w_in, v_cv_b_in, v_cv_dw_w, v_cv_dw_b, v_cv_ln_g, v_cv_ln_b, v_cv_w_out, v_cv_b_out, v_final_g):
    t, d = x.shape[1], x.shape[2]
    depth = ada_w.shape[0]
    assert depth == 2 and ffn_w_in.shape[:2] == (2, 2) and gm_w_in.shape[0] == 1 and cv_w_in.shape[0] == 1
    dl = d // N_DEV
    bn = ffn_w_in.shape[3]
    fl = ffn_w_out.shape[2]
    f = fl * N_DEV
    el = gm_w_in.shape[2]
    e = el * N_DEV // 2
    hn, l = gm_ws.shape[1], gm_ws.shape[2]
    kw = cv_dw_w.shape[1]
    cl = ada_w.shape[2]
    me = 4 * lax.axis_index("x") + 2 * lax.axis_index("y") + lax.axis_index("c")
    me1 = me.astype(jnp.int32).reshape(1)
    chip1 = (2 * lax.axis_index("x") + lax.axis_index("y")).astype(jnp.int32).reshape(1)
    core1 = lax.axis_index("c").astype(jnp.int32).reshape(1)
    _Seq.last, _Seq.tokens = None, []

    xs = x[0]
    tgt = loss_target[0]

    ag_groups = [("win00", [(ffn_w_in, (0, 0))]), ("wout00", [(ffn_w_out, (0, 0))]),
                 ("gm", [(gm_w_in, (0,)), (gm_w_out, (0,))]),
                 ("win01", [(ffn_w_in, (0, 1))]), ("wout01", [(ffn_w_out, (0, 1))]),
                 ("win10", [(ffn_w_in, (1, 0))]), ("wout10", [(ffn_w_out, (1, 0))]),
                 ("cv", [(cv_w_in, (0,)), (cv_w_out, (0,))]),
                 ("win11", [(ffn_w_in, (1, 1))]), ("wout11", [(ffn_w_out, (1, 1))])]
    ag_flight = {}

    def ag_start(gi, after=()):
        gname, members = ag_groups[gi]
        bufs = [_cast_to_slot(w, lead, me1, name=f"cast_{gname}_{k}", after=after) for k, (w, lead) in enumerate(members)]
        ag_flight[gi] = _ag_start(bufs, name=f"ag_start_{gname}")

    def ag_forward(gi):
        if gi in ag_flight and "send_fwd" not in ag_flight[gi]:
            ag_flight[gi] = _ag_mid(ag_flight[gi], name=f"ag_mid_{ag_groups[gi][0]}")

    def ag_take(gi):
        ag_forward(gi)
        bufs = _ag_end(ag_flight.pop(gi), name=f"ag_end_{ag_groups[gi][0]}")
        ag_forward(gi + 1)
        if gi + 2 < len(ag_groups):
            ag_start(gi + 2)
        return [b[0] for b in bufs]

    small_in = [c, norm_g, cv_b_in, cv_dw_w, cv_dw_b, cv_ln_g, cv_ln_b, cv_b_out]
    pack1 = _pack(small_in)
    (pack1_all,) = _all_gather([pack1[None, None]], name="ag_small")
    parts = _unpack(pack1_all[0], [s.shape for s in small_in])
    c_all = parts[0].reshape(N_DEV, d)
    ng_full = jnp.moveaxis(parts[1], 0, 2).reshape(depth, 3, d)
    cvb_in_full = parts[2].reshape(1, 2 * e)
    dww_full = jnp.moveaxis(parts[3][:, 0], 0, 1).reshape(kw, e)
    dwb_full, cln_g_full, cln_b_full, cvb_out_full = [p.reshape(1, d) for p in parts[4:8]]

    c_pad = jnp.pad(c_all, ((0, 16 - N_DEV), (0, 0)))
    ada_b_loc = lax.dynamic_slice_in_dim(ada_b, me * cl, cl, axis=1).reshape(depth, 1, cl)
    mod_part = _ada_fwd(c_pad, ada_w, ada_b_loc, name="ada_fwd")[:, :N_DEV]
    (mod_all,) = _all_gather([_pack([mod_part])[None, None]], name="ag_mod")
    mod_all = _unpack(mod_all[0], [mod_part.shape])[0]
    mod_mine = lax.dynamic_index_in_dim(mod_all, me, axis=2, keepdims=False)
    mod = jnp.moveaxis(mod_mine, 0, 1).reshape(depth, 3, 3, 1, d)

    ag_start(0, after=[mod_all])
    ag_start(1, after=[mod_all])

    ws = gm_ws[0]
    bsb = jnp.broadcast_to(gm_bs[0][:, :, None], (hn, l, e // hn))
    gm_g, gm_b = gm_ln_g, gm_ln_b

    saved = []
    xcur = xs
    next_group = 0
    for i in range(depth):
        for s in range(3):
            shift, scale, gate = mod[i, s, 0], mod[i, s, 1], mod[i, s, 2]
            g_norm = ng_full[i, s][None]
            tag = f"l{i}s{s}"
            h = _norm_mod(xcur, g_norm, scale, shift, name=f"norm_mod_{tag}")
            if s != 1:
                (w_in_blk,) = ag_take(next_group)
                gg, uu, act = _ffn_in(h, w_in_blk, 0, name=f"ffn_in_{tag}")
                w_out3 = ag_take(next_group + 1)[0].reshape(1, f, d)
                next_group += 2
                xnext, yv = _out_proj(act, w_out3, 0, xcur, gate, None, 0.5, name=f"ffn_out_{tag}")
                saved.append(dict(x=xcur, h=h, g=gg, u=uu, y=yv, w_in=w_in_blk, w_out=w_out3))
            elif i % 2 == 0:
                gm_in_blk, gm_out = ag_take(next_group)
                gm_out3 = gm_out.reshape(1, e, d)
                next_group += 1
                pre = _in_proj(h, gm_in_blk, None, name=f"gm_in_{tag}")
                uu, vn = _gm_act(pre, gm_g, gm_b, name=f"gm_act_{tag}")
                sg = _sgu_fwd(uu, vn, ws, bsb, name=f"sgu_fwd_{tag}")
                xnext, yv = _out_proj(sg, gm_out3, 0, xcur, gate, None, 1.0, name=f"gm_out_{tag}")
                saved.append(dict(x=xcur, h=h, pre=pre, u=uu, vn=vn, sg=sg, y=yv, w_in=gm_in_blk, w_out=gm_out3))
            else:
                cv_in_blk, cv_out = ag_take(next_group)
                cv_out3 = cv_out.reshape(1, e, d)
                next_group += 1
                p = _in_proj(h, cv_in_blk, cvb_in_full, name=f"cv_in_{tag}")
                yc = _dwconv_fwd(p, dww_full, dwb_full, name=f"dwconv_fwd_{tag}")
                ys = _cv_act(yc, cln_g_full, cln_b_full, name=f"cv_act_{tag}")
                xnext, yv = _out_proj(ys, cv_out3, 0, xcur, gate, cvb_out_full, 1.0, name=f"cv_out_{tag}")
                saved.append(dict(x=xcur, h=h, p=p, yc=yc, ys=ys, y=yv, w_in=cv_in_blk, w_out=cv_out3))
            xcur = xnext

    sq, dx, d_final_g = _final_loss(xcur, tgt, final_g[None], name="final_loss")
    loss = lax.psum(0.5 / d * jnp.sum(sq), ("x", "y", "c"))

    dmod = [[[None] * 3 for _ in range(3)] for _ in range(depth)]
    d_norm_g = [[None] * 3 for _ in range(depth)]
    small = {}

    stacked = {
        "ffn_w_in": [a.reshape(4, d, bn) for a in (ffn_w_in, m_ffn_w_in, v_ffn_w_in)],
        "ffn_w_out": [a.reshape(4, fl, d) for a in (ffn_w_out, m_ffn_w_out, v_ffn_w_out)],
        "gm_w_in": [gm_w_in, m_gm_w_in, v_gm_w_in], "gm_w_out": [gm_w_out, m_gm_w_out, v_gm_w_out],
        "cv_w_in": [cv_w_in, m_cv_w_in, v_cv_w_in], "cv_w_out": [cv_w_out, m_cv_w_out, v_cv_w_out],
    }
    res_big = {}

    def rs_sibling(g4s, tag):
        return _rs_start(g4s, name=f"rs_start_{tag}"), tag

    def rs_chips(flight):
        st, tag = flight
        g4s, lands = _rs_mid(st, name=f"rs_mid_{tag}")
        sums = [_add_sibling(g4, land, core1, name=f"rs_add_{tag}_{k}") for k, (g4, land) in enumerate(zip(g4s, lands))]
        return _rs_start2(sums, name=f"rs_start2_{tag}"), tag

    def rs_finish(flight, targets):
        st, tag = flight
        sums, recvs = _rs_end(st, name=f"rs_end_{tag}")
        for (pname, k), hsum, recv in zip(targets, sums, recvs):
            w_st, m_st, v_st = stacked[pname]
            res_big[pname] = _adamw_stacked(hsum, recv, chip1, w_st, m_st, v_st, k, res_big.get(pname),
                                            name=f"adamw_{pname}_{k}")

    pending = []
    last_sibling = None
    for i in reversed(range(depth)):
        for s in reversed(range(3)):
            sv = saved[3 * i + s]
            shift, scale, gate = mod[i, s, 0], mod[i, s, 1], mod[i, s, 2]
            g_norm = ng_full[i, s][None]
            tag = f"l{i}s{s}"
            if s != 1:
                widx = 2 * i + s // 2
                dy, (dgate,) = _residual_bwd(dx, sv["y"], gate, 0.5, False, name=f"res_bwd_{tag}")
                dgu, act = _ffn_da(dy, sv["w_out"], 0, sv["g"], sv["u"], name=f"ffn_da_{tag}")
                g_out = _mm_tn(act, dy[None], bn, d, False, name=f"ffn_dwout_{tag}").reshape(N_CHIP, 2, fl, d)
                sib_out = rs_sibling([g_out], f"{tag}_out")
                g_in = _mm_tn(sv["h"], dgu, d, bn, True, name=f"ffn_dwin_{tag}").reshape(N_CHIP, 2, d, bn)
                new_flights = [(rs_chips(sib_out), [("ffn_w_out", widx)])]
                sib = rs_sibling([g_in], f"{tag}_in")
                targets = [("ffn_w_in", widx)]
                dh = _mm_nt_blocks(dgu, sv["w_in"], 0, name=f"ffn_dh_{tag}")
            elif i % 2 == 0:
                dy, (dgate,) = _residual_bwd(dx, sv["y"], gate, 1.0, False, name=f"res_bwd_{tag}")
                ds = _mm_nt(dy, sv["w_out"], 0, name=f"gm_ds_{tag}")
                g_out = _mm_tn(sv["sg"], dy[None], _tile(e, 1024, V7X_LANES), d, False,
                               name=f"gm_dwout_{tag}").reshape(N_CHIP, 2, dl, d)
                du, dvn, dws, dbs = _sgu_bwd(ds, sv["u"], sv["vn"], ws, bsb, name=f"sgu_bwd_{tag}")
                dpre, dlng, dlnb = _gm_act_bwd(sv["pre"], du, dvn, gm_g, name=f"gm_act_bwd_{tag}")
                small["gm_ln_g"], small["gm_ln_b"] = dlng, dlnb
                small["gm_ws"], small["gm_bs"] = dws, dbs[:, :, 0]
                g_in = _mm_tn(sv["h"], dpre[None], d, el, True, name=f"gm_dwin_{tag}").reshape(N_CHIP, 2, d, el)
                targets, new_flights = [("gm_w_in", 0), ("gm_w_out", 0)], []
                sib = rs_sibling([g_in, g_out], tag)
                dh = _mm_nt_blocks(dpre[None], sv["w_in"], 0, name=f"gm_dh_{tag}")
            else:
                dy, (dgate, dbout) = _residual_bwd(dx, sv["y"], gate, 1.0, True, name=f"res_bwd_{tag}")
                dys = _mm_nt(dy, sv["w_out"], 0, name=f"cv_dys_{tag}")
                g_out = _mm_tn(sv["ys"], dy[None], _tile(e, 1024, V7X_LANES), d, False,
                               name=f"cv_dwout_{tag}").reshape(N_CHIP, 2, dl, d)
                dyc, dlng, dlnb, ddwb = _cv_act_bwd(dys, sv["yc"], cln_g_full, cln_b_full, name=f"cv_act_bwd_{tag}")
                dp, ddww, dba, dbg = _dwconv_bwd(dyc, sv["p"], dww_full, name=f"dwconv_bwd_{tag}")
                small["cv_b_out"], small["cv_ln_g"], small["cv_ln_b"], small["cv_dw_b"] = dbout, dlng, dlnb, ddwb
                small["cv_dw_w"] = ddww
                small["cv_b_in"] = jnp.concatenate([dba, dbg], axis=1)
                g_in = _mm_tn(sv["h"], dp, d, el, True, name=f"cv_dwin_{tag}").reshape(N_CHIP, 2, d, el)
                targets, new_flights = [("cv_w_in", 0), ("cv_w_out", 0)], []
                sib = rs_sibling([g_in, g_out], tag)
                dh = _mm_nt_blocks(dp, sv["w_in"], 0, name=f"cv_dh_{tag}")
            if i == 0 and s == 0:
                last_sibling = (sib, targets)
            else:
                new_flights.append((rs_chips(sib), targets))
            dx, dscale, dshift, dgn = _norm_mod_bwd(dh, sv["x"], dx, g_norm, scale, name=f"norm_mod_bwd_{tag}")
            dmod[i][s] = [dshift, dscale, dgate]
            d_norm_g[i][s] = dgn
            for flight in pending:
                rs_finish(*flight)
            pending = new_flights
    grad_x = dx[None]

    dmod_mine = jnp.concatenate([v for per_l in dmod for per_s in per_l for v in per_s], axis=1)
    dng_mine = jnp.concatenate([v for per_l in d_norm_g for v in per_l], axis=1)
    small_out = [dmod_mine, dng_mine, small["gm_ln_g"], small["gm_ln_b"], small["gm_ws"], small["gm_bs"],
                 small["cv_b_in"], small["cv_dw_w"], small["cv_dw_b"], small["cv_ln_g"], small["cv_ln_b"],
                 small["cv_b_out"], d_final_g]
    shapes2 = [s.shape for s in small_out]
    (pack2_all,) = _all_gather([_pack(small_out, rows_align=256)[None, None]], name="ag_small_grads")
    _Seq.last = pack2_all
    pending.append((rs_chips(last_sibling[0]), last_sibling[1]))
    summed = _unpack(_sum_devices(pack2_all[0], name="sum_small_grads"), shapes2)
    dmod_all = _unpack(pack2_all[0], shapes2)[0].reshape(N_DEV, depth, 9 * d)

    def my_cols(full, width):
        return lax.dynamic_slice_in_dim(full, me * width, width, axis=full.ndim - 1)

    g_ada_b = summed[0].reshape(depth, 9 * d)
    g_norm_g = my_cols(summed[1].reshape(depth, 3, d), dl)
    g_small = {
        "ada_b": g_ada_b, "norm_g": g_norm_g,
        "gm_ln_g": summed[2], "gm_ln_b": summed[3], "gm_ws": summed[4][None], "gm_bs": summed[5][None],
        "cv_b_in": my_cols(summed[6], el), "cv_dw_w": my_cols(summed[7], dl)[None],
        "cv_dw_b": my_cols(summed[8], dl), "cv_ln_g": my_cols(summed[9], dl), "cv_ln_b": my_cols(summed[10], dl),
        "cv_b_out": my_cols(summed[11], dl), "final_g": summed[12].reshape(d),
    }

    dm_loc = jnp.moveaxis(my_cols(dmod_all, cl), 0, 1)
    dm_loc = jnp.pad(dm_loc, ((0, 0), (0, 16 - N_DEV), (0, 0)))
    res_ada_w = _ada_bwd(c_pad, dm_loc, ada_w, m_ada_w, v_ada_w, name="ada_bwd_adamw")

    def flat2(a):
        return a.reshape(-1, a.shape[-1])

    small_params = {
        "ada_b": (ada_b, m_ada_b, v_ada_b), "norm_g": (norm_g, m_norm_g, v_norm_g),
        "gm_ln_g": (gm_ln_g, m_gm_ln_g, v_gm_ln_g), "gm_ln_b": (gm_ln_b, m_gm_ln_b, v_gm_ln_b),
        "gm_ws": (gm_ws, m_gm_ws, v_gm_ws), "gm_bs": (gm_bs, m_gm_bs, v_gm_bs),
        "cv_b_in": (cv_b_in, m_cv_b_in, v_cv_b_in), "cv_dw_w": (cv_dw_w, m_cv_dw_w, v_cv_dw_w),
        "cv_dw_b": (cv_dw_b, m_cv_dw_b, v_cv_dw_b), "cv_ln_g": (cv_ln_g, m_cv_ln_g, v_cv_ln_g),
        "cv_ln_b": (cv_ln_b, m_cv_ln_b, v_cv_ln_b), "cv_b_out": (cv_b_out, m_cv_b_out, v_cv_b_out),
        "final_g": (final_g, m_final_g, v_final_g),
    }
    res_small = {}
    for key, (w, m, v) in small_params.items():
        g2 = flat2(g_small[key].reshape(w.shape)) if w.ndim > 1 else g_small[key].reshape(1, -1)
        w2, m2, v2 = [flat2(a) if a.ndim > 1 else a.reshape(1, -1) for a in (w, m, v)]
        res_small[key] = [o.reshape(w.shape) for o in _adamw([(g2[None], 0)], w2, m2, v2, name=f"adamw_{key}")]

    for flight in pending:
        rs_finish(*flight)

    def big(name, k):
        if name == "ada_w":
            return res_ada_w[k]
        return res_big[name][k].reshape(stacked_shape[name])

    stacked_shape = {"ffn_w_in": ffn_w_in.shape, "ffn_w_out": ffn_w_out.shape, "gm_w_in": gm_w_in.shape,
                     "gm_w_out": gm_w_out.shape, "cv_w_in": cv_w_in.shape, "cv_w_out": cv_w_out.shape}

    order = ["ada_w", "ada_b", "norm_g", "ffn_w_in", "ffn_w_out", "gm_w_in", "gm_ln_g", "gm_ln_b", "gm_ws", "gm_bs",
             "gm_w_out", "cv_w_in", "cv_b_in", "cv_dw_w", "cv_dw_b", "cv_ln_g", "cv_ln_b", "cv_w_out", "cv_b_out",
             "final_g"]
    outs = [loss, grad_x]
    for k in range(4):
        for name in order:
            outs.append(res_small[name][k] if name in res_small else big(name, k))
    return tuple(outs)
```

```python
import functools

import jax
import jax.numpy as jnp
from jax import lax
from jax.experimental import pallas as pl
from jax.experimental.pallas import tpu as pltpu

F32 = jnp.float32
BF = jnp.bfloat16
MESH = pl.DeviceIdType.MESH

N_DEV = 8
N_CHIP = 4
NORM_EPS = 1e-6
ADAM_LR = 0.001
ADAM_B1 = 0.9
ADAM_B2 = 0.999
ADAM_EPS = 1e-08
ADAM_WD = 0.01
ADAM_STEP = 10

V7X_SUBLANES = 8
V7X_LANES = 128
PACK_ALIGN = V7X_SUBLANES * V7X_LANES
V7X_VMEM_LIMIT = 56 * 1024 * 1024


def _tile(n, pref, align):
    if n <= pref:
        return n
    t = pref - pref % align
    while t >= align:
        if n % t == 0:
            return t
        t -= align
    return n


ANY = pl.BlockSpec(memory_space=pl.ANY)


class _Seq:
    last = None
    tokens = []


def _call(body, *, name, grid, in_specs, out_specs, out_shape, scratch=(), sem=None, prefetch=None, aliases=None,
          after=()):
    def run(*args):
        tokens, _Seq.tokens = _Seq.tokens + list(after), []
        lead = 0 if prefetch is None else 1
        n_in, n_tok = lead + len(args), len(tokens)

        def wrapped(*refs):
            body(*refs[:n_in], *refs[n_in + n_tok:])

        specs = list(in_specs) + [ANY] * n_tok
        params = pltpu.CompilerParams(dimension_semantics=sem, vmem_limit_bytes=V7X_VMEM_LIMIT)
        if prefetch is None:
            res = pl.pallas_call(wrapped, out_shape=out_shape, grid=grid, in_specs=specs, out_specs=out_specs,
                                 scratch_shapes=scratch, name=name, compiler_params=params,
                                 input_output_aliases=aliases or {})(*args, *tokens)
        else:
            grid_spec = pltpu.PrefetchScalarGridSpec(num_scalar_prefetch=1, grid=grid, in_specs=specs,
                                                     out_specs=out_specs, scratch_shapes=scratch)
            res = pl.pallas_call(wrapped, out_shape=out_shape, grid_spec=grid_spec, name=name,
                                 compiler_params=params, input_output_aliases=aliases or {})(prefetch, *args, *tokens)
        _Seq.last = res[0] if isinstance(res, (list, tuple)) else res
        return res
    return run


def _sds(shape, dtype):
    return jax.ShapeDtypeStruct(tuple(shape), dtype)


def _sigmoid(v):
    return 1.0 / (1.0 + jnp.exp(-v))


def _erf(v):
    a = jnp.abs(v)
    t = 1.0 / (1.0 + 0.3275911 * a)
    poly = t * (0.254829592 + t * (-0.284496736 + t * (1.421413741 + t * (-1.453152027 + t * 1.061405429))))
    r = 1.0 - poly * jnp.exp(-a * a)
    return jnp.where(v < 0, -r, r)


def _gelu(v):
    return 0.5 * v * (1.0 + _erf(v * 0.7071067811865476))


def _gelu_grad(v):
    cdf = 0.5 * (1.0 + _erf(v * 0.7071067811865476))
    pdf = 0.3989422804014327 * jnp.exp(-0.5 * v * v)
    return cdf + v * pdf


def _fold_rows(val):
    rows, w = val.shape
    return val.reshape(rows // V7X_SUBLANES, V7X_SUBLANES, w).sum(axis=0)


def _rowwise(fn, name, rows_in, vecs_in, rows_out, acc_widths, tm=256):
    t = rows_in[0].shape[0]
    tm = _tile(t, tm, V7X_SUBLANES)
    steps = t // tm
    nr, nv, no, na = len(rows_in), len(vecs_in), len(rows_out), len(acc_widths)

    def body(*refs):
        rin, vin = refs[:nr], refs[nr:nr + nv]
        rout = refs[nr + nv:nr + nv + no]
        aout = refs[nr + nv + no:nr + nv + no + na]
        accs = refs[nr + nv + no + na:]
        i = pl.program_id(0)
        outs, acc_vals = fn(*[r[...] for r in rin], *[v[...] for v in vin])
        for r, o in zip(rout, outs):
            r[...] = o.astype(r.dtype)
        if na:
            @pl.when(i == 0)
            def _():
                for a in accs:
                    a[...] = jnp.zeros_like(a)

            for a, val in zip(accs, acc_vals):
                a[...] += _fold_rows(val)

            @pl.when(i == steps - 1)
            def _():
                for o, a in zip(aout, accs):
                    o[...] = jnp.sum(a[...], axis=0, keepdims=True)

    in_specs = [pl.BlockSpec((tm, r.shape[1]), lambda i: (i, 0)) for r in rows_in]
    in_specs += [pl.BlockSpec(v.shape, functools.partial(lambda nd, i: (0,) * nd, v.ndim)) for v in vecs_in]
    out_specs = [pl.BlockSpec((tm, r.shape[1]), lambda i: (i, 0)) for r in rows_out]
    out_specs += [pl.BlockSpec((1, w), lambda i: (0, 0)) for w in acc_widths]
    out_shape = list(rows_out) + [_sds((1, w), F32) for w in acc_widths]
    scratch = [pltpu.VMEM((V7X_SUBLANES, w), F32) for w in acc_widths]
    res = _call(body, name=name, grid=(steps,), in_specs=in_specs, out_specs=out_specs, out_shape=out_shape,
                scratch=scratch, sem=("arbitrary",) if na else ("parallel",))(*rows_in, *vecs_in)
    return res[:no], res[no:]


def _norm_mod(x, g, scale, shift, name):
    def fn(xv, gv, sc, sh):
        r = lax.rsqrt(jnp.mean(xv * xv, axis=-1, keepdims=True) + NORM_EPS)
        return ((xv * r * gv) * (1.0 + sc) + sh,), ()
    (h,), _ = _rowwise(fn, name, [x], [g, scale, shift], [_sds(x.shape, BF)], [])
    return h


def _residual_bwd(dxp, y, gate, coef, with_colsum, name):
    def fn(dv, yv, gt):
        dy = (coef * gt) * dv
        accs = (coef * dv * yv.astype(F32),)
        if with_colsum:
            accs += (dy,)
        return (dy,), accs
    d = dxp.shape[1]
    (dy,), accs = _rowwise(fn, name, [dxp, y], [gate], [_sds(dxp.shape, BF)], [d, d] if with_colsum else [d])
    return dy, accs


def _final_loss(x, target, g, name):
    d = x.shape[1]

    def fn(xv, tv, gv):
        r = lax.rsqrt(jnp.mean(xv * xv, axis=-1, keepdims=True) + NORM_EPS)
        xhat = xv * r
        err = xhat * gv - tv
        dy = err * (1.0 / d)
        dxhat = dy * gv
        dx = r * (dxhat - xhat * jnp.mean(dxhat * xhat, axis=-1, keepdims=True))
        return (dx,), (err * err, dy * xhat)
    (dx,), (sq, dg) = _rowwise(fn, name, [x, target], [g], [_sds(x.shape, F32)], [d, d])
    return sq, dx, dg


def _gm_act(pre, ln_g, ln_b, name):
    e = pre.shape[1] // 2

    def fn(pv, gv, bv):
        p = pv.astype(F32)
        u = _gelu(p[:, :e])
        v = _gelu(p[:, e:])
        mu = jnp.mean(v, axis=-1, keepdims=True)
        vc = v - mu
        rstd = lax.rsqrt(jnp.mean(vc * vc, axis=-1, keepdims=True) + NORM_EPS)
        return (u, vc * rstd * gv + bv), ()
    t = pre.shape[0]
    (u, vn), _ = _rowwise(fn, name, [pre], [ln_g, ln_b], [_sds((t, e), BF), _sds((t, e), BF)], [])
    return u, vn


def _gm_act_bwd(pre, du, dvn, ln_g, name):
    e = pre.shape[1] // 2

    def fn(pv, duv, dvv, gv):
        p = pv.astype(F32)
        pu, pvv = p[:, :e], p[:, e:]
        v = _gelu(pvv)
        mu = jnp.mean(v, axis=-1, keepdims=True)
        vc = v - mu
        rstd = lax.rsqrt(jnp.mean(vc * vc, axis=-1, keepdims=True) + NORM_EPS)
        vhat = vc * rstd
        dvn_f = dvv.astype(F32)
        dvhat = dvn_f * gv
        dv = rstd * (dvhat - jnp.mean(dvhat, axis=-1, keepdims=True)
                     - vhat * jnp.mean(dvhat * vhat, axis=-1, keepdims=True))
        dpu = duv.astype(F32) * _gelu_grad(pu)
        dpv = dv * _gelu_grad(pvv)
        return (jnp.concatenate([dpu, dpv], axis=1),), (dvn_f * vhat, dvn_f)
    (dpre,), (dg, db) = _rowwise(fn, name, [pre, du, dvn], [ln_g], [_sds(pre.shape, BF)], [e, e], tm=128)
    return dpre, dg, db


def _cv_act(yc, ln_g, ln_b, name):
    def fn(yv, gv, bv):
        mu = jnp.mean(yv, axis=-1, keepdims=True)
        c = yv - mu
        rstd = lax.rsqrt(jnp.mean(c * c, axis=-1, keepdims=True) + NORM_EPS)
        yn = c * rstd * gv + bv
        return (yn * _sigmoid(yn),), ()
    (ys,), _ = _rowwise(fn, name, [yc], [ln_g, ln_b], [_sds(yc.shape, BF)], [])
    return ys


def _cv_act_bwd(dys, yc, ln_g, ln_b, name):
    def fn(dv, yv, gv, bv):
        mu = jnp.mean(yv, axis=-1, keepdims=True)
        c = yv - mu
        rstd = lax.rsqrt(jnp.mean(c * c, axis=-1, keepdims=True) + NORM_EPS)
        yhat = c * rstd
        yn = yhat * gv + bv
        sig = _sigmoid(yn)
        dyn = dv.astype(F32) * (sig * (1.0 + yn * (1.0 - sig)))
        dyhat = dyn * gv
        dyc = rstd * (dyhat - jnp.mean(dyhat, axis=-1, keepdims=True)
                      - yhat * jnp.mean(dyhat * yhat, axis=-1, keepdims=True))
        return (dyc,), (dyn * yhat, dyn, dyc)
    cw = yc.shape[1]
    (dyc,), (dg, db, dbias) = _rowwise(fn, name, [dys, yc], [ln_g, ln_b], [_sds(yc.shape, F32)], [cw, cw, cw])
    return dyc, dg, db, dbias


MM_SEG_ROWS = 256


def _ffn_in(h, w_blk, blk0, name):
    t, d = h.shape
    bn = w_blk.shape[2]
    half = N_DEV // 2
    f = half * bn
    tm = _tile(t, 512, V7X_SUBLANES)
    seg_rows = _tile(tm, MM_SEG_ROWS, 2 * V7X_SUBLANES)

    def body(h_ref, wg_ref, wu_ref, g_ref, u_ref, a_ref):
        for seg in range(tm // seg_rows):
            rows = pl.ds(seg * seg_rows, seg_rows)
            hv = h_ref[rows, :]
            g = jnp.dot(hv, wg_ref[...], preferred_element_type=F32)
            u = jnp.dot(hv, wu_ref[...], preferred_element_type=F32)
            g_ref[rows, :] = g.astype(BF)
            u_ref[rows, :] = u.astype(BF)
            a_ref[rows, :] = (g * _sigmoid(g) * u).astype(BF)

    out = _sds((t, f), BF)
    tile = pl.BlockSpec((tm, bn), lambda j, i: (i, j))
    return _call(
        body, name=name, grid=(half, t // tm),
        in_specs=[pl.BlockSpec((tm, d), lambda j, i: (i, 0)),
                  pl.BlockSpec((None, d, bn), lambda j, i: (blk0 + j, 0, 0)),
                  pl.BlockSpec((None, d, bn), lambda j, i: (blk0 + half + j, 0, 0))],
        out_specs=[tile, tile, tile], out_shape=[out, out, out], sem=("parallel", "parallel"))(h, w_blk, w_blk)


def _in_proj(h, w_blk, bias, name):
    t, d = h.shape
    bn = w_blk.shape[2]
    tm = _tile(t, 1024, V7X_SUBLANES)

    def body(*refs):
        if bias is None:
            h_ref, w_ref, o_ref = refs
            o_ref[...] = jnp.dot(h_ref[...], w_ref[...], preferred_element_type=F32).astype(BF)
        else:
            h_ref, w_ref, b_ref, o_ref = refs
            o_ref[...] = (jnp.dot(h_ref[...], w_ref[...], preferred_element_type=F32) + b_ref[...]).astype(BF)

    in_specs = [pl.BlockSpec((tm, d), lambda j, i: (i, 0)), pl.BlockSpec((None, d, bn), lambda j, i: (j, 0, 0))]
    args = [h, w_blk]
    if bias is not None:
        in_specs.append(pl.BlockSpec((1, bn), lambda j, i: (0, j)))
        args.append(bias)
    return _call(body, name=name, grid=(N_DEV, t // tm), in_specs=in_specs,
                 out_specs=pl.BlockSpec((tm, bn), lambda j, i: (i, j)), out_shape=_sds((t, N_DEV * bn), BF),
                 sem=("parallel", "parallel"))(*args)


def _out_proj(a, w3, widx, x, gate, bias, coef, name):
    t, k = a.shape
    d = w3.shape[2]
    tm = _tile(t, 512, V7X_SUBLANES)
    tn = _tile(d, 512, V7X_LANES)

    def body(*refs):
        if bias is None:
            a_ref, w_ref, x_ref, g_ref, xo_ref, y_ref = refs
            y = jnp.dot(a_ref[...], w_ref[...], preferred_element_type=F32)
        else:
            a_ref, w_ref, x_ref, g_ref, b_ref, xo_ref, y_ref = refs
            y = jnp.dot(a_ref[...], w_ref[...], preferred_element_type=F32) + b_ref[...]
        y_ref[...] = y.astype(BF)
        xo_ref[...] = x_ref[...] + (coef * g_ref[...]) * y

    tile = pl.BlockSpec((tm, tn), lambda j, i: (i, j))
    vec = pl.BlockSpec((1, tn), lambda j, i: (0, j))
    in_specs = [pl.BlockSpec((tm, k), lambda j, i: (i, 0)),
                pl.BlockSpec((None, k, tn), lambda j, i: (widx, 0, j)), tile, vec]
    args = [a, w3, x, gate]
    if bias is not None:
        in_specs.append(vec)
        args.append(bias)
    return _call(body, name=name, grid=(d // tn, t // tm), in_specs=in_specs, out_specs=[tile, tile],
                 out_shape=[_sds((t, d), F32), _sds((t, d), BF)], sem=("parallel", "parallel"))(*args)


def _ffn_da(dy, w3, widx, g, u, name):
    t, d = dy.shape
    f = w3.shape[1]
    bn = f // (N_DEV // 2)
    tm = _tile(t, 512, V7X_SUBLANES)
    seg_rows = _tile(tm, MM_SEG_ROWS, 2 * V7X_SUBLANES)

    def body(dy_ref, w_ref, g_ref, u_ref, dgu_ref, a_ref):
        for seg in range(tm // seg_rows):
            rows = pl.ds(seg * seg_rows, seg_rows)
            da = lax.dot_general(dy_ref[rows, :], w_ref[...], (((1,), (1,)), ((), ())), preferred_element_type=F32)
            gv = g_ref[rows, :].astype(F32)
            uv = u_ref[rows, :].astype(F32)
            sig = _sigmoid(gv)
            sl = gv * sig
            dgu_ref[0, rows, :] = (da * uv * (sig * (1.0 + gv * (1.0 - sig)))).astype(BF)
            dgu_ref[1, rows, :] = (da * sl).astype(BF)
            a_ref[rows, :] = (sl * uv).astype(BF)

    tile = pl.BlockSpec((tm, bn), lambda j, i: (i, j))
    return _call(
        body, name=name, grid=(f // bn, t // tm),
        in_specs=[pl.BlockSpec((tm, d), lambda j, i: (i, 0)),
                  pl.BlockSpec((None, bn, d), lambda j, i: (widx, j, 0)), tile, tile],
        out_specs=[pl.BlockSpec((2, tm, bn), lambda j, i: (0, i, j)), tile],
        out_shape=[_sds((2, t, f), BF), _sds((t, f), BF)], sem=("parallel", "parallel"))(dy, w3, g, u)


def _mm_nt(dy, w3, widx, name):
    t, k = dy.shape
    n = w3.shape[1]
    tm = _tile(t, 512, V7X_SUBLANES)
    tn = _tile(n, 1024, V7X_LANES)

    def body(dy_ref, w_ref, o_ref):
        o_ref[...] = lax.dot_general(dy_ref[...], w_ref[...], (((1,), (1,)), ((), ())),
                                     preferred_element_type=F32).astype(BF)

    return _call(body, name=name, grid=(n // tn, t // tm),
                 in_specs=[pl.BlockSpec((tm, k), lambda j, i: (i, 0)),
                           pl.BlockSpec((None, tn, k), lambda j, i: (widx, j, 0))],
                 out_specs=pl.BlockSpec((tm, tn), lambda j, i: (i, j)), out_shape=_sds((t, n), BF),
                 sem=("parallel", "parallel"))(dy, w3)


NORM_BWD_ROWS = 128


def _dh_norm_bwd(z3, w_blk, x, dxp, g, scale, name):
    lead, t, _ = z3.shape
    d, bn = w_blk.shape[1], w_blk.shape[2]
    per = N_DEV // lead
    tm = _tile(t, 512, NORM_BWD_ROWS)
    ni = t // tm

    def body(z_ref, w_ref, x_ref, dp_ref, g_ref, sc_ref, dx_ref, dscale_ref, dshift_ref, dg_ref,
             acc_ref, a_scale, a_shift, a_g):
        i, k = pl.program_id(0), pl.program_id(1)

        @pl.when(k == 0)
        def _():
            acc_ref[...] = jnp.zeros_like(acc_ref)

        @pl.when((i == 0) & (k == 0))
        def _():
            a_scale[...] = jnp.zeros_like(a_scale)
            a_shift[...] = jnp.zeros_like(a_shift)
            a_g[...] = jnp.zeros_like(a_g)

        acc_ref[...] += lax.dot_general(z_ref[...], w_ref[...], (((1,), (1,)), ((), ())),
                                        preferred_element_type=F32)

        @pl.when(k == N_DEV - 1)
        def _():
            gv, sc = g_ref[...], sc_ref[...]

            def chunk(ci, carry):
                rows = pl.ds(pl.multiple_of(ci * NORM_BWD_ROWS, NORM_BWD_ROWS), NORM_BWD_ROWS)
                dh, xv = acc_ref[rows, :], x_ref[rows, :]
                r = lax.rsqrt(jnp.mean(xv * xv, axis=-1, keepdims=True) + NORM_EPS)
                xhat = xv * r
                dn = dh * (1.0 + sc)
                dxhat = dn * gv
                dx_ref[rows, :] = r * (dxhat - xhat * jnp.mean(dxhat * xhat, axis=-1, keepdims=True)) + dp_ref[rows, :]
                a_scale[...] += _fold_rows(dh * (xhat * gv))
                a_shift[...] += _fold_rows(dh)
                a_g[...] += _fold_rows(dn * xhat)
                return carry
            lax.fori_loop(0, tm // NORM_BWD_ROWS, chunk, 0)

        @pl.when((i == ni - 1) & (k == N_DEV - 1))
        def _():
            dscale_ref[...] = jnp.sum(a_scale[...], axis=0, keepdims=True)
            dshift_ref[...] = jnp.sum(a_shift[...], axis=0, keepdims=True)
            dg_ref[...] = jnp.sum(a_g[...], axis=0, keepdims=True)

    rows = pl.BlockSpec((tm, d), lambda i, k: (i, 0))
    vec = pl.BlockSpec((1, d), lambda i, k: (0, 0))
    return _call(body, name=name, grid=(ni, N_DEV),
                 in_specs=[pl.BlockSpec((None, tm, bn), lambda i, k: (k // per, i, k % per)),
                           pl.BlockSpec((None, d, bn), lambda i, k: (k, 0, 0)), rows, rows, vec, vec],
                 out_specs=[rows, vec, vec, vec],
                 out_shape=[_sds((t, d), F32), _sds((1, d), F32), _sds((1, d), F32), _sds((1, d), F32)],
                 scratch=[pltpu.VMEM((tm, d), F32)] + [pltpu.VMEM((V7X_SUBLANES, d), F32)] * 3,
                 sem=("arbitrary", "arbitrary"))(z3, w_blk, x, dxp, g, scale)


def _mm_tn(a, b3, ta, tb, blocked, name):
    t, ka = a.shape
    lead, _, w = b3.shape
    per = w // tb
    nj = lead * per
    tk = _tile(t, 512, V7X_SUBLANES)
    nk = t // tk

    def body(a_ref, b_ref, o_ref, acc_ref):
        k = pl.program_id(2)

        @pl.when(k == 0)
        def _():
            acc_ref[...] = jnp.zeros_like(acc_ref)

        acc_ref[...] += lax.dot_general(a_ref[...], b_ref[...], (((0,), (0,)), ((), ())),
                                        preferred_element_type=F32)

        @pl.when(k == nk - 1)
        def _():
            o_ref[...] = acc_ref[...].astype(BF)

    if blocked:
        out_shape = _sds((nj, ka, tb), BF)
        out_spec = pl.BlockSpec((None, ta, tb), lambda i, j, k: (j, i, 0))
    else:
        out_shape = _sds((1, ka, w), BF)
        out_spec = pl.BlockSpec((None, ta, tb), lambda i, j, k: (0, i, j))
    return _call(body, name=name, grid=(ka // ta, nj, nk),
                 in_specs=[pl.BlockSpec((tk, ta), lambda i, j, k: (k, i)),
                           pl.BlockSpec((None, tk, tb), lambda i, j, k: (j // per, k, j % per))],
                 out_specs=out_spec, out_shape=out_shape, scratch=[pltpu.VMEM((ta, tb), F32)],
                 sem=("parallel", "parallel", "arbitrary"))(a, b3)


def _causal(ws):
    l = ws.shape[0]
    row = lax.broadcasted_iota(jnp.int32, (l, l), 0)
    col = lax.broadcasted_iota(jnp.int32, (l, l), 1)
    return jnp.where(col <= row, ws, 0.0)


def _sgu_fwd(u, vn, ws, bsb, name):
    t, e = u.shape
    hn, l, _ = ws.shape
    dh = e // hn
    nc = t // l

    def body(u_ref, v_ref, ws_ref, bs_ref, s_ref):
        wsc = _causal(ws_ref[...]).astype(BF)
        bias = bs_ref[...]

        def chunk(c, carry):
            rows = pl.ds(pl.multiple_of(c * l, l), l)
            vo = jnp.dot(wsc, v_ref[rows, :], preferred_element_type=F32) + bias
            s_ref[rows, :] = (u_ref[rows, :].astype(F32) * vo).astype(BF)
            return carry
        lax.fori_loop(0, nc, chunk, 0)

    col = pl.BlockSpec((t, dh), lambda h: (0, h))
    return _call(body, name=name, grid=(hn,),
                 in_specs=[col, col, pl.BlockSpec((None, l, l), lambda h: (h, 0, 0)),
                           pl.BlockSpec((None, l, dh), lambda h: (h, 0, 0))],
                 out_specs=col, out_shape=_sds((t, e), BF), sem=("parallel",))(u, vn, ws, bsb)


def _sgu_bwd(ds, u, vn, ws, bsb, name):
    t, e = u.shape
    hn, l, _ = ws.shape
    dh = e // hn
    nc = t // l

    def body(ds_ref, u_ref, v_ref, ws_ref, bs_ref, du_ref, dv_ref, dws_ref, dbs_ref, accw_ref, accb_ref):
        wsc = _causal(ws_ref[...]).astype(BF)
        bias = bs_ref[...]
        accw_ref[...] = jnp.zeros_like(accw_ref)
        accb_ref[...] = jnp.zeros_like(accb_ref)

        def chunk(c, carry):
            rows = pl.ds(pl.multiple_of(c * l, l), l)
            vc = v_ref[rows, :]
            dsv = ds_ref[rows, :].astype(F32)
            vo = jnp.dot(wsc, vc, preferred_element_type=F32) + bias
            du_ref[rows, :] = (dsv * vo).astype(BF)
            dvo = dsv * u_ref[rows, :].astype(F32)
            dvo_b = dvo.astype(BF)
            accb_ref[...] += dvo
            accw_ref[...] += lax.dot_general(dvo_b, vc, (((1,), (1,)), ((), ())), preferred_element_type=F32)
            dv_ref[rows, :] = lax.dot_general(wsc, dvo_b, (((0,), (0,)), ((), ())),
                                              preferred_element_type=F32).astype(BF)
            return carry
        lax.fori_loop(0, nc, chunk, 0)
        dws_ref[...] = _causal(accw_ref[...])
        dbs_ref[...] = jnp.broadcast_to(jnp.sum(accb_ref[...], axis=1, keepdims=True), (l, dh))

    col = pl.BlockSpec((t, dh), lambda h: (0, h))
    return _call(body, name=name, grid=(hn,),
                 in_specs=[col, col, col, pl.BlockSpec((None, l, l), lambda h: (h, 0, 0)),
                           pl.BlockSpec((None, l, dh), lambda h: (h, 0, 0))],
                 out_specs=[col, col, pl.BlockSpec((None, l, l), lambda h: (h, 0, 0)),
                            pl.BlockSpec((None, l, dh), lambda h: (h, 0, 0))],
                 out_shape=[_sds((t, e), BF), _sds((t, e), BF), _sds((hn, l, l), F32), _sds((hn, l, dh), F32)],
                 scratch=[pltpu.VMEM((l, l), F32), pltpu.VMEM((l, dh), F32)],
                 sem=("parallel",))(ds, u, vn, ws, bsb)


AG_AHEAD = 3

CONV_HALO = 32
CONV_ROWS = 64
CONV_LANES = 256


def _shifted_windows(win_ref, sh_ref, rows):
    for b in range(1, V7X_SUBLANES):
        sh_ref[b - 1, 0:rows, :] = win_ref[b:b + rows, :]


def _window_rows(win_ref, sh_ref, shift, r0, rows):
    a, b = divmod(shift, V7X_SUBLANES)
    start = pl.multiple_of(r0 + V7X_SUBLANES * a, V7X_SUBLANES)
    if b == 0:
        return win_ref[pl.ds(start, rows), :]
    return sh_ref[b - 1, pl.ds(start, rows), :]


def _dwconv_fwd(p, dw_w, dw_b, name):
    t, c2 = p.shape
    cw = c2 // 2
    kw = dw_w.shape[0]
    cb = _tile(cw, CONV_LANES, V7X_LANES)
    ncb = cw // cb
    tm = _tile(t, 512, CONV_ROWS)
    off = CONV_HALO - (kw - 1)

    def body(a_ref, g_ref, ap_ref, gp_ref, w_ref, b_ref, o_ref, win_ref, sh_ref):
        i = pl.program_id(1)
        prev = ap_ref[...].astype(F32) * _sigmoid(gp_ref[...].astype(F32))
        win_ref[0:CONV_HALO, :] = jnp.where(i > 0, prev, 0.0)
        win_ref[CONV_HALO:, :] = a_ref[...].astype(F32) * _sigmoid(g_ref[...].astype(F32))
        _shifted_windows(win_ref, sh_ref, tm + CONV_HALO - V7X_SUBLANES)

        def chunk(ci, carry):
            r0 = ci * CONV_ROWS
            acc = jnp.zeros((CONV_ROWS, cb), F32) + b_ref[...]
            for k in range(kw):
                acc = acc + w_ref[k:k + 1, :] * _window_rows(win_ref, sh_ref, off + k, r0, CONV_ROWS)
            o_ref[pl.ds(pl.multiple_of(r0, CONV_ROWS), CONV_ROWS), :] = acc
            return carry
        lax.fori_loop(0, tm // CONV_ROWS, chunk, 0)

    hpt = tm // CONV_HALO
    cur_a = pl.BlockSpec((tm, cb), lambda j, i: (i, j))
    cur_g = pl.BlockSpec((tm, cb), lambda j, i: (i, ncb + j))
    prev_a = pl.BlockSpec((CONV_HALO, cb), lambda j, i: (jnp.maximum(i * hpt - 1, 0), j))
    prev_g = pl.BlockSpec((CONV_HALO, cb), lambda j, i: (jnp.maximum(i * hpt - 1, 0), ncb + j))
    return _call(body, name=name, grid=(ncb, t // tm),
                 in_specs=[cur_a, cur_g, prev_a, prev_g, pl.BlockSpec((kw, cb), lambda j, i: (0, j)),
                           pl.BlockSpec((1, cb), lambda j, i: (0, j))],
                 out_specs=pl.BlockSpec((tm, cb), lambda j, i: (i, j)), out_shape=_sds((t, cw), F32),
                 scratch=[pltpu.VMEM((tm + CONV_HALO, cb), F32),
                          pltpu.VMEM((V7X_SUBLANES - 1, tm + CONV_HALO - V7X_SUBLANES, cb), F32)],
                 sem=("parallel", "parallel"))(p, p, p, p, dw_w, dw_b)


def _dwconv_bwd(dyc, p, dw_w, name):
    t, c2 = p.shape
    cw = c2 // 2
    kw = dw_w.shape[0]
    cb = _tile(cw, CONV_LANES, V7X_LANES)
    ncb = cw // cb
    tm = _tile(t, 512, CONV_ROWS)
    nt = t // tm
    off = CONV_HALO - (kw - 1)
    kpad = -(-kw // V7X_SUBLANES) * V7X_SUBLANES
    sh_rows = tm + CONV_HALO - V7X_SUBLANES

    def body(d_ref, dn_ref, a_ref, g_ref, ap_ref, gp_ref, w_ref,
             dp_ref, dw_ref, dba_ref, dbg_ref, dwin_ref, ywin_ref, dsh_ref, ysh_ref, accw_ref, acca_ref, accg_ref):
        i = pl.program_id(1)

        @pl.when(i == 0)
        def _():
            accw_ref[...] = jnp.zeros_like(accw_ref)
            acca_ref[...] = jnp.zeros_like(acca_ref)
            accg_ref[...] = jnp.zeros_like(accg_ref)

        prev = ap_ref[...].astype(F32) * _sigmoid(gp_ref[...].astype(F32))
        ywin_ref[0:CONV_HALO, :] = jnp.where(i > 0, prev, 0.0)
        ywin_ref[CONV_HALO:, :] = a_ref[...].astype(F32) * _sigmoid(g_ref[...].astype(F32))
        dwin_ref[0:tm, :] = d_ref[...]
        dwin_ref[tm:, :] = jnp.where(i < nt - 1, dn_ref[...], 0.0)
        _shifted_windows(ywin_ref, ysh_ref, sh_rows)
        _shifted_windows(dwin_ref, dsh_ref, sh_rows)

        def chunk(ci, carry):
            r0 = ci * CONV_ROWS
            rows = pl.ds(pl.multiple_of(r0, CONV_ROWS), CONV_ROWS)
            dcur = d_ref[rows, :]
            dyg = jnp.zeros((CONV_ROWS, cb), F32)
            for k in range(kw):
                dyg = dyg + w_ref[k:k + 1, :] * _window_rows(dwin_ref, dsh_ref, kw - 1 - k, r0, CONV_ROWS)
                accw_ref[k] += _fold_rows(dcur * _window_rows(ywin_ref, ysh_ref, off + k, r0, CONV_ROWS))
            av = a_ref[rows, :].astype(F32)
            sig = _sigmoid(g_ref[rows, :].astype(F32))
            da = dyg * sig
            dg = dyg * av * sig * (1.0 - sig)
            dp_ref[0, rows, :] = da.astype(BF)
            dp_ref[1, rows, :] = dg.astype(BF)
            acca_ref[...] += _fold_rows(da)
            accg_ref[...] += _fold_rows(dg)
            return carry
        lax.fori_loop(0, tm // CONV_ROWS, chunk, 0)

        @pl.when(i == nt - 1)
        def _():
            dw_ref[...] = jnp.sum(accw_ref[...], axis=1)
            dba_ref[...] = jnp.sum(acca_ref[...], axis=0, keepdims=True)
            dbg_ref[...] = jnp.sum(accg_ref[...], axis=0, keepdims=True)

    hpt = tm // CONV_HALO
    last_halo = t // CONV_HALO - 1
    tile = pl.BlockSpec((tm, cb), lambda j, i: (i, j))
    cur_g = pl.BlockSpec((tm, cb), lambda j, i: (i, ncb + j))
    nxt = pl.BlockSpec((CONV_HALO, cb), lambda j, i: (jnp.minimum((i + 1) * hpt, last_halo), j))
    prev_a = pl.BlockSpec((CONV_HALO, cb), lambda j, i: (jnp.maximum(i * hpt - 1, 0), j))
    prev_g = pl.BlockSpec((CONV_HALO, cb), lambda j, i: (jnp.maximum(i * hpt - 1, 0), ncb + j))
    vec = pl.BlockSpec((1, cb), lambda j, i: (0, j))
    dp, ddw, dba, dbg = _call(
        body, name=name, grid=(ncb, nt),
        in_specs=[tile, nxt, tile, cur_g, prev_a, prev_g, pl.BlockSpec((kw, cb), lambda j, i: (0, j))],
        out_specs=[pl.BlockSpec((2, tm, cb), lambda j, i: (0, i, j)), pl.BlockSpec((kpad, cb), lambda j, i: (0, j)),
                   vec, vec],
        out_shape=[_sds((2, t, cw), BF), _sds((kpad, cw), F32), _sds((1, cw), F32), _sds((1, cw), F32)],
        scratch=[pltpu.VMEM((tm + CONV_HALO, cb), F32), pltpu.VMEM((tm + CONV_HALO, cb), F32),
                 pltpu.VMEM((V7X_SUBLANES - 1, sh_rows, cb), F32), pltpu.VMEM((V7X_SUBLANES - 1, sh_rows, cb), F32),
                 pltpu.VMEM((kpad, V7X_SUBLANES, cb), F32), pltpu.VMEM((V7X_SUBLANES, cb), F32),
                 pltpu.VMEM((V7X_SUBLANES, cb), F32)],
        sem=("parallel", "arbitrary"))(dyc, dyc, p, p, p, p, dw_w)
    return dp, ddw[:kw], dba, dbg


def _adam_math(g, w, m, v):
    m2 = ADAM_B1 * m + (1.0 - ADAM_B1) * g
    v2 = ADAM_B2 * v + (1.0 - ADAM_B2) * (g * g)
    m_hat = m2 / (1.0 - ADAM_B1 ** ADAM_STEP)
    v_hat = v2 / (1.0 - ADAM_B2 ** ADAM_STEP)
    delta = -ADAM_LR * (m_hat / (jnp.sqrt(v_hat) + ADAM_EPS) + ADAM_WD * w)
    return delta, m2, v2


def _adamw(g_parts, w, m, v, name):
    r, c = w.shape
    tr = _tile(r, 256, V7X_SUBLANES)
    ng = len(g_parts)

    def body(*refs):
        g = refs[0][...].astype(F32)
        for s in refs[1:ng]:
            g = g + s[...].astype(F32)
        w_ref, m_ref, v_ref, go_ref, d_ref, mo_ref, vo_ref = refs[ng:]
        delta, m2, v2 = _adam_math(g, w_ref[...], m_ref[...], v_ref[...])
        go_ref[...] = g
        d_ref[...] = delta
        mo_ref[...] = m2
        vo_ref[...] = v2

    tile = pl.BlockSpec((tr, c), lambda i: (i, 0))
    in_specs = [pl.BlockSpec((None, tr, c), functools.partial(lambda s, i: (s, i, 0), s)) for _, s in g_parts]
    out = _sds((r, c), F32)
    return _call(body, name=name, grid=(r // tr,), in_specs=in_specs + [tile, tile, tile],
                 out_specs=[tile] * 4, out_shape=[out] * 4, sem=("parallel",))(*[a for a, _ in g_parts], w, m, v)


def _adamw_stacked(h, recv, chip, w_st, m_st, v_st, k, prev, name):
    kk, r, c = w_st.shape
    tr = _tile(r, 256, V7X_SUBLANES)
    if prev is None:
        prev = [lax.empty((kk, r, c), F32) for _ in range(4)]

    def body(chip_ref, h_ref, r0_ref, r1_ref, r2_ref, w_ref, m_ref, v_ref, pg, pd, pm, pv,
             go_ref, d_ref, mo_ref, vo_ref):
        g = (h_ref[...].astype(F32) + r0_ref[...].astype(F32)) + (r1_ref[...].astype(F32) + r2_ref[...].astype(F32))
        delta, m2, v2 = _adam_math(g, w_ref[...], m_ref[...], v_ref[...])
        go_ref[...] = g
        d_ref[...] = delta
        mo_ref[...] = m2
        vo_ref[...] = v2

    own = pl.BlockSpec((None, tr, c), lambda i, chip_ref: (chip_ref[0], i, 0))
    rcv = [pl.BlockSpec((None, tr, c), functools.partial(lambda s, i, chip_ref: (s, i, 0), s)) for s in range(3)]
    blk = pl.BlockSpec((None, tr, c), lambda i, chip_ref: (k, i, 0))
    out = _sds((kk, r, c), F32)
    return _call(body, name=name, grid=(r // tr,), in_specs=[own] + rcv + [blk, blk, blk] + [ANY] * 4,
                 out_specs=[blk] * 4, out_shape=[out] * 4, sem=("parallel",), prefetch=chip,
                 aliases={8: 0, 9: 1, 10: 2, 11: 3})(h, recv, recv, recv, w_st, m_st, v_st, *prev)


def _add_sibling(g4, land, core, name):
    n, _, r, c = g4.shape
    tr = _tile(r, 512, V7X_SUBLANES)

    def body(core_ref, a_ref, b_ref, o_ref):
        o_ref[...] = (a_ref[...].astype(F32) + b_ref[...].astype(F32)).astype(BF)

    return _call(body, name=name, grid=(n, r // tr),
                 in_specs=[pl.BlockSpec((None, None, tr, c), lambda p, i, core_ref: (p, core_ref[0], i, 0)),
                           pl.BlockSpec((None, None, tr, c), lambda p, i, core_ref: (p, 0, i, 0))],
                 out_specs=pl.BlockSpec((None, tr, c), lambda p, i, core_ref: (p, i, 0)),
                 out_shape=_sds((n, r, c), BF), sem=("parallel", "parallel"), prefetch=core)(g4, land)


def _cast_to_slot(w, lead, me, name, after=()):
    r, c = w.shape[-2:]
    nl = len(lead)
    tr = _tile(r, 512, 2 * V7X_SUBLANES)

    def body(me_ref, w_ref, o_ref):
        o_ref[...] = w_ref[...].astype(BF)

    return _call(body, name=name, grid=(r // tr,),
                 in_specs=[pl.BlockSpec((None,) * nl + (tr, c), lambda i, me_ref: tuple(lead) + (i, 0))],
                 out_specs=pl.BlockSpec((None, None, tr, c), lambda i, me_ref: (0, me_ref[0], i, 0)),
                 out_shape=_sds((1, N_DEV, r, c), BF), sem=("parallel",), prefetch=me, after=after)(w)


def _ada_fwd(c_pad, ada_w, ada_b, name):
    nl, d, cl = ada_w.shape
    rows = c_pad.shape[0]
    tn = _tile(cl, 256, V7X_LANES)

    def body(c_ref, w_ref, b_ref, o_ref):
        cv = c_ref[...]
        cond = (cv * _sigmoid(cv)).astype(BF)
        o_ref[...] = jnp.dot(cond, w_ref[...].astype(BF), preferred_element_type=F32) + b_ref[...]

    return _call(body, name=name, grid=(nl, cl // tn),
                 in_specs=[pl.BlockSpec((rows, d), lambda l, j: (0, 0)),
                           pl.BlockSpec((None, d, tn), lambda l, j: (l, 0, j)),
                           pl.BlockSpec((None, 1, tn), lambda l, j: (l, 0, j))],
                 out_specs=pl.BlockSpec((None, rows, tn), lambda l, j: (l, 0, j)),
                 out_shape=_sds((nl, rows, cl), F32), sem=("parallel", "parallel"))(c_pad, ada_w, ada_b)


def _ada_bwd(c_pad, dmod, w, m, v, name):
    nl, d, cl = w.shape
    rows = c_pad.shape[0]
    tn = _tile(cl, 256, V7X_LANES)

    def body(c_ref, dm_ref, w_ref, m_ref, v_ref, go_ref, d_ref, mo_ref, vo_ref):
        cv = c_ref[...]
        cond = (cv * _sigmoid(cv)).astype(BF)
        g = lax.dot_general(cond, dm_ref[...].astype(BF), (((0,), (0,)), ((), ())), preferred_element_type=F32)
        delta, m2, v2 = _adam_math(g, w_ref[...], m_ref[...], v_ref[...])
        go_ref[...] = g
        d_ref[...] = delta
        mo_ref[...] = m2
        vo_ref[...] = v2

    tile = pl.BlockSpec((None, d, tn), lambda l, j: (l, 0, j))
    out = _sds((nl, d, cl), F32)
    return _call(body, name=name, grid=(nl, cl // tn),
                 in_specs=[pl.BlockSpec((rows, d), lambda l, j: (0, 0)),
                           pl.BlockSpec((None, rows, tn), lambda l, j: (l, 0, j)), tile, tile, tile],
                 out_specs=[tile] * 4, out_shape=[out] * 4, sem=("parallel", "parallel"))(c_pad, dmod, w, m, v)


def _sum_devices(parts, name):
    n, r, c = parts.shape
    tr = _tile(r, 512, V7X_SUBLANES)

    def body(p_ref, o_ref):
        acc = p_ref[0]
        for k in range(1, n):
            acc = acc + p_ref[k]
        o_ref[...] = acc

    return _call(body, name=name, grid=(r // tr,), in_specs=[pl.BlockSpec((n, tr, c), lambda i: (0, i, 0))],
                 out_specs=pl.BlockSpec((tr, c), lambda i: (i, 0)), out_shape=_sds((r, c), F32),
                 sem=("parallel",))(parts)


def _mesh_pos():
    return lax.axis_index("x"), lax.axis_index("y"), lax.axis_index("c")


def _other_chips(x, y):
    return [(1 - x, y), (x, 1 - y), (1 - x, 1 - y)]


def _all_gather(arrs, name):
    n = len(arrs)

    def body(*refs):
        ins, outs = refs[:n], refs[n:2 * n]
        send_sems, recv_sems, local_sems = refs[2 * n:]
        x, y, c = _mesh_pos()
        me, sibling = (x, y, c), (x, y, 1 - c)
        chips = _other_chips(x, y)

        def slot(a, pos):
            px, py, pc = pos
            return outs[a].at[:, pl.ds(4 * px + 2 * py + pc, 1)]

        def copy(a, k, block, to, src=None):
            return pltpu.make_async_remote_copy(
                src_ref=slot(a, block) if src is None else src, dst_ref=slot(a, block),
                send_sem=send_sems.at[a, k], recv_sem=recv_sems.at[a, k], device_id=to, device_id_type=MESH)

        mine = [pltpu.make_async_copy(ins[a], slot(a, me), local_sems.at[a]) for a in range(n)]
        for cp in mine:
            cp.start()
        first = []
        for a in range(n):
            first.append(copy(a, 0, me, sibling, src=ins[a]))
            first += [copy(a, 1 + j, me, (*chip, c), src=ins[a]) for j, chip in enumerate(chips)]
        for cp in first:
            cp.start()
        passed = []
        for a in range(n):
            for j, chip in enumerate(chips):
                copy(a, 1 + j, (*chip, c), me).wait_recv()
                fwd = copy(a, 4 + j, (*chip, c), sibling)
                fwd.start()
                passed.append(fwd)
        for a in range(n):
            copy(a, 0, sibling, me).wait_recv()
            for j, chip in enumerate(chips):
                copy(a, 4 + j, (*chip, 1 - c), me).wait_recv()
        for cp in first + passed:
            cp.wait_send()
        for cp in mine:
            cp.wait()

    out_shape = [_sds((a.shape[0], N_DEV) + a.shape[2:], a.dtype) for a in arrs]
    return pl.pallas_call(
        body, out_shape=out_shape, in_specs=[ANY] * n, out_specs=[ANY] * n, name=name,
        scratch_shapes=[pltpu.SemaphoreType.DMA((n, N_DEV - 1)), pltpu.SemaphoreType.DMA((n, N_DEV - 1)),
                        pltpu.SemaphoreType.DMA((n,))])(*arrs)


HBM = pl.BlockSpec(memory_space=pltpu.HBM)
SEM = pl.BlockSpec(memory_space=pltpu.SEMAPHORE)


def _hbm(v):
    return pltpu.with_memory_space_constraint(v, pltpu.HBM)


def _comm_call(body, name, bufs, sems_in, sems_out):
    after = [] if not sems_in or _Seq.last is None or any(_Seq.last is b for b in bufs) else [_Seq.last]
    nb, ni, na, no = len(bufs), len(sems_in), len(after), len(sems_out)

    def wrapped(*refs):
        body(refs[:nb], refs[nb:nb + ni], refs[nb + ni + na:nb + ni + na + no])
        if no:
            refs[-1][...] = jnp.zeros_like(refs[-1])

    out_shape = [pltpu.SemaphoreType.DMA(s) for s in sems_out] + [pltpu.HBM(b.shape, b.dtype) for b in bufs]
    out_specs = [SEM] * no + [HBM] * nb
    if no:
        out_shape.append(_sds((V7X_SUBLANES, V7X_LANES), F32))
        out_specs.append(pl.BlockSpec(memory_space=pltpu.VMEM))
    res = pl.pallas_call(
        wrapped, name=name, out_shape=out_shape, in_specs=[HBM] * nb + [SEM] * ni + [ANY] * na, out_specs=out_specs,
        input_output_aliases={i: no + i for i in range(nb)},
        compiler_params=pltpu.CompilerParams(has_side_effects=pltpu.SideEffectType.DATAFLOW_SIDE_EFFECTING),
    )(*bufs, *sems_in, *after)
    out_bufs = list(res[no:no + nb])
    if no:
        _Seq.tokens.append(res[-1])
    _Seq.last = out_bufs[0]
    return list(res[:no]), out_bufs


def _remote(src, dst, send_sem, recv_sem, to):
    return pltpu.make_async_remote_copy(src_ref=src, dst_ref=dst, send_sem=send_sem, recv_sem=recv_sem,
                                        device_id=to, device_id_type=MESH)


def _slot(ref, pos):
    px, py, pc = pos
    return ref.at[:, pl.ds(4 * px + 2 * py + pc, 1)]


def _ag_start(bufs, name):
    n = len(bufs)

    def body(b, _, sems):
        send_sib, recv_sib, send_ici, recv_ici = sems
        x, y, c = _mesh_pos()
        for a in range(n):
            mine = _slot(b[a], (x, y, c))
            _remote(mine, mine, send_sib.at[a], recv_sib.at[a], (x, y, 1 - c)).start()
            for j, (px, py) in enumerate(_other_chips(x, y)):
                _remote(mine, mine, send_ici.at[3 * a + j], recv_ici.at[3 * a + j], (px, py, c)).start()

    sems, bufs = _comm_call(body, name, [_hbm(b) for b in bufs], [], [(n,), (n,), (3 * n,), (3 * n,)])
    return dict(bufs=bufs, send_sib=sems[0], recv_sib=sems[1], send_ici=sems[2], recv_ici=sems[3])


def _ag_mid(st, name):
    n = len(st["bufs"])

    def body(b, sems_in, sems):
        (recv_ici,) = sems_in
        send_fwd, recv_fwd = sems
        x, y, c = _mesh_pos()
        for a in range(n):
            for j, (px, py) in enumerate(_other_chips(x, y)):
                blk = _slot(b[a], (px, py, c))
                _remote(blk, blk, send_fwd.at[3 * a + j], recv_ici.at[3 * a + j], (x, y, 1 - c)).wait_recv()
                _remote(blk, blk, send_fwd.at[3 * a + j], recv_fwd.at[3 * a + j], (x, y, 1 - c)).start()

    sems, bufs = _comm_call(body, name, st["bufs"], [st["recv_ici"]], [(3 * n,), (3 * n,)])
    return dict(st, bufs=bufs, send_fwd=sems[0], recv_fwd=sems[1])


def _ag_end(st, name):
    n = len(st["bufs"])

    def body(b, sems_in, _):
        send_sib, recv_sib, send_ici, send_fwd, recv_fwd = sems_in
        x, y, c = _mesh_pos()
        sibling = (x, y, 1 - c)
        for a in range(n):
            mine, sib_blk = _slot(b[a], (x, y, c)), _slot(b[a], sibling)
            _remote(mine, mine, send_sib.at[a], recv_sib.at[a], sibling).wait_send()
            _remote(sib_blk, sib_blk, send_sib.at[a], recv_sib.at[a], sibling).wait_recv()
            for j, (px, py) in enumerate(_other_chips(x, y)):
                blk, sib_got = _slot(b[a], (px, py, c)), _slot(b[a], (px, py, 1 - c))
                _remote(mine, mine, send_ici.at[3 * a + j], recv_sib.at[a], (px, py, c)).wait_send()
                _remote(blk, blk, send_fwd.at[3 * a + j], recv_fwd.at[3 * a + j], sibling).wait_send()
                _remote(sib_got, sib_got, send_fwd.at[3 * a + j], recv_fwd.at[3 * a + j], sibling).wait_recv()

    _, bufs = _comm_call(body, name, st["bufs"],
                         [st[k] for k in ("send_sib", "recv_sib", "send_ici", "send_fwd", "recv_fwd")], [])
    return bufs


def _rs_start(g4s, name):
    n = len(g4s)
    lands = [lax.empty((N_CHIP, 1) + g.shape[2:], g.dtype) for g in g4s]

    def body(b, _, sems):
        send, recv = sems
        x, y, c = _mesh_pos()
        for a in range(n):
            _remote(b[a].at[:, pl.ds(1 - c, 1)], b[n + a], send.at[a], recv.at[a], (x, y, 1 - c)).start()

    sems, bufs = _comm_call(body, name, [_hbm(v) for v in list(g4s) + lands], [], [(n,), (n,)])
    return dict(bufs=bufs, send=sems[0], recv=sems[1])


def _rs_mid(st, name):
    n = len(st["bufs"]) // 2

    def body(b, sems_in, _):
        send, recv = sems_in
        x, y, c = _mesh_pos()
        for a in range(n):
            cp = _remote(b[a].at[:, pl.ds(1 - c, 1)], b[n + a], send.at[a], recv.at[a], (x, y, 1 - c))
            cp.wait_send()
            cp.wait_recv()

    _, bufs = _comm_call(body, name, st["bufs"], [st["send"], st["recv"]], [])
    return bufs[:n], bufs[n:]


def _rs_start2(sums, name):
    n = len(sums)
    lands = [lax.empty((N_CHIP - 1,) + s.shape[1:], s.dtype) for s in sums]

    def body(b, _, sems):
        send, recv = sems
        x, y, c = _mesh_pos()
        for a in range(n):
            for j, (px, py) in enumerate(_other_chips(x, y)):
                _remote(b[a].at[pl.ds(2 * px + py, 1)], b[n + a].at[pl.ds(j, 1)], send.at[3 * a + j], recv.at[3 * a + j],
                        (px, py, c)).start()

    sems, bufs = _comm_call(body, name, [_hbm(v) for v in list(sums) + lands], [], [(3 * n,), (3 * n,)])
    return dict(bufs=bufs, send=sems[0], recv=sems[1])


def _rs_end(st, name):
    n = len(st["bufs"]) // 2

    def body(b, sems_in, _):
        send, recv = sems_in
        x, y, c = _mesh_pos()
        for a in range(n):
            for j, (px, py) in enumerate(_other_chips(x, y)):
                cp = _remote(b[a].at[pl.ds(2 * px + py, 1)], b[n + a].at[pl.ds(j, 1)], send.at[3 * a + j], recv.at[3 * a + j],
                             (px, py, c))
                cp.wait_send()
                cp.wait_recv()

    _, bufs = _comm_call(body, name, st["bufs"], [st["send"], st["recv"]], [])
    return bufs[:n], bufs[n:]


def _pack(parts, rows_align=V7X_SUBLANES):
    flat, total = [], 0
    for p in parts:
        v = p.reshape(-1).astype(F32)
        pad = -v.shape[0] % PACK_ALIGN
        flat.append(jnp.pad(v, (0, pad)) if pad else v)
        total += v.shape[0] + pad
    tail = -total % (rows_align * V7X_LANES)
    if tail:
        flat.append(jnp.zeros((tail,), F32))
    return jnp.concatenate(flat).reshape(-1, V7X_LANES)


def _unpack(buf, shapes):
    lead = buf.shape[:-2]
    flat = buf.reshape(lead + (-1,))
    out, pos = [], 0
    for s in shapes:
        size = 1
        for d in s:
            size *= d
        out.append(flat[..., pos:pos + size].reshape(lead + tuple(s)))
        pos += size + (-size % PACK_ALIGN)
    return out


def kernel(x, c, ada_w, ada_b, norm_g, ffn_w_in, ffn_w_out, gm_w_in, gm_ln_g, gm_ln_b, gm_ws, gm_bs, gm_w_out, cv_w_in, cv_b_in, cv_dw_w, cv_dw_b, cv_ln_g, cv_ln_b, cv_w_out, cv_b_out, final_g, loss_target, m_ada_w, m_ada_b, m_norm_g, m_ffn_w_in, m_ffn_w_out, m_gm_w_in, m_gm_ln_g, m_gm_ln_b, m_gm_ws, m_gm_bs, m_gm_w_out, m_cv_w_in, m_cv_b_in, m_cv_dw_w, m_cv_dw_b, m_cv_ln_g, m_cv_ln_b, m_cv_w_out, m_cv_b_out, m_final_g, v_ada_w, v_ada_b, v_norm_g, v_ffn_w_in, v_ffn_w_out, v_gm_w_in, v_gm_ln_g, v_gm_ln_b, v_gm_ws, v_gm_bs, v_gm_w_out, v_cv_w_in, v_cv_b_in, v_cv_dw_w, v_cv_dw_b, v_cv_ln_g, v_cv_ln_b, v_cv_w_out, v_cv_b_out, v_final_g):
    t, d = x.shape[1], x.shape[2]
    depth = ada_w.shape[0]
    assert depth == 2 and ffn_w_in.shape[:2] == (2, 2) and gm_w_in.shape[0] == 1 and cv_w_in.shape[0] == 1
    dl = d // N_DEV
    bn = ffn_w_in.shape[3]
    fl = ffn_w_out.shape[2]
    f = fl * N_DEV
    el = gm_w_in.shape[2]
    e = el * N_DEV // 2
    hn, l = gm_ws.shape[1], gm_ws.shape[2]
    kw = cv_dw_w.shape[1]
    cl = ada_w.shape[2]
    me = 4 * lax.axis_index("x") + 2 * lax.axis_index("y") + lax.axis_index("c")
    me1 = me.astype(jnp.int32).reshape(1)
    chip1 = (2 * lax.axis_index("x") + lax.axis_index("y")).astype(jnp.int32).reshape(1)
    core1 = lax.axis_index("c").astype(jnp.int32).reshape(1)
    _Seq.last, _Seq.tokens = None, []

    xs = x[0]
    tgt = loss_target[0]

    ag_groups = [("win00", [(ffn_w_in, (0, 0))]), ("wout00", [(ffn_w_out, (0, 0))]),
                 ("gm", [(gm_w_in, (0,)), (gm_w_out, (0,))]),
                 ("win01", [(ffn_w_in, (0, 1))]), ("wout01", [(ffn_w_out, (0, 1))]),
                 ("win10", [(ffn_w_in, (1, 0))]), ("wout10", [(ffn_w_out, (1, 0))]),
                 ("cv", [(cv_w_in, (0,)), (cv_w_out, (0,))]),
                 ("win11", [(ffn_w_in, (1, 1))]), ("wout11", [(ffn_w_out, (1, 1))])]
    ag_flight = {}

    def ag_start(gi, after=()):
        gname, members = ag_groups[gi]
        bufs = [_cast_to_slot(w, lead, me1, name=f"cast_{gname}_{k}", after=after) for k, (w, lead) in enumerate(members)]
        ag_flight[gi] = _ag_start(bufs, name=f"ag_start_{gname}")

    def ag_forward(gi):
        if gi in ag_flight and "send_fwd" not in ag_flight[gi]:
            ag_flight[gi] = _ag_mid(ag_flight[gi], name=f"ag_mid_{ag_groups[gi][0]}")

    def ag_take(gi):
        ag_forward(gi)
        bufs = _ag_end(ag_flight.pop(gi), name=f"ag_end_{ag_groups[gi][0]}")
        if gi > 0:
            ag_forward(gi + 1)
        if gi + AG_AHEAD < len(ag_groups):
            ag_start(gi + AG_AHEAD)
        return [b[0] for b in bufs]

    small_in = [c, norm_g, cv_b_in, cv_dw_w, cv_dw_b, cv_ln_g, cv_ln_b, cv_b_out]
    pack1 = _pack(small_in)
    (pack1_all,) = _all_gather([pack1[None, None]], name="ag_small")
    parts = _unpack(pack1_all[0], [s.shape for s in small_in])
    c_all = parts[0].reshape(N_DEV, d)
    ng_full = jnp.moveaxis(parts[1], 0, 2).reshape(depth, 3, d)
    cvb_in_full = parts[2].reshape(1, 2 * e)
    dww_full = jnp.moveaxis(parts[3][:, 0], 0, 1).reshape(kw, e)
    dwb_full, cln_g_full, cln_b_full, cvb_out_full = [p.reshape(1, d) for p in parts[4:8]]

    c_pad = jnp.pad(c_all, ((0, 16 - N_DEV), (0, 0)))
    ada_b_loc = lax.dynamic_slice_in_dim(ada_b, me * cl, cl, axis=1).reshape(depth, 1, cl)
    mod_part = _ada_fwd(c_pad, ada_w, ada_b_loc, name="ada_fwd")[:, :N_DEV]
    (mod_all,) = _all_gather([_pack([mod_part])[None, None]], name="ag_mod")
    mod_all = _unpack(mod_all[0], [mod_part.shape])[0]
    mod_mine = lax.dynamic_index_in_dim(mod_all, me, axis=2, keepdims=False)
    mod = jnp.moveaxis(mod_mine, 0, 1).reshape(depth, 3, 3, 1, d)

    for gi in range(AG_AHEAD):
        ag_start(gi, after=[mod_all])

    ws = gm_ws[0]
    bsb = jnp.broadcast_to(gm_bs[0][:, :, None], (hn, l, e // hn))
    gm_g, gm_b = gm_ln_g, gm_ln_b

    saved = []
    xcur = xs
    next_group = 0
    for i in range(depth):
        for s in range(3):
            shift, scale, gate = mod[i, s, 0], mod[i, s, 1], mod[i, s, 2]
            g_norm = ng_full[i, s][None]
            tag = f"l{i}s{s}"
            h = _norm_mod(xcur, g_norm, scale, shift, name=f"norm_mod_{tag}")
            if s != 1:
                (w_in_blk,) = ag_take(next_group)
                gg, uu, act = _ffn_in(h, w_in_blk, 0, name=f"ffn_in_{tag}")
                w_out3 = ag_take(next_group + 1)[0].reshape(1, f, d)
                next_group += 2
                xnext, yv = _out_proj(act, w_out3, 0, xcur, gate, None, 0.5, name=f"ffn_out_{tag}")
                saved.append(dict(x=xcur, h=h, g=gg, u=uu, y=yv, w_in=w_in_blk, w_out=w_out3))
            elif i % 2 == 0:
                gm_in_blk, gm_out = ag_take(next_group)
                gm_out3 = gm_out.reshape(1, e, d)
                next_group += 1
                pre = _in_proj(h, gm_in_blk, None, name=f"gm_in_{tag}")
                uu, vn = _gm_act(pre, gm_g, gm_b, name=f"gm_act_{tag}")
                sg = _sgu_fwd(uu, vn, ws, bsb, name=f"sgu_fwd_{tag}")
                xnext, yv = _out_proj(sg, gm_out3, 0, xcur, gate, None, 1.0, name=f"gm_out_{tag}")
                saved.append(dict(x=xcur, h=h, pre=pre, u=uu, vn=vn, sg=sg, y=yv, w_in=gm_in_blk, w_out=gm_out3))
            else:
                cv_in_blk, cv_out = ag_take(next_group)
                cv_out3 = cv_out.reshape(1, e, d)
                next_group += 1
                p = _in_proj(h, cv_in_blk, cvb_in_full, name=f"cv_in_{tag}")
                yc = _dwconv_fwd(p, dww_full, dwb_full, name=f"dwconv_fwd_{tag}")
                ys = _cv_act(yc, cln_g_full, cln_b_full, name=f"cv_act_{tag}")
                xnext, yv = _out_proj(ys, cv_out3, 0, xcur, gate, cvb_out_full, 1.0, name=f"cv_out_{tag}")
                saved.append(dict(x=xcur, h=h, p=p, yc=yc, ys=ys, y=yv, w_in=cv_in_blk, w_out=cv_out3))
            xcur = xnext

    sq, dx, d_final_g = _final_loss(xcur, tgt, final_g[None], name="final_loss")
    loss = lax.psum(0.5 / d * jnp.sum(sq), ("x", "y", "c"))

    dmod = [[[None] * 3 for _ in range(3)] for _ in range(depth)]
    d_norm_g = [[None] * 3 for _ in range(depth)]
    small = {}

    stacked = {
        "ffn_w_in": [a.reshape(4, d, bn) for a in (ffn_w_in, m_ffn_w_in, v_ffn_w_in)],
        "ffn_w_out": [a.reshape(4, fl, d) for a in (ffn_w_out, m_ffn_w_out, v_ffn_w_out)],
        "gm_w_in": [gm_w_in, m_gm_w_in, v_gm_w_in], "gm_w_out": [gm_w_out, m_gm_w_out, v_gm_w_out],
        "cv_w_in": [cv_w_in, m_cv_w_in, v_cv_w_in], "cv_w_out": [cv_w_out, m_cv_w_out, v_cv_w_out],
    }
    res_big = {}

    def rs_sibling(g4s, tag):
        return _rs_start(g4s, name=f"rs_start_{tag}"), tag

    def rs_chips(flight):
        st, tag = flight
        g4s, lands = _rs_mid(st, name=f"rs_mid_{tag}")
        sums = [_add_sibling(g4, land, core1, name=f"rs_add_{tag}_{k}") for k, (g4, land) in enumerate(zip(g4s, lands))]
        return _rs_start2(sums, name=f"rs_start2_{tag}"), tag

    def rs_finish(flight, targets):
        st, tag = flight
        sums, recvs = _rs_end(st, name=f"rs_end_{tag}")
        for (pname, k), hsum, recv in zip(targets, sums, recvs):
            w_st, m_st, v_st = stacked[pname]
            res_big[pname] = _adamw_stacked(hsum, recv, chip1, w_st, m_st, v_st, k, res_big.get(pname),
                                            name=f"adamw_{pname}_{k}")

    pending = []
    last_sibling = None
    for i in reversed(range(depth)):
        for s in reversed(range(3)):
            sv = saved[3 * i + s]
            shift, scale, gate = mod[i, s, 0], mod[i, s, 1], mod[i, s, 2]
            g_norm = ng_full[i, s][None]
            tag = f"l{i}s{s}"
            last = i == 0 and s == 0
            if s != 1:
                widx = 2 * i + s // 2
                dy, (dgate,) = _residual_bwd(dx, sv["y"], gate, 0.5, False, name=f"res_bwd_{tag}")
                dgu, act = _ffn_da(dy, sv["w_out"], 0, sv["g"], sv["u"], name=f"ffn_da_{tag}")
                if last:
                    g_in = _mm_tn(sv["h"], dgu, d, bn, True, name=f"ffn_dwin_{tag}").reshape(N_CHIP, 2, d, bn)
                    sib_in = rs_sibling([g_in], f"{tag}_in")
                    g_out = _mm_tn(act, dy[None], bn, d, False, name=f"ffn_dwout_{tag}").reshape(N_CHIP, 2, fl, d)
                    new_flights = [(rs_chips(sib_in), [("ffn_w_in", widx)])]
                    sib, targets = rs_sibling([g_out], f"{tag}_out"), [("ffn_w_out", widx)]
                else:
                    g_out = _mm_tn(act, dy[None], bn, d, False, name=f"ffn_dwout_{tag}").reshape(N_CHIP, 2, fl, d)
                    sib_out = rs_sibling([g_out], f"{tag}_out")
                    g_in = _mm_tn(sv["h"], dgu, d, bn, True, name=f"ffn_dwin_{tag}").reshape(N_CHIP, 2, d, bn)
                    new_flights = [(rs_chips(sib_out), [("ffn_w_out", widx)])]
                    sib, targets = rs_sibling([g_in], f"{tag}_in"), [("ffn_w_in", widx)]
                z3, w_blk = dgu, sv["w_in"]
            elif i % 2 == 0:
                dy, (dgate,) = _residual_bwd(dx, sv["y"], gate, 1.0, False, name=f"res_bwd_{tag}")
                ds = _mm_nt(dy, sv["w_out"], 0, name=f"gm_ds_{tag}")
                g_out = _mm_tn(sv["sg"], dy[None], _tile(e, 1024, V7X_LANES), d, False,
                               name=f"gm_dwout_{tag}").reshape(N_CHIP, 2, dl, d)
                du, dvn, dws, dbs = _sgu_bwd(ds, sv["u"], sv["vn"], ws, bsb, name=f"sgu_bwd_{tag}")
                dpre, dlng, dlnb = _gm_act_bwd(sv["pre"], du, dvn, gm_g, name=f"gm_act_bwd_{tag}")
                small["gm_ln_g"], small["gm_ln_b"] = dlng, dlnb
                small["gm_ws"], small["gm_bs"] = dws, dbs[:, :, 0]
                g_in = _mm_tn(sv["h"], dpre[None], d, el, True, name=f"gm_dwin_{tag}").reshape(N_CHIP, 2, d, el)
                targets, new_flights = [("gm_w_in", 0), ("gm_w_out", 0)], []
                sib = rs_sibling([g_in, g_out], tag)
                z3, w_blk = dpre[None], sv["w_in"]
            else:
                dy, (dgate, dbout) = _residual_bwd(dx, sv["y"], gate, 1.0, True, name=f"res_bwd_{tag}")
                dys = _mm_nt(dy, sv["w_out"], 0, name=f"cv_dys_{tag}")
                g_out = _mm_tn(sv["ys"], dy[None], _tile(e, 1024, V7X_LANES), d, False,
                               name=f"cv_dwout_{tag}").reshape(N_CHIP, 2, dl, d)
                dyc, dlng, dlnb, ddwb = _cv_act_bwd(dys, sv["yc"], cln_g_full, cln_b_full, name=f"cv_act_bwd_{tag}")
                dp, ddww, dba, dbg = _dwconv_bwd(dyc, sv["p"], dww_full, name=f"dwconv_bwd_{tag}")
                small["cv_b_out"], small["cv_ln_g"], small["cv_ln_b"], small["cv_dw_b"] = dbout, dlng, dlnb, ddwb
                small["cv_dw_w"] = ddww
                small["cv_b_in"] = jnp.concatenate([dba, dbg], axis=1)
                g_in = _mm_tn(sv["h"], dp, d, el, True, name=f"cv_dwin_{tag}").reshape(N_CHIP, 2, d, el)
                targets, new_flights = [("cv_w_in", 0), ("cv_w_out", 0)], []
                sib = rs_sibling([g_in, g_out], tag)
                z3, w_blk = dp, sv["w_in"]
            dx, dscale, dshift, dgn = _dh_norm_bwd(z3, w_blk, sv["x"], dx, g_norm, scale, name=f"dh_norm_bwd_{tag}")
            if last:
                last_sibling = (sib, targets)
            else:
                new_flights.append((rs_chips(sib), targets))
            dmod[i][s] = [dshift, dscale, dgate]
            d_norm_g[i][s] = dgn
            for flight in pending:
                rs_finish(*flight)
            pending = new_flights
    grad_x = dx[None]

    dmod_mine = jnp.concatenate([v for per_l in dmod for per_s in per_l for v in per_s], axis=1)
    dng_mine = jnp.concatenate([v for per_l in d_norm_g for v in per_l], axis=1)
    small_out = [dmod_mine, dng_mine, small["gm_ln_g"], small["gm_ln_b"], small["gm_ws"], small["gm_bs"],
                 small["cv_b_in"], small["cv_dw_w"], small["cv_dw_b"], small["cv_ln_g"], small["cv_ln_b"],
                 small["cv_b_out"], d_final_g]
    shapes2 = [s.shape for s in small_out]
    (pack2_all,) = _all_gather([_pack(small_out, rows_align=256)[None, None]], name="ag_small_grads")
    _Seq.last = pack2_all
    pending.append((rs_chips(last_sibling[0]), last_sibling[1]))
    summed = _unpack(_sum_devices(pack2_all[0], name="sum_small_grads"), shapes2)
    dmod_all = _unpack(pack2_all[0], shapes2)[0].reshape(N_DEV, depth, 9 * d)

    def my_cols(full, width):
        return lax.dynamic_slice_in_dim(full, me * width, width, axis=full.ndim - 1)

    g_ada_b = summed[0].reshape(depth, 9 * d)
    g_norm_g = my_cols(summed[1].reshape(depth, 3, d), dl)
    g_small = {
        "ada_b": g_ada_b, "norm_g": g_norm_g,
        "gm_ln_g": summed[2], "gm_ln_b": summed[3], "gm_ws": summed[4][None], "gm_bs": summed[5][None],
        "cv_b_in": my_cols(summed[6], el), "cv_dw_w": my_cols(summed[7], dl)[None],
        "cv_dw_b": my_cols(summed[8], dl), "cv_ln_g": my_cols(summed[9], dl), "cv_ln_b": my_cols(summed[10], dl),
        "cv_b_out": my_cols(summed[11], dl), "final_g": summed[12].reshape(d),
    }

    dm_loc = jnp.moveaxis(my_cols(dmod_all, cl), 0, 1)
    dm_loc = jnp.pad(dm_loc, ((0, 0), (0, 16 - N_DEV), (0, 0)))
    res_ada_w = _ada_bwd(c_pad, dm_loc, ada_w, m_ada_w, v_ada_w, name="ada_bwd_adamw")

    def flat2(a):
        return a.reshape(-1, a.shape[-1])

    small_params = {
        "ada_b": (ada_b, m_ada_b, v_ada_b), "norm_g": (norm_g, m_norm_g, v_norm_g),
        "gm_ln_g": (gm_ln_g, m_gm_ln_g, v_gm_ln_g), "gm_ln_b": (gm_ln_b, m_gm_ln_b, v_gm_ln_b),
        "gm_ws": (gm_ws, m_gm_ws, v_gm_ws), "gm_bs": (gm_bs, m_gm_bs, v_gm_bs),
        "cv_b_in": (cv_b_in, m_cv_b_in, v_cv_b_in), "cv_dw_w": (cv_dw_w, m_cv_dw_w, v_cv_dw_w),
        "cv_dw_b": (cv_dw_b, m_cv_dw_b, v_cv_dw_b), "cv_ln_g": (cv_ln_g, m_cv_ln_g, v_cv_ln_g),
        "cv_ln_b": (cv_ln_b, m_cv_ln_b, v_cv_ln_b), "cv_b_out": (cv_b_out, m_cv_b_out, v_cv_b_out),
        "final_g": (final_g, m_final_g, v_final_g),
    }
    res_small = {}
    for key, (w, m, v) in small_params.items():
        g2 = flat2(g_small[key].reshape(w.shape)) if w.ndim > 1 else g_small[key].reshape(1, -1)
        w2, m2, v2 = [flat2(a) if a.ndim > 1 else a.reshape(1, -1) for a in (w, m, v)]
        res_small[key] = [o.reshape(w.shape) for o in _adamw([(g2[None], 0)], w2, m2, v2, name=f"adamw_{key}")]

    for flight in pending:
        rs_finish(*flight)

    def big(name, k):
        if name == "ada_w":
            return res_ada_w[k]
        return res_big[name][k].reshape(stacked_shape[name])

    stacked_shape = {"ffn_w_in": ffn_w_in.shape, "ffn_w_out": ffn_w_out.shape, "gm_w_in": gm_w_in.shape,
                     "gm_w_out": gm_w_out.shape, "cv_w_in": cv_w_in.shape, "cv_w_out": cv_w_out.shape}

    order = ["ada_w", "ada_b", "norm_g", "ffn_w_in", "ffn_w_out", "gm_w_in", "gm_ln_g", "gm_ln_b", "gm_ws", "gm_bs",
             "gm_w_out", "cv_w_in", "cv_b_in", "cv_dw_w", "cv_dw_b", "cv_ln_g", "cv_ln_b", "cv_w_out", "cv_b_out",
             "final_g"]
    outs = [loss, grad_x]
    for k in range(4):
        for name in order:
            outs.append(res_small[name][k] if name in res_small else big(name, k))
    return tuple(outs)
```

```python
import functools

import jax
import jax.numpy as jnp
from jax import lax
from jax.experimental import pallas as pl
from jax.experimental.pallas import tpu as pltpu

F32 = jnp.float32
BF = jnp.bfloat16
MESH = pl.DeviceIdType.MESH

N_DEV = 8
N_CHIP = 4
NORM_EPS = 1e-6
ADAM_LR = 0.001
ADAM_B1 = 0.9
ADAM_B2 = 0.999
ADAM_EPS = 1e-08
ADAM_WD = 0.01
ADAM_STEP = 10

V7X_SUBLANES = 8
V7X_LANES = 128
PACK_ALIGN = V7X_SUBLANES * V7X_LANES
V7X_VMEM_LIMIT = 56 * 1024 * 1024


def _tile(n, pref, align):
    if n <= pref:
        return n
    t = pref - pref % align
    while t >= align:
        if n % t == 0:
            return t
        t -= align
    return n


ANY = pl.BlockSpec(memory_space=pl.ANY)


class _Seq:
    last = None
    tokens = []


def _call(body, *, name, grid, in_specs, out_specs, out_shape, scratch=(), sem=None, prefetch=None, aliases=None,
          after=(), on_path=True):
    def run(*args):
        if on_path:
            tokens, _Seq.tokens = _Seq.tokens + list(after), []
        else:
            tokens = list(after)
        lead = 0 if prefetch is None else 1
        n_in, n_tok = lead + len(args), len(tokens)

        def wrapped(*refs):
            body(*refs[:n_in], *refs[n_in + n_tok:])

        specs = list(in_specs) + [ANY] * n_tok
        params = pltpu.CompilerParams(dimension_semantics=sem, vmem_limit_bytes=V7X_VMEM_LIMIT)
        if prefetch is None:
            res = pl.pallas_call(wrapped, out_shape=out_shape, grid=grid, in_specs=specs, out_specs=out_specs,
                                 scratch_shapes=scratch, name=name, compiler_params=params,
                                 input_output_aliases=aliases or {})(*args, *tokens)
        else:
            grid_spec = pltpu.PrefetchScalarGridSpec(num_scalar_prefetch=1, grid=grid, in_specs=specs,
                                                     out_specs=out_specs, scratch_shapes=scratch)
            res = pl.pallas_call(wrapped, out_shape=out_shape, grid_spec=grid_spec, name=name,
                                 compiler_params=params, input_output_aliases=aliases or {})(prefetch, *args, *tokens)
        if on_path:
            _Seq.last = res[0] if isinstance(res, (list, tuple)) else res
        return res
    return run


def _sds(shape, dtype):
    return jax.ShapeDtypeStruct(tuple(shape), dtype)


def _sigmoid(v):
    return 1.0 / (1.0 + jnp.exp(-v))


def _normal_cdf_pdf(v):
    a = jnp.abs(v) * 0.7071067811865476
    t = 1.0 / (1.0 + 0.3275911 * a)
    poly = t * (0.254829592 + t * (-0.284496736 + t * (1.421413741 + t * (-1.453152027 + t * 1.061405429))))
    e = jnp.exp(-0.5 * v * v)
    half_erf = 0.5 - 0.5 * poly * e
    return 0.5 + jnp.where(v < 0, -half_erf, half_erf), 0.3989422804014327 * e


def _gelu(v):
    return v * _normal_cdf_pdf(v)[0]


def _fold_rows(val):
    rows, w = val.shape
    return val.reshape(rows // V7X_SUBLANES, V7X_SUBLANES, w).sum(axis=0)


def _rowwise(fn, name, rows_in, vecs_in, rows_out, acc_widths, tm=256):
    t = rows_in[0].shape[0]
    tm = _tile(t, tm, V7X_SUBLANES)
    steps = t // tm
    nr, nv, no, na = len(rows_in), len(vecs_in), len(rows_out), len(acc_widths)

    def body(*refs):
        rin, vin = refs[:nr], refs[nr:nr + nv]
        rout = refs[nr + nv:nr + nv + no]
        aout = refs[nr + nv + no:nr + nv + no + na]
        accs = refs[nr + nv + no + na:]
        i = pl.program_id(0)
        outs, acc_vals = fn(*[r[...] for r in rin], *[v[...] for v in vin])
        for r, o in zip(rout, outs):
            r[...] = o.astype(r.dtype)
        if na:
            @pl.when(i == 0)
            def _():
                for a in accs:
                    a[...] = jnp.zeros_like(a)

            for a, val in zip(accs, acc_vals):
                a[...] += _fold_rows(val)

            @pl.when(i == steps - 1)
            def _():
                for o, a in zip(aout, accs):
                    o[...] = jnp.sum(a[...], axis=0, keepdims=True)

    in_specs = [pl.BlockSpec((tm, r.shape[1]), lambda i: (i, 0)) for r in rows_in]
    in_specs += [pl.BlockSpec(v.shape, functools.partial(lambda nd, i: (0,) * nd, v.ndim)) for v in vecs_in]
    out_specs = [pl.BlockSpec((tm, r.shape[1]), lambda i: (i, 0)) for r in rows_out]
    out_specs += [pl.BlockSpec((1, w), lambda i: (0, 0)) for w in acc_widths]
    out_shape = list(rows_out) + [_sds((1, w), F32) for w in acc_widths]
    scratch = [pltpu.VMEM((V7X_SUBLANES, w), F32) for w in acc_widths]
    res = _call(body, name=name, grid=(steps,), in_specs=in_specs, out_specs=out_specs, out_shape=out_shape,
                scratch=scratch, sem=("arbitrary",) if na else ("parallel",))(*rows_in, *vecs_in)
    return res[:no], res[no:]


def _norm_mod(x, g, scale, shift, name):
    def fn(xv, gv, sc, sh):
        r = lax.rsqrt(jnp.mean(xv * xv, axis=-1, keepdims=True) + NORM_EPS)
        return ((xv * r * gv) * (1.0 + sc) + sh,), ()
    (h,), _ = _rowwise(fn, name, [x], [g, scale, shift], [_sds(x.shape, BF)], [])
    return h


def _residual_bwd(dxp, y, gate, coef, with_colsum, name):
    def fn(dv, yv, gt):
        dy = (coef * gt) * dv
        accs = (coef * dv * yv.astype(F32),)
        if with_colsum:
            accs += (dy,)
        return (dy,), accs
    d = dxp.shape[1]
    (dy,), accs = _rowwise(fn, name, [dxp, y], [gate], [_sds(dxp.shape, BF)], [d, d] if with_colsum else [d])
    return dy, accs


def _final_loss(x, target, g, name):
    d = x.shape[1]

    def fn(xv, tv, gv):
        r = lax.rsqrt(jnp.mean(xv * xv, axis=-1, keepdims=True) + NORM_EPS)
        xhat = xv * r
        err = xhat * gv - tv
        dy = err * (1.0 / d)
        dxhat = dy * gv
        dx = r * (dxhat - xhat * jnp.mean(dxhat * xhat, axis=-1, keepdims=True))
        return (dx,), (err * err, dy * xhat)
    (dx,), (sq, dg) = _rowwise(fn, name, [x, target], [g], [_sds(x.shape, F32)], [d, d])
    return sq, dx, dg


def _gm_act(pre, ln_g, ln_b, name):
    e = pre.shape[1] // 2

    def fn(pv, gv, bv):
        p = pv.astype(F32)
        u = _gelu(p[:, :e])
        v = _gelu(p[:, e:])
        mu = jnp.mean(v, axis=-1, keepdims=True)
        vc = v - mu
        rstd = lax.rsqrt(jnp.mean(vc * vc, axis=-1, keepdims=True) + NORM_EPS)
        return (u, vc * rstd * gv + bv), ()
    t = pre.shape[0]
    (u, vn), _ = _rowwise(fn, name, [pre], [ln_g, ln_b], [_sds((t, e), BF), _sds((t, e), BF)], [])
    return u, vn


def _gm_act_bwd(pre, du, dvn, ln_g, name):
    e = pre.shape[1] // 2

    def fn(pv, duv, dvv, gv):
        p = pv.astype(F32)
        pu, pvv = p[:, :e], p[:, e:]
        cdf_u, pdf_u = _normal_cdf_pdf(pu)
        cdf_v, pdf_v = _normal_cdf_pdf(pvv)
        v = pvv * cdf_v
        mu = jnp.mean(v, axis=-1, keepdims=True)
        vc = v - mu
        rstd = lax.rsqrt(jnp.mean(vc * vc, axis=-1, keepdims=True) + NORM_EPS)
        vhat = vc * rstd
        dvn_f = dvv.astype(F32)
        dvhat = dvn_f * gv
        dv = rstd * (dvhat - jnp.mean(dvhat, axis=-1, keepdims=True)
                     - vhat * jnp.mean(dvhat * vhat, axis=-1, keepdims=True))
        dpu = duv.astype(F32) * (cdf_u + pu * pdf_u)
        dpv = dv * (cdf_v + pvv * pdf_v)
        return (jnp.concatenate([dpu, dpv], axis=1),), (dvn_f * vhat, dvn_f)
    (dpre,), (dg, db) = _rowwise(fn, name, [pre, du, dvn], [ln_g], [_sds(pre.shape, BF)], [e, e], tm=128)
    return dpre, dg, db


def _cv_act(yc, ln_g, ln_b, name):
    def fn(yv, gv, bv):
        mu = jnp.mean(yv, axis=-1, keepdims=True)
        c = yv - mu
        rstd = lax.rsqrt(jnp.mean(c * c, axis=-1, keepdims=True) + NORM_EPS)
        yn = c * rstd * gv + bv
        return (yn * _sigmoid(yn),), ()
    (ys,), _ = _rowwise(fn, name, [yc], [ln_g, ln_b], [_sds(yc.shape, BF)], [])
    return ys


def _cv_act_bwd(dys, yc, ln_g, ln_b, name):
    def fn(dv, yv, gv, bv):
        mu = jnp.mean(yv, axis=-1, keepdims=True)
        c = yv - mu
        rstd = lax.rsqrt(jnp.mean(c * c, axis=-1, keepdims=True) + NORM_EPS)
        yhat = c * rstd
        yn = yhat * gv + bv
        sig = _sigmoid(yn)
        dyn = dv.astype(F32) * (sig * (1.0 + yn * (1.0 - sig)))
        dyhat = dyn * gv
        dyc = rstd * (dyhat - jnp.mean(dyhat, axis=-1, keepdims=True)
                      - yhat * jnp.mean(dyhat * yhat, axis=-1, keepdims=True))
        return (dyc,), (dyn * yhat, dyn, dyc)
    cw = yc.shape[1]
    (dyc,), (dg, db, dbias) = _rowwise(fn, name, [dys, yc], [ln_g, ln_b], [_sds(yc.shape, F32)], [cw, cw, cw])
    return dyc, dg, db, dbias


MM_SEG_ROWS = 256


def _ffn_in(h, w_blk, blk0, name):
    t, d = h.shape
    bn = w_blk.shape[2]
    half = N_DEV // 2
    f = half * bn
    tm = _tile(t, 512, V7X_SUBLANES)
    seg_rows = _tile(tm, MM_SEG_ROWS, 2 * V7X_SUBLANES)

    def body(h_ref, wg_ref, wu_ref, g_ref, u_ref, a_ref):
        for seg in range(tm // seg_rows):
            rows = pl.ds(seg * seg_rows, seg_rows)
            hv = h_ref[rows, :]
            g = jnp.dot(hv, wg_ref[...], preferred_element_type=F32)
            u = jnp.dot(hv, wu_ref[...], preferred_element_type=F32)
            g_ref[rows, :] = g.astype(BF)
            u_ref[rows, :] = u.astype(BF)
            a_ref[rows, :] = (g * _sigmoid(g) * u).astype(BF)

    out = _sds((t, f), BF)
    tile = pl.BlockSpec((tm, bn), lambda j, i: (i, j))
    return _call(
        body, name=name, grid=(half, t // tm),
        in_specs=[pl.BlockSpec((tm, d), lambda j, i: (i, 0)),
                  pl.BlockSpec((None, d, bn), lambda j, i: (blk0 + j, 0, 0)),
                  pl.BlockSpec((None, d, bn), lambda j, i: (blk0 + half + j, 0, 0))],
        out_specs=[tile, tile, tile], out_shape=[out, out, out], sem=("parallel", "parallel"))(h, w_blk, w_blk)


def _in_proj(h, w_blk, bias, name):
    t, d = h.shape
    bn = w_blk.shape[2]
    tm = _tile(t, 1024, V7X_SUBLANES)

    def body(*refs):
        if bias is None:
            h_ref, w_ref, o_ref = refs
            o_ref[...] = jnp.dot(h_ref[...], w_ref[...], preferred_element_type=F32).astype(BF)
        else:
            h_ref, w_ref, b_ref, o_ref = refs
            o_ref[...] = (jnp.dot(h_ref[...], w_ref[...], preferred_element_type=F32) + b_ref[...]).astype(BF)

    in_specs = [pl.BlockSpec((tm, d), lambda j, i: (i, 0)), pl.BlockSpec((None, d, bn), lambda j, i: (j, 0, 0))]
    args = [h, w_blk]
    if bias is not None:
        in_specs.append(pl.BlockSpec((1, bn), lambda j, i: (0, j)))
        args.append(bias)
    return _call(body, name=name, grid=(N_DEV, t // tm), in_specs=in_specs,
                 out_specs=pl.BlockSpec((tm, bn), lambda j, i: (i, j)), out_shape=_sds((t, N_DEV * bn), BF),
                 sem=("parallel", "parallel"))(*args)


def _out_proj(a, w3, widx, x, gate, bias, coef, name):
    t, k = a.shape
    d = w3.shape[2]
    tm = _tile(t, 512, V7X_SUBLANES)
    tn = _tile(d, 512, V7X_LANES)

    def body(*refs):
        if bias is None:
            a_ref, w_ref, x_ref, g_ref, xo_ref, y_ref = refs
            y = jnp.dot(a_ref[...], w_ref[...], preferred_element_type=F32)
        else:
            a_ref, w_ref, x_ref, g_ref, b_ref, xo_ref, y_ref = refs
            y = jnp.dot(a_ref[...], w_ref[...], preferred_element_type=F32) + b_ref[...]
        y_ref[...] = y.astype(BF)
        xo_ref[...] = x_ref[...] + (coef * g_ref[...]) * y

    tile = pl.BlockSpec((tm, tn), lambda j, i: (i, j))
    vec = pl.BlockSpec((1, tn), lambda j, i: (0, j))
    in_specs = [pl.BlockSpec((tm, k), lambda j, i: (i, 0)),
                pl.BlockSpec((None, k, tn), lambda j, i: (widx, 0, j)), tile, vec]
    args = [a, w3, x, gate]
    if bias is not None:
        in_specs.append(vec)
        args.append(bias)
    return _call(body, name=name, grid=(d // tn, t // tm), in_specs=in_specs, out_specs=[tile, tile],
                 out_shape=[_sds((t, d), F32), _sds((t, d), BF)], sem=("parallel", "parallel"))(*args)


def _ffn_da(dy, w3, widx, g, u, name):
    t, d = dy.shape
    f = w3.shape[1]
    bn = f // (N_DEV // 2)
    tm = _tile(t, 512, V7X_SUBLANES)
    seg_rows = _tile(tm, MM_SEG_ROWS, 2 * V7X_SUBLANES)

    def body(dy_ref, w_ref, g_ref, u_ref, dgu_ref, a_ref):
        for seg in range(tm // seg_rows):
            rows = pl.ds(seg * seg_rows, seg_rows)
            da = lax.dot_general(dy_ref[rows, :], w_ref[...], (((1,), (1,)), ((), ())), preferred_element_type=F32)
            gv = g_ref[rows, :].astype(F32)
            uv = u_ref[rows, :].astype(F32)
            sig = _sigmoid(gv)
            sl = gv * sig
            dgu_ref[0, rows, :] = (da * uv * (sig * (1.0 + gv * (1.0 - sig)))).astype(BF)
            dgu_ref[1, rows, :] = (da * sl).astype(BF)
            a_ref[rows, :] = (sl * uv).astype(BF)

    tile = pl.BlockSpec((tm, bn), lambda j, i: (i, j))
    return _call(
        body, name=name, grid=(f // bn, t // tm),
        in_specs=[pl.BlockSpec((tm, d), lambda j, i: (i, 0)),
                  pl.BlockSpec((None, bn, d), lambda j, i: (widx, j, 0)), tile, tile],
        out_specs=[pl.BlockSpec((2, tm, bn), lambda j, i: (0, i, j)), tile],
        out_shape=[_sds((2, t, f), BF), _sds((t, f), BF)], sem=("parallel", "parallel"))(dy, w3, g, u)


def _mm_nt(dy, w3, widx, name):
    t, k = dy.shape
    n = w3.shape[1]
    tm = _tile(t, 512, V7X_SUBLANES)
    tn = _tile(n, 1024, V7X_LANES)

    def body(dy_ref, w_ref, o_ref):
        o_ref[...] = lax.dot_general(dy_ref[...], w_ref[...], (((1,), (1,)), ((), ())),
                                     preferred_element_type=F32).astype(BF)

    return _call(body, name=name, grid=(n // tn, t // tm),
                 in_specs=[pl.BlockSpec((tm, k), lambda j, i: (i, 0)),
                           pl.BlockSpec((None, tn, k), lambda j, i: (widx, j, 0))],
                 out_specs=pl.BlockSpec((tm, tn), lambda j, i: (i, j)), out_shape=_sds((t, n), BF),
                 sem=("parallel", "parallel"))(dy, w3)


NORM_BWD_ROWS = 128


def _dh_norm_bwd(z3, w_blk, x, dxp, g, scale, name):
    lead, t, _ = z3.shape
    d, bn = w_blk.shape[1], w_blk.shape[2]
    per = N_DEV // lead
    tm = _tile(t, 512, NORM_BWD_ROWS)
    ni = t // tm

    def body(z_ref, w_ref, x_ref, dp_ref, g_ref, sc_ref, dx_ref, dscale_ref, dshift_ref, dg_ref,
             acc_ref, a_scale, a_shift, a_g):
        i, k = pl.program_id(0), pl.program_id(1)

        @pl.when(k == 0)
        def _():
            acc_ref[...] = jnp.zeros_like(acc_ref)

        @pl.when((i == 0) & (k == 0))
        def _():
            a_scale[...] = jnp.zeros_like(a_scale)
            a_shift[...] = jnp.zeros_like(a_shift)
            a_g[...] = jnp.zeros_like(a_g)

        acc_ref[...] += lax.dot_general(z_ref[...], w_ref[...], (((1,), (1,)), ((), ())),
                                        preferred_element_type=F32)

        @pl.when(k == N_DEV - 1)
        def _():
            gv, sc = g_ref[...], sc_ref[...]

            def chunk(ci, carry):
                rows = pl.ds(pl.multiple_of(ci * NORM_BWD_ROWS, NORM_BWD_ROWS), NORM_BWD_ROWS)
                dh, xv = acc_ref[rows, :], x_ref[rows, :]
                r = lax.rsqrt(jnp.mean(xv * xv, axis=-1, keepdims=True) + NORM_EPS)
                xhat = xv * r
                dn = dh * (1.0 + sc)
                dxhat = dn * gv
                dx_ref[rows, :] = r * (dxhat - xhat * jnp.mean(dxhat * xhat, axis=-1, keepdims=True)) + dp_ref[rows, :]
                a_scale[...] += _fold_rows(dh * (xhat * gv))
                a_shift[...] += _fold_rows(dh)
                a_g[...] += _fold_rows(dn * xhat)
                return carry
            lax.fori_loop(0, tm // NORM_BWD_ROWS, chunk, 0)

        @pl.when((i == ni - 1) & (k == N_DEV - 1))
        def _():
            dscale_ref[...] = jnp.sum(a_scale[...], axis=0, keepdims=True)
            dshift_ref[...] = jnp.sum(a_shift[...], axis=0, keepdims=True)
            dg_ref[...] = jnp.sum(a_g[...], axis=0, keepdims=True)

    rows = pl.BlockSpec((tm, d), lambda i, k: (i, 0))
    vec = pl.BlockSpec((1, d), lambda i, k: (0, 0))
    return _call(body, name=name, grid=(ni, N_DEV),
                 in_specs=[pl.BlockSpec((None, tm, bn), lambda i, k: (k // per, i, k % per)),
                           pl.BlockSpec((None, d, bn), lambda i, k: (k, 0, 0)), rows, rows, vec, vec],
                 out_specs=[rows, vec, vec, vec],
                 out_shape=[_sds((t, d), F32), _sds((1, d), F32), _sds((1, d), F32), _sds((1, d), F32)],
                 scratch=[pltpu.VMEM((tm, d), F32)] + [pltpu.VMEM((V7X_SUBLANES, d), F32)] * 3,
                 sem=("arbitrary", "arbitrary"))(z3, w_blk, x, dxp, g, scale)


def _mm_tn(a, b3, ta, tb, blocked, name):
    t, ka = a.shape
    lead, _, w = b3.shape
    per = w // tb
    nj = lead * per
    tk = t
    while tk > 512 and 4 * tk * (ta + tb) + 8 * ta * tb > V7X_VMEM_LIMIT * 3 // 4:
        tk //= 2
    tk = _tile(t, tk, V7X_SUBLANES)
    nk = t // tk

    def body(a_ref, b_ref, o_ref, acc_ref):
        k = pl.program_id(2)

        @pl.when(k == 0)
        def _():
            acc_ref[...] = jnp.zeros_like(acc_ref)

        acc_ref[...] += lax.dot_general(a_ref[...], b_ref[...], (((0,), (0,)), ((), ())),
                                        preferred_element_type=F32)

        @pl.when(k == nk - 1)
        def _():
            o_ref[...] = acc_ref[...].astype(BF)

    if blocked:
        out_shape = _sds((nj, ka, tb), BF)
        out_spec = pl.BlockSpec((None, ta, tb), lambda i, j, k: (j, i, 0))
    else:
        out_shape = _sds((1, ka, w), BF)
        out_spec = pl.BlockSpec((None, ta, tb), lambda i, j, k: (0, i, j))
    return _call(body, name=name, grid=(ka // ta, nj, nk),
                 in_specs=[pl.BlockSpec((tk, ta), lambda i, j, k: (k, i)),
                           pl.BlockSpec((None, tk, tb), lambda i, j, k: (j // per, k, j % per))],
                 out_specs=out_spec, out_shape=out_shape, scratch=[pltpu.VMEM((ta, tb), F32)],
                 sem=("parallel", "parallel", "arbitrary"))(a, b3)


def _causal(ws):
    l = ws.shape[0]
    row = lax.broadcasted_iota(jnp.int32, (l, l), 0)
    col = lax.broadcasted_iota(jnp.int32, (l, l), 1)
    return jnp.where(col <= row, ws, 0.0)


def _sgu_fwd(u, vn, ws, bsb, name):
    t, e = u.shape
    hn, l, _ = ws.shape
    dh = e // hn
    nc = t // l

    def body(u_ref, v_ref, ws_ref, bs_ref, s_ref):
        wsc = _causal(ws_ref[...]).astype(BF)
        bias = bs_ref[...]

        def chunk(c, carry):
            rows = pl.ds(pl.multiple_of(c * l, l), l)
            vo = jnp.dot(wsc, v_ref[rows, :], preferred_element_type=F32) + bias
            s_ref[rows, :] = (u_ref[rows, :].astype(F32) * vo).astype(BF)
            return carry
        lax.fori_loop(0, nc, chunk, 0)

    col = pl.BlockSpec((t, dh), lambda h: (0, h))
    return _call(body, name=name, grid=(hn,),
                 in_specs=[col, col, pl.BlockSpec((None, l, l), lambda h: (h, 0, 0)),
                           pl.BlockSpec((None, l, dh), lambda h: (h, 0, 0))],
                 out_specs=col, out_shape=_sds((t, e), BF), sem=("parallel",))(u, vn, ws, bsb)


def _sgu_bwd(ds, u, vn, ws, bsb, name):
    t, e = u.shape
    hn, l, _ = ws.shape
    dh = e // hn
    nc = t // l

    def body(ds_ref, u_ref, v_ref, ws_ref, bs_ref, du_ref, dv_ref, dws_ref, dbs_ref, accw_ref, accb_ref):
        wsc = _causal(ws_ref[...]).astype(BF)
        bias = bs_ref[...]
        accw_ref[...] = jnp.zeros_like(accw_ref)
        accb_ref[...] = jnp.zeros_like(accb_ref)

        def chunk(c, carry):
            rows = pl.ds(pl.multiple_of(c * l, l), l)
            vc = v_ref[rows, :]
            dsv = ds_ref[rows, :].astype(F32)
            vo = jnp.dot(wsc, vc, preferred_element_type=F32) + bias
            du_ref[rows, :] = (dsv * vo).astype(BF)
            dvo = dsv * u_ref[rows, :].astype(F32)
            dvo_b = dvo.astype(BF)
            accb_ref[...] += dvo
            accw_ref[...] += lax.dot_general(dvo_b, vc, (((1,), (1,)), ((), ())), preferred_element_type=F32)
            dv_ref[rows, :] = lax.dot_general(wsc, dvo_b, (((0,), (0,)), ((), ())),
                                              preferred_element_type=F32).astype(BF)
            return carry
        lax.fori_loop(0, nc, chunk, 0)
        dws_ref[...] = _causal(accw_ref[...])
        dbs_ref[...] = jnp.broadcast_to(jnp.sum(accb_ref[...], axis=1, keepdims=True), (l, dh))

    col = pl.BlockSpec((t, dh), lambda h: (0, h))
    return _call(body, name=name, grid=(hn,),
                 in_specs=[col, col, col, pl.BlockSpec((None, l, l), lambda h: (h, 0, 0)),
                           pl.BlockSpec((None, l, dh), lambda h: (h, 0, 0))],
                 out_specs=[col, col, pl.BlockSpec((None, l, l), lambda h: (h, 0, 0)),
                            pl.BlockSpec((None, l, dh), lambda h: (h, 0, 0))],
                 out_shape=[_sds((t, e), BF), _sds((t, e), BF), _sds((hn, l, l), F32), _sds((hn, l, dh), F32)],
                 scratch=[pltpu.VMEM((l, l), F32), pltpu.VMEM((l, dh), F32)],
                 sem=("parallel",))(ds, u, vn, ws, bsb)


AG_AHEAD = 3

CONV_HALO = 32
CONV_ROWS = 64
CONV_LANES = 256


def _shifted_windows(win_ref, sh_ref, rows):
    for b in range(1, V7X_SUBLANES):
        sh_ref[b - 1, 0:rows, :] = win_ref[b:b + rows, :]


def _window_rows(win_ref, sh_ref, shift, r0, rows):
    a, b = divmod(shift, V7X_SUBLANES)
    start = pl.multiple_of(r0 + V7X_SUBLANES * a, V7X_SUBLANES)
    if b == 0:
        return win_ref[pl.ds(start, rows), :]
    return sh_ref[b - 1, pl.ds(start, rows), :]


def _dwconv_fwd(p, dw_w, dw_b, name):
    t, c2 = p.shape
    cw = c2 // 2
    kw = dw_w.shape[0]
    cb = _tile(cw, CONV_LANES, V7X_LANES)
    ncb = cw // cb
    tm = _tile(t, 512, CONV_ROWS)
    off = CONV_HALO - (kw - 1)

    def body(a_ref, g_ref, ap_ref, gp_ref, w_ref, b_ref, o_ref, win_ref, sh_ref):
        i = pl.program_id(1)
        prev = ap_ref[...].astype(F32) * _sigmoid(gp_ref[...].astype(F32))
        win_ref[0:CONV_HALO, :] = jnp.where(i > 0, prev, 0.0)
        win_ref[CONV_HALO:, :] = a_ref[...].astype(F32) * _sigmoid(g_ref[...].astype(F32))
        _shifted_windows(win_ref, sh_ref, tm + CONV_HALO - V7X_SUBLANES)

        def chunk(ci, carry):
            r0 = ci * CONV_ROWS
            acc = jnp.zeros((CONV_ROWS, cb), F32) + b_ref[...]
            for k in range(kw):
                acc = acc + w_ref[k:k + 1, :] * _window_rows(win_ref, sh_ref, off + k, r0, CONV_ROWS)
            o_ref[pl.ds(pl.multiple_of(r0, CONV_ROWS), CONV_ROWS), :] = acc
            return carry
        lax.fori_loop(0, tm // CONV_ROWS, chunk, 0)

    hpt = tm // CONV_HALO
    cur_a = pl.BlockSpec((tm, cb), lambda j, i: (i, j))
    cur_g = pl.BlockSpec((tm, cb), lambda j, i: (i, ncb + j))
    prev_a = pl.BlockSpec((CONV_HALO, cb), lambda j, i: (jnp.maximum(i * hpt - 1, 0), j))
    prev_g = pl.BlockSpec((CONV_HALO, cb), lambda j, i: (jnp.maximum(i * hpt - 1, 0), ncb + j))
    return _call(body, name=name, grid=(ncb, t // tm),
                 in_specs=[cur_a, cur_g, prev_a, prev_g, pl.BlockSpec((kw, cb), lambda j, i: (0, j)),
                           pl.BlockSpec((1, cb), lambda j, i: (0, j))],
                 out_specs=pl.BlockSpec((tm, cb), lambda j, i: (i, j)), out_shape=_sds((t, cw), F32),
                 scratch=[pltpu.VMEM((tm + CONV_HALO, cb), F32),
                          pltpu.VMEM((V7X_SUBLANES - 1, tm + CONV_HALO - V7X_SUBLANES, cb), F32)],
                 sem=("parallel", "parallel"))(p, p, p, p, dw_w, dw_b)


def _dwconv_bwd(dyc, p, dw_w, name):
    t, c2 = p.shape
    cw = c2 // 2
    kw = dw_w.shape[0]
    cb = _tile(cw, CONV_LANES, V7X_LANES)
    ncb = cw // cb
    tm = _tile(t, 512, CONV_ROWS)
    nt = t // tm
    off = CONV_HALO - (kw - 1)
    kpad = -(-kw // V7X_SUBLANES) * V7X_SUBLANES
    sh_rows = tm + CONV_HALO - V7X_SUBLANES

    def body(d_ref, dn_ref, a_ref, g_ref, ap_ref, gp_ref, w_ref,
             dp_ref, dw_ref, dba_ref, dbg_ref, dwin_ref, ywin_ref, dsh_ref, ysh_ref, accw_ref, acca_ref, accg_ref):
        i = pl.program_id(1)

        @pl.when(i == 0)
        def _():
            accw_ref[...] = jnp.zeros_like(accw_ref)
            acca_ref[...] = jnp.zeros_like(acca_ref)
            accg_ref[...] = jnp.zeros_like(accg_ref)

        prev = ap_ref[...].astype(F32) * _sigmoid(gp_ref[...].astype(F32))
        ywin_ref[0:CONV_HALO, :] = jnp.where(i > 0, prev, 0.0)
        ywin_ref[CONV_HALO:, :] = a_ref[...].astype(F32) * _sigmoid(g_ref[...].astype(F32))
        dwin_ref[0:tm, :] = d_ref[...]
        dwin_ref[tm:, :] = jnp.where(i < nt - 1, dn_ref[...], 0.0)
        _shifted_windows(ywin_ref, ysh_ref, sh_rows)
        _shifted_windows(dwin_ref, dsh_ref, sh_rows)

        def chunk(ci, carry):
            r0 = ci * CONV_ROWS
            rows = pl.ds(pl.multiple_of(r0, CONV_ROWS), CONV_ROWS)
            dcur = d_ref[rows, :]
            dyg = jnp.zeros((CONV_ROWS, cb), F32)
            for k in range(kw):
                dyg = dyg + w_ref[k:k + 1, :] * _window_rows(dwin_ref, dsh_ref, kw - 1 - k, r0, CONV_ROWS)
                accw_ref[k] += _fold_rows(dcur * _window_rows(ywin_ref, ysh_ref, off + k, r0, CONV_ROWS))
            av = a_ref[rows, :].astype(F32)
            sig = _sigmoid(g_ref[rows, :].astype(F32))
            da = dyg * sig
            dg = dyg * av * sig * (1.0 - sig)
            dp_ref[0, rows, :] = da.astype(BF)
            dp_ref[1, rows, :] = dg.astype(BF)
            acca_ref[...] += _fold_rows(da)
            accg_ref[...] += _fold_rows(dg)
            return carry
        lax.fori_loop(0, tm // CONV_ROWS, chunk, 0)

        @pl.when(i == nt - 1)
        def _():
            dw_ref[...] = jnp.sum(accw_ref[...], axis=1)
            dba_ref[...] = jnp.sum(acca_ref[...], axis=0, keepdims=True)
            dbg_ref[...] = jnp.sum(accg_ref[...], axis=0, keepdims=True)

    hpt = tm // CONV_HALO
    last_halo = t // CONV_HALO - 1
    tile = pl.BlockSpec((tm, cb), lambda j, i: (i, j))
    cur_g = pl.BlockSpec((tm, cb), lambda j, i: (i, ncb + j))
    nxt = pl.BlockSpec((CONV_HALO, cb), lambda j, i: (jnp.minimum((i + 1) * hpt, last_halo), j))
    prev_a = pl.BlockSpec((CONV_HALO, cb), lambda j, i: (jnp.maximum(i * hpt - 1, 0), j))
    prev_g = pl.BlockSpec((CONV_HALO, cb), lambda j, i: (jnp.maximum(i * hpt - 1, 0), ncb + j))
    vec = pl.BlockSpec((1, cb), lambda j, i: (0, j))
    dp, ddw, dba, dbg = _call(
        body, name=name, grid=(ncb, nt),
        in_specs=[tile, nxt, tile, cur_g, prev_a, prev_g, pl.BlockSpec((kw, cb), lambda j, i: (0, j))],
        out_specs=[pl.BlockSpec((2, tm, cb), lambda j, i: (0, i, j)), pl.BlockSpec((kpad, cb), lambda j, i: (0, j)),
                   vec, vec],
        out_shape=[_sds((2, t, cw), BF), _sds((kpad, cw), F32), _sds((1, cw), F32), _sds((1, cw), F32)],
        scratch=[pltpu.VMEM((tm + CONV_HALO, cb), F32), pltpu.VMEM((tm + CONV_HALO, cb), F32),
                 pltpu.VMEM((V7X_SUBLANES - 1, sh_rows, cb), F32), pltpu.VMEM((V7X_SUBLANES - 1, sh_rows, cb), F32),
                 pltpu.VMEM((kpad, V7X_SUBLANES, cb), F32), pltpu.VMEM((V7X_SUBLANES, cb), F32),
                 pltpu.VMEM((V7X_SUBLANES, cb), F32)],
        sem=("parallel", "arbitrary"))(dyc, dyc, p, p, p, p, dw_w)
    return dp, ddw[:kw], dba, dbg


def _adam_math(g, w, m, v):
    m2 = ADAM_B1 * m + (1.0 - ADAM_B1) * g
    v2 = ADAM_B2 * v + (1.0 - ADAM_B2) * (g * g)
    m_hat = m2 / (1.0 - ADAM_B1 ** ADAM_STEP)
    v_hat = v2 / (1.0 - ADAM_B2 ** ADAM_STEP)
    delta = -ADAM_LR * (m_hat / (jnp.sqrt(v_hat) + ADAM_EPS) + ADAM_WD * w)
    return delta, m2, v2


def _adamw(g_parts, w, m, v, name):
    r, c = w.shape
    tr = _tile(r, 256, V7X_SUBLANES)
    ng = len(g_parts)

    def body(*refs):
        g = refs[0][...].astype(F32)
        for s in refs[1:ng]:
            g = g + s[...].astype(F32)
        w_ref, m_ref, v_ref, go_ref, d_ref, mo_ref, vo_ref = refs[ng:]
        delta, m2, v2 = _adam_math(g, w_ref[...], m_ref[...], v_ref[...])
        go_ref[...] = g
        d_ref[...] = delta
        mo_ref[...] = m2
        vo_ref[...] = v2

    tile = pl.BlockSpec((tr, c), lambda i: (i, 0))
    in_specs = [pl.BlockSpec((None, tr, c), functools.partial(lambda s, i: (s, i, 0), s)) for _, s in g_parts]
    out = _sds((r, c), F32)
    return _call(body, name=name, grid=(r // tr,), in_specs=in_specs + [tile, tile, tile],
                 out_specs=[tile] * 4, out_shape=[out] * 4, sem=("parallel",))(*[a for a, _ in g_parts], w, m, v)


def _adamw_stacked(h, recv, chip, w_st, m_st, v_st, k, prev, name):
    kk, r, c = w_st.shape
    tr = _tile(r, 256, V7X_SUBLANES)
    if prev is None:
        prev = [lax.empty((kk, r, c), F32) for _ in range(4)]

    def body(chip_ref, h_ref, r0_ref, r1_ref, r2_ref, w_ref, m_ref, v_ref, pg, pd, pm, pv,
             go_ref, d_ref, mo_ref, vo_ref):
        g = (h_ref[...].astype(F32) + r0_ref[...].astype(F32)) + (r1_ref[...].astype(F32) + r2_ref[...].astype(F32))
        delta, m2, v2 = _adam_math(g, w_ref[...], m_ref[...], v_ref[...])
        go_ref[...] = g
        d_ref[...] = delta
        mo_ref[...] = m2
        vo_ref[...] = v2

    own = pl.BlockSpec((None, tr, c), lambda i, chip_ref: (chip_ref[0], i, 0))
    rcv = [pl.BlockSpec((None, tr, c), functools.partial(lambda s, i, chip_ref: (s, i, 0), s)) for s in range(3)]
    blk = pl.BlockSpec((None, tr, c), lambda i, chip_ref: (k, i, 0))
    out = _sds((kk, r, c), F32)
    return _call(body, name=name, grid=(r // tr,), in_specs=[own] + rcv + [blk, blk, blk] + [ANY] * 4,
                 out_specs=[blk] * 4, out_shape=[out] * 4, sem=("parallel",), prefetch=chip, on_path=False,
                 aliases={8: 0, 9: 1, 10: 2, 11: 3})(h, recv, recv, recv, w_st, m_st, v_st, *prev)


def _add_sibling(g4, land, core, name):
    n, _, r, c = g4.shape
    tr = _tile(r, 512, V7X_SUBLANES)

    def body(core_ref, a_ref, b_ref, o_ref):
        o_ref[...] = (a_ref[...].astype(F32) + b_ref[...].astype(F32)).astype(BF)

    return _call(body, name=name, grid=(n, r // tr),
                 in_specs=[pl.BlockSpec((None, None, tr, c), lambda p, i, core_ref: (p, core_ref[0], i, 0)),
                           pl.BlockSpec((None, None, tr, c), lambda p, i, core_ref: (p, 0, i, 0))],
                 out_specs=pl.BlockSpec((None, tr, c), lambda p, i, core_ref: (p, i, 0)),
                 out_shape=_sds((n, r, c), BF), sem=("parallel", "parallel"), prefetch=core)(g4, land)


def _cast_to_slot(w, lead, me, name, after=()):
    r, c = w.shape[-2:]
    nl = len(lead)
    tr = _tile(r, 512, 2 * V7X_SUBLANES)

    def body(me_ref, w_ref, o_ref):
        o_ref[...] = w_ref[...].astype(BF)

    return _call(body, name=name, grid=(r // tr,),
                 in_specs=[pl.BlockSpec((None,) * nl + (tr, c), lambda i, me_ref: tuple(lead) + (i, 0))],
                 out_specs=pl.BlockSpec((None, None, tr, c), lambda i, me_ref: (0, me_ref[0], i, 0)),
                 out_shape=_sds((1, N_DEV, r, c), BF), sem=("parallel",), prefetch=me, after=after)(w)


def _ada_fwd(c_pad, ada_w, ada_b, name):
    nl, d, cl = ada_w.shape
    rows = c_pad.shape[0]
    tn = _tile(cl, 256, V7X_LANES)

    def body(c_ref, w_ref, b_ref, o_ref):
        cv = c_ref[...]
        cond = (cv * _sigmoid(cv)).astype(BF)
        o_ref[...] = jnp.dot(cond, w_ref[...].astype(BF), preferred_element_type=F32) + b_ref[...]

    return _call(body, name=name, grid=(nl, cl // tn),
                 in_specs=[pl.BlockSpec((rows, d), lambda l, j: (0, 0)),
                           pl.BlockSpec((None, d, tn), lambda l, j: (l, 0, j)),
                           pl.BlockSpec((None, 1, tn), lambda l, j: (l, 0, j))],
                 out_specs=pl.BlockSpec((None, rows, tn), lambda l, j: (l, 0, j)),
                 out_shape=_sds((nl, rows, cl), F32), sem=("parallel", "parallel"))(c_pad, ada_w, ada_b)


def _ada_bwd(c_pad, dmod, w, m, v, name):
    nl, d, cl = w.shape
    rows = c_pad.shape[0]
    tn = _tile(cl, 256, V7X_LANES)

    def body(c_ref, dm_ref, w_ref, m_ref, v_ref, go_ref, d_ref, mo_ref, vo_ref):
        cv = c_ref[...]
        cond = (cv * _sigmoid(cv)).astype(BF)
        g = lax.dot_general(cond, dm_ref[...].astype(BF), (((0,), (0,)), ((), ())), preferred_element_type=F32)
        delta, m2, v2 = _adam_math(g, w_ref[...], m_ref[...], v_ref[...])
        go_ref[...] = g
        d_ref[...] = delta
        mo_ref[...] = m2
        vo_ref[...] = v2

    tile = pl.BlockSpec((None, d, tn), lambda l, j: (l, 0, j))
    out = _sds((nl, d, cl), F32)
    return _call(body, name=name, grid=(nl, cl // tn),
                 in_specs=[pl.BlockSpec((rows, d), lambda l, j: (0, 0)),
                           pl.BlockSpec((None, rows, tn), lambda l, j: (l, 0, j)), tile, tile, tile],
                 out_specs=[tile] * 4, out_shape=[out] * 4, sem=("parallel", "parallel"))(c_pad, dmod, w, m, v)


def _sum_devices(parts, name):
    n, r, c = parts.shape
    tr = _tile(r, 512, V7X_SUBLANES)

    def body(p_ref, o_ref):
        acc = p_ref[0]
        for k in range(1, n):
            acc = acc + p_ref[k]
        o_ref[...] = acc

    return _call(body, name=name, grid=(r // tr,), in_specs=[pl.BlockSpec((n, tr, c), lambda i: (0, i, 0))],
                 out_specs=pl.BlockSpec((tr, c), lambda i: (i, 0)), out_shape=_sds((r, c), F32),
                 sem=("parallel",))(parts)


def _mesh_pos():
    return lax.axis_index("x"), lax.axis_index("y"), lax.axis_index("c")


def _other_chips(x, y):
    return [(1 - x, y), (x, 1 - y), (1 - x, 1 - y)]


def _all_gather(arrs, name):
    n = len(arrs)

    def body(*refs):
        ins, outs = refs[:n], refs[n:2 * n]
        send_sems, recv_sems, local_sems = refs[2 * n:]
        x, y, c = _mesh_pos()
        me, sibling = (x, y, c), (x, y, 1 - c)
        chips = _other_chips(x, y)

        def slot(a, pos):
            px, py, pc = pos
            return outs[a].at[:, pl.ds(4 * px + 2 * py + pc, 1)]

        def copy(a, k, block, to, src=None):
            return pltpu.make_async_remote_copy(
                src_ref=slot(a, block) if src is None else src, dst_ref=slot(a, block),
                send_sem=send_sems.at[a, k], recv_sem=recv_sems.at[a, k], device_id=to, device_id_type=MESH)

        mine = [pltpu.make_async_copy(ins[a], slot(a, me), local_sems.at[a]) for a in range(n)]
        for cp in mine:
            cp.start()
        first = []
        for a in range(n):
            first.append(copy(a, 0, me, sibling, src=ins[a]))
            first += [copy(a, 1 + j, me, (*chip, c), src=ins[a]) for j, chip in enumerate(chips)]
        for cp in first:
            cp.start()
        passed = []
        for a in range(n):
            for j, chip in enumerate(chips):
                copy(a, 1 + j, (*chip, c), me).wait_recv()
                fwd = copy(a, 4 + j, (*chip, c), sibling)
                fwd.start()
                passed.append(fwd)
        for a in range(n):
            copy(a, 0, sibling, me).wait_recv()
            for j, chip in enumerate(chips):
                copy(a, 4 + j, (*chip, 1 - c), me).wait_recv()
        for cp in first + passed:
            cp.wait_send()
        for cp in mine:
            cp.wait()

    out_shape = [_sds((a.shape[0], N_DEV) + a.shape[2:], a.dtype) for a in arrs]
    return pl.pallas_call(
        body, out_shape=out_shape, in_specs=[ANY] * n, out_specs=[ANY] * n, name=name,
        scratch_shapes=[pltpu.SemaphoreType.DMA((n, N_DEV - 1)), pltpu.SemaphoreType.DMA((n, N_DEV - 1)),
                        pltpu.SemaphoreType.DMA((n,))])(*arrs)


HBM = pl.BlockSpec(memory_space=pltpu.HBM)
SEM = pl.BlockSpec(memory_space=pltpu.SEMAPHORE)


def _hbm(v):
    return pltpu.with_memory_space_constraint(v, pltpu.HBM)


def _comm_call(body, name, bufs, sems_in, sems_out):
    after = [] if not sems_in or _Seq.last is None or any(_Seq.last is b for b in bufs) else [_Seq.last]
    nb, ni, na, no = len(bufs), len(sems_in), len(after), len(sems_out)

    def wrapped(*refs):
        body(refs[:nb], refs[nb:nb + ni], refs[nb + ni + na:nb + ni + na + no])
        if no:
            refs[-1][...] = jnp.zeros_like(refs[-1])

    out_shape = [pltpu.SemaphoreType.DMA(s) for s in sems_out] + [pltpu.HBM(b.shape, b.dtype) for b in bufs]
    out_specs = [SEM] * no + [HBM] * nb
    if no:
        out_shape.append(_sds((V7X_SUBLANES, V7X_LANES), F32))
        out_specs.append(pl.BlockSpec(memory_space=pltpu.VMEM))
    res = pl.pallas_call(
        wrapped, name=name, out_shape=out_shape, in_specs=[HBM] * nb + [SEM] * ni + [ANY] * na, out_specs=out_specs,
        input_output_aliases={i: no + i for i in range(nb)},
        compiler_params=pltpu.CompilerParams(has_side_effects=pltpu.SideEffectType.DATAFLOW_SIDE_EFFECTING),
    )(*bufs, *sems_in, *after)
    out_bufs = list(res[no:no + nb])
    if no:
        _Seq.tokens.append(res[-1])
    _Seq.last = out_bufs[0]
    return list(res[:no]), out_bufs


def _remote(src, dst, send_sem, recv_sem, to):
    return pltpu.make_async_remote_copy(src_ref=src, dst_ref=dst, send_sem=send_sem, recv_sem=recv_sem,
                                        device_id=to, device_id_type=MESH)


def _slot(ref, pos):
    px, py, pc = pos
    return ref.at[:, pl.ds(4 * px + 2 * py + pc, 1)]


def _ag_start(bufs, name):
    n = len(bufs)

    def body(b, _, sems):
        send_sib, recv_sib, send_ici, recv_ici = sems
        x, y, c = _mesh_pos()
        for a in range(n):
            mine = _slot(b[a], (x, y, c))
            _remote(mine, mine, send_sib.at[a], recv_sib.at[a], (x, y, 1 - c)).start()
            for j, (px, py) in enumerate(_other_chips(x, y)):
                _remote(mine, mine, send_ici.at[3 * a + j], recv_ici.at[3 * a + j], (px, py, c)).start()

    sems, bufs = _comm_call(body, name, [_hbm(b) for b in bufs], [], [(n,), (n,), (3 * n,), (3 * n,)])
    return dict(bufs=bufs, send_sib=sems[0], recv_sib=sems[1], send_ici=sems[2], recv_ici=sems[3])


def _ag_mid(st, name):
    n = len(st["bufs"])

    def body(b, sems_in, sems):
        (recv_ici,) = sems_in
        send_fwd, recv_fwd = sems
        x, y, c = _mesh_pos()
        for a in range(n):
            for j, (px, py) in enumerate(_other_chips(x, y)):
                blk = _slot(b[a], (px, py, c))
                _remote(blk, blk, send_fwd.at[3 * a + j], recv_ici.at[3 * a + j], (x, y, 1 - c)).wait_recv()
                _remote(blk, blk, send_fwd.at[3 * a + j], recv_fwd.at[3 * a + j], (x, y, 1 - c)).start()

    sems, bufs = _comm_call(body, name, st["bufs"], [st["recv_ici"]], [(3 * n,), (3 * n,)])
    return dict(st, bufs=bufs, send_fwd=sems[0], recv_fwd=sems[1])


def _ag_end(st, name):
    n = len(st["bufs"])

    def body(b, sems_in, _):
        send_sib, recv_sib, send_ici, send_fwd, recv_fwd = sems_in
        x, y, c = _mesh_pos()
        sibling = (x, y, 1 - c)
        for a in range(n):
            mine, sib_blk = _slot(b[a], (x, y, c)), _slot(b[a], sibling)
            _remote(mine, mine, send_sib.at[a], recv_sib.at[a], sibling).wait_send()
            _remote(sib_blk, sib_blk, send_sib.at[a], recv_sib.at[a], sibling).wait_recv()
            for j, (px, py) in enumerate(_other_chips(x, y)):
                blk, sib_got = _slot(b[a], (px, py, c)), _slot(b[a], (px, py, 1 - c))
                _remote(mine, mine, send_ici.at[3 * a + j], recv_sib.at[a], (px, py, c)).wait_send()
                _remote(blk, blk, send_fwd.at[3 * a + j], recv_fwd.at[3 * a + j], sibling).wait_send()
                _remote(sib_got, sib_got, send_fwd.at[3 * a + j], recv_fwd.at[3 * a + j], sibling).wait_recv()

    _, bufs = _comm_call(body, name, st["bufs"],
                         [st[k] for k in ("send_sib", "recv_sib", "send_ici", "send_fwd", "recv_fwd")], [])
    return bufs


def _rs_start(g4s, name):
    n = len(g4s)
    lands = [lax.empty((N_CHIP, 1) + g.shape[2:], g.dtype) for g in g4s]

    def body(b, _, sems):
        send, recv = sems
        x, y, c = _mesh_pos()
        for a in range(n):
            _remote(b[a].at[:, pl.ds(1 - c, 1)], b[n + a], send.at[a], recv.at[a], (x, y, 1 - c)).start()

    sems, bufs = _comm_call(body, name, [_hbm(v) for v in list(g4s) + lands], [], [(n,), (n,)])
    return dict(bufs=bufs, send=sems[0], recv=sems[1])


def _rs_mid(st, name):
    n = len(st["bufs"]) // 2

    def body(b, sems_in, _):
        send, recv = sems_in
        x, y, c = _mesh_pos()
        for a in range(n):
            cp = _remote(b[a].at[:, pl.ds(1 - c, 1)], b[n + a], send.at[a], recv.at[a], (x, y, 1 - c))
            cp.wait_send()
            cp.wait_recv()

    _, bufs = _comm_call(body, name, st["bufs"], [st["send"], st["recv"]], [])
    return bufs[:n], bufs[n:]


def _rs_start2(sums, name):
    n = len(sums)
    lands = [lax.empty((N_CHIP - 1,) + s.shape[1:], s.dtype) for s in sums]

    def body(b, _, sems):
        send, recv = sems
        x, y, c = _mesh_pos()
        for a in range(n):
            for j, (px, py) in enumerate(_other_chips(x, y)):
                _remote(b[a].at[pl.ds(2 * px + py, 1)], b[n + a].at[pl.ds(j, 1)], send.at[3 * a + j], recv.at[3 * a + j],
                        (px, py, c)).start()

    sems, bufs = _comm_call(body, name, [_hbm(v) for v in list(sums) + lands], [], [(3 * n,), (3 * n,)])
    return dict(bufs=bufs, send=sems[0], recv=sems[1])


def _rs_end(st, name):
    n = len(st["bufs"]) // 2

    def body(b, sems_in, _):
        send, recv = sems_in
        x, y, c = _mesh_pos()
        for a in range(n):
            for j, (px, py) in enumerate(_other_chips(x, y)):
                cp = _remote(b[a].at[pl.ds(2 * px + py, 1)], b[n + a].at[pl.ds(j, 1)], send.at[3 * a + j], recv.at[3 * a + j],
                             (px, py, c))
                cp.wait_send()
                cp.wait_recv()

    _, bufs = _comm_call(body, name, st["bufs"], [st["send"], st["recv"]], [])
    return bufs[:n], bufs[n:]


def _pack(parts, rows_align=V7X_SUBLANES):
    flat, total = [], 0
    for p in parts:
        v = p.reshape(-1).astype(F32)
        pad = -v.shape[0] % PACK_ALIGN
        flat.append(jnp.pad(v, (0, pad)) if pad else v)
        total += v.shape[0] + pad
    tail = -total % (rows_align * V7X_LANES)
    if tail:
        flat.append(jnp.zeros((tail,), F32))
    return jnp.concatenate(flat).reshape(-1, V7X_LANES)


def _unpack(buf, shapes):
    lead = buf.shape[:-2]
    flat = buf.reshape(lead + (-1,))
    out, pos = [], 0
    for s in shapes:
        size = 1
        for d in s:
            size *= d
        out.append(flat[..., pos:pos + size].reshape(lead + tuple(s)))
        pos += size + (-size % PACK_ALIGN)
    return out


def kernel(x, c, ada_w, ada_b, norm_g, ffn_w_in, ffn_w_out, gm_w_in, gm_ln_g, gm_ln_b, gm_ws, gm_bs, gm_w_out, cv_w_in, cv_b_in, cv_dw_w, cv_dw_b, cv_ln_g, cv_ln_b, cv_w_out, cv_b_out, final_g, loss_target, m_ada_w, m_ada_b, m_norm_g, m_ffn_w_in, m_ffn_w_out, m_gm_w_in, m_gm_ln_g, m_gm_ln_b, m_gm_ws, m_gm_bs, m_gm_w_out, m_cv_w_in, m_cv_b_in, m_cv_dw_w, m_cv_dw_b, m_cv_ln_g, m_cv_ln_b, m_cv_w_out, m_cv_b_out, m_final_g, v_ada_w, v_ada_b, v_norm_g, v_ffn_w_in, v_ffn_w_out, v_gm_w_in, v_gm_ln_g, v_gm_ln_b, v_gm_ws, v_gm_bs, v_gm_w_out, v_cv_w_in, v_cv_b_in, v_cv_dw_w, v_cv_dw_b, v_cv_ln_g, v_cv_ln_b, v_cv_w_out, v_cv_b_out, v_final_g):
    t, d = x.shape[1], x.shape[2]
    depth = ada_w.shape[0]
    assert depth == 2 and ffn_w_in.shape[:2] == (2, 2) and gm_w_in.shape[0] == 1 and cv_w_in.shape[0] == 1
    dl = d // N_DEV
    bn = ffn_w_in.shape[3]
    fl = ffn_w_out.shape[2]
    f = fl * N_DEV
    el = gm_w_in.shape[2]
    e = el * N_DEV // 2
    hn, l = gm_ws.shape[1], gm_ws.shape[2]
    kw = cv_dw_w.shape[1]
    cl = ada_w.shape[2]
    me = 4 * lax.axis_index("x") + 2 * lax.axis_index("y") + lax.axis_index("c")
    me1 = me.astype(jnp.int32).reshape(1)
    chip1 = (2 * lax.axis_index("x") + lax.axis_index("y")).astype(jnp.int32).reshape(1)
    core1 = lax.axis_index("c").astype(jnp.int32).reshape(1)
    _Seq.last, _Seq.tokens = None, []

    xs = x[0]
    tgt = loss_target[0]

    ag_groups = [("win00", [(ffn_w_in, (0, 0))]), ("wout00", [(ffn_w_out, (0, 0))]),
                 ("gm", [(gm_w_in, (0,)), (gm_w_out, (0,))]),
                 ("win01", [(ffn_w_in, (0, 1))]), ("wout01", [(ffn_w_out, (0, 1))]),
                 ("win10", [(ffn_w_in, (1, 0))]), ("wout10", [(ffn_w_out, (1, 0))]),
                 ("cv", [(cv_w_in, (0,)), (cv_w_out, (0,))]),
                 ("win11", [(ffn_w_in, (1, 1))]), ("wout11", [(ffn_w_out, (1, 1))])]
    ag_flight = {}

    ag_slots = {}

    def ag_cast(gi, after=()):
        gname, members = ag_groups[gi]
        ag_slots[gi] = [_cast_to_slot(w, lead, me1, name=f"cast_{gname}_{k}", after=after)
                        for k, (w, lead) in enumerate(members)]

    def ag_start(gi):
        ag_flight[gi] = _ag_start(ag_slots.pop(gi), name=f"ag_start_{ag_groups[gi][0]}")

    def ag_forward(gi):
        if gi in ag_flight and "send_fwd" not in ag_flight[gi]:
            ag_flight[gi] = _ag_mid(ag_flight[gi], name=f"ag_mid_{ag_groups[gi][0]}")

    def ag_take(gi):
        ag_forward(gi)
        bufs = _ag_end(ag_flight.pop(gi), name=f"ag_end_{ag_groups[gi][0]}")
        if gi > 0:
            ag_forward(gi + 1)
        if gi + AG_AHEAD < len(ag_groups):
            ag_start(gi + AG_AHEAD)
        return [b[0] for b in bufs]

    small_in = [c, norm_g, cv_b_in, cv_dw_w, cv_dw_b, cv_ln_g, cv_ln_b, cv_b_out]
    pack1 = _pack(small_in)
    (pack1_all,) = _all_gather([pack1[None, None]], name="ag_small")
    parts = _unpack(pack1_all[0], [s.shape for s in small_in])
    c_all = parts[0].reshape(N_DEV, d)
    ng_full = jnp.moveaxis(parts[1], 0, 2).reshape(depth, 3, d)
    cvb_in_full = parts[2].reshape(1, 2 * e)
    dww_full = jnp.moveaxis(parts[3][:, 0], 0, 1).reshape(kw, e)
    dwb_full, cln_g_full, cln_b_full, cvb_out_full = [p.reshape(1, d) for p in parts[4:8]]

    c_pad = jnp.pad(c_all, ((0, 16 - N_DEV), (0, 0)))
    ada_b_loc = lax.dynamic_slice_in_dim(ada_b, me * cl, cl, axis=1).reshape(depth, 1, cl)
    mod_part = _ada_fwd(c_pad, ada_w, ada_b_loc, name="ada_fwd")[:, :N_DEV]
    (mod_all,) = _all_gather([_pack([mod_part])[None, None]], name="ag_mod")
    mod_all = _unpack(mod_all[0], [mod_part.shape])[0]
    mod_mine = lax.dynamic_index_in_dim(mod_all, me, axis=2, keepdims=False)
    mod = jnp.moveaxis(mod_mine, 0, 1).reshape(depth, 3, 3, 1, d)

    for gi in range(len(ag_groups)):
        ag_cast(gi, after=[mod_all])
        if gi < AG_AHEAD:
            ag_start(gi)

    ws = gm_ws[0]
    bsb = jnp.broadcast_to(gm_bs[0][:, :, None], (hn, l, e // hn))
    gm_g, gm_b = gm_ln_g, gm_ln_b

    saved = []
    xcur = xs
    next_group = 0
    for i in range(depth):
        for s in range(3):
            shift, scale, gate = mod[i, s, 0], mod[i, s, 1], mod[i, s, 2]
            g_norm = ng_full[i, s][None]
            tag = f"l{i}s{s}"
            h = _norm_mod(xcur, g_norm, scale, shift, name=f"norm_mod_{tag}")
            if s != 1:
                (w_in_blk,) = ag_take(next_group)
                gg, uu, act = _ffn_in(h, w_in_blk, 0, name=f"ffn_in_{tag}")
                w_out3 = ag_take(next_group + 1)[0].reshape(1, f, d)
                next_group += 2
                xnext, yv = _out_proj(act, w_out3, 0, xcur, gate, None, 0.5, name=f"ffn_out_{tag}")
                saved.append(dict(x=xcur, h=h, g=gg, u=uu, y=yv, w_in=w_in_blk, w_out=w_out3))
            elif i % 2 == 0:
                gm_in_blk, gm_out = ag_take(next_group)
                gm_out3 = gm_out.reshape(1, e, d)
                next_group += 1
                pre = _in_proj(h, gm_in_blk, None, name=f"gm_in_{tag}")
                uu, vn = _gm_act(pre, gm_g, gm_b, name=f"gm_act_{tag}")
                sg = _sgu_fwd(uu, vn, ws, bsb, name=f"sgu_fwd_{tag}")
                xnext, yv = _out_proj(sg, gm_out3, 0, xcur, gate, None, 1.0, name=f"gm_out_{tag}")
                saved.append(dict(x=xcur, h=h, pre=pre, u=uu, vn=vn, sg=sg, y=yv, w_in=gm_in_blk, w_out=gm_out3))
            else:
                cv_in_blk, cv_out = ag_take(next_group)
                cv_out3 = cv_out.reshape(1, e, d)
                next_group += 1
                p = _in_proj(h, cv_in_blk, cvb_in_full, name=f"cv_in_{tag}")
                yc = _dwconv_fwd(p, dww_full, dwb_full, name=f"dwconv_fwd_{tag}")
                ys = _cv_act(yc, cln_g_full, cln_b_full, name=f"cv_act_{tag}")
                xnext, yv = _out_proj(ys, cv_out3, 0, xcur, gate, cvb_out_full, 1.0, name=f"cv_out_{tag}")
                saved.append(dict(x=xcur, h=h, p=p, yc=yc, ys=ys, y=yv, w_in=cv_in_blk, w_out=cv_out3))
            xcur = xnext

    sq, dx, d_final_g = _final_loss(xcur, tgt, final_g[None], name="final_loss")
    loss = lax.psum(0.5 / d * jnp.sum(sq), ("x", "y", "c"))

    dmod = [[[None] * 3 for _ in range(3)] for _ in range(depth)]
    d_norm_g = [[None] * 3 for _ in range(depth)]
    small = {}

    stacked = {
        "ffn_w_in": [a.reshape(4, d, bn) for a in (ffn_w_in, m_ffn_w_in, v_ffn_w_in)],
        "ffn_w_out": [a.reshape(4, fl, d) for a in (ffn_w_out, m_ffn_w_out, v_ffn_w_out)],
        "gm_w_in": [gm_w_in, m_gm_w_in, v_gm_w_in], "gm_w_out": [gm_w_out, m_gm_w_out, v_gm_w_out],
        "cv_w_in": [cv_w_in, m_cv_w_in, v_cv_w_in], "cv_w_out": [cv_w_out, m_cv_w_out, v_cv_w_out],
    }
    res_big = {}

    def rs_sibling(g4s, tag):
        return _rs_start(g4s, name=f"rs_start_{tag}"), tag

    def rs_chips(flight):
        st, tag = flight
        g4s, lands = _rs_mid(st, name=f"rs_mid_{tag}")
        sums = [_add_sibling(g4, land, core1, name=f"rs_add_{tag}_{k}") for k, (g4, land) in enumerate(zip(g4s, lands))]
        return _rs_start2(sums, name=f"rs_start2_{tag}"), tag

    def rs_finish(flight, targets):
        st, tag = flight
        sums, recvs = _rs_end(st, name=f"rs_end_{tag}")
        for (pname, k), hsum, recv in zip(targets, sums, recvs):
            w_st, m_st, v_st = stacked[pname]
            res_big[pname] = _adamw_stacked(hsum, recv, chip1, w_st, m_st, v_st, k, res_big.get(pname),
                                            name=f"adamw_{pname}_{k}")

    pending = []
    last_sibling = None
    for i in reversed(range(depth)):
        for s in reversed(range(3)):
            sv = saved[3 * i + s]
            shift, scale, gate = mod[i, s, 0], mod[i, s, 1], mod[i, s, 2]
            g_norm = ng_full[i, s][None]
            tag = f"l{i}s{s}"
            last = i == 0 and s == 0
            if s != 1:
                widx = 2 * i + s // 2
                dy, (dgate,) = _residual_bwd(dx, sv["y"], gate, 0.5, False, name=f"res_bwd_{tag}")
                dgu, act = _ffn_da(dy, sv["w_out"], 0, sv["g"], sv["u"], name=f"ffn_da_{tag}")
                if last:
                    g_in = _mm_tn(sv["h"], dgu, d, bn, True, name=f"ffn_dwin_{tag}").reshape(N_CHIP, 2, d, bn)
                    sib_in = rs_sibling([g_in], f"{tag}_in")
                    g_out = _mm_tn(act, dy[None], bn, d, False, name=f"ffn_dwout_{tag}").reshape(N_CHIP, 2, fl, d)
                    new_flights = [(rs_chips(sib_in), [("ffn_w_in", widx)])]
                    sib, targets = rs_sibling([g_out], f"{tag}_out"), [("ffn_w_out", widx)]
                else:
                    g_out = _mm_tn(act, dy[None], bn, d, False, name=f"ffn_dwout_{tag}").reshape(N_CHIP, 2, fl, d)
                    sib_out = rs_sibling([g_out], f"{tag}_out")
                    g_in = _mm_tn(sv["h"], dgu, d, bn, True, name=f"ffn_dwin_{tag}").reshape(N_CHIP, 2, d, bn)
                    new_flights = [(rs_chips(sib_out), [("ffn_w_out", widx)])]
                    sib, targets = rs_sibling([g_in], f"{tag}_in"), [("ffn_w_in", widx)]
                z3, w_blk = dgu, sv["w_in"]
            elif i % 2 == 0:
                dy, (dgate,) = _residual_bwd(dx, sv["y"], gate, 1.0, False, name=f"res_bwd_{tag}")
                ds = _mm_nt(dy, sv["w_out"], 0, name=f"gm_ds_{tag}")
                g_out = _mm_tn(sv["sg"], dy[None], _tile(e, 1024, V7X_LANES), d, False,
                               name=f"gm_dwout_{tag}").reshape(N_CHIP, 2, dl, d)
                du, dvn, dws, dbs = _sgu_bwd(ds, sv["u"], sv["vn"], ws, bsb, name=f"sgu_bwd_{tag}")
                dpre, dlng, dlnb = _gm_act_bwd(sv["pre"], du, dvn, gm_g, name=f"gm_act_bwd_{tag}")
                small["gm_ln_g"], small["gm_ln_b"] = dlng, dlnb
                small["gm_ws"], small["gm_bs"] = dws, dbs[:, :, 0]
                g_in = _mm_tn(sv["h"], dpre[None], d, el, True, name=f"gm_dwin_{tag}").reshape(N_CHIP, 2, d, el)
                targets, new_flights = [("gm_w_in", 0), ("gm_w_out", 0)], []
                sib = rs_sibling([g_in, g_out], tag)
                z3, w_blk = dpre[None], sv["w_in"]
            else:
                dy, (dgate, dbout) = _residual_bwd(dx, sv["y"], gate, 1.0, True, name=f"res_bwd_{tag}")
                dys = _mm_nt(dy, sv["w_out"], 0, name=f"cv_dys_{tag}")
                g_out = _mm_tn(sv["ys"], dy[None], _tile(e, 1024, V7X_LANES), d, False,
                               name=f"cv_dwout_{tag}").reshape(N_CHIP, 2, dl, d)
                dyc, dlng, dlnb, ddwb = _cv_act_bwd(dys, sv["yc"], cln_g_full, cln_b_full, name=f"cv_act_bwd_{tag}")
                dp, ddww, dba, dbg = _dwconv_bwd(dyc, sv["p"], dww_full, name=f"dwconv_bwd_{tag}")
                small["cv_b_out"], small["cv_ln_g"], small["cv_ln_b"], small["cv_dw_b"] = dbout, dlng, dlnb, ddwb
                small["cv_dw_w"] = ddww
                small["cv_b_in"] = jnp.concatenate([dba, dbg], axis=1)
                g_in = _mm_tn(sv["h"], dp, d, el, True, name=f"cv_dwin_{tag}").reshape(N_CHIP, 2, d, el)
                targets, new_flights = [("cv_w_in", 0), ("cv_w_out", 0)], []
                sib = rs_sibling([g_in, g_out], tag)
                z3, w_blk = dp, sv["w_in"]
            dx, dscale, dshift, dgn = _dh_norm_bwd(z3, w_blk, sv["x"], dx, g_norm, scale, name=f"dh_norm_bwd_{tag}")
            if last:
                last_sibling = (sib, targets)
            else:
                new_flights.append((rs_chips(sib), targets))
            dmod[i][s] = [dshift, dscale, dgate]
            d_norm_g[i][s] = dgn
            for flight in pending:
                rs_finish(*flight)
            pending = new_flights
    grad_x = dx[None]

    dmod_mine = jnp.concatenate([v for per_l in dmod for per_s in per_l for v in per_s], axis=1)
    dng_mine = jnp.concatenate([v for per_l in d_norm_g for v in per_l], axis=1)
    small_out = [dmod_mine, dng_mine, small["gm_ln_g"], small["gm_ln_b"], small["gm_ws"], small["gm_bs"],
                 small["cv_b_in"], small["cv_dw_w"], small["cv_dw_b"], small["cv_ln_g"], small["cv_ln_b"],
                 small["cv_b_out"], d_final_g]
    shapes2 = [s.shape for s in small_out]
    (pack2_all,) = _all_gather([_pack(small_out, rows_align=256)[None, None]], name="ag_small_grads")
    _Seq.last = pack2_all
    pending.append((rs_chips(last_sibling[0]), last_sibling[1]))
    summed = _unpack(_sum_devices(pack2_all[0], name="sum_small_grads"), shapes2)
    dmod_all = _unpack(pack2_all[0], shapes2)[0].reshape(N_DEV, depth, 9 * d)

    def my_cols(full, width):
        return lax.dynamic_slice_in_dim(full, me * width, width, axis=full.ndim - 1)

    g_ada_b = summed[0].reshape(depth, 9 * d)
    g_norm_g = my_cols(summed[1].reshape(depth, 3, d), dl)
    g_small = {
        "ada_b": g_ada_b, "norm_g": g_norm_g,
        "gm_ln_g": summed[2], "gm_ln_b": summed[3], "gm_ws": summed[4][None], "gm_bs": summed[5][None],
        "cv_b_in": my_cols(summed[6], el), "cv_dw_w": my_cols(summed[7], dl)[None],
        "cv_dw_b": my_cols(summed[8], dl), "cv_ln_g": my_cols(summed[9], dl), "cv_ln_b": my_cols(summed[10], dl),
        "cv_b_out": my_cols(summed[11], dl), "final_g": summed[12].reshape(d),
    }

    dm_loc = jnp.moveaxis(my_cols(dmod_all, cl), 0, 1)
    dm_loc = jnp.pad(dm_loc, ((0, 0), (0, 16 - N_DEV), (0, 0)))
    res_ada_w = _ada_bwd(c_pad, dm_loc, ada_w, m_ada_w, v_ada_w, name="ada_bwd_adamw")

    def flat2(a):
        return a.reshape(-1, a.shape[-1])

    small_params = {
        "ada_b": (ada_b, m_ada_b, v_ada_b), "norm_g": (norm_g, m_norm_g, v_norm_g),
        "gm_ln_g": (gm_ln_g, m_gm_ln_g, v_gm_ln_g), "gm_ln_b": (gm_ln_b, m_gm_ln_b, v_gm_ln_b),
        "gm_ws": (gm_ws, m_gm_ws, v_gm_ws), "gm_bs": (gm_bs, m_gm_bs, v_gm_bs),
        "cv_b_in": (cv_b_in, m_cv_b_in, v_cv_b_in), "cv_dw_w": (cv_dw_w, m_cv_dw_w, v_cv_dw_w),
        "cv_dw_b": (cv_dw_b, m_cv_dw_b, v_cv_dw_b), "cv_ln_g": (cv_ln_g, m_cv_ln_g, v_cv_ln_g),
        "cv_ln_b": (cv_ln_b, m_cv_ln_b, v_cv_ln_b), "cv_b_out": (cv_b_out, m_cv_b_out, v_cv_b_out),
        "final_g": (final_g, m_final_g, v_final_g),
    }
    res_small = {}
    for key, (w, m, v) in small_params.items():
        g2 = flat2(g_small[key].reshape(w.shape)) if w.ndim > 1 else g_small[key].reshape(1, -1)
        w2, m2, v2 = [flat2(a) if a.ndim > 1 else a.reshape(1, -1) for a in (w, m, v)]
        res_small[key] = [o.reshape(w.shape) for o in _adamw([(g2[None], 0)], w2, m2, v2, name=f"adamw_{key}")]

    for flight in pending:
        rs_finish(*flight)

    def big(name, k):
        if name == "ada_w":
            return res_ada_w[k]
        return res_big[name][k].reshape(stacked_shape[name])

    stacked_shape = {"ffn_w_in": ffn_w_in.shape, "ffn_w_out": ffn_w_out.shape, "gm_w_in": gm_w_in.shape,
                     "gm_w_out": gm_w_out.shape, "cv_w_in": cv_w_in.shape, "cv_w_out": cv_w_out.shape}

    order = ["ada_w", "ada_b", "norm_g", "ffn_w_in", "ffn_w_out", "gm_w_in", "gm_ln_g", "gm_ln_b", "gm_ws", "gm_bs",
             "gm_w_out", "cv_w_in", "cv_b_in", "cv_dw_w", "cv_dw_b", "cv_ln_g", "cv_ln_b", "cv_w_out", "cv_b_out",
             "final_g"]
    outs = [loss, grad_x]
    for k in range(4):
        for name in order:
            outs.append(res_small[name][k] if name in res_small else big(name, k))
    return tuple(outs)
```

```python
import functools

import jax
import jax.numpy as jnp
from jax import lax
from jax.experimental import pallas as pl
from jax.experimental.pallas import tpu as pltpu

F32 = jnp.float32
BF = jnp.bfloat16
MESH = pl.DeviceIdType.MESH

N_DEV = 8
N_CHIP = 4
NORM_EPS = 1e-6
ADAM_LR = 0.001
ADAM_B1 = 0.9
ADAM_B2 = 0.999
ADAM_EPS = 1e-08
ADAM_WD = 0.01
ADAM_STEP = 10

V7X_SUBLANES = 8
V7X_LANES = 128
PACK_ALIGN = V7X_SUBLANES * V7X_LANES
V7X_VMEM_LIMIT = 56 * 1024 * 1024


def _tile(n, pref, align):
    if n <= pref:
        return n
    t = pref - pref % align
    while t >= align:
        if n % t == 0:
            return t
        t -= align
    return n


ANY = pl.BlockSpec(memory_space=pl.ANY)


class _Seq:
    last = None
    tokens = []


def _call(body, *, name, grid, in_specs, out_specs, out_shape, scratch=(), sem=None, prefetch=None, aliases=None,
          after=(), on_path=True):
    def run(*args):
        if on_path:
            tokens, _Seq.tokens = _Seq.tokens + list(after), []
        else:
            tokens = list(after)
        lead = 0 if prefetch is None else 1
        n_in, n_tok = lead + len(args), len(tokens)

        def wrapped(*refs):
            body(*refs[:n_in], *refs[n_in + n_tok:])

        specs = list(in_specs) + [ANY] * n_tok
        params = pltpu.CompilerParams(dimension_semantics=sem, vmem_limit_bytes=V7X_VMEM_LIMIT)
        if prefetch is None:
            res = pl.pallas_call(wrapped, out_shape=out_shape, grid=grid, in_specs=specs, out_specs=out_specs,
                                 scratch_shapes=scratch, name=name, compiler_params=params,
                                 input_output_aliases=aliases or {})(*args, *tokens)
        else:
            grid_spec = pltpu.PrefetchScalarGridSpec(num_scalar_prefetch=1, grid=grid, in_specs=specs,
                                                     out_specs=out_specs, scratch_shapes=scratch)
            res = pl.pallas_call(wrapped, out_shape=out_shape, grid_spec=grid_spec, name=name,
                                 compiler_params=params, input_output_aliases=aliases or {})(prefetch, *args, *tokens)
        if on_path:
            _Seq.last = res[0] if isinstance(res, (list, tuple)) else res
        return res
    return run


def _sds(shape, dtype):
    return jax.ShapeDtypeStruct(tuple(shape), dtype)


def _sigmoid(v):
    return 1.0 / (1.0 + jnp.exp(-v))


def _normal_cdf_pdf(v):
    a = jnp.abs(v) * 0.7071067811865476
    t = 1.0 / (1.0 + 0.3275911 * a)
    poly = t * (0.254829592 + t * (-0.284496736 + t * (1.421413741 + t * (-1.453152027 + t * 1.061405429))))
    e = jnp.exp(-0.5 * v * v)
    half_erf = 0.5 - 0.5 * poly * e
    return 0.5 + jnp.where(v < 0, -half_erf, half_erf), 0.3989422804014327 * e


def _gelu(v):
    return v * _normal_cdf_pdf(v)[0]


def _fold_rows(val):
    rows, w = val.shape
    return val.reshape(rows // V7X_SUBLANES, V7X_SUBLANES, w).sum(axis=0)


def _rowwise(fn, name, rows_in, vecs_in, rows_out, acc_widths, tm=256):
    t = rows_in[0].shape[0]
    tm = _tile(t, tm, V7X_SUBLANES)
    steps = t // tm
    nr, nv, no, na = len(rows_in), len(vecs_in), len(rows_out), len(acc_widths)

    def body(*refs):
        rin, vin = refs[:nr], refs[nr:nr + nv]
        rout = refs[nr + nv:nr + nv + no]
        aout = refs[nr + nv + no:nr + nv + no + na]
        accs = refs[nr + nv + no + na:]
        i = pl.program_id(0)
        outs, acc_vals = fn(*[r[...] for r in rin], *[v[...] for v in vin])
        for r, o in zip(rout, outs):
            r[...] = o.astype(r.dtype)
        if na:
            @pl.when(i == 0)
            def _():
                for a in accs:
                    a[...] = jnp.zeros_like(a)

            for a, val in zip(accs, acc_vals):
                a[...] += _fold_rows(val)

            @pl.when(i == steps - 1)
            def _():
                for o, a in zip(aout, accs):
                    o[...] = jnp.sum(a[...], axis=0, keepdims=True)

    in_specs = [pl.BlockSpec((tm, r.shape[1]), lambda i: (i, 0)) for r in rows_in]
    in_specs += [pl.BlockSpec(v.shape, functools.partial(lambda nd, i: (0,) * nd, v.ndim)) for v in vecs_in]
    out_specs = [pl.BlockSpec((tm, r.shape[1]), lambda i: (i, 0)) for r in rows_out]
    out_specs += [pl.BlockSpec((1, w), lambda i: (0, 0)) for w in acc_widths]
    out_shape = list(rows_out) + [_sds((1, w), F32) for w in acc_widths]
    scratch = [pltpu.VMEM((V7X_SUBLANES, w), F32) for w in acc_widths]
    res = _call(body, name=name, grid=(steps,), in_specs=in_specs, out_specs=out_specs, out_shape=out_shape,
                scratch=scratch, sem=("arbitrary",) if na else ("parallel",))(*rows_in, *vecs_in)
    return res[:no], res[no:]


def _norm_mod(x, g, scale, shift, name):
    def fn(xv, gv, sc, sh):
        r = lax.rsqrt(jnp.mean(xv * xv, axis=-1, keepdims=True) + NORM_EPS)
        return ((xv * r * gv) * (1.0 + sc) + sh,), ()
    (h,), _ = _rowwise(fn, name, [x], [g, scale, shift], [_sds(x.shape, BF)], [])
    return h


def _final_loss(x, target, g, y, gate, coef, name):
    d = x.shape[1]

    def fn(xv, tv, yv, gv, gt):
        r = lax.rsqrt(jnp.mean(xv * xv, axis=-1, keepdims=True) + NORM_EPS)
        xhat = xv * r
        err = xhat * gv - tv
        dl = err * (1.0 / d)
        dxhat = dl * gv
        dx = r * (dxhat - xhat * jnp.mean(dxhat * xhat, axis=-1, keepdims=True))
        return (dx, (coef * gt) * dx), (err * err, dl * xhat, coef * dx * yv.astype(F32))
    (dx, dy), (sq, dg, dgate) = _rowwise(fn, name, [x, target, y], [g, gate],
                                         [_sds(x.shape, F32), _sds(x.shape, BF)], [d, d, d])
    return sq, dx, dg, dy, dgate


def _gm_act(pre, ln_g, ln_b, name):
    e = pre.shape[1] // 2

    def fn(pv, gv, bv):
        p = pv.astype(F32)
        u = _gelu(p[:, :e])
        v = _gelu(p[:, e:])
        mu = jnp.mean(v, axis=-1, keepdims=True)
        vc = v - mu
        rstd = lax.rsqrt(jnp.mean(vc * vc, axis=-1, keepdims=True) + NORM_EPS)
        return (u, vc * rstd * gv + bv), ()
    t = pre.shape[0]
    (u, vn), _ = _rowwise(fn, name, [pre], [ln_g, ln_b], [_sds((t, e), BF), _sds((t, e), BF)], [])
    return u, vn


def _gm_act_bwd(pre, du, dvn, ln_g, name):
    e = pre.shape[1] // 2

    def fn(pv, duv, dvv, gv):
        p = pv.astype(F32)
        pu, pvv = p[:, :e], p[:, e:]
        cdf_u, pdf_u = _normal_cdf_pdf(pu)
        cdf_v, pdf_v = _normal_cdf_pdf(pvv)
        v = pvv * cdf_v
        mu = jnp.mean(v, axis=-1, keepdims=True)
        vc = v - mu
        rstd = lax.rsqrt(jnp.mean(vc * vc, axis=-1, keepdims=True) + NORM_EPS)
        vhat = vc * rstd
        dvn_f = dvv.astype(F32)
        dvhat = dvn_f * gv
        dv = rstd * (dvhat - jnp.mean(dvhat, axis=-1, keepdims=True)
                     - vhat * jnp.mean(dvhat * vhat, axis=-1, keepdims=True))
        dpu = duv.astype(F32) * (cdf_u + pu * pdf_u)
        dpv = dv * (cdf_v + pvv * pdf_v)
        return (jnp.concatenate([dpu, dpv], axis=1),), (dvn_f * vhat, dvn_f)
    (dpre,), (dg, db) = _rowwise(fn, name, [pre, du, dvn], [ln_g], [_sds(pre.shape, BF)], [e, e], tm=128)
    return dpre, dg, db


def _cv_act(yc, ln_g, ln_b, name):
    def fn(yv, gv, bv):
        mu = jnp.mean(yv, axis=-1, keepdims=True)
        c = yv - mu
        rstd = lax.rsqrt(jnp.mean(c * c, axis=-1, keepdims=True) + NORM_EPS)
        yn = c * rstd * gv + bv
        return (yn * _sigmoid(yn),), ()
    (ys,), _ = _rowwise(fn, name, [yc], [ln_g, ln_b], [_sds(yc.shape, BF)], [])
    return ys


def _cv_act_bwd(dys, yc, ln_g, ln_b, name):
    def fn(dv, yv, gv, bv):
        mu = jnp.mean(yv, axis=-1, keepdims=True)
        c = yv - mu
        rstd = lax.rsqrt(jnp.mean(c * c, axis=-1, keepdims=True) + NORM_EPS)
        yhat = c * rstd
        yn = yhat * gv + bv
        sig = _sigmoid(yn)
        dyn = dv.astype(F32) * (sig * (1.0 + yn * (1.0 - sig)))
        dyhat = dyn * gv
        dyc = rstd * (dyhat - jnp.mean(dyhat, axis=-1, keepdims=True)
                      - yhat * jnp.mean(dyhat * yhat, axis=-1, keepdims=True))
        return (dyc,), (dyn * yhat, dyn, dyc)
    cw = yc.shape[1]
    (dyc,), (dg, db, dbias) = _rowwise(fn, name, [dys, yc], [ln_g, ln_b], [_sds(yc.shape, F32)], [cw, cw, cw])
    return dyc, dg, db, dbias


MM_ROWS = 1024
MM_SEG_ROWS = 256


def _ffn_in(h, w_blk, blk0, name):
    t, d = h.shape
    bn = w_blk.shape[2]
    half = N_DEV // 2
    f = half * bn
    tm = _tile(t, MM_ROWS, V7X_SUBLANES)
    seg_rows = _tile(tm, MM_SEG_ROWS, 2 * V7X_SUBLANES)

    def body(h_ref, wg_ref, wu_ref, g_ref, u_ref, a_ref):
        for seg in range(tm // seg_rows):
            rows = pl.ds(seg * seg_rows, seg_rows)
            hv = h_ref[rows, :]
            g = jnp.dot(hv, wg_ref[...], preferred_element_type=F32)
            u = jnp.dot(hv, wu_ref[...], preferred_element_type=F32)
            g_ref[rows, :] = g.astype(BF)
            u_ref[rows, :] = u.astype(BF)
            a_ref[rows, :] = (g * _sigmoid(g) * u).astype(BF)

    out = _sds((t, f), BF)
    tile = pl.BlockSpec((tm, bn), lambda j, i: (i, j))
    return _call(
        body, name=name, grid=(half, t // tm),
        in_specs=[pl.BlockSpec((tm, d), lambda j, i: (i, 0)),
                  pl.BlockSpec((None, d, bn), lambda j, i: (blk0 + j, 0, 0)),
                  pl.BlockSpec((None, d, bn), lambda j, i: (blk0 + half + j, 0, 0))],
        out_specs=[tile, tile, tile], out_shape=[out, out, out], sem=("parallel", "parallel"))(h, w_blk, w_blk)


def _in_proj(h, w_blk, bias, name):
    t, d = h.shape
    bn = w_blk.shape[2]
    tm = _tile(t, 1024, V7X_SUBLANES)

    def body(*refs):
        if bias is None:
            h_ref, w_ref, o_ref = refs
            o_ref[...] = jnp.dot(h_ref[...], w_ref[...], preferred_element_type=F32).astype(BF)
        else:
            h_ref, w_ref, b_ref, o_ref = refs
            o_ref[...] = (jnp.dot(h_ref[...], w_ref[...], preferred_element_type=F32) + b_ref[...]).astype(BF)

    in_specs = [pl.BlockSpec((tm, d), lambda j, i: (i, 0)), pl.BlockSpec((None, d, bn), lambda j, i: (j, 0, 0))]
    args = [h, w_blk]
    if bias is not None:
        in_specs.append(pl.BlockSpec((1, bn), lambda j, i: (0, j)))
        args.append(bias)
    return _call(body, name=name, grid=(N_DEV, t // tm), in_specs=in_specs,
                 out_specs=pl.BlockSpec((tm, bn), lambda j, i: (i, j)), out_shape=_sds((t, N_DEV * bn), BF),
                 sem=("parallel", "parallel"))(*args)


def _out_proj(a, w3, widx, x, gate, bias, coef, name):
    t, k = a.shape
    d = w3.shape[2]
    tm = _tile(t, MM_ROWS, V7X_SUBLANES)
    tn = _tile(d, 512, V7X_LANES)

    def body(*refs):
        if bias is None:
            a_ref, w_ref, x_ref, g_ref, xo_ref, y_ref = refs
            y = jnp.dot(a_ref[...], w_ref[...], preferred_element_type=F32)
        else:
            a_ref, w_ref, x_ref, g_ref, b_ref, xo_ref, y_ref = refs
            y = jnp.dot(a_ref[...], w_ref[...], preferred_element_type=F32) + b_ref[...]
        y_ref[...] = y.astype(BF)
        xo_ref[...] = x_ref[...] + (coef * g_ref[...]) * y

    tile = pl.BlockSpec((tm, tn), lambda j, i: (i, j))
    vec = pl.BlockSpec((1, tn), lambda j, i: (0, j))
    in_specs = [pl.BlockSpec((tm, k), lambda j, i: (i, 0)),
                pl.BlockSpec((None, k, tn), lambda j, i: (widx, 0, j)), tile, vec]
    args = [a, w3, x, gate]
    if bias is not None:
        in_specs.append(vec)
        args.append(bias)
    return _call(body, name=name, grid=(d // tn, t // tm), in_specs=in_specs, out_specs=[tile, tile],
                 out_shape=[_sds((t, d), F32), _sds((t, d), BF)], sem=("parallel", "parallel"))(*args)


def _ffn_da(dy, w3, widx, g, u, name):
    t, d = dy.shape
    f = w3.shape[1]
    bn = f // (N_DEV // 2)
    tm = _tile(t, MM_ROWS, V7X_SUBLANES)
    seg_rows = _tile(tm, MM_SEG_ROWS, 2 * V7X_SUBLANES)

    def body(dy_ref, w_ref, g_ref, u_ref, dgu_ref, a_ref):
        for seg in range(tm // seg_rows):
            rows = pl.ds(seg * seg_rows, seg_rows)
            da = lax.dot_general(dy_ref[rows, :], w_ref[...], (((1,), (1,)), ((), ())), preferred_element_type=F32)
            gv = g_ref[rows, :].astype(F32)
            uv = u_ref[rows, :].astype(F32)
            sig = _sigmoid(gv)
            sl = gv * sig
            dgu_ref[0, rows, :] = (da * uv * (sig * (1.0 + gv * (1.0 - sig)))).astype(BF)
            dgu_ref[1, rows, :] = (da * sl).astype(BF)
            a_ref[rows, :] = (sl * uv).astype(BF)

    tile = pl.BlockSpec((tm, bn), lambda j, i: (i, j))
    return _call(
        body, name=name, grid=(f // bn, t // tm),
        in_specs=[pl.BlockSpec((tm, d), lambda j, i: (i, 0)),
                  pl.BlockSpec((None, bn, d), lambda j, i: (widx, j, 0)), tile, tile],
        out_specs=[pl.BlockSpec((2, tm, bn), lambda j, i: (0, i, j)), tile],
        out_shape=[_sds((2, t, f), BF), _sds((t, f), BF)], sem=("parallel", "parallel"))(dy, w3, g, u)


def _mm_nt(dy, w3, widx, name):
    t, k = dy.shape
    n = w3.shape[1]
    tm = _tile(t, MM_ROWS, V7X_SUBLANES)
    tn = _tile(n, 1024, V7X_LANES)

    def body(dy_ref, w_ref, o_ref):
        o_ref[...] = lax.dot_general(dy_ref[...], w_ref[...], (((1,), (1,)), ((), ())),
                                     preferred_element_type=F32).astype(BF)

    return _call(body, name=name, grid=(n // tn, t // tm),
                 in_specs=[pl.BlockSpec((tm, k), lambda j, i: (i, 0)),
                           pl.BlockSpec((None, tn, k), lambda j, i: (widx, j, 0))],
                 out_specs=pl.BlockSpec((tm, tn), lambda j, i: (i, j)), out_shape=_sds((t, n), BF),
                 sem=("parallel", "parallel"))(dy, w3)


NORM_BWD_ROWS = 128


def _dh_norm_bwd(z3, w_blk, x, dxp, g, scale, nxt, name):
    lead, t, _ = z3.shape
    d, bn = w_blk.shape[1], w_blk.shape[2]
    per = N_DEV // lead
    tm = _tile(t, 512, NORM_BWD_ROWS)
    ni = t // tm
    n_in = 6 if nxt is None else 8
    coef, colsum = (None, False) if nxt is None else nxt[2:]
    n_acc = 3 if nxt is None else (5 if colsum else 4)
    n_rows = 1 if nxt is None else 2

    def body(*refs):
        z_ref, w_ref, x_ref, dp_ref, g_ref, sc_ref = refs[:6]
        row_outs = refs[n_in:n_in + n_rows]
        vec_outs = refs[n_in + n_rows:n_in + n_rows + n_acc]
        acc_ref, accs = refs[n_in + n_rows + n_acc], refs[n_in + n_rows + n_acc + 1:]
        i, k = pl.program_id(0), pl.program_id(1)

        @pl.when(k == 0)
        def _():
            acc_ref[...] = jnp.zeros_like(acc_ref)

        @pl.when((i == 0) & (k == 0))
        def _():
            for a in accs:
                a[...] = jnp.zeros_like(a)

        acc_ref[...] += lax.dot_general(z_ref[...], w_ref[...], (((1,), (1,)), ((), ())),
                                        preferred_element_type=F32)

        @pl.when(k == N_DEV - 1)
        def _():
            gv, sc = g_ref[...], sc_ref[...]

            def chunk(ci, carry):
                rows = pl.ds(pl.multiple_of(ci * NORM_BWD_ROWS, NORM_BWD_ROWS), NORM_BWD_ROWS)
                dh, xv = acc_ref[rows, :], x_ref[rows, :]
                r = lax.rsqrt(jnp.mean(xv * xv, axis=-1, keepdims=True) + NORM_EPS)
                xhat = xv * r
                dn = dh * (1.0 + sc)
                dxhat = dn * gv
                dx = r * (dxhat - xhat * jnp.mean(dxhat * xhat, axis=-1, keepdims=True)) + dp_ref[rows, :]
                row_outs[0][rows, :] = dx
                accs[0][...] += _fold_rows(dh * (xhat * gv))
                accs[1][...] += _fold_rows(dh)
                accs[2][...] += _fold_rows(dn * xhat)
                if nxt is not None:
                    y_ref, gate_ref = refs[6], refs[7]
                    dy = (coef * gate_ref[...]) * dx
                    row_outs[1][rows, :] = dy.astype(BF)
                    accs[3][...] += _fold_rows(coef * dx * y_ref[rows, :].astype(F32))
                    if colsum:
                        accs[4][...] += _fold_rows(dy)
                return carry
            lax.fori_loop(0, tm // NORM_BWD_ROWS, chunk, 0)

        @pl.when((i == ni - 1) & (k == N_DEV - 1))
        def _():
            for o, a in zip(vec_outs, accs):
                o[...] = jnp.sum(a[...], axis=0, keepdims=True)

    rows = pl.BlockSpec((tm, d), lambda i, k: (i, 0))
    vec = pl.BlockSpec((1, d), lambda i, k: (0, 0))
    in_specs = [pl.BlockSpec((None, tm, bn), lambda i, k: (k // per, i, k % per)),
                pl.BlockSpec((None, d, bn), lambda i, k: (k, 0, 0)), rows, rows, vec, vec]
    args = [z3, w_blk, x, dxp, g, scale]
    out_specs, out_shape = [rows], [_sds((t, d), F32)]
    if nxt is not None:
        in_specs += [rows, vec]
        args += [nxt[0], nxt[1]]
        out_specs.append(rows)
        out_shape.append(_sds((t, d), BF))
    out_specs += [vec] * n_acc
    out_shape += [_sds((1, d), F32)] * n_acc
    return _call(body, name=name, grid=(ni, N_DEV), in_specs=in_specs, out_specs=out_specs, out_shape=out_shape,
                 scratch=[pltpu.VMEM((tm, d), F32)] + [pltpu.VMEM((V7X_SUBLANES, d), F32)] * n_acc,
                 sem=("arbitrary", "arbitrary"))(*args)


def _mm_tn(a, b3, ta, tb, blocked, name):
    t, ka = a.shape
    lead, _, w = b3.shape
    per = w // tb
    nj = lead * per
    tk = t
    while tk > 512 and 4 * tk * (ta + tb) + 8 * ta * tb > V7X_VMEM_LIMIT * 3 // 4:
        tk //= 2
    tk = _tile(t, tk, V7X_SUBLANES)
    nk = t // tk

    def body(a_ref, b_ref, o_ref, acc_ref):
        k = pl.program_id(2)

        @pl.when(k == 0)
        def _():
            acc_ref[...] = jnp.zeros_like(acc_ref)

        acc_ref[...] += lax.dot_general(a_ref[...], b_ref[...], (((0,), (0,)), ((), ())),
                                        preferred_element_type=F32)

        @pl.when(k == nk - 1)
        def _():
            o_ref[...] = acc_ref[...].astype(BF)

    if blocked:
        out_shape = _sds((nj, ka, tb), BF)
        out_spec = pl.BlockSpec((None, ta, tb), lambda i, j, k: (j, i, 0))
    else:
        out_shape = _sds((1, ka, w), BF)
        out_spec = pl.BlockSpec((None, ta, tb), lambda i, j, k: (0, i, j))
    return _call(body, name=name, grid=(ka // ta, nj, nk),
                 in_specs=[pl.BlockSpec((tk, ta), lambda i, j, k: (k, i)),
                           pl.BlockSpec((None, tk, tb), lambda i, j, k: (j // per, k, j % per))],
                 out_specs=out_spec, out_shape=out_shape, scratch=[pltpu.VMEM((ta, tb), F32)],
                 sem=("parallel", "parallel", "arbitrary"))(a, b3)


def _causal(ws):
    l = ws.shape[0]
    row = lax.broadcasted_iota(jnp.int32, (l, l), 0)
    col = lax.broadcasted_iota(jnp.int32, (l, l), 1)
    return jnp.where(col <= row, ws, 0.0)


def _sgu_fwd(u, vn, ws, bsb, name):
    t, e = u.shape
    hn, l, _ = ws.shape
    dh = e // hn
    nc = t // l

    def body(u_ref, v_ref, ws_ref, bs_ref, s_ref):
        wsc = _causal(ws_ref[...]).astype(BF)
        bias = bs_ref[...]

        def chunk(c, carry):
            rows = pl.ds(pl.multiple_of(c * l, l), l)
            vo = jnp.dot(wsc, v_ref[rows, :], preferred_element_type=F32) + bias
            s_ref[rows, :] = (u_ref[rows, :].astype(F32) * vo).astype(BF)
            return carry
        lax.fori_loop(0, nc, chunk, 0)

    col = pl.BlockSpec((t, dh), lambda h: (0, h))
    return _call(body, name=name, grid=(hn,),
                 in_specs=[col, col, pl.BlockSpec((None, l, l), lambda h: (h, 0, 0)),
                           pl.BlockSpec((None, l, dh), lambda h: (h, 0, 0))],
                 out_specs=col, out_shape=_sds((t, e), BF), sem=("parallel",))(u, vn, ws, bsb)


def _sgu_bwd(ds, u, vn, ws, bsb, name):
    t, e = u.shape
    hn, l, _ = ws.shape
    dh = e // hn
    nc = t // l

    def body(ds_ref, u_ref, v_ref, ws_ref, bs_ref, du_ref, dv_ref, dws_ref, dbs_ref, accw_ref, accb_ref):
        wsc = _causal(ws_ref[...]).astype(BF)
        bias = bs_ref[...]
        accw_ref[...] = jnp.zeros_like(accw_ref)
        accb_ref[...] = jnp.zeros_like(accb_ref)

        def chunk(c, carry):
            rows = pl.ds(pl.multiple_of(c * l, l), l)
            vc = v_ref[rows, :]
            dsv = ds_ref[rows, :].astype(F32)
            vo = jnp.dot(wsc, vc, preferred_element_type=F32) + bias
            du_ref[rows, :] = (dsv * vo).astype(BF)
            dvo = dsv * u_ref[rows, :].astype(F32)
            dvo_b = dvo.astype(BF)
            accb_ref[...] += dvo
            accw_ref[...] += lax.dot_general(dvo_b, vc, (((1,), (1,)), ((), ())), preferred_element_type=F32)
            dv_ref[rows, :] = lax.dot_general(wsc, dvo_b, (((0,), (0,)), ((), ())),
                                              preferred_element_type=F32).astype(BF)
            return carry
        lax.fori_loop(0, nc, chunk, 0)
        dws_ref[...] = _causal(accw_ref[...])
        dbs_ref[...] = jnp.broadcast_to(jnp.sum(accb_ref[...], axis=1, keepdims=True), (l, dh))

    col = pl.BlockSpec((t, dh), lambda h: (0, h))
    return _call(body, name=name, grid=(hn,),
                 in_specs=[col, col, col, pl.BlockSpec((None, l, l), lambda h: (h, 0, 0)),
                           pl.BlockSpec((None, l, dh), lambda h: (h, 0, 0))],
                 out_specs=[col, col, pl.BlockSpec((None, l, l), lambda h: (h, 0, 0)),
                            pl.BlockSpec((None, l, dh), lambda h: (h, 0, 0))],
                 out_shape=[_sds((t, e), BF), _sds((t, e), BF), _sds((hn, l, l), F32), _sds((hn, l, dh), F32)],
                 scratch=[pltpu.VMEM((l, l), F32), pltpu.VMEM((l, dh), F32)],
                 sem=("parallel",))(ds, u, vn, ws, bsb)


AG_AHEAD = 3

CONV_HALO = 32
CONV_ROWS = 64
CONV_LANES = 256


def _shifted_windows(win_ref, sh_ref, rows):
    for b in range(1, V7X_SUBLANES):
        sh_ref[b - 1, 0:rows, :] = win_ref[b:b + rows, :]


def _window_rows(win_ref, sh_ref, shift, r0, rows):
    a, b = divmod(shift, V7X_SUBLANES)
    start = pl.multiple_of(r0 + V7X_SUBLANES * a, V7X_SUBLANES)
    if b == 0:
        return win_ref[pl.ds(start, rows), :]
    return sh_ref[b - 1, pl.ds(start, rows), :]


def _dwconv_fwd(p, dw_w, dw_b, name):
    t, c2 = p.shape
    cw = c2 // 2
    kw = dw_w.shape[0]
    cb = _tile(cw, CONV_LANES, V7X_LANES)
    ncb = cw // cb
    tm = _tile(t, 512, CONV_ROWS)
    off = CONV_HALO - (kw - 1)

    def body(a_ref, g_ref, ap_ref, gp_ref, w_ref, b_ref, o_ref, win_ref, sh_ref):
        i = pl.program_id(1)
        prev = ap_ref[...].astype(F32) * _sigmoid(gp_ref[...].astype(F32))
        win_ref[0:CONV_HALO, :] = jnp.where(i > 0, prev, 0.0)
        win_ref[CONV_HALO:, :] = a_ref[...].astype(F32) * _sigmoid(g_ref[...].astype(F32))
        _shifted_windows(win_ref, sh_ref, tm + CONV_HALO - V7X_SUBLANES)

        def chunk(ci, carry):
            r0 = ci * CONV_ROWS
            acc = jnp.zeros((CONV_ROWS, cb), F32) + b_ref[...]
            for k in range(kw):
                acc = acc + w_ref[k:k + 1, :] * _window_rows(win_ref, sh_ref, off + k, r0, CONV_ROWS)
            o_ref[pl.ds(pl.multiple_of(r0, CONV_ROWS), CONV_ROWS), :] = acc
            return carry
        lax.fori_loop(0, tm // CONV_ROWS, chunk, 0)

    hpt = tm // CONV_HALO
    cur_a = pl.BlockSpec((tm, cb), lambda j, i: (i, j))
    cur_g = pl.BlockSpec((tm, cb), lambda j, i: (i, ncb + j))
    prev_a = pl.BlockSpec((CONV_HALO, cb), lambda j, i: (jnp.maximum(i * hpt - 1, 0), j))
    prev_g = pl.BlockSpec((CONV_HALO, cb), lambda j, i: (jnp.maximum(i * hpt - 1, 0), ncb + j))
    return _call(body, name=name, grid=(ncb, t // tm),
                 in_specs=[cur_a, cur_g, prev_a, prev_g, pl.BlockSpec((kw, cb), lambda j, i: (0, j)),
                           pl.BlockSpec((1, cb), lambda j, i: (0, j))],
                 out_specs=pl.BlockSpec((tm, cb), lambda j, i: (i, j)), out_shape=_sds((t, cw), F32),
                 scratch=[pltpu.VMEM((tm + CONV_HALO, cb), F32),
                          pltpu.VMEM((V7X_SUBLANES - 1, tm + CONV_HALO - V7X_SUBLANES, cb), F32)],
                 sem=("parallel", "parallel"))(p, p, p, p, dw_w, dw_b)


def _dwconv_bwd(dyc, p, dw_w, name):
    t, c2 = p.shape
    cw = c2 // 2
    kw = dw_w.shape[0]
    cb = _tile(cw, CONV_LANES, V7X_LANES)
    ncb = cw // cb
    tm = _tile(t, 512, CONV_ROWS)
    nt = t // tm
    off = CONV_HALO - (kw - 1)
    kpad = -(-kw // V7X_SUBLANES) * V7X_SUBLANES
    sh_rows = tm + CONV_HALO - V7X_SUBLANES

    def body(d_ref, dn_ref, a_ref, g_ref, ap_ref, gp_ref, w_ref,
             dp_ref, dw_ref, dba_ref, dbg_ref, dwin_ref, ywin_ref, dsh_ref, ysh_ref, accw_ref, acca_ref, accg_ref):
        i = pl.program_id(1)

        @pl.when(i == 0)
        def _():
            accw_ref[...] = jnp.zeros_like(accw_ref)
            acca_ref[...] = jnp.zeros_like(acca_ref)
            accg_ref[...] = jnp.zeros_like(accg_ref)

        prev = ap_ref[...].astype(F32) * _sigmoid(gp_ref[...].astype(F32))
        ywin_ref[0:CONV_HALO, :] = jnp.where(i > 0, prev, 0.0)
        ywin_ref[CONV_HALO:, :] = a_ref[...].astype(F32) * _sigmoid(g_ref[...].astype(F32))
        dwin_ref[0:tm, :] = d_ref[...]
        dwin_ref[tm:, :] = jnp.where(i < nt - 1, dn_ref[...], 0.0)
        _shifted_windows(ywin_ref, ysh_ref, sh_rows)
        _shifted_windows(dwin_ref, dsh_ref, sh_rows)

        def chunk(ci, carry):
            r0 = ci * CONV_ROWS
            rows = pl.ds(pl.multiple_of(r0, CONV_ROWS), CONV_ROWS)
            dcur = d_ref[rows, :]
            dyg = jnp.zeros((CONV_ROWS, cb), F32)
            for k in range(kw):
                dyg = dyg + w_ref[k:k + 1, :] * _window_rows(dwin_ref, dsh_ref, kw - 1 - k, r0, CONV_ROWS)
                accw_ref[k] += _fold_rows(dcur * _window_rows(ywin_ref, ysh_ref, off + k, r0, CONV_ROWS))
            av = a_ref[rows, :].astype(F32)
            sig = _sigmoid(g_ref[rows, :].astype(F32))
            da = dyg * sig
            dg = dyg * av * sig * (1.0 - sig)
            dp_ref[0, rows, :] = da.astype(BF)
            dp_ref[1, rows, :] = dg.astype(BF)
            acca_ref[...] += _fold_rows(da)
            accg_ref[...] += _fold_rows(dg)
            return carry
        lax.fori_loop(0, tm // CONV_ROWS, chunk, 0)

        @pl.when(i == nt - 1)
        def _():
            dw_ref[...] = jnp.sum(accw_ref[...], axis=1)
            dba_ref[...] = jnp.sum(acca_ref[...], axis=0, keepdims=True)
            dbg_ref[...] = jnp.sum(accg_ref[...], axis=0, keepdims=True)

    hpt = tm // CONV_HALO
    last_halo = t // CONV_HALO - 1
    tile = pl.BlockSpec((tm, cb), lambda j, i: (i, j))
    cur_g = pl.BlockSpec((tm, cb), lambda j, i: (i, ncb + j))
    nxt = pl.BlockSpec((CONV_HALO, cb), lambda j, i: (jnp.minimum((i + 1) * hpt, last_halo), j))
    prev_a = pl.BlockSpec((CONV_HALO, cb), lambda j, i: (jnp.maximum(i * hpt - 1, 0), j))
    prev_g = pl.BlockSpec((CONV_HALO, cb), lambda j, i: (jnp.maximum(i * hpt - 1, 0), ncb + j))
    vec = pl.BlockSpec((1, cb), lambda j, i: (0, j))
    dp, ddw, dba, dbg = _call(
        body, name=name, grid=(ncb, nt),
        in_specs=[tile, nxt, tile, cur_g, prev_a, prev_g, pl.BlockSpec((kw, cb), lambda j, i: (0, j))],
        out_specs=[pl.BlockSpec((2, tm, cb), lambda j, i: (0, i, j)), pl.BlockSpec((kpad, cb), lambda j, i: (0, j)),
                   vec, vec],
        out_shape=[_sds((2, t, cw), BF), _sds((kpad, cw), F32), _sds((1, cw), F32), _sds((1, cw), F32)],
        scratch=[pltpu.VMEM((tm + CONV_HALO, cb), F32), pltpu.VMEM((tm + CONV_HALO, cb), F32),
                 pltpu.VMEM((V7X_SUBLANES - 1, sh_rows, cb), F32), pltpu.VMEM((V7X_SUBLANES - 1, sh_rows, cb), F32),
                 pltpu.VMEM((kpad, V7X_SUBLANES, cb), F32), pltpu.VMEM((V7X_SUBLANES, cb), F32),
                 pltpu.VMEM((V7X_SUBLANES, cb), F32)],
        sem=("parallel", "arbitrary"))(dyc, dyc, p, p, p, p, dw_w)
    return dp, ddw[:kw], dba, dbg


def _adam_math(g, w, m, v):
    m2 = ADAM_B1 * m + (1.0 - ADAM_B1) * g
    v2 = ADAM_B2 * v + (1.0 - ADAM_B2) * (g * g)
    m_hat = m2 / (1.0 - ADAM_B1 ** ADAM_STEP)
    v_hat = v2 / (1.0 - ADAM_B2 ** ADAM_STEP)
    delta = -ADAM_LR * (m_hat / (jnp.sqrt(v_hat) + ADAM_EPS) + ADAM_WD * w)
    return delta, m2, v2


def _adamw(g_parts, w, m, v, name):
    r, c = w.shape
    tr = _tile(r, 256, V7X_SUBLANES)
    ng = len(g_parts)

    def body(*refs):
        g = refs[0][...].astype(F32)
        for s in refs[1:ng]:
            g = g + s[...].astype(F32)
        w_ref, m_ref, v_ref, go_ref, d_ref, mo_ref, vo_ref = refs[ng:]
        delta, m2, v2 = _adam_math(g, w_ref[...], m_ref[...], v_ref[...])
        go_ref[...] = g
        d_ref[...] = delta
        mo_ref[...] = m2
        vo_ref[...] = v2

    tile = pl.BlockSpec((tr, c), lambda i: (i, 0))
    in_specs = [pl.BlockSpec((None, tr, c), functools.partial(lambda s, i: (s, i, 0), s)) for _, s in g_parts]
    out = _sds((r, c), F32)
    return _call(body, name=name, grid=(r // tr,), in_specs=in_specs + [tile, tile, tile],
                 out_specs=[tile] * 4, out_shape=[out] * 4, sem=("parallel",))(*[a for a, _ in g_parts], w, m, v)


def _adamw_stacked(h, recv, chip, w_st, m_st, v_st, k, prev, name):
    kk, r, c = w_st.shape
    tr = _tile(r, 256, V7X_SUBLANES)
    if prev is None:
        prev = [lax.empty((kk, r, c), F32) for _ in range(4)]

    def body(chip_ref, h_ref, r0_ref, r1_ref, r2_ref, w_ref, m_ref, v_ref, pg, pd, pm, pv,
             go_ref, d_ref, mo_ref, vo_ref):
        g = (h_ref[...].astype(F32) + r0_ref[...].astype(F32)) + (r1_ref[...].astype(F32) + r2_ref[...].astype(F32))
        delta, m2, v2 = _adam_math(g, w_ref[...], m_ref[...], v_ref[...])
        go_ref[...] = g
        d_ref[...] = delta
        mo_ref[...] = m2
        vo_ref[...] = v2

    own = pl.BlockSpec((None, tr, c), lambda i, chip_ref: (chip_ref[0], i, 0))
    rcv = [pl.BlockSpec((None, tr, c), functools.partial(lambda s, i, chip_ref: (s, i, 0), s)) for s in range(3)]
    blk = pl.BlockSpec((None, tr, c), lambda i, chip_ref: (k, i, 0))
    out = _sds((kk, r, c), F32)
    return _call(body, name=name, grid=(r // tr,), in_specs=[own] + rcv + [blk, blk, blk] + [ANY] * 4,
                 out_specs=[blk] * 4, out_shape=[out] * 4, sem=("parallel",), prefetch=chip, on_path=False,
                 aliases={8: 0, 9: 1, 10: 2, 11: 3})(h, recv, recv, recv, w_st, m_st, v_st, *prev)


def _add_sibling(g4, land, core, name):
    n, _, r, c = g4.shape
    tr = _tile(r, 512, V7X_SUBLANES)

    def body(core_ref, a_ref, b_ref, o_ref):
        o_ref[...] = (a_ref[...].astype(F32) + b_ref[...].astype(F32)).astype(BF)

    return _call(body, name=name, grid=(n, r // tr),
                 in_specs=[pl.BlockSpec((None, None, tr, c), lambda p, i, core_ref: (p, core_ref[0], i, 0)),
                           pl.BlockSpec((None, None, tr, c), lambda p, i, core_ref: (p, 0, i, 0))],
                 out_specs=pl.BlockSpec((None, tr, c), lambda p, i, core_ref: (p, i, 0)),
                 out_shape=_sds((n, r, c), BF), sem=("parallel", "parallel"), prefetch=core)(g4, land)


def _cast_to_slot(w, lead, me, name, after=()):
    r, c = w.shape[-2:]
    nl = len(lead)
    tr = _tile(r, 512, 2 * V7X_SUBLANES)

    def body(me_ref, w_ref, o_ref):
        o_ref[...] = w_ref[...].astype(BF)

    return _call(body, name=name, grid=(r // tr,),
                 in_specs=[pl.BlockSpec((None,) * nl + (tr, c), lambda i, me_ref: tuple(lead) + (i, 0))],
                 out_specs=pl.BlockSpec((None, None, tr, c), lambda i, me_ref: (0, me_ref[0], i, 0)),
                 out_shape=_sds((1, N_DEV, r, c), BF), sem=("parallel",), prefetch=me, after=after)(w)


def _ada_fwd(c_pad, ada_w, ada_b, name):
    nl, d, cl = ada_w.shape
    rows = c_pad.shape[0]
    tn = _tile(cl, 256, V7X_LANES)

    def body(c_ref, w_ref, b_ref, o_ref):
        cv = c_ref[...]
        cond = (cv * _sigmoid(cv)).astype(BF)
        o_ref[...] = jnp.dot(cond, w_ref[...].astype(BF), preferred_element_type=F32) + b_ref[...]

    return _call(body, name=name, grid=(nl, cl // tn),
                 in_specs=[pl.BlockSpec((rows, d), lambda l, j: (0, 0)),
                           pl.BlockSpec((None, d, tn), lambda l, j: (l, 0, j)),
                           pl.BlockSpec((None, 1, tn), lambda l, j: (l, 0, j))],
                 out_specs=pl.BlockSpec((None, rows, tn), lambda l, j: (l, 0, j)),
                 out_shape=_sds((nl, rows, cl), F32), sem=("parallel", "parallel"))(c_pad, ada_w, ada_b)


def _ada_bwd(c_pad, dmod, w, m, v, name):
    nl, d, cl = w.shape
    rows = c_pad.shape[0]
    tn = _tile(cl, 256, V7X_LANES)

    def body(c_ref, dm_ref, w_ref, m_ref, v_ref, go_ref, d_ref, mo_ref, vo_ref):
        cv = c_ref[...]
        cond = (cv * _sigmoid(cv)).astype(BF)
        g = lax.dot_general(cond, dm_ref[...].astype(BF), (((0,), (0,)), ((), ())), preferred_element_type=F32)
        delta, m2, v2 = _adam_math(g, w_ref[...], m_ref[...], v_ref[...])
        go_ref[...] = g
        d_ref[...] = delta
        mo_ref[...] = m2
        vo_ref[...] = v2

    tile = pl.BlockSpec((None, d, tn), lambda l, j: (l, 0, j))
    out = _sds((nl, d, cl), F32)
    return _call(body, name=name, grid=(nl, cl // tn),
                 in_specs=[pl.BlockSpec((rows, d), lambda l, j: (0, 0)),
                           pl.BlockSpec((None, rows, tn), lambda l, j: (l, 0, j)), tile, tile, tile],
                 out_specs=[tile] * 4, out_shape=[out] * 4, sem=("parallel", "parallel"))(c_pad, dmod, w, m, v)


def _sum_devices(parts, name):
    n, r, c = parts.shape
    tr = _tile(r, 512, V7X_SUBLANES)

    def body(p_ref, o_ref):
        acc = p_ref[0]
        for k in range(1, n):
            acc = acc + p_ref[k]
        o_ref[...] = acc

    return _call(body, name=name, grid=(r // tr,), in_specs=[pl.BlockSpec((n, tr, c), lambda i: (0, i, 0))],
                 out_specs=pl.BlockSpec((tr, c), lambda i: (i, 0)), out_shape=_sds((r, c), F32),
                 sem=("parallel",))(parts)


def _mesh_pos():
    return lax.axis_index("x"), lax.axis_index("y"), lax.axis_index("c")


def _other_chips(x, y):
    return [(1 - x, y), (x, 1 - y), (1 - x, 1 - y)]


def _all_gather(arrs, name):
    n = len(arrs)

    def body(*refs):
        ins, outs = refs[:n], refs[n:2 * n]
        send_sems, recv_sems, local_sems = refs[2 * n:]
        x, y, c = _mesh_pos()
        me, sibling = (x, y, c), (x, y, 1 - c)
        chips = _other_chips(x, y)

        def slot(a, pos):
            px, py, pc = pos
            return outs[a].at[:, pl.ds(4 * px + 2 * py + pc, 1)]

        def copy(a, k, block, to, src=None):
            return pltpu.make_async_remote_copy(
                src_ref=slot(a, block) if src is None else src, dst_ref=slot(a, block),
                send_sem=send_sems.at[a, k], recv_sem=recv_sems.at[a, k], device_id=to, device_id_type=MESH)

        mine = [pltpu.make_async_copy(ins[a], slot(a, me), local_sems.at[a]) for a in range(n)]
        for cp in mine:
            cp.start()
        first = []
        for a in range(n):
            first.append(copy(a, 0, me, sibling, src=ins[a]))
            first += [copy(a, 1 + j, me, (*chip, c), src=ins[a]) for j, chip in enumerate(chips)]
        for cp in first:
            cp.start()
        passed = []
        for a in range(n):
            for j, chip in enumerate(chips):
                copy(a, 1 + j, (*chip, c), me).wait_recv()
                fwd = copy(a, 4 + j, (*chip, c), sibling)
                fwd.start()
                passed.append(fwd)
        for a in range(n):
            copy(a, 0, sibling, me).wait_recv()
            for j, chip in enumerate(chips):
                copy(a, 4 + j, (*chip, 1 - c), me).wait_recv()
        for cp in first + passed:
            cp.wait_send()
        for cp in mine:
            cp.wait()

    out_shape = [_sds((a.shape[0], N_DEV) + a.shape[2:], a.dtype) for a in arrs]
    return pl.pallas_call(
        body, out_shape=out_shape, in_specs=[ANY] * n, out_specs=[ANY] * n, name=name,
        scratch_shapes=[pltpu.SemaphoreType.DMA((n, N_DEV - 1)), pltpu.SemaphoreType.DMA((n, N_DEV - 1)),
                        pltpu.SemaphoreType.DMA((n,))])(*arrs)


HBM = pl.BlockSpec(memory_space=pltpu.HBM)
SEM = pl.BlockSpec(memory_space=pltpu.SEMAPHORE)


def _hbm(v):
    return pltpu.with_memory_space_constraint(v, pltpu.HBM)


def _comm_call(body, name, bufs, sems_in, sems_out):
    after = [] if not sems_in or _Seq.last is None or any(_Seq.last is b for b in bufs) else [_Seq.last]
    nb, ni, na, no = len(bufs), len(sems_in), len(after), len(sems_out)

    def wrapped(*refs):
        body(refs[:nb], refs[nb:nb + ni], refs[nb + ni + na:nb + ni + na + no])
        if no:
            refs[-1][...] = jnp.zeros_like(refs[-1])

    out_shape = [pltpu.SemaphoreType.DMA(s) for s in sems_out] + [pltpu.HBM(b.shape, b.dtype) for b in bufs]
    out_specs = [SEM] * no + [HBM] * nb
    if no:
        out_shape.append(_sds((V7X_SUBLANES, V7X_LANES), F32))
        out_specs.append(pl.BlockSpec(memory_space=pltpu.VMEM))
    res = pl.pallas_call(
        wrapped, name=name, out_shape=out_shape, in_specs=[HBM] * nb + [SEM] * ni + [ANY] * na, out_specs=out_specs,
        input_output_aliases={i: no + i for i in range(nb)},
        compiler_params=pltpu.CompilerParams(has_side_effects=pltpu.SideEffectType.DATAFLOW_SIDE_EFFECTING),
    )(*bufs, *sems_in, *after)
    out_bufs = list(res[no:no + nb])
    if no:
        _Seq.tokens.append(res[-1])
    _Seq.last = out_bufs[0]
    return list(res[:no]), out_bufs


def _remote(src, dst, send_sem, recv_sem, to):
    return pltpu.make_async_remote_copy(src_ref=src, dst_ref=dst, send_sem=send_sem, recv_sem=recv_sem,
                                        device_id=to, device_id_type=MESH)


def _slot(ref, pos):
    px, py, pc = pos
    return ref.at[:, pl.ds(4 * px + 2 * py + pc, 1)]


def _ag_start(bufs, name):
    n = len(bufs)

    def body(b, _, sems):
        send_sib, recv_sib, send_ici, recv_ici = sems
        x, y, c = _mesh_pos()
        for a in range(n):
            mine = _slot(b[a], (x, y, c))
            _remote(mine, mine, send_sib.at[a], recv_sib.at[a], (x, y, 1 - c)).start()
            for j, (px, py) in enumerate(_other_chips(x, y)):
                _remote(mine, mine, send_ici.at[3 * a + j], recv_ici.at[3 * a + j], (px, py, c)).start()

    sems, bufs = _comm_call(body, name, [_hbm(b) for b in bufs], [], [(n,), (n,), (3 * n,), (3 * n,)])
    return dict(bufs=bufs, send_sib=sems[0], recv_sib=sems[1], send_ici=sems[2], recv_ici=sems[3])


def _ag_mid(st, name):
    n = len(st["bufs"])

    def body(b, sems_in, sems):
        (recv_ici,) = sems_in
        send_fwd, recv_fwd = sems
        x, y, c = _mesh_pos()
        for a in range(n):
            for j, (px, py) in enumerate(_other_chips(x, y)):
                blk = _slot(b[a], (px, py, c))
                _remote(blk, blk, send_fwd.at[3 * a + j], recv_ici.at[3 * a + j], (x, y, 1 - c)).wait_recv()
                _remote(blk, blk, send_fwd.at[3 * a + j], recv_fwd.at[3 * a + j], (x, y, 1 - c)).start()

    sems, bufs = _comm_call(body, name, st["bufs"], [st["recv_ici"]], [(3 * n,), (3 * n,)])
    return dict(st, bufs=bufs, send_fwd=sems[0], recv_fwd=sems[1])


def _ag_end(st, name):
    n = len(st["bufs"])

    def body(b, sems_in, _):
        send_sib, recv_sib, send_ici, send_fwd, recv_fwd = sems_in
        x, y, c = _mesh_pos()
        sibling = (x, y, 1 - c)
        for a in range(n):
            mine, sib_blk = _slot(b[a], (x, y, c)), _slot(b[a], sibling)
            _remote(mine, mine, send_sib.at[a], recv_sib.at[a], sibling).wait_send()
            _remote(sib_blk, sib_blk, send_sib.at[a], recv_sib.at[a], sibling).wait_recv()
            for j, (px, py) in enumerate(_other_chips(x, y)):
                blk, sib_got = _slot(b[a], (px, py, c)), _slot(b[a], (px, py, 1 - c))
                _remote(mine, mine, send_ici.at[3 * a + j], recv_sib.at[a], (px, py, c)).wait_send()
                _remote(blk, blk, send_fwd.at[3 * a + j], recv_fwd.at[3 * a + j], sibling).wait_send()
                _remote(sib_got, sib_got, send_fwd.at[3 * a + j], recv_fwd.at[3 * a + j], sibling).wait_recv()

    _, bufs = _comm_call(body, name, st["bufs"],
                         [st[k] for k in ("send_sib", "recv_sib", "send_ici", "send_fwd", "recv_fwd")], [])
    return bufs


def _rs_start(g4s, name):
    n = len(g4s)
    lands = [lax.empty((N_CHIP, 1) + g.shape[2:], g.dtype) for g in g4s]

    def body(b, _, sems):
        send, recv = sems
        x, y, c = _mesh_pos()
        for a in range(n):
            _remote(b[a].at[:, pl.ds(1 - c, 1)], b[n + a], send.at[a], recv.at[a], (x, y, 1 - c)).start()

    sems, bufs = _comm_call(body, name, [_hbm(v) for v in list(g4s) + lands], [], [(n,), (n,)])
    return dict(bufs=bufs, send=sems[0], recv=sems[1])


def _rs_mid(st, name):
    n = len(st["bufs"]) // 2

    def body(b, sems_in, _):
        send, recv = sems_in
        x, y, c = _mesh_pos()
        for a in range(n):
            cp = _remote(b[a].at[:, pl.ds(1 - c, 1)], b[n + a], send.at[a], recv.at[a], (x, y, 1 - c))
            cp.wait_send()
            cp.wait_recv()

    _, bufs = _comm_call(body, name, st["bufs"], [st["send"], st["recv"]], [])
    return bufs[:n], bufs[n:]


def _rs_start2(sums, name):
    n = len(sums)
    lands = [lax.empty((N_CHIP - 1,) + s.shape[1:], s.dtype) for s in sums]

    def body(b, _, sems):
        send, recv = sems
        x, y, c = _mesh_pos()
        for a in range(n):
            for j, (px, py) in enumerate(_other_chips(x, y)):
                _remote(b[a].at[pl.ds(2 * px + py, 1)], b[n + a].at[pl.ds(j, 1)], send.at[3 * a + j], recv.at[3 * a + j],
                        (px, py, c)).start()

    sems, bufs = _comm_call(body, name, [_hbm(v) for v in list(sums) + lands], [], [(3 * n,), (3 * n,)])
    return dict(bufs=bufs, send=sems[0], recv=sems[1])


def _rs_end(st, name):
    n = len(st["bufs"]) // 2

    def body(b, sems_in, _):
        send, recv = sems_in
        x, y, c = _mesh_pos()
        for a in range(n):
            for j, (px, py) in enumerate(_other_chips(x, y)):
                cp = _remote(b[a].at[pl.ds(2 * px + py, 1)], b[n + a].at[pl.ds(j, 1)], send.at[3 * a + j], recv.at[3 * a + j],
                             (px, py, c))
                cp.wait_send()
                cp.wait_recv()

    _, bufs = _comm_call(body, name, st["bufs"], [st["send"], st["recv"]], [])
    return bufs[:n], bufs[n:]


def _pack(parts, rows_align=V7X_SUBLANES):
    flat, total = [], 0
    for p in parts:
        v = p.reshape(-1).astype(F32)
        pad = -v.shape[0] % PACK_ALIGN
        flat.append(jnp.pad(v, (0, pad)) if pad else v)
        total += v.shape[0] + pad
    tail = -total % (rows_align * V7X_LANES)
    if tail:
        flat.append(jnp.zeros((tail,), F32))
    return jnp.concatenate(flat).reshape(-1, V7X_LANES)


def _unpack(buf, shapes):
    lead = buf.shape[:-2]
    flat = buf.reshape(lead + (-1,))
    out, pos = [], 0
    for s in shapes:
        size = 1
        for d in s:
            size *= d
        out.append(flat[..., pos:pos + size].reshape(lead + tuple(s)))
        pos += size + (-size % PACK_ALIGN)
    return out


def kernel(x, c, ada_w, ada_b, norm_g, ffn_w_in, ffn_w_out, gm_w_in, gm_ln_g, gm_ln_b, gm_ws, gm_bs, gm_w_out, cv_w_in, cv_b_in, cv_dw_w, cv_dw_b, cv_ln_g, cv_ln_b, cv_w_out, cv_b_out, final_g, loss_target, m_ada_w, m_ada_b, m_norm_g, m_ffn_w_in, m_ffn_w_out, m_gm_w_in, m_gm_ln_g, m_gm_ln_b, m_gm_ws, m_gm_bs, m_gm_w_out, m_cv_w_in, m_cv_b_in, m_cv_dw_w, m_cv_dw_b, m_cv_ln_g, m_cv_ln_b, m_cv_w_out, m_cv_b_out, m_final_g, v_ada_w, v_ada_b, v_norm_g, v_ffn_w_in, v_ffn_w_out, v_gm_w_in, v_gm_ln_g, v_gm_ln_b, v_gm_ws, v_gm_bs, v_gm_w_out, v_cv_w_in, v_cv_b_in, v_cv_dw_w, v_cv_dw_b, v_cv_ln_g, v_cv_ln_b, v_cv_w_out, v_cv_b_out, v_final_g):
    t, d = x.shape[1], x.shape[2]
    depth = ada_w.shape[0]
    assert depth == 2 and ffn_w_in.shape[:2] == (2, 2) and gm_w_in.shape[0] == 1 and cv_w_in.shape[0] == 1
    dl = d // N_DEV
    bn = ffn_w_in.shape[3]
    fl = ffn_w_out.shape[2]
    f = fl * N_DEV
    el = gm_w_in.shape[2]
    e = el * N_DEV // 2
    hn, l = gm_ws.shape[1], gm_ws.shape[2]
    kw = cv_dw_w.shape[1]
    cl = ada_w.shape[2]
    me = 4 * lax.axis_index("x") + 2 * lax.axis_index("y") + lax.axis_index("c")
    me1 = me.astype(jnp.int32).reshape(1)
    chip1 = (2 * lax.axis_index("x") + lax.axis_index("y")).astype(jnp.int32).reshape(1)
    core1 = lax.axis_index("c").astype(jnp.int32).reshape(1)
    _Seq.last, _Seq.tokens = None, []

    xs = x[0]
    tgt = loss_target[0]

    ag_groups = [("win00", [(ffn_w_in, (0, 0))]), ("wout00", [(ffn_w_out, (0, 0))]),
                 ("gm", [(gm_w_in, (0,)), (gm_w_out, (0,))]),
                 ("win01", [(ffn_w_in, (0, 1))]), ("wout01", [(ffn_w_out, (0, 1))]),
                 ("win10", [(ffn_w_in, (1, 0))]), ("wout10", [(ffn_w_out, (1, 0))]),
                 ("cv", [(cv_w_in, (0,)), (cv_w_out, (0,))]),
                 ("win11", [(ffn_w_in, (1, 1))]), ("wout11", [(ffn_w_out, (1, 1))])]
    ag_flight = {}

    ag_slots = {}

    def ag_cast(gi, after=()):
        gname, members = ag_groups[gi]
        ag_slots[gi] = [_cast_to_slot(w, lead, me1, name=f"cast_{gname}_{k}", after=after)
                        for k, (w, lead) in enumerate(members)]

    def ag_start(gi):
        ag_flight[gi] = _ag_start(ag_slots.pop(gi), name=f"ag_start_{ag_groups[gi][0]}")

    def ag_forward(gi):
        if gi in ag_flight and "send_fwd" not in ag_flight[gi]:
            ag_flight[gi] = _ag_mid(ag_flight[gi], name=f"ag_mid_{ag_groups[gi][0]}")

    def ag_take(gi):
        ag_forward(gi)
        bufs = _ag_end(ag_flight.pop(gi), name=f"ag_end_{ag_groups[gi][0]}")
        if gi > 0:
            ag_forward(gi + 1)
        if gi + AG_AHEAD < len(ag_groups):
            ag_start(gi + AG_AHEAD)
        return [b[0] for b in bufs]

    small_in = [c, norm_g, cv_b_in, cv_dw_w, cv_dw_b, cv_ln_g, cv_ln_b, cv_b_out]
    pack1 = _pack(small_in)
    (pack1_all,) = _all_gather([pack1[None, None]], name="ag_small")
    parts = _unpack(pack1_all[0], [s.shape for s in small_in])
    c_all = parts[0].reshape(N_DEV, d)
    ng_full = jnp.moveaxis(parts[1], 0, 2).reshape(depth, 3, d)
    cvb_in_full = parts[2].reshape(1, 2 * e)
    dww_full = jnp.moveaxis(parts[3][:, 0], 0, 1).reshape(kw, e)
    dwb_full, cln_g_full, cln_b_full, cvb_out_full = [p.reshape(1, d) for p in parts[4:8]]

    c_pad = jnp.pad(c_all, ((0, 16 - N_DEV), (0, 0)))
    ada_b_loc = lax.dynamic_slice_in_dim(ada_b, me * cl, cl, axis=1).reshape(depth, 1, cl)
    mod_part = _ada_fwd(c_pad, ada_w, ada_b_loc, name="ada_fwd")[:, :N_DEV]
    (mod_all,) = _all_gather([_pack([mod_part])[None, None]], name="ag_mod")
    mod_all = _unpack(mod_all[0], [mod_part.shape])[0]
    mod_mine = lax.dynamic_index_in_dim(mod_all, me, axis=2, keepdims=False)
    mod = jnp.moveaxis(mod_mine, 0, 1).reshape(depth, 3, 3, 1, d)

    for gi in range(len(ag_groups)):
        ag_cast(gi, after=[mod_all])
        if gi < AG_AHEAD:
            ag_start(gi)

    ws = gm_ws[0]
    bsb = jnp.broadcast_to(gm_bs[0][:, :, None], (hn, l, e // hn))
    gm_g, gm_b = gm_ln_g, gm_ln_b

    saved = []
    xcur = xs
    next_group = 0
    for i in range(depth):
        for s in range(3):
            shift, scale, gate = mod[i, s, 0], mod[i, s, 1], mod[i, s, 2]
            g_norm = ng_full[i, s][None]
            tag = f"l{i}s{s}"
            h = _norm_mod(xcur, g_norm, scale, shift, name=f"norm_mod_{tag}")
            if s != 1:
                (w_in_blk,) = ag_take(next_group)
                gg, uu, act = _ffn_in(h, w_in_blk, 0, name=f"ffn_in_{tag}")
                w_out3 = ag_take(next_group + 1)[0].reshape(1, f, d)
                next_group += 2
                xnext, yv = _out_proj(act, w_out3, 0, xcur, gate, None, 0.5, name=f"ffn_out_{tag}")
                saved.append(dict(x=xcur, h=h, g=gg, u=uu, y=yv, w_in=w_in_blk, w_out=w_out3))
            elif i % 2 == 0:
                gm_in_blk, gm_out = ag_take(next_group)
                gm_out3 = gm_out.reshape(1, e, d)
                next_group += 1
                pre = _in_proj(h, gm_in_blk, None, name=f"gm_in_{tag}")
                uu, vn = _gm_act(pre, gm_g, gm_b, name=f"gm_act_{tag}")
                sg = _sgu_fwd(uu, vn, ws, bsb, name=f"sgu_fwd_{tag}")
                xnext, yv = _out_proj(sg, gm_out3, 0, xcur, gate, None, 1.0, name=f"gm_out_{tag}")
                saved.append(dict(x=xcur, h=h, pre=pre, u=uu, vn=vn, sg=sg, y=yv, w_in=gm_in_blk, w_out=gm_out3))
            else:
                cv_in_blk, cv_out = ag_take(next_group)
                cv_out3 = cv_out.reshape(1, e, d)
                next_group += 1
                p = _in_proj(h, cv_in_blk, cvb_in_full, name=f"cv_in_{tag}")
                yc = _dwconv_fwd(p, dww_full, dwb_full, name=f"dwconv_fwd_{tag}")
                ys = _cv_act(yc, cln_g_full, cln_b_full, name=f"cv_act_{tag}")
                xnext, yv = _out_proj(ys, cv_out3, 0, xcur, gate, cvb_out_full, 1.0, name=f"cv_out_{tag}")
                saved.append(dict(x=xcur, h=h, p=p, yc=yc, ys=ys, y=yv, w_in=cv_in_blk, w_out=cv_out3))
            xcur = xnext

    def bwd_head(i, s):
        return saved[3 * i + s]["y"], mod[i, s, 2], 0.5 if s != 1 else 1.0, s == 1 and i % 2 == 1

    sq, dx, d_final_g, dy, dgate = _final_loss(xcur, tgt, final_g[None], *bwd_head(depth - 1, 2)[:3], name="final_loss")
    dbout = None
    loss = lax.psum(0.5 / d * jnp.sum(sq), ("x", "y", "c"))

    dmod = [[[None] * 3 for _ in range(3)] for _ in range(depth)]
    d_norm_g = [[None] * 3 for _ in range(depth)]
    small = {}

    stacked = {
        "ffn_w_in": [a.reshape(4, d, bn) for a in (ffn_w_in, m_ffn_w_in, v_ffn_w_in)],
        "ffn_w_out": [a.reshape(4, fl, d) for a in (ffn_w_out, m_ffn_w_out, v_ffn_w_out)],
        "gm_w_in": [gm_w_in, m_gm_w_in, v_gm_w_in], "gm_w_out": [gm_w_out, m_gm_w_out, v_gm_w_out],
        "cv_w_in": [cv_w_in, m_cv_w_in, v_cv_w_in], "cv_w_out": [cv_w_out, m_cv_w_out, v_cv_w_out],
    }
    res_big = {}

    def rs_sibling(g4s, tag):
        return _rs_start(g4s, name=f"rs_start_{tag}"), tag

    def rs_chips(flight):
        st, tag = flight
        g4s, lands = _rs_mid(st, name=f"rs_mid_{tag}")
        sums = [_add_sibling(g4, land, core1, name=f"rs_add_{tag}_{k}") for k, (g4, land) in enumerate(zip(g4s, lands))]
        return _rs_start2(sums, name=f"rs_start2_{tag}"), tag

    def rs_finish(flight, targets):
        st, tag = flight
        sums, recvs = _rs_end(st, name=f"rs_end_{tag}")
        for (pname, k), hsum, recv in zip(targets, sums, recvs):
            w_st, m_st, v_st = stacked[pname]
            res_big[pname] = _adamw_stacked(hsum, recv, chip1, w_st, m_st, v_st, k, res_big.get(pname),
                                            name=f"adamw_{pname}_{k}")

    pending = []
    last_sibling = None
    for i in reversed(range(depth)):
        for s in reversed(range(3)):
            sv = saved[3 * i + s]
            shift, scale, gate = mod[i, s, 0], mod[i, s, 1], mod[i, s, 2]
            g_norm = ng_full[i, s][None]
            tag = f"l{i}s{s}"
            last = i == 0 and s == 0
            if s != 1:
                widx = 2 * i + s // 2
                dgu, act = _ffn_da(dy, sv["w_out"], 0, sv["g"], sv["u"], name=f"ffn_da_{tag}")
                if last:
                    g_in = _mm_tn(sv["h"], dgu, d, bn, True, name=f"ffn_dwin_{tag}").reshape(N_CHIP, 2, d, bn)
                    sib_in = rs_sibling([g_in], f"{tag}_in")
                    g_out = _mm_tn(act, dy[None], bn, d, False, name=f"ffn_dwout_{tag}").reshape(N_CHIP, 2, fl, d)
                    new_flights = [(rs_chips(sib_in), [("ffn_w_in", widx)])]
                    sib, targets = rs_sibling([g_out], f"{tag}_out"), [("ffn_w_out", widx)]
                else:
                    g_out = _mm_tn(act, dy[None], bn, d, False, name=f"ffn_dwout_{tag}").reshape(N_CHIP, 2, fl, d)
                    sib_out = rs_sibling([g_out], f"{tag}_out")
                    g_in = _mm_tn(sv["h"], dgu, d, bn, True, name=f"ffn_dwin_{tag}").reshape(N_CHIP, 2, d, bn)
                    new_flights = [(rs_chips(sib_out), [("ffn_w_out", widx)])]
                    sib, targets = rs_sibling([g_in], f"{tag}_in"), [("ffn_w_in", widx)]
                z3, w_blk = dgu, sv["w_in"]
            elif i % 2 == 0:
                ds = _mm_nt(dy, sv["w_out"], 0, name=f"gm_ds_{tag}")
                g_out = _mm_tn(sv["sg"], dy[None], _tile(e, 1024, V7X_LANES), d, False,
                               name=f"gm_dwout_{tag}").reshape(N_CHIP, 2, dl, d)
                du, dvn, dws, dbs = _sgu_bwd(ds, sv["u"], sv["vn"], ws, bsb, name=f"sgu_bwd_{tag}")
                dpre, dlng, dlnb = _gm_act_bwd(sv["pre"], du, dvn, gm_g, name=f"gm_act_bwd_{tag}")
                small["gm_ln_g"], small["gm_ln_b"] = dlng, dlnb
                small["gm_ws"], small["gm_bs"] = dws, dbs[:, :, 0]
                g_in = _mm_tn(sv["h"], dpre[None], d, el, True, name=f"gm_dwin_{tag}").reshape(N_CHIP, 2, d, el)
                targets, new_flights = [("gm_w_in", 0), ("gm_w_out", 0)], []
                sib = rs_sibling([g_in, g_out], tag)
                z3, w_blk = dpre[None], sv["w_in"]
            else:
                dys = _mm_nt(dy, sv["w_out"], 0, name=f"cv_dys_{tag}")
                g_out = _mm_tn(sv["ys"], dy[None], _tile(e, 1024, V7X_LANES), d, False,
                               name=f"cv_dwout_{tag}").reshape(N_CHIP, 2, dl, d)
                dyc, dlng, dlnb, ddwb = _cv_act_bwd(dys, sv["yc"], cln_g_full, cln_b_full, name=f"cv_act_bwd_{tag}")
                dp, ddww, dba, dbg = _dwconv_bwd(dyc, sv["p"], dww_full, name=f"dwconv_bwd_{tag}")
                small["cv_b_out"], small["cv_ln_g"], small["cv_ln_b"], small["cv_dw_b"] = dbout, dlng, dlnb, ddwb
                small["cv_dw_w"] = ddww
                small["cv_b_in"] = jnp.concatenate([dba, dbg], axis=1)
                g_in = _mm_tn(sv["h"], dp, d, el, True, name=f"cv_dwin_{tag}").reshape(N_CHIP, 2, d, el)
                targets, new_flights = [("cv_w_in", 0), ("cv_w_out", 0)], []
                sib = rs_sibling([g_in, g_out], tag)
                z3, w_blk = dp, sv["w_in"]
            nxt = None if last else bwd_head(*((i, s - 1) if s > 0 else (i - 1, 2)))
            res = _dh_norm_bwd(z3, w_blk, sv["x"], dx, g_norm, scale, nxt, name=f"dh_norm_bwd_{tag}")
            if last:
                (dx, dscale, dshift, dgn), last_sibling = res, (sib, targets)
            else:
                new_flights.append((rs_chips(sib), targets))
                dx, dy_next, dscale, dshift, dgn, dgate_next = res[:6]
                dbout_next = res[6] if nxt[3] else None
            dmod[i][s] = [dshift, dscale, dgate]
            d_norm_g[i][s] = dgn
            if not last:
                dy, dgate, dbout = dy_next, dgate_next, dbout_next
            for flight in pending:
                rs_finish(*flight)
            pending = new_flights
    grad_x = dx[None]

    dmod_mine = jnp.concatenate([v for per_l in dmod for per_s in per_l for v in per_s], axis=1)
    dng_mine = jnp.concatenate([v for per_l in d_norm_g for v in per_l], axis=1)
    small_out = [dmod_mine, dng_mine, small["gm_ln_g"], small["gm_ln_b"], small["gm_ws"], small["gm_bs"],
                 small["cv_b_in"], small["cv_dw_w"], small["cv_dw_b"], small["cv_ln_g"], small["cv_ln_b"],
                 small["cv_b_out"], d_final_g]
    shapes2 = [s.shape for s in small_out]
    (pack2_all,) = _all_gather([_pack(small_out, rows_align=256)[None, None]], name="ag_small_grads")
    _Seq.last = pack2_all
    pending.append((rs_chips(last_sibling[0]), last_sibling[1]))
    summed = _unpack(_sum_devices(pack2_all[0], name="sum_small_grads"), shapes2)
    dmod_all = _unpack(pack2_all[0], shapes2)[0].reshape(N_DEV, depth, 9 * d)

    def my_cols(full, width):
        return lax.dynamic_slice_in_dim(full, me * width, width, axis=full.ndim - 1)

    g_ada_b = summed[0].reshape(depth, 9 * d)
    g_norm_g = my_cols(summed[1].reshape(depth, 3, d), dl)
    g_small = {
        "ada_b": g_ada_b, "norm_g": g_norm_g,
        "gm_ln_g": summed[2], "gm_ln_b": summed[3], "gm_ws": summed[4][None], "gm_bs": summed[5][None],
        "cv_b_in": my_cols(summed[6], el), "cv_dw_w": my_cols(summed[7], dl)[None],
        "cv_dw_b": my_cols(summed[8], dl), "cv_ln_g": my_cols(summed[9], dl), "cv_ln_b": my_cols(summed[10], dl),
        "cv_b_out": my_cols(summed[11], dl), "final_g": summed[12].reshape(d),
    }

    dm_loc = jnp.moveaxis(my_cols(dmod_all, cl), 0, 1)
    dm_loc = jnp.pad(dm_loc, ((0, 0), (0, 16 - N_DEV), (0, 0)))
    res_ada_w = _ada_bwd(c_pad, dm_loc, ada_w, m_ada_w, v_ada_w, name="ada_bwd_adamw")

    def flat2(a):
        return a.reshape(-1, a.shape[-1])

    small_params = {
        "ada_b": (ada_b, m_ada_b, v_ada_b), "norm_g": (norm_g, m_norm_g, v_norm_g),
        "gm_ln_g": (gm_ln_g, m_gm_ln_g, v_gm_ln_g), "gm_ln_b": (gm_ln_b, m_gm_ln_b, v_gm_ln_b),
        "gm_ws": (gm_ws, m_gm_ws, v_gm_ws), "gm_bs": (gm_bs, m_gm_bs, v_gm_bs),
        "cv_b_in": (cv_b_in, m_cv_b_in, v_cv_b_in), "cv_dw_w": (cv_dw_w, m_cv_dw_w, v_cv_dw_w),
        "cv_dw_b": (cv_dw_b, m_cv_dw_b, v_cv_dw_b), "cv_ln_g": (cv_ln_g, m_cv_ln_g, v_cv_ln_g),
        "cv_ln_b": (cv_ln_b, m_cv_ln_b, v_cv_ln_b), "cv_b_out": (cv_b_out, m_cv_b_out, v_cv_b_out),
        "final_g": (final_g, m_final_g, v_final_g),
    }
    res_small = {}
    for key, (w, m, v) in small_params.items():
        g2 = flat2(g_small[key].reshape(w.shape)) if w.ndim > 1 else g_small[key].reshape(1, -1)
        w2, m2, v2 = [flat2(a) if a.ndim > 1 else a.reshape(1, -1) for a in (w, m, v)]
        res_small[key] = [o.reshape(w.shape) for o in _adamw([(g2[None], 0)], w2, m2, v2, name=f"adamw_{key}")]

    for flight in pending:
        rs_finish(*flight)

    def big(name, k):
        if name == "ada_w":
            return res_ada_w[k]
        return res_big[name][k].reshape(stacked_shape[name])

    stacked_shape = {"ffn_w_in": ffn_w_in.shape, "ffn_w_out": ffn_w_out.shape, "gm_w_in": gm_w_in.shape,
                     "gm_w_out": gm_w_out.shape, "cv_w_in": cv_w_in.shape, "cv_w_out": cv_w_out.shape}

    order = ["ada_w", "ada_b", "norm_g", "ffn_w_in", "ffn_w_out", "gm_w_in", "gm_ln_g", "gm_ln_b", "gm_ws", "gm_bs",
             "gm_w_out", "cv_w_in", "cv_b_in", "cv_dw_w", "cv_dw_b", "cv_ln_g", "cv_ln_b", "cv_w_out", "cv_b_out",
             "final_g"]
    outs = [loss, grad_x]
    for k in range(4):
        for name in order:
            outs.append(res_small[name][k] if name in res_small else big(name, k))
    return tuple(outs)
```

```python
import functools

import jax
import jax.numpy as jnp
from jax import lax
from jax.experimental import pallas as pl
from jax.experimental.pallas import tpu as pltpu

F32 = jnp.float32
BF = jnp.bfloat16
MESH = pl.DeviceIdType.MESH

N_DEV = 8
N_CHIP = 4
NORM_EPS = 1e-6
ADAM_LR = 0.001
ADAM_B1 = 0.9
ADAM_B2 = 0.999
ADAM_EPS = 1e-08
ADAM_WD = 0.01
ADAM_STEP = 10

V7X_SUBLANES = 8
V7X_LANES = 128
PACK_ALIGN = V7X_SUBLANES * V7X_LANES
V7X_VMEM_LIMIT = 56 * 1024 * 1024


def _tile(n, pref, align):
    if n <= pref:
        return n
    t = pref - pref % align
    while t >= align:
        if n % t == 0:
            return t
        t -= align
    return n


ANY = pl.BlockSpec(memory_space=pl.ANY)


class _Seq:
    last = None
    tokens = []


def _call(body, *, name, grid, in_specs, out_specs, out_shape, scratch=(), sem=None, prefetch=None, aliases=None,
          after=(), on_path=True):
    def run(*args):
        if on_path:
            tokens, _Seq.tokens = _Seq.tokens + list(after), []
        else:
            tokens = list(after)
        lead = 0 if prefetch is None else 1
        n_in, n_tok = lead + len(args), len(tokens)

        def wrapped(*refs):
            body(*refs[:n_in], *refs[n_in + n_tok:])

        specs = list(in_specs) + [ANY] * n_tok
        params = pltpu.CompilerParams(dimension_semantics=sem, vmem_limit_bytes=V7X_VMEM_LIMIT)
        if prefetch is None:
            res = pl.pallas_call(wrapped, out_shape=out_shape, grid=grid, in_specs=specs, out_specs=out_specs,
                                 scratch_shapes=scratch, name=name, compiler_params=params,
                                 input_output_aliases=aliases or {})(*args, *tokens)
        else:
            grid_spec = pltpu.PrefetchScalarGridSpec(num_scalar_prefetch=1, grid=grid, in_specs=specs,
                                                     out_specs=out_specs, scratch_shapes=scratch)
            res = pl.pallas_call(wrapped, out_shape=out_shape, grid_spec=grid_spec, name=name,
                                 compiler_params=params, input_output_aliases=aliases or {})(prefetch, *args, *tokens)
        if on_path:
            _Seq.last = res[0] if isinstance(res, (list, tuple)) else res
        return res
    return run


def _sds(shape, dtype):
    return jax.ShapeDtypeStruct(tuple(shape), dtype)


def _sigmoid(v):
    return 1.0 / (1.0 + jnp.exp(-v))


def _normal_cdf_pdf(v):
    a = jnp.abs(v) * 0.7071067811865476
    t = 1.0 / (1.0 + 0.3275911 * a)
    poly = t * (0.254829592 + t * (-0.284496736 + t * (1.421413741 + t * (-1.453152027 + t * 1.061405429))))
    e = jnp.exp(-0.5 * v * v)
    half_erf = 0.5 - 0.5 * poly * e
    return 0.5 + jnp.where(v < 0, -half_erf, half_erf), 0.3989422804014327 * e


def _gelu(v):
    return v * _normal_cdf_pdf(v)[0]


def _fold_rows(val):
    rows, w = val.shape
    return val.reshape(rows // V7X_SUBLANES, V7X_SUBLANES, w).sum(axis=0)


def _rowwise(fn, name, rows_in, vecs_in, rows_out, acc_widths, tm=256):
    t = rows_in[0].shape[0]
    tm = _tile(t, tm, V7X_SUBLANES)
    steps = t // tm
    nr, nv, no, na = len(rows_in), len(vecs_in), len(rows_out), len(acc_widths)

    def body(*refs):
        rin, vin = refs[:nr], refs[nr:nr + nv]
        rout = refs[nr + nv:nr + nv + no]
        aout = refs[nr + nv + no:nr + nv + no + na]
        accs = refs[nr + nv + no + na:]
        i = pl.program_id(0)
        outs, acc_vals = fn(*[r[...] for r in rin], *[v[...] for v in vin])
        for r, o in zip(rout, outs):
            r[...] = o.astype(r.dtype)
        if na:
            @pl.when(i == 0)
            def _():
                for a in accs:
                    a[...] = jnp.zeros_like(a)

            for a, val in zip(accs, acc_vals):
                a[...] += _fold_rows(val)

            @pl.when(i == steps - 1)
            def _():
                for o, a in zip(aout, accs):
                    o[...] = jnp.sum(a[...], axis=0, keepdims=True)

    in_specs = [pl.BlockSpec((tm, r.shape[1]), lambda i: (i, 0)) for r in rows_in]
    in_specs += [pl.BlockSpec(v.shape, functools.partial(lambda nd, i: (0,) * nd, v.ndim)) for v in vecs_in]
    out_specs = [pl.BlockSpec((tm, r.shape[1]), lambda i: (i, 0)) for r in rows_out]
    out_specs += [pl.BlockSpec((1, w), lambda i: (0, 0)) for w in acc_widths]
    out_shape = list(rows_out) + [_sds((1, w), F32) for w in acc_widths]
    scratch = [pltpu.VMEM((V7X_SUBLANES, w), F32) for w in acc_widths]
    res = _call(body, name=name, grid=(steps,), in_specs=in_specs, out_specs=out_specs, out_shape=out_shape,
                scratch=scratch, sem=("arbitrary",) if na else ("parallel",))(*rows_in, *vecs_in)
    return res[:no], res[no:]


def _norm_mod(x, g, scale, shift, name):
    def fn(xv, gv, sc, sh):
        r = lax.rsqrt(jnp.mean(xv * xv, axis=-1, keepdims=True) + NORM_EPS)
        return ((xv * r * gv) * (1.0 + sc) + sh,), ()
    (h,), _ = _rowwise(fn, name, [x], [g, scale, shift], [_sds(x.shape, BF)], [])
    return h


def _final_loss(x, target, g, y, gate, coef, name):
    d = x.shape[1]

    def fn(xv, tv, yv, gv, gt):
        r = lax.rsqrt(jnp.mean(xv * xv, axis=-1, keepdims=True) + NORM_EPS)
        xhat = xv * r
        err = xhat * gv - tv
        dl = err * (1.0 / d)
        dxhat = dl * gv
        dx = r * (dxhat - xhat * jnp.mean(dxhat * xhat, axis=-1, keepdims=True))
        return (dx, (coef * gt) * dx), (err * err, dl * xhat, coef * dx * yv.astype(F32))
    (dx, dy), (sq, dg, dgate) = _rowwise(fn, name, [x, target, y], [g, gate],
                                         [_sds(x.shape, F32), _sds(x.shape, BF)], [d, d, d])
    return sq, dx, dg, dy, dgate


def _gm_act(pre, ln_g, ln_b, name):
    e = pre.shape[1] // 2

    def fn(pv, gv, bv):
        p = pv.astype(F32)
        u = _gelu(p[:, :e])
        v = _gelu(p[:, e:])
        mu = jnp.mean(v, axis=-1, keepdims=True)
        vc = v - mu
        rstd = lax.rsqrt(jnp.mean(vc * vc, axis=-1, keepdims=True) + NORM_EPS)
        return (u, vc * rstd * gv + bv), ()
    t = pre.shape[0]
    (u, vn), _ = _rowwise(fn, name, [pre], [ln_g, ln_b], [_sds((t, e), BF), _sds((t, e), BF)], [])
    return u, vn


def _gm_act_bwd(pre, du, dvn, ln_g, name):
    e = pre.shape[1] // 2

    def fn(pv, duv, dvv, gv):
        p = pv.astype(F32)
        pu, pvv = p[:, :e], p[:, e:]
        cdf_u, pdf_u = _normal_cdf_pdf(pu)
        cdf_v, pdf_v = _normal_cdf_pdf(pvv)
        v = pvv * cdf_v
        mu = jnp.mean(v, axis=-1, keepdims=True)
        vc = v - mu
        rstd = lax.rsqrt(jnp.mean(vc * vc, axis=-1, keepdims=True) + NORM_EPS)
        vhat = vc * rstd
        dvn_f = dvv.astype(F32)
        dvhat = dvn_f * gv
        dv = rstd * (dvhat - jnp.mean(dvhat, axis=-1, keepdims=True)
                     - vhat * jnp.mean(dvhat * vhat, axis=-1, keepdims=True))
        dpu = duv.astype(F32) * (cdf_u + pu * pdf_u)
        dpv = dv * (cdf_v + pvv * pdf_v)
        return (jnp.concatenate([dpu, dpv], axis=1),), (dvn_f * vhat, dvn_f)
    (dpre,), (dg, db) = _rowwise(fn, name, [pre, du, dvn], [ln_g], [_sds(pre.shape, BF)], [e, e], tm=128)
    return dpre, dg, db


def _cv_act(yc, ln_g, ln_b, name):
    def fn(yv, gv, bv):
        mu = jnp.mean(yv, axis=-1, keepdims=True)
        c = yv - mu
        rstd = lax.rsqrt(jnp.mean(c * c, axis=-1, keepdims=True) + NORM_EPS)
        yn = c * rstd * gv + bv
        return (yn * _sigmoid(yn),), ()
    (ys,), _ = _rowwise(fn, name, [yc], [ln_g, ln_b], [_sds(yc.shape, BF)], [])
    return ys


def _cv_act_bwd(dys, yc, ln_g, ln_b, name):
    def fn(dv, yv, gv, bv):
        mu = jnp.mean(yv, axis=-1, keepdims=True)
        c = yv - mu
        rstd = lax.rsqrt(jnp.mean(c * c, axis=-1, keepdims=True) + NORM_EPS)
        yhat = c * rstd
        yn = yhat * gv + bv
        sig = _sigmoid(yn)
        dyn = dv.astype(F32) * (sig * (1.0 + yn * (1.0 - sig)))
        dyhat = dyn * gv
        dyc = rstd * (dyhat - jnp.mean(dyhat, axis=-1, keepdims=True)
                      - yhat * jnp.mean(dyhat * yhat, axis=-1, keepdims=True))
        return (dyc,), (dyn * yhat, dyn, dyc)
    cw = yc.shape[1]
    (dyc,), (dg, db, dbias) = _rowwise(fn, name, [dys, yc], [ln_g, ln_b], [_sds(yc.shape, F32)], [cw, cw, cw])
    return dyc, dg, db, dbias


MM_ROWS = 1024
MM_SEG_ROWS = 256


def _ffn_in(h, w_blk, blk0, name):
    t, d = h.shape
    bn = w_blk.shape[2]
    half = N_DEV // 2
    f = half * bn
    tm = _tile(t, MM_ROWS, V7X_SUBLANES)
    seg_rows = _tile(tm, MM_SEG_ROWS, 2 * V7X_SUBLANES)

    def body(h_ref, wg_ref, wu_ref, dg_ref, du_ref, a_ref):
        for seg in range(tm // seg_rows):
            rows = pl.ds(seg * seg_rows, seg_rows)
            hv = h_ref[rows, :]
            g = jnp.dot(hv, wg_ref[...], preferred_element_type=F32)
            u = jnp.dot(hv, wu_ref[...], preferred_element_type=F32)
            sig = _sigmoid(g)
            sl = g * sig
            dg_ref[rows, :] = (u * (sig * (1.0 + g * (1.0 - sig)))).astype(BF)
            du_ref[rows, :] = sl.astype(BF)
            a_ref[rows, :] = (sl * u).astype(BF)

    out = _sds((t, f), BF)
    tile = pl.BlockSpec((tm, bn), lambda j, i: (i, j))
    return _call(
        body, name=name, grid=(half, t // tm),
        in_specs=[pl.BlockSpec((tm, d), lambda j, i: (i, 0)),
                  pl.BlockSpec((None, d, bn), lambda j, i: (blk0 + j, 0, 0)),
                  pl.BlockSpec((None, d, bn), lambda j, i: (blk0 + half + j, 0, 0))],
        out_specs=[tile, tile, tile], out_shape=[out, out, out], sem=("parallel", "parallel"))(h, w_blk, w_blk)


def _in_proj(h, w_blk, bias, name):
    t, d = h.shape
    bn = w_blk.shape[2]
    tm = _tile(t, 1024, V7X_SUBLANES)

    def body(*refs):
        if bias is None:
            h_ref, w_ref, o_ref = refs
            o_ref[...] = jnp.dot(h_ref[...], w_ref[...], preferred_element_type=F32).astype(BF)
        else:
            h_ref, w_ref, b_ref, o_ref = refs
            o_ref[...] = (jnp.dot(h_ref[...], w_ref[...], preferred_element_type=F32) + b_ref[...]).astype(BF)

    in_specs = [pl.BlockSpec((tm, d), lambda j, i: (i, 0)), pl.BlockSpec((None, d, bn), lambda j, i: (j, 0, 0))]
    args = [h, w_blk]
    if bias is not None:
        in_specs.append(pl.BlockSpec((1, bn), lambda j, i: (0, j)))
        args.append(bias)
    return _call(body, name=name, grid=(N_DEV, t // tm), in_specs=in_specs,
                 out_specs=pl.BlockSpec((tm, bn), lambda j, i: (i, j)), out_shape=_sds((t, N_DEV * bn), BF),
                 sem=("parallel", "parallel"))(*args)


def _out_proj(a, w3, widx, x, gate, bias, coef, name):
    t, k = a.shape
    d = w3.shape[2]
    tm = _tile(t, MM_ROWS, V7X_SUBLANES)
    tn = _tile(d, 512, V7X_LANES)

    def body(*refs):
        if bias is None:
            a_ref, w_ref, x_ref, g_ref, xo_ref, y_ref = refs
            y = jnp.dot(a_ref[...], w_ref[...], preferred_element_type=F32)
        else:
            a_ref, w_ref, x_ref, g_ref, b_ref, xo_ref, y_ref = refs
            y = jnp.dot(a_ref[...], w_ref[...], preferred_element_type=F32) + b_ref[...]
        y_ref[...] = y.astype(BF)
        xo_ref[...] = x_ref[...] + (coef * g_ref[...]) * y

    tile = pl.BlockSpec((tm, tn), lambda j, i: (i, j))
    vec = pl.BlockSpec((1, tn), lambda j, i: (0, j))
    in_specs = [pl.BlockSpec((tm, k), lambda j, i: (i, 0)),
                pl.BlockSpec((None, k, tn), lambda j, i: (widx, 0, j)), tile, vec]
    args = [a, w3, x, gate]
    if bias is not None:
        in_specs.append(vec)
        args.append(bias)
    return _call(body, name=name, grid=(d // tn, t // tm), in_specs=in_specs, out_specs=[tile, tile],
                 out_shape=[_sds((t, d), F32), _sds((t, d), BF)], sem=("parallel", "parallel"))(*args)


def _ffn_da(dy, w3, widx, fg, fu, name):
    t, d = dy.shape
    f = w3.shape[1]
    bn = f // (N_DEV // 2)
    tm = _tile(t, MM_ROWS, V7X_SUBLANES)
    seg_rows = _tile(tm, MM_SEG_ROWS, 2 * V7X_SUBLANES)

    def body(dy_ref, w_ref, fg_ref, fu_ref, dgu_ref):
        for seg in range(tm // seg_rows):
            rows = pl.ds(seg * seg_rows, seg_rows)
            da = lax.dot_general(dy_ref[rows, :], w_ref[...], (((1,), (1,)), ((), ())), preferred_element_type=F32)
            dgu_ref[0, rows, :] = (da * fg_ref[rows, :].astype(F32)).astype(BF)
            dgu_ref[1, rows, :] = (da * fu_ref[rows, :].astype(F32)).astype(BF)

    tile = pl.BlockSpec((tm, bn), lambda j, i: (i, j))
    return _call(
        body, name=name, grid=(f // bn, t // tm),
        in_specs=[pl.BlockSpec((tm, d), lambda j, i: (i, 0)),
                  pl.BlockSpec((None, bn, d), lambda j, i: (widx, j, 0)), tile, tile],
        out_specs=pl.BlockSpec((2, tm, bn), lambda j, i: (0, i, j)),
        out_shape=_sds((2, t, f), BF), sem=("parallel", "parallel"))(dy, w3, fg, fu)


def _mm_nt(dy, w3, widx, name):
    t, k = dy.shape
    n = w3.shape[1]
    tm = _tile(t, MM_ROWS, V7X_SUBLANES)
    tn = _tile(n, 1024, V7X_LANES)

    def body(dy_ref, w_ref, o_ref):
        o_ref[...] = lax.dot_general(dy_ref[...], w_ref[...], (((1,), (1,)), ((), ())),
                                     preferred_element_type=F32).astype(BF)

    return _call(body, name=name, grid=(n // tn, t // tm),
                 in_specs=[pl.BlockSpec((tm, k), lambda j, i: (i, 0)),
                           pl.BlockSpec((None, tn, k), lambda j, i: (widx, j, 0))],
                 out_specs=pl.BlockSpec((tm, tn), lambda j, i: (i, j)), out_shape=_sds((t, n), BF),
                 sem=("parallel", "parallel"))(dy, w3)


NORM_BWD_ROWS = 128


def _dh_norm_bwd(z3, w_blk, x, dxp, g, scale, nxt, name):
    lead, t, _ = z3.shape
    d, bn = w_blk.shape[1], w_blk.shape[2]
    per = N_DEV // lead
    tm = _tile(t, 512, NORM_BWD_ROWS)
    ni = t // tm
    n_in = 6 if nxt is None else 8
    coef, colsum = (None, False) if nxt is None else nxt[2:]
    n_acc = 3 if nxt is None else (5 if colsum else 4)
    n_rows = 1 if nxt is None else 2

    def body(*refs):
        z_ref, w_ref, x_ref, dp_ref, g_ref, sc_ref = refs[:6]
        row_outs = refs[n_in:n_in + n_rows]
        vec_outs = refs[n_in + n_rows:n_in + n_rows + n_acc]
        acc_ref, accs = refs[n_in + n_rows + n_acc], refs[n_in + n_rows + n_acc + 1:]
        i, k = pl.program_id(0), pl.program_id(1)

        @pl.when(k == 0)
        def _():
            acc_ref[...] = jnp.zeros_like(acc_ref)

        @pl.when((i == 0) & (k == 0))
        def _():
            for a in accs:
                a[...] = jnp.zeros_like(a)

        acc_ref[...] += lax.dot_general(z_ref[...], w_ref[...], (((1,), (1,)), ((), ())),
                                        preferred_element_type=F32)

        @pl.when(k == N_DEV - 1)
        def _():
            gv, sc = g_ref[...], sc_ref[...]

            def chunk(ci, carry):
                rows = pl.ds(pl.multiple_of(ci * NORM_BWD_ROWS, NORM_BWD_ROWS), NORM_BWD_ROWS)
                dh, xv = acc_ref[rows, :], x_ref[rows, :]
                r = lax.rsqrt(jnp.mean(xv * xv, axis=-1, keepdims=True) + NORM_EPS)
                xhat = xv * r
                dn = dh * (1.0 + sc)
                dxhat = dn * gv
                dx = r * (dxhat - xhat * jnp.mean(dxhat * xhat, axis=-1, keepdims=True)) + dp_ref[rows, :]
                row_outs[0][rows, :] = dx
                accs[0][...] += _fold_rows(dh * (xhat * gv))
                accs[1][...] += _fold_rows(dh)
                accs[2][...] += _fold_rows(dn * xhat)
                if nxt is not None:
                    y_ref, gate_ref = refs[6], refs[7]
                    dy = (coef * gate_ref[...]) * dx
                    row_outs[1][rows, :] = dy.astype(BF)
                    accs[3][...] += _fold_rows(coef * dx * y_ref[rows, :].astype(F32))
                    if colsum:
                        accs[4][...] += _fold_rows(dy)
                return carry
            lax.fori_loop(0, tm // NORM_BWD_ROWS, chunk, 0)

        @pl.when((i == ni - 1) & (k == N_DEV - 1))
        def _():
            for o, a in zip(vec_outs, accs):
                o[...] = jnp.sum(a[...], axis=0, keepdims=True)

    rows = pl.BlockSpec((tm, d), lambda i, k: (i, 0))
    vec = pl.BlockSpec((1, d), lambda i, k: (0, 0))
    in_specs = [pl.BlockSpec((None, tm, bn), lambda i, k: (k // per, i, k % per)),
                pl.BlockSpec((None, d, bn), lambda i, k: (k, 0, 0)), rows, rows, vec, vec]
    args = [z3, w_blk, x, dxp, g, scale]
    out_specs, out_shape = [rows], [_sds((t, d), F32)]
    if nxt is not None:
        in_specs += [rows, vec]
        args += [nxt[0], nxt[1]]
        out_specs.append(rows)
        out_shape.append(_sds((t, d), BF))
    out_specs += [vec] * n_acc
    out_shape += [_sds((1, d), F32)] * n_acc
    return _call(body, name=name, grid=(ni, N_DEV), in_specs=in_specs, out_specs=out_specs, out_shape=out_shape,
                 scratch=[pltpu.VMEM((tm, d), F32)] + [pltpu.VMEM((V7X_SUBLANES, d), F32)] * n_acc,
                 sem=("arbitrary", "arbitrary"))(*args)


def _mm_tn(a, b3, ta, tb, blocked, name):
    t, ka = a.shape
    lead, _, w = b3.shape
    per = w // tb
    nj = lead * per
    tk = t
    while tk > 512 and 4 * tk * (ta + tb) + 8 * ta * tb > V7X_VMEM_LIMIT * 3 // 4:
        tk //= 2
    tk = _tile(t, tk, V7X_SUBLANES)
    nk = t // tk

    def body(a_ref, b_ref, o_ref, acc_ref):
        k = pl.program_id(2)

        @pl.when(k == 0)
        def _():
            acc_ref[...] = jnp.zeros_like(acc_ref)

        acc_ref[...] += lax.dot_general(a_ref[...], b_ref[...], (((0,), (0,)), ((), ())),
                                        preferred_element_type=F32)

        @pl.when(k == nk - 1)
        def _():
            o_ref[...] = acc_ref[...].astype(BF)

    if blocked:
        out_shape = _sds((nj, ka, tb), BF)
        out_spec = pl.BlockSpec((None, ta, tb), lambda i, j, k: (j, i, 0))
    else:
        out_shape = _sds((1, ka, w), BF)
        out_spec = pl.BlockSpec((None, ta, tb), lambda i, j, k: (0, i, j))
    return _call(body, name=name, grid=(ka // ta, nj, nk),
                 in_specs=[pl.BlockSpec((tk, ta), lambda i, j, k: (k, i)),
                           pl.BlockSpec((None, tk, tb), lambda i, j, k: (j // per, k, j % per))],
                 out_specs=out_spec, out_shape=out_shape, scratch=[pltpu.VMEM((ta, tb), F32)],
                 sem=("parallel", "parallel", "arbitrary"))(a, b3)


def _causal(ws):
    l = ws.shape[0]
    row = lax.broadcasted_iota(jnp.int32, (l, l), 0)
    col = lax.broadcasted_iota(jnp.int32, (l, l), 1)
    return jnp.where(col <= row, ws, 0.0)


def _sgu_fwd(u, vn, ws, bsb, name):
    t, e = u.shape
    hn, l, _ = ws.shape
    dh = e // hn
    nc = t // l

    def body(u_ref, v_ref, ws_ref, bs_ref, s_ref):
        wsc = _causal(ws_ref[...]).astype(BF)
        bias = bs_ref[...]

        def chunk(c, carry):
            rows = pl.ds(pl.multiple_of(c * l, l), l)
            vo = jnp.dot(wsc, v_ref[rows, :], preferred_element_type=F32) + bias
            s_ref[rows, :] = (u_ref[rows, :].astype(F32) * vo).astype(BF)
            return carry
        lax.fori_loop(0, nc, chunk, 0)

    col = pl.BlockSpec((t, dh), lambda h: (0, h))
    return _call(body, name=name, grid=(hn,),
                 in_specs=[col, col, pl.BlockSpec((None, l, l), lambda h: (h, 0, 0)),
                           pl.BlockSpec((None, l, dh), lambda h: (h, 0, 0))],
                 out_specs=col, out_shape=_sds((t, e), BF), sem=("parallel",))(u, vn, ws, bsb)


def _sgu_bwd(ds, u, vn, ws, bsb, name):
    t, e = u.shape
    hn, l, _ = ws.shape
    dh = e // hn
    nc = t // l

    def body(ds_ref, u_ref, v_ref, ws_ref, bs_ref, du_ref, dv_ref, dws_ref, dbs_ref, accw_ref, accb_ref):
        wsc = _causal(ws_ref[...]).astype(BF)
        bias = bs_ref[...]
        accw_ref[...] = jnp.zeros_like(accw_ref)
        accb_ref[...] = jnp.zeros_like(accb_ref)

        def chunk(c, carry):
            rows = pl.ds(pl.multiple_of(c * l, l), l)
            vc = v_ref[rows, :]
            dsv = ds_ref[rows, :].astype(F32)
            vo = jnp.dot(wsc, vc, preferred_element_type=F32) + bias
            du_ref[rows, :] = (dsv * vo).astype(BF)
            dvo = dsv * u_ref[rows, :].astype(F32)
            dvo_b = dvo.astype(BF)
            accb_ref[...] += dvo
            accw_ref[...] += lax.dot_general(dvo_b, vc, (((1,), (1,)), ((), ())), preferred_element_type=F32)
            dv_ref[rows, :] = lax.dot_general(wsc, dvo_b, (((0,), (0,)), ((), ())),
                                              preferred_element_type=F32).astype(BF)
            return carry
        lax.fori_loop(0, nc, chunk, 0)
        dws_ref[...] = _causal(accw_ref[...])
        dbs_ref[...] = jnp.broadcast_to(jnp.sum(accb_ref[...], axis=1, keepdims=True), (l, dh))

    col = pl.BlockSpec((t, dh), lambda h: (0, h))
    return _call(body, name=name, grid=(hn,),
                 in_specs=[col, col, col, pl.BlockSpec((None, l, l), lambda h: (h, 0, 0)),
                           pl.BlockSpec((None, l, dh), lambda h: (h, 0, 0))],
                 out_specs=[col, col, pl.BlockSpec((None, l, l), lambda h: (h, 0, 0)),
                            pl.BlockSpec((None, l, dh), lambda h: (h, 0, 0))],
                 out_shape=[_sds((t, e), BF), _sds((t, e), BF), _sds((hn, l, l), F32), _sds((hn, l, dh), F32)],
                 scratch=[pltpu.VMEM((l, l), F32), pltpu.VMEM((l, dh), F32)],
                 sem=("parallel",))(ds, u, vn, ws, bsb)


AG_AHEAD = 3

CONV_HALO = 32
CONV_ROWS = 64
CONV_LANES = 256


def _shifted_windows(win_ref, sh_ref, rows):
    for b in range(1, V7X_SUBLANES):
        sh_ref[b - 1, 0:rows, :] = win_ref[b:b + rows, :]


def _window_rows(win_ref, sh_ref, shift, r0, rows):
    a, b = divmod(shift, V7X_SUBLANES)
    start = pl.multiple_of(r0 + V7X_SUBLANES * a, V7X_SUBLANES)
    if b == 0:
        return win_ref[pl.ds(start, rows), :]
    return sh_ref[b - 1, pl.ds(start, rows), :]


def _dwconv_fwd(p, dw_w, dw_b, name):
    t, c2 = p.shape
    cw = c2 // 2
    kw = dw_w.shape[0]
    cb = _tile(cw, CONV_LANES, V7X_LANES)
    ncb = cw // cb
    tm = _tile(t, 512, CONV_ROWS)
    off = CONV_HALO - (kw - 1)

    def body(a_ref, g_ref, ap_ref, gp_ref, w_ref, b_ref, o_ref, win_ref, sh_ref):
        i = pl.program_id(1)
        prev = ap_ref[...].astype(F32) * _sigmoid(gp_ref[...].astype(F32))
        win_ref[0:CONV_HALO, :] = jnp.where(i > 0, prev, 0.0)
        win_ref[CONV_HALO:, :] = a_ref[...].astype(F32) * _sigmoid(g_ref[...].astype(F32))
        _shifted_windows(win_ref, sh_ref, tm + CONV_HALO - V7X_SUBLANES)

        def chunk(ci, carry):
            r0 = ci * CONV_ROWS
            acc = jnp.zeros((CONV_ROWS, cb), F32) + b_ref[...]
            for k in range(kw):
                acc = acc + w_ref[k:k + 1, :] * _window_rows(win_ref, sh_ref, off + k, r0, CONV_ROWS)
            o_ref[pl.ds(pl.multiple_of(r0, CONV_ROWS), CONV_ROWS), :] = acc
            return carry
        lax.fori_loop(0, tm // CONV_ROWS, chunk, 0)

    hpt = tm // CONV_HALO
    cur_a = pl.BlockSpec((tm, cb), lambda j, i: (i, j))
    cur_g = pl.BlockSpec((tm, cb), lambda j, i: (i, ncb + j))
    prev_a = pl.BlockSpec((CONV_HALO, cb), lambda j, i: (jnp.maximum(i * hpt - 1, 0), j))
    prev_g = pl.BlockSpec((CONV_HALO, cb), lambda j, i: (jnp.maximum(i * hpt - 1, 0), ncb + j))
    return _call(body, name=name, grid=(ncb, t // tm),
                 in_specs=[cur_a, cur_g, prev_a, prev_g, pl.BlockSpec((kw, cb), lambda j, i: (0, j)),
                           pl.BlockSpec((1, cb), lambda j, i: (0, j))],
                 out_specs=pl.BlockSpec((tm, cb), lambda j, i: (i, j)), out_shape=_sds((t, cw), F32),
                 scratch=[pltpu.VMEM((tm + CONV_HALO, cb), F32),
                          pltpu.VMEM((V7X_SUBLANES - 1, tm + CONV_HALO - V7X_SUBLANES, cb), F32)],
                 sem=("parallel", "parallel"))(p, p, p, p, dw_w, dw_b)


def _dwconv_bwd(dyc, p, dw_w, name):
    t, c2 = p.shape
    cw = c2 // 2
    kw = dw_w.shape[0]
    cb = _tile(cw, CONV_LANES, V7X_LANES)
    ncb = cw // cb
    tm = _tile(t, 512, CONV_ROWS)
    nt = t // tm
    off = CONV_HALO - (kw - 1)
    kpad = -(-kw // V7X_SUBLANES) * V7X_SUBLANES
    sh_rows = tm + CONV_HALO - V7X_SUBLANES

    def body(d_ref, dn_ref, a_ref, g_ref, ap_ref, gp_ref, w_ref,
             dp_ref, dw_ref, dba_ref, dbg_ref, dwin_ref, ywin_ref, dsh_ref, ysh_ref, accw_ref, acca_ref, accg_ref):
        i = pl.program_id(1)

        @pl.when(i == 0)
        def _():
            accw_ref[...] = jnp.zeros_like(accw_ref)
            acca_ref[...] = jnp.zeros_like(acca_ref)
            accg_ref[...] = jnp.zeros_like(accg_ref)

        prev = ap_ref[...].astype(F32) * _sigmoid(gp_ref[...].astype(F32))
        ywin_ref[0:CONV_HALO, :] = jnp.where(i > 0, prev, 0.0)
        ywin_ref[CONV_HALO:, :] = a_ref[...].astype(F32) * _sigmoid(g_ref[...].astype(F32))
        dwin_ref[0:tm, :] = d_ref[...]
        dwin_ref[tm:, :] = jnp.where(i < nt - 1, dn_ref[...], 0.0)
        _shifted_windows(ywin_ref, ysh_ref, sh_rows)
        _shifted_windows(dwin_ref, dsh_ref, sh_rows)

        def chunk(ci, carry):
            r0 = ci * CONV_ROWS
            rows = pl.ds(pl.multiple_of(r0, CONV_ROWS), CONV_ROWS)
            dcur = d_ref[rows, :]
            dyg = jnp.zeros((CONV_ROWS, cb), F32)
            for k in range(kw):
                dyg = dyg + w_ref[k:k + 1, :] * _window_rows(dwin_ref, dsh_ref, kw - 1 - k, r0, CONV_ROWS)
                accw_ref[k] += _fold_rows(dcur * _window_rows(ywin_ref, ysh_ref, off + k, r0, CONV_ROWS))
            av = a_ref[rows, :].astype(F32)
            sig = _sigmoid(g_ref[rows, :].astype(F32))
            da = dyg * sig
            dg = dyg * av * sig * (1.0 - sig)
            dp_ref[0, rows, :] = da.astype(BF)
            dp_ref[1, rows, :] = dg.astype(BF)
            acca_ref[...] += _fold_rows(da)
            accg_ref[...] += _fold_rows(dg)
            return carry
        lax.fori_loop(0, tm // CONV_ROWS, chunk, 0)

        @pl.when(i == nt - 1)
        def _():
            dw_ref[...] = jnp.sum(accw_ref[...], axis=1)
            dba_ref[...] = jnp.sum(acca_ref[...], axis=0, keepdims=True)
            dbg_ref[...] = jnp.sum(accg_ref[...], axis=0, keepdims=True)

    hpt = tm // CONV_HALO
    last_halo = t // CONV_HALO - 1
    tile = pl.BlockSpec((tm, cb), lambda j, i: (i, j))
    cur_g = pl.BlockSpec((tm, cb), lambda j, i: (i, ncb + j))
    nxt = pl.BlockSpec((CONV_HALO, cb), lambda j, i: (jnp.minimum((i + 1) * hpt, last_halo), j))
    prev_a = pl.BlockSpec((CONV_HALO, cb), lambda j, i: (jnp.maximum(i * hpt - 1, 0), j))
    prev_g = pl.BlockSpec((CONV_HALO, cb), lambda j, i: (jnp.maximum(i * hpt - 1, 0), ncb + j))
    vec = pl.BlockSpec((1, cb), lambda j, i: (0, j))
    dp, ddw, dba, dbg = _call(
        body, name=name, grid=(ncb, nt),
        in_specs=[tile, nxt, tile, cur_g, prev_a, prev_g, pl.BlockSpec((kw, cb), lambda j, i: (0, j))],
        out_specs=[pl.BlockSpec((2, tm, cb), lambda j, i: (0, i, j)), pl.BlockSpec((kpad, cb), lambda j, i: (0, j)),
                   vec, vec],
        out_shape=[_sds((2, t, cw), BF), _sds((kpad, cw), F32), _sds((1, cw), F32), _sds((1, cw), F32)],
        scratch=[pltpu.VMEM((tm + CONV_HALO, cb), F32), pltpu.VMEM((tm + CONV_HALO, cb), F32),
                 pltpu.VMEM((V7X_SUBLANES - 1, sh_rows, cb), F32), pltpu.VMEM((V7X_SUBLANES - 1, sh_rows, cb), F32),
                 pltpu.VMEM((kpad, V7X_SUBLANES, cb), F32), pltpu.VMEM((V7X_SUBLANES, cb), F32),
                 pltpu.VMEM((V7X_SUBLANES, cb), F32)],
        sem=("parallel", "arbitrary"))(dyc, dyc, p, p, p, p, dw_w)
    return dp, ddw[:kw], dba, dbg


def _adam_math(g, w, m, v):
    m2 = ADAM_B1 * m + (1.0 - ADAM_B1) * g
    v2 = ADAM_B2 * v + (1.0 - ADAM_B2) * (g * g)
    m_hat = m2 / (1.0 - ADAM_B1 ** ADAM_STEP)
    v_hat = v2 / (1.0 - ADAM_B2 ** ADAM_STEP)
    delta = -ADAM_LR * (m_hat / (jnp.sqrt(v_hat) + ADAM_EPS) + ADAM_WD * w)
    return delta, m2, v2


def _adamw(g_parts, w, m, v, name):
    r, c = w.shape
    tr = _tile(r, 256, V7X_SUBLANES)
    ng = len(g_parts)

    def body(*refs):
        g = refs[0][...].astype(F32)
        for s in refs[1:ng]:
            g = g + s[...].astype(F32)
        w_ref, m_ref, v_ref, go_ref, d_ref, mo_ref, vo_ref = refs[ng:]
        delta, m2, v2 = _adam_math(g, w_ref[...], m_ref[...], v_ref[...])
        go_ref[...] = g
        d_ref[...] = delta
        mo_ref[...] = m2
        vo_ref[...] = v2

    tile = pl.BlockSpec((tr, c), lambda i: (i, 0))
    in_specs = [pl.BlockSpec((None, tr, c), functools.partial(lambda s, i: (s, i, 0), s)) for _, s in g_parts]
    out = _sds((r, c), F32)
    return _call(body, name=name, grid=(r // tr,), in_specs=in_specs + [tile, tile, tile],
                 out_specs=[tile] * 4, out_shape=[out] * 4, sem=("parallel",))(*[a for a, _ in g_parts], w, m, v)


def _adamw_stacked(h, recv, chip, w_st, m_st, v_st, k, prev, name):
    kk, r, c = w_st.shape
    tr = _tile(r, 256, V7X_SUBLANES)
    if prev is None:
        prev = [lax.empty((kk, r, c), F32) for _ in range(4)]

    def body(chip_ref, h_ref, r0_ref, r1_ref, r2_ref, w_ref, m_ref, v_ref, pg, pd, pm, pv,
             go_ref, d_ref, mo_ref, vo_ref):
        g = (h_ref[...].astype(F32) + r0_ref[...].astype(F32)) + (r1_ref[...].astype(F32) + r2_ref[...].astype(F32))
        delta, m2, v2 = _adam_math(g, w_ref[...], m_ref[...], v_ref[...])
        go_ref[...] = g
        d_ref[...] = delta
        mo_ref[...] = m2
        vo_ref[...] = v2

    own = pl.BlockSpec((None, tr, c), lambda i, chip_ref: (chip_ref[0], i, 0))
    rcv = [pl.BlockSpec((None, tr, c), functools.partial(lambda s, i, chip_ref: (s, i, 0), s)) for s in range(3)]
    blk = pl.BlockSpec((None, tr, c), lambda i, chip_ref: (k, i, 0))
    out = _sds((kk, r, c), F32)
    return _call(body, name=name, grid=(r // tr,), in_specs=[own] + rcv + [blk, blk, blk] + [ANY] * 4,
                 out_specs=[blk] * 4, out_shape=[out] * 4, sem=("parallel",), prefetch=chip, on_path=False,
                 aliases={8: 0, 9: 1, 10: 2, 11: 3})(h, recv, recv, recv, w_st, m_st, v_st, *prev)


def _add_sibling(g4, land, core, name):
    n, _, r, c = g4.shape
    tr = _tile(r, 512, V7X_SUBLANES)

    def body(core_ref, a_ref, b_ref, o_ref):
        o_ref[...] = (a_ref[...].astype(F32) + b_ref[...].astype(F32)).astype(BF)

    return _call(body, name=name, grid=(n, r // tr),
                 in_specs=[pl.BlockSpec((None, None, tr, c), lambda p, i, core_ref: (p, core_ref[0], i, 0)),
                           pl.BlockSpec((None, None, tr, c), lambda p, i, core_ref: (p, 0, i, 0))],
                 out_specs=pl.BlockSpec((None, tr, c), lambda p, i, core_ref: (p, i, 0)),
                 out_shape=_sds((n, r, c), BF), sem=("parallel", "parallel"), prefetch=core)(g4, land)


def _cast_to_slot(w, lead, me, name, after=()):
    r, c = w.shape[-2:]
    nl = len(lead)
    tr = _tile(r, 512, 2 * V7X_SUBLANES)

    def body(me_ref, w_ref, o_ref):
        o_ref[...] = w_ref[...].astype(BF)

    return _call(body, name=name, grid=(r // tr,),
                 in_specs=[pl.BlockSpec((None,) * nl + (tr, c), lambda i, me_ref: tuple(lead) + (i, 0))],
                 out_specs=pl.BlockSpec((None, None, tr, c), lambda i, me_ref: (0, me_ref[0], i, 0)),
                 out_shape=_sds((1, N_DEV, r, c), BF), sem=("parallel",), prefetch=me, after=after)(w)


def _ada_fwd(c_pad, ada_w, ada_b, name):
    nl, d, cl = ada_w.shape
    rows = c_pad.shape[0]
    tn = _tile(cl, 256, V7X_LANES)

    def body(c_ref, w_ref, b_ref, o_ref):
        cv = c_ref[...]
        cond = (cv * _sigmoid(cv)).astype(BF)
        o_ref[...] = jnp.dot(cond, w_ref[...].astype(BF), preferred_element_type=F32) + b_ref[...]

    return _call(body, name=name, grid=(nl, cl // tn),
                 in_specs=[pl.BlockSpec((rows, d), lambda l, j: (0, 0)),
                           pl.BlockSpec((None, d, tn), lambda l, j: (l, 0, j)),
                           pl.BlockSpec((None, 1, tn), lambda l, j: (l, 0, j))],
                 out_specs=pl.BlockSpec((None, rows, tn), lambda l, j: (l, 0, j)),
                 out_shape=_sds((nl, rows, cl), F32), sem=("parallel", "parallel"))(c_pad, ada_w, ada_b)


def _ada_bwd(c_pad, dmod, w, m, v, name):
    nl, d, cl = w.shape
    rows = c_pad.shape[0]
    tn = _tile(cl, 256, V7X_LANES)

    def body(c_ref, dm_ref, w_ref, m_ref, v_ref, go_ref, d_ref, mo_ref, vo_ref):
        cv = c_ref[...]
        cond = (cv * _sigmoid(cv)).astype(BF)
        g = lax.dot_general(cond, dm_ref[...].astype(BF), (((0,), (0,)), ((), ())), preferred_element_type=F32)
        delta, m2, v2 = _adam_math(g, w_ref[...], m_ref[...], v_ref[...])
        go_ref[...] = g
        d_ref[...] = delta
        mo_ref[...] = m2
        vo_ref[...] = v2

    tile = pl.BlockSpec((None, d, tn), lambda l, j: (l, 0, j))
    out = _sds((nl, d, cl), F32)
    return _call(body, name=name, grid=(nl, cl // tn),
                 in_specs=[pl.BlockSpec((rows, d), lambda l, j: (0, 0)),
                           pl.BlockSpec((None, rows, tn), lambda l, j: (l, 0, j)), tile, tile, tile],
                 out_specs=[tile] * 4, out_shape=[out] * 4, sem=("parallel", "parallel"))(c_pad, dmod, w, m, v)


def _sum_devices(parts, name):
    n, r, c = parts.shape
    tr = _tile(r, 512, V7X_SUBLANES)

    def body(p_ref, o_ref):
        acc = p_ref[0]
        for k in range(1, n):
            acc = acc + p_ref[k]
        o_ref[...] = acc

    return _call(body, name=name, grid=(r // tr,), in_specs=[pl.BlockSpec((n, tr, c), lambda i: (0, i, 0))],
                 out_specs=pl.BlockSpec((tr, c), lambda i: (i, 0)), out_shape=_sds((r, c), F32),
                 sem=("parallel",))(parts)


def _mesh_pos():
    return lax.axis_index("x"), lax.axis_index("y"), lax.axis_index("c")


def _other_chips(x, y):
    return [(1 - x, y), (x, 1 - y), (1 - x, 1 - y)]


def _all_gather(arrs, name):
    n = len(arrs)

    def body(*refs):
        ins, outs = refs[:n], refs[n:2 * n]
        send_sems, recv_sems, local_sems = refs[2 * n:]
        x, y, c = _mesh_pos()
        me, sibling = (x, y, c), (x, y, 1 - c)
        chips = _other_chips(x, y)

        def slot(a, pos):
            px, py, pc = pos
            return outs[a].at[:, pl.ds(4 * px + 2 * py + pc, 1)]

        def copy(a, k, block, to, src=None):
            return pltpu.make_async_remote_copy(
                src_ref=slot(a, block) if src is None else src, dst_ref=slot(a, block),
                send_sem=send_sems.at[a, k], recv_sem=recv_sems.at[a, k], device_id=to, device_id_type=MESH)

        mine = [pltpu.make_async_copy(ins[a], slot(a, me), local_sems.at[a]) for a in range(n)]
        for cp in mine:
            cp.start()
        first = []
        for a in range(n):
            first.append(copy(a, 0, me, sibling, src=ins[a]))
            first += [copy(a, 1 + j, me, (*chip, c), src=ins[a]) for j, chip in enumerate(chips)]
        for cp in first:
            cp.start()
        passed = []
        for a in range(n):
            for j, chip in enumerate(chips):
                copy(a, 1 + j, (*chip, c), me).wait_recv()
                fwd = copy(a, 4 + j, (*chip, c), sibling)
                fwd.start()
                passed.append(fwd)
        for a in range(n):
            copy(a, 0, sibling, me).wait_recv()
            for j, chip in enumerate(chips):
                copy(a, 4 + j, (*chip, 1 - c), me).wait_recv()
        for cp in first + passed:
            cp.wait_send()
        for cp in mine:
            cp.wait()

    out_shape = [_sds((a.shape[0], N_DEV) + a.shape[2:], a.dtype) for a in arrs]
    return pl.pallas_call(
        body, out_shape=out_shape, in_specs=[ANY] * n, out_specs=[ANY] * n, name=name,
        scratch_shapes=[pltpu.SemaphoreType.DMA((n, N_DEV - 1)), pltpu.SemaphoreType.DMA((n, N_DEV - 1)),
                        pltpu.SemaphoreType.DMA((n,))])(*arrs)


HBM = pl.BlockSpec(memory_space=pltpu.HBM)
SEM = pl.BlockSpec(memory_space=pltpu.SEMAPHORE)


def _hbm(v):
    return pltpu.with_memory_space_constraint(v, pltpu.HBM)


def _comm_call(body, name, bufs, sems_in, sems_out):
    after = [] if not sems_in or _Seq.last is None or any(_Seq.last is b for b in bufs) else [_Seq.last]
    nb, ni, na, no = len(bufs), len(sems_in), len(after), len(sems_out)

    def wrapped(*refs):
        body(refs[:nb], refs[nb:nb + ni], refs[nb + ni + na:nb + ni + na + no])
        if no:
            refs[-1][...] = jnp.zeros_like(refs[-1])

    out_shape = [pltpu.SemaphoreType.DMA(s) for s in sems_out] + [pltpu.HBM(b.shape, b.dtype) for b in bufs]
    out_specs = [SEM] * no + [HBM] * nb
    if no:
        out_shape.append(_sds((V7X_SUBLANES, V7X_LANES), F32))
        out_specs.append(pl.BlockSpec(memory_space=pltpu.VMEM))
    res = pl.pallas_call(
        wrapped, name=name, out_shape=out_shape, in_specs=[HBM] * nb + [SEM] * ni + [ANY] * na, out_specs=out_specs,
        input_output_aliases={i: no + i for i in range(nb)},
        compiler_params=pltpu.CompilerParams(has_side_effects=pltpu.SideEffectType.DATAFLOW_SIDE_EFFECTING),
    )(*bufs, *sems_in, *after)
    out_bufs = list(res[no:no + nb])
    if no:
        _Seq.tokens.append(res[-1])
    _Seq.last = out_bufs[0]
    return list(res[:no]), out_bufs


def _remote(src, dst, send_sem, recv_sem, to):
    return pltpu.make_async_remote_copy(src_ref=src, dst_ref=dst, send_sem=send_sem, recv_sem=recv_sem,
                                        device_id=to, device_id_type=MESH)


def _slot(ref, pos):
    px, py, pc = pos
    return ref.at[:, pl.ds(4 * px + 2 * py + pc, 1)]


def _ag_start(bufs, name):
    n = len(bufs)

    def body(b, _, sems):
        send_sib, recv_sib, send_ici, recv_ici = sems
        x, y, c = _mesh_pos()
        for a in range(n):
            mine = _slot(b[a], (x, y, c))
            _remote(mine, mine, send_sib.at[a], recv_sib.at[a], (x, y, 1 - c)).start()
            for j, (px, py) in enumerate(_other_chips(x, y)):
                _remote(mine, mine, send_ici.at[3 * a + j], recv_ici.at[3 * a + j], (px, py, c)).start()

    sems, bufs = _comm_call(body, name, [_hbm(b) for b in bufs], [], [(n,), (n,), (3 * n,), (3 * n,)])
    return dict(bufs=bufs, send_sib=sems[0], recv_sib=sems[1], send_ici=sems[2], recv_ici=sems[3])


def _ag_mid(st, name):
    n = len(st["bufs"])

    def body(b, sems_in, sems):
        (recv_ici,) = sems_in
        send_fwd, recv_fwd = sems
        x, y, c = _mesh_pos()
        for a in range(n):
            for j, (px, py) in enumerate(_other_chips(x, y)):
                blk = _slot(b[a], (px, py, c))
                _remote(blk, blk, send_fwd.at[3 * a + j], recv_ici.at[3 * a + j], (x, y, 1 - c)).wait_recv()
                _remote(blk, blk, send_fwd.at[3 * a + j], recv_fwd.at[3 * a + j], (x, y, 1 - c)).start()

    sems, bufs = _comm_call(body, name, st["bufs"], [st["recv_ici"]], [(3 * n,), (3 * n,)])
    return dict(st, bufs=bufs, send_fwd=sems[0], recv_fwd=sems[1])


def _ag_end(st, name):
    n = len(st["bufs"])

    def body(b, sems_in, _):
        send_sib, recv_sib, send_ici, send_fwd, recv_fwd = sems_in
        x, y, c = _mesh_pos()
        sibling = (x, y, 1 - c)
        for a in range(n):
            mine, sib_blk = _slot(b[a], (x, y, c)), _slot(b[a], sibling)
            _remote(mine, mine, send_sib.at[a], recv_sib.at[a], sibling).wait_send()
            _remote(sib_blk, sib_blk, send_sib.at[a], recv_sib.at[a], sibling).wait_recv()
            for j, (px, py) in enumerate(_other_chips(x, y)):
                blk, sib_got = _slot(b[a], (px, py, c)), _slot(b[a], (px, py, 1 - c))
                _remote(mine, mine, send_ici.at[3 * a + j], recv_sib.at[a], (px, py, c)).wait_send()
                _remote(blk, blk, send_fwd.at[3 * a + j], recv_fwd.at[3 * a + j], sibling).wait_send()
                _remote(sib_got, sib_got, send_fwd.at[3 * a + j], recv_fwd.at[3 * a + j], sibling).wait_recv()

    _, bufs = _comm_call(body, name, st["bufs"],
                         [st[k] for k in ("send_sib", "recv_sib", "send_ici", "send_fwd", "recv_fwd")], [])
    return bufs


def _rs_start(g4s, name):
    n = len(g4s)
    lands = [lax.empty((N_CHIP, 1) + g.shape[2:], g.dtype) for g in g4s]

    def body(b, _, sems):
        send, recv = sems
        x, y, c = _mesh_pos()
        for a in range(n):
            _remote(b[a].at[:, pl.ds(1 - c, 1)], b[n + a], send.at[a], recv.at[a], (x, y, 1 - c)).start()

    sems, bufs = _comm_call(body, name, [_hbm(v) for v in list(g4s) + lands], [], [(n,), (n,)])
    return dict(bufs=bufs, send=sems[0], recv=sems[1])


def _rs_mid(st, name):
    n = len(st["bufs"]) // 2

    def body(b, sems_in, _):
        send, recv = sems_in
        x, y, c = _mesh_pos()
        for a in range(n):
            cp = _remote(b[a].at[:, pl.ds(1 - c, 1)], b[n + a], send.at[a], recv.at[a], (x, y, 1 - c))
            cp.wait_send()
            cp.wait_recv()

    _, bufs = _comm_call(body, name, st["bufs"], [st["send"], st["recv"]], [])
    return bufs[:n], bufs[n:]


def _rs_start2(sums, name):
    n = len(sums)
    lands = [lax.empty((N_CHIP - 1,) + s.shape[1:], s.dtype) for s in sums]

    def body(b, _, sems):
        send, recv = sems
        x, y, c = _mesh_pos()
        for a in range(n):
            for j, (px, py) in enumerate(_other_chips(x, y)):
                _remote(b[a].at[pl.ds(2 * px + py, 1)], b[n + a].at[pl.ds(j, 1)], send.at[3 * a + j], recv.at[3 * a + j],
                        (px, py, c)).start()

    sems, bufs = _comm_call(body, name, [_hbm(v) for v in list(sums) + lands], [], [(3 * n,), (3 * n,)])
    return dict(bufs=bufs, send=sems[0], recv=sems[1])


def _rs_end(st, name):
    n = len(st["bufs"]) // 2

    def body(b, sems_in, _):
        send, recv = sems_in
        x, y, c = _mesh_pos()
        for a in range(n):
            for j, (px, py) in enumerate(_other_chips(x, y)):
                cp = _remote(b[a].at[pl.ds(2 * px + py, 1)], b[n + a].at[pl.ds(j, 1)], send.at[3 * a + j], recv.at[3 * a + j],
                             (px, py, c))
                cp.wait_send()
                cp.wait_recv()

    _, bufs = _comm_call(body, name, st["bufs"], [st["send"], st["recv"]], [])
    return bufs[:n], bufs[n:]


def _pack(parts, rows_align=V7X_SUBLANES):
    flat, total = [], 0
    for p in parts:
        v = p.reshape(-1).astype(F32)
        pad = -v.shape[0] % PACK_ALIGN
        flat.append(jnp.pad(v, (0, pad)) if pad else v)
        total += v.shape[0] + pad
    tail = -total % (rows_align * V7X_LANES)
    if tail:
        flat.append(jnp.zeros((tail,), F32))
    return jnp.concatenate(flat).reshape(-1, V7X_LANES)


def _unpack(buf, shapes):
    lead = buf.shape[:-2]
    flat = buf.reshape(lead + (-1,))
    out, pos = [], 0
    for s in shapes:
        size = 1
        for d in s:
            size *= d
        out.append(flat[..., pos:pos + size].reshape(lead + tuple(s)))
        pos += size + (-size % PACK_ALIGN)
    return out


def kernel(x, c, ada_w, ada_b, norm_g, ffn_w_in, ffn_w_out, gm_w_in, gm_ln_g, gm_ln_b, gm_ws, gm_bs, gm_w_out, cv_w_in, cv_b_in, cv_dw_w, cv_dw_b, cv_ln_g, cv_ln_b, cv_w_out, cv_b_out, final_g, loss_target, m_ada_w, m_ada_b, m_norm_g, m_ffn_w_in, m_ffn_w_out, m_gm_w_in, m_gm_ln_g, m_gm_ln_b, m_gm_ws, m_gm_bs, m_gm_w_out, m_cv_w_in, m_cv_b_in, m_cv_dw_w, m_cv_dw_b, m_cv_ln_g, m_cv_ln_b, m_cv_w_out, m_cv_b_out, m_final_g, v_ada_w, v_ada_b, v_norm_g, v_ffn_w_in, v_ffn_w_out, v_gm_w_in, v_gm_ln_g, v_gm_ln_b, v_gm_ws, v_gm_bs, v_gm_w_out, v_cv_w_in, v_cv_b_in, v_cv_dw_w, v_cv_dw_b, v_cv_ln_g, v_cv_ln_b, v_cv_w_out, v_cv_b_out, v_final_g):
    t, d = x.shape[1], x.shape[2]
    depth = ada_w.shape[0]
    assert depth == 2 and ffn_w_in.shape[:2] == (2, 2) and gm_w_in.shape[0] == 1 and cv_w_in.shape[0] == 1
    dl = d // N_DEV
    bn = ffn_w_in.shape[3]
    fl = ffn_w_out.shape[2]
    f = fl * N_DEV
    el = gm_w_in.shape[2]
    e = el * N_DEV // 2
    hn, l = gm_ws.shape[1], gm_ws.shape[2]
    kw = cv_dw_w.shape[1]
    cl = ada_w.shape[2]
    me = 4 * lax.axis_index("x") + 2 * lax.axis_index("y") + lax.axis_index("c")
    me1 = me.astype(jnp.int32).reshape(1)
    chip1 = (2 * lax.axis_index("x") + lax.axis_index("y")).astype(jnp.int32).reshape(1)
    core1 = lax.axis_index("c").astype(jnp.int32).reshape(1)
    _Seq.last, _Seq.tokens = None, []

    xs = x[0]
    tgt = loss_target[0]

    ag_groups = [("win00", [(ffn_w_in, (0, 0))]), ("wout00", [(ffn_w_out, (0, 0))]),
                 ("gm", [(gm_w_in, (0,)), (gm_w_out, (0,))]),
                 ("win01", [(ffn_w_in, (0, 1))]), ("wout01", [(ffn_w_out, (0, 1))]),
                 ("win10", [(ffn_w_in, (1, 0))]), ("wout10", [(ffn_w_out, (1, 0))]),
                 ("cv", [(cv_w_in, (0,)), (cv_w_out, (0,))]),
                 ("win11", [(ffn_w_in, (1, 1))]), ("wout11", [(ffn_w_out, (1, 1))])]
    ag_flight = {}

    ag_slots = {}

    def ag_cast(gi, after=()):
        gname, members = ag_groups[gi]
        ag_slots[gi] = [_cast_to_slot(w, lead, me1, name=f"cast_{gname}_{k}", after=after)
                        for k, (w, lead) in enumerate(members)]

    def ag_start(gi):
        ag_flight[gi] = _ag_start(ag_slots.pop(gi), name=f"ag_start_{ag_groups[gi][0]}")

    def ag_forward(gi):
        if gi in ag_flight and "send_fwd" not in ag_flight[gi]:
            ag_flight[gi] = _ag_mid(ag_flight[gi], name=f"ag_mid_{ag_groups[gi][0]}")

    def ag_take(gi):
        ag_forward(gi)
        bufs = _ag_end(ag_flight.pop(gi), name=f"ag_end_{ag_groups[gi][0]}")
        if gi > 0:
            ag_forward(gi + 1)
        if gi + AG_AHEAD < len(ag_groups):
            ag_start(gi + AG_AHEAD)
        return [b[0] for b in bufs]

    small_in = [c, norm_g, cv_b_in, cv_dw_w, cv_dw_b, cv_ln_g, cv_ln_b, cv_b_out]
    pack1 = _pack(small_in)
    (pack1_all,) = _all_gather([pack1[None, None]], name="ag_small")
    parts = _unpack(pack1_all[0], [s.shape for s in small_in])
    c_all = parts[0].reshape(N_DEV, d)
    ng_full = jnp.moveaxis(parts[1], 0, 2).reshape(depth, 3, d)
    cvb_in_full = parts[2].reshape(1, 2 * e)
    dww_full = jnp.moveaxis(parts[3][:, 0], 0, 1).reshape(kw, e)
    dwb_full, cln_g_full, cln_b_full, cvb_out_full = [p.reshape(1, d) for p in parts[4:8]]

    c_pad = jnp.pad(c_all, ((0, 16 - N_DEV), (0, 0)))
    ada_b_loc = lax.dynamic_slice_in_dim(ada_b, me * cl, cl, axis=1).reshape(depth, 1, cl)
    mod_part = _ada_fwd(c_pad, ada_w, ada_b_loc, name="ada_fwd")[:, :N_DEV]
    (mod_all,) = _all_gather([_pack([mod_part])[None, None]], name="ag_mod")
    mod_all = _unpack(mod_all[0], [mod_part.shape])[0]
    mod_mine = lax.dynamic_index_in_dim(mod_all, me, axis=2, keepdims=False)
    mod = jnp.moveaxis(mod_mine, 0, 1).reshape(depth, 3, 3, 1, d)

    for gi in range(len(ag_groups)):
        ag_cast(gi, after=[mod_all])
        if gi < AG_AHEAD:
            ag_start(gi)

    ws = gm_ws[0]
    bsb = jnp.broadcast_to(gm_bs[0][:, :, None], (hn, l, e // hn))
    gm_g, gm_b = gm_ln_g, gm_ln_b

    saved = []
    xcur = xs
    next_group = 0
    for i in range(depth):
        for s in range(3):
            shift, scale, gate = mod[i, s, 0], mod[i, s, 1], mod[i, s, 2]
            g_norm = ng_full[i, s][None]
            tag = f"l{i}s{s}"
            h = _norm_mod(xcur, g_norm, scale, shift, name=f"norm_mod_{tag}")
            if s != 1:
                (w_in_blk,) = ag_take(next_group)
                fg, fu, act = _ffn_in(h, w_in_blk, 0, name=f"ffn_in_{tag}")
                w_out3 = ag_take(next_group + 1)[0].reshape(1, f, d)
                next_group += 2
                xnext, yv = _out_proj(act, w_out3, 0, xcur, gate, None, 0.5, name=f"ffn_out_{tag}")
                saved.append(dict(x=xcur, h=h, fg=fg, fu=fu, act=act, y=yv, w_in=w_in_blk, w_out=w_out3))
            elif i % 2 == 0:
                gm_in_blk, gm_out = ag_take(next_group)
                gm_out3 = gm_out.reshape(1, e, d)
                next_group += 1
                pre = _in_proj(h, gm_in_blk, None, name=f"gm_in_{tag}")
                uu, vn = _gm_act(pre, gm_g, gm_b, name=f"gm_act_{tag}")
                sg = _sgu_fwd(uu, vn, ws, bsb, name=f"sgu_fwd_{tag}")
                xnext, yv = _out_proj(sg, gm_out3, 0, xcur, gate, None, 1.0, name=f"gm_out_{tag}")
                saved.append(dict(x=xcur, h=h, pre=pre, u=uu, vn=vn, sg=sg, y=yv, w_in=gm_in_blk, w_out=gm_out3))
            else:
                cv_in_blk, cv_out = ag_take(next_group)
                cv_out3 = cv_out.reshape(1, e, d)
                next_group += 1
                p = _in_proj(h, cv_in_blk, cvb_in_full, name=f"cv_in_{tag}")
                yc = _dwconv_fwd(p, dww_full, dwb_full, name=f"dwconv_fwd_{tag}")
                ys = _cv_act(yc, cln_g_full, cln_b_full, name=f"cv_act_{tag}")
                xnext, yv = _out_proj(ys, cv_out3, 0, xcur, gate, cvb_out_full, 1.0, name=f"cv_out_{tag}")
                saved.append(dict(x=xcur, h=h, p=p, yc=yc, ys=ys, y=yv, w_in=cv_in_blk, w_out=cv_out3))
            xcur = xnext

    def bwd_head(i, s):
        return saved[3 * i + s]["y"], mod[i, s, 2], 0.5 if s != 1 else 1.0, s == 1 and i % 2 == 1

    sq, dx, d_final_g, dy, dgate = _final_loss(xcur, tgt, final_g[None], *bwd_head(depth - 1, 2)[:3], name="final_loss")
    dbout = None
    loss = lax.psum(0.5 / d * jnp.sum(sq), ("x", "y", "c"))

    dmod = [[[None] * 3 for _ in range(3)] for _ in range(depth)]
    d_norm_g = [[None] * 3 for _ in range(depth)]
    small = {}

    stacked = {
        "ffn_w_in": [a.reshape(4, d, bn) for a in (ffn_w_in, m_ffn_w_in, v_ffn_w_in)],
        "ffn_w_out": [a.reshape(4, fl, d) for a in (ffn_w_out, m_ffn_w_out, v_ffn_w_out)],
        "gm_w_in": [gm_w_in, m_gm_w_in, v_gm_w_in], "gm_w_out": [gm_w_out, m_gm_w_out, v_gm_w_out],
        "cv_w_in": [cv_w_in, m_cv_w_in, v_cv_w_in], "cv_w_out": [cv_w_out, m_cv_w_out, v_cv_w_out],
    }
    res_big = {}

    def rs_sibling(g4s, tag):
        return _rs_start(g4s, name=f"rs_start_{tag}"), tag

    def rs_chips(flight):
        st, tag = flight
        g4s, lands = _rs_mid(st, name=f"rs_mid_{tag}")
        sums = [_add_sibling(g4, land, core1, name=f"rs_add_{tag}_{k}") for k, (g4, land) in enumerate(zip(g4s, lands))]
        return _rs_start2(sums, name=f"rs_start2_{tag}"), tag

    def rs_finish(flight, targets):
        st, tag = flight
        sums, recvs = _rs_end(st, name=f"rs_end_{tag}")
        for (pname, k), hsum, recv in zip(targets, sums, recvs):
            w_st, m_st, v_st = stacked[pname]
            res_big[pname] = _adamw_stacked(hsum, recv, chip1, w_st, m_st, v_st, k, res_big.get(pname),
                                            name=f"adamw_{pname}_{k}")

    pending = []
    last_sibling = None
    for i in reversed(range(depth)):
        for s in reversed(range(3)):
            sv = saved[3 * i + s]
            shift, scale, gate = mod[i, s, 0], mod[i, s, 1], mod[i, s, 2]
            g_norm = ng_full[i, s][None]
            tag = f"l{i}s{s}"
            last = i == 0 and s == 0
            if s != 1:
                widx = 2 * i + s // 2
                dgu, act = _ffn_da(dy, sv["w_out"], 0, sv["fg"], sv["fu"], name=f"ffn_da_{tag}"), sv["act"]
                if last:
                    g_in = _mm_tn(sv["h"], dgu, d, bn, True, name=f"ffn_dwin_{tag}").reshape(N_CHIP, 2, d, bn)
                    sib_in = rs_sibling([g_in], f"{tag}_in")
                    g_out = _mm_tn(act, dy[None], bn, d, False, name=f"ffn_dwout_{tag}").reshape(N_CHIP, 2, fl, d)
                    new_flights = [(rs_chips(sib_in), [("ffn_w_in", widx)])]
                    sib, targets = rs_sibling([g_out], f"{tag}_out"), [("ffn_w_out", widx)]
                else:
                    g_out = _mm_tn(act, dy[None], bn, d, False, name=f"ffn_dwout_{tag}").reshape(N_CHIP, 2, fl, d)
                    sib_out = rs_sibling([g_out], f"{tag}_out")
                    g_in = _mm_tn(sv["h"], dgu, d, bn, True, name=f"ffn_dwin_{tag}").reshape(N_CHIP, 2, d, bn)
                    new_flights = [(rs_chips(sib_out), [("ffn_w_out", widx)])]
                    sib, targets = rs_sibling([g_in], f"{tag}_in"), [("ffn_w_in", widx)]
                z3, w_blk = dgu, sv["w_in"]
            elif i % 2 == 0:
                ds = _mm_nt(dy, sv["w_out"], 0, name=f"gm_ds_{tag}")
                g_out = _mm_tn(sv["sg"], dy[None], _tile(e, 1024, V7X_LANES), d, False,
                               name=f"gm_dwout_{tag}").reshape(N_CHIP, 2, dl, d)
                du, dvn, dws, dbs = _sgu_bwd(ds, sv["u"], sv["vn"], ws, bsb, name=f"sgu_bwd_{tag}")
                dpre, dlng, dlnb = _gm_act_bwd(sv["pre"], du, dvn, gm_g, name=f"gm_act_bwd_{tag}")
                small["gm_ln_g"], small["gm_ln_b"] = dlng, dlnb
                small["gm_ws"], small["gm_bs"] = dws, dbs[:, :, 0]
                g_in = _mm_tn(sv["h"], dpre[None], d, el, True, name=f"gm_dwin_{tag}").reshape(N_CHIP, 2, d, el)
                targets, new_flights = [("gm_w_in", 0), ("gm_w_out", 0)], []
                sib = rs_sibling([g_in, g_out], tag)
                z3, w_blk = dpre[None], sv["w_in"]
            else:
                dys = _mm_nt(dy, sv["w_out"], 0, name=f"cv_dys_{tag}")
                g_out = _mm_tn(sv["ys"], dy[None], _tile(e, 1024, V7X_LANES), d, False,
                               name=f"cv_dwout_{tag}").reshape(N_CHIP, 2, dl, d)
                dyc, dlng, dlnb, ddwb = _cv_act_bwd(dys, sv["yc"], cln_g_full, cln_b_full, name=f"cv_act_bwd_{tag}")
                dp, ddww, dba, dbg = _dwconv_bwd(dyc, sv["p"], dww_full, name=f"dwconv_bwd_{tag}")
                small["cv_b_out"], small["cv_ln_g"], small["cv_ln_b"], small["cv_dw_b"] = dbout, dlng, dlnb, ddwb
                small["cv_dw_w"] = ddww
                small["cv_b_in"] = jnp.concatenate([dba, dbg], axis=1)
                g_in = _mm_tn(sv["h"], dp, d, el, True, name=f"cv_dwin_{tag}").reshape(N_CHIP, 2, d, el)
                targets, new_flights = [("cv_w_in", 0), ("cv_w_out", 0)], []
                sib = rs_sibling([g_in, g_out], tag)
                z3, w_blk = dp, sv["w_in"]
            nxt = None if last else bwd_head(*((i, s - 1) if s > 0 else (i - 1, 2)))
            res = _dh_norm_bwd(z3, w_blk, sv["x"], dx, g_norm, scale, nxt, name=f"dh_norm_bwd_{tag}")
            if last:
                (dx, dscale, dshift, dgn), last_sibling = res, (sib, targets)
            else:
                new_flights.append((rs_chips(sib), targets))
                dx, dy_next, dscale, dshift, dgn, dgate_next = res[:6]
                dbout_next = res[6] if nxt[3] else None
            dmod[i][s] = [dshift, dscale, dgate]
            d_norm_g[i][s] = dgn
            if not last:
                dy, dgate, dbout = dy_next, dgate_next, dbout_next
            for flight in pending:
                rs_finish(*flight)
            pending = new_flights
    grad_x = dx[None]

    dmod_mine = jnp.concatenate([v for per_l in dmod for per_s in per_l for v in per_s], axis=1)
    dng_mine = jnp.concatenate([v for per_l in d_norm_g for v in per_l], axis=1)
    small_out = [dmod_mine, dng_mine, small["gm_ln_g"], small["gm_ln_b"], small["gm_ws"], small["gm_bs"],
                 small["cv_b_in"], small["cv_dw_w"], small["cv_dw_b"], small["cv_ln_g"], small["cv_ln_b"],
                 small["cv_b_out"], d_final_g]
    shapes2 = [s.shape for s in small_out]
    (pack2_all,) = _all_gather([_pack(small_out, rows_align=256)[None, None]], name="ag_small_grads")
    _Seq.last = pack2_all
    pending.append((rs_chips(last_sibling[0]), last_sibling[1]))
    summed = _unpack(_sum_devices(pack2_all[0], name="sum_small_grads"), shapes2)
    dmod_all = _unpack(pack2_all[0], shapes2)[0].reshape(N_DEV, depth, 9 * d)

    def my_cols(full, width):
        return lax.dynamic_slice_in_dim(full, me * width, width, axis=full.ndim - 1)

    g_ada_b = summed[0].reshape(depth, 9 * d)
    g_norm_g = my_cols(summed[1].reshape(depth, 3, d), dl)
    g_small = {
        "ada_b": g_ada_b, "norm_g": g_norm_g,
        "gm_ln_g": summed[2], "gm_ln_b": summed[3], "gm_ws": summed[4][None], "gm_bs": summed[5][None],
        "cv_b_in": my_cols(summed[6], el), "cv_dw_w": my_cols(summed[7], dl)[None],
        "cv_dw_b": my_cols(summed[8], dl), "cv_ln_g": my_cols(summed[9], dl), "cv_ln_b": my_cols(summed[10], dl),
        "cv_b_out": my_cols(summed[11], dl), "final_g": summed[12].reshape(d),
    }

    dm_loc = jnp.moveaxis(my_cols(dmod_all, cl), 0, 1)
    dm_loc = jnp.pad(dm_loc, ((0, 0), (0, 16 - N_DEV), (0, 0)))
    res_ada_w = _ada_bwd(c_pad, dm_loc, ada_w, m_ada_w, v_ada_w, name="ada_bwd_adamw")

    def flat2(a):
        return a.reshape(-1, a.shape[-1])

    small_params = {
        "ada_b": (ada_b, m_ada_b, v_ada_b), "norm_g": (norm_g, m_norm_g, v_norm_g),
        "gm_ln_g": (gm_ln_g, m_gm_ln_g, v_gm_ln_g), "gm_ln_b": (gm_ln_b, m_gm_ln_b, v_gm_ln_b),
        "gm_ws": (gm_ws, m_gm_ws, v_gm_ws), "gm_bs": (gm_bs, m_gm_bs, v_gm_bs),
        "cv_b_in": (cv_b_in, m_cv_b_in, v_cv_b_in), "cv_dw_w": (cv_dw_w, m_cv_dw_w, v_cv_dw_w),
        "cv_dw_b": (cv_dw_b, m_cv_dw_b, v_cv_dw_b), "cv_ln_g": (cv_ln_g, m_cv_ln_g, v_cv_ln_g),
        "cv_ln_b": (cv_ln_b, m_cv_ln_b, v_cv_ln_b), "cv_b_out": (cv_b_out, m_cv_b_out, v_cv_b_out),
        "final_g": (final_g, m_final_g, v_final_g),
    }
    res_small = {}
    for key, (w, m, v) in small_params.items():
        g2 = flat2(g_small[key].reshape(w.shape)) if w.ndim > 1 else g_small[key].reshape(1, -1)
        w2, m2, v2 = [flat2(a) if a.ndim > 1 else a.reshape(1, -1) for a in (w, m, v)]
        res_small[key] = [o.reshape(w.shape) for o in _adamw([(g2[None], 0)], w2, m2, v2, name=f"adamw_{key}")]

    for flight in pending:
        rs_finish(*flight)

    def big(name, k):
        if name == "ada_w":
            return res_ada_w[k]
        return res_big[name][k].reshape(stacked_shape[name])

    stacked_shape = {"ffn_w_in": ffn_w_in.shape, "ffn_w_out": ffn_w_out.shape, "gm_w_in": gm_w_in.shape,
                     "gm_w_out": gm_w_out.shape, "cv_w_in": cv_w_in.shape, "cv_w_out": cv_w_out.shape}

    order = ["ada_w", "ada_b", "norm_g", "ffn_w_in", "ffn_w_out", "gm_w_in", "gm_ln_g", "gm_ln_b", "gm_ws", "gm_bs",
             "gm_w_out", "cv_w_in", "cv_b_in", "cv_dw_w", "cv_dw_b", "cv_ln_g", "cv_ln_b", "cv_w_out", "cv_b_out",
             "final_g"]
    outs = [loss, grad_x]
    for k in range(4):
        for name in order:
            outs.append(res_small[name][k] if name in res_small else big(name, k))
    return tuple(outs)
```

```python
import functools

import jax
import jax.numpy as jnp
from jax import lax
from jax.experimental import pallas as pl
from jax.experimental.pallas import tpu as pltpu

F32 = jnp.float32
BF = jnp.bfloat16
MESH = pl.DeviceIdType.MESH

N_DEV = 8
N_CHIP = 4
NORM_EPS = 1e-6
ADAM_LR = 0.001
ADAM_B1 = 0.9
ADAM_B2 = 0.999
ADAM_EPS = 1e-08
ADAM_WD = 0.01
ADAM_STEP = 10

V7X_SUBLANES = 8
V7X_LANES = 128
PACK_ALIGN = V7X_SUBLANES * V7X_LANES
V7X_VMEM_LIMIT = 56 * 1024 * 1024


def _tile(n, pref, align):
    if n <= pref:
        return n
    t = pref - pref % align
    while t >= align:
        if n % t == 0:
            return t
        t -= align
    return n


ANY = pl.BlockSpec(memory_space=pl.ANY)


class _Seq:
    last = None
    tokens = []


def _call(body, *, name, grid, in_specs, out_specs, out_shape, scratch=(), sem=None, prefetch=None, aliases=None,
          after=(), on_path=True):
    def run(*args):
        if on_path:
            tokens, _Seq.tokens = _Seq.tokens + list(after), []
        else:
            tokens = list(after)
        lead = 0 if prefetch is None else 1
        n_in, n_tok = lead + len(args), len(tokens)

        def wrapped(*refs):
            body(*refs[:n_in], *refs[n_in + n_tok:])

        specs = list(in_specs) + [ANY] * n_tok
        params = pltpu.CompilerParams(dimension_semantics=sem, vmem_limit_bytes=V7X_VMEM_LIMIT)
        if prefetch is None:
            res = pl.pallas_call(wrapped, out_shape=out_shape, grid=grid, in_specs=specs, out_specs=out_specs,
                                 scratch_shapes=scratch, name=name, compiler_params=params,
                                 input_output_aliases=aliases or {})(*args, *tokens)
        else:
            grid_spec = pltpu.PrefetchScalarGridSpec(num_scalar_prefetch=1, grid=grid, in_specs=specs,
                                                     out_specs=out_specs, scratch_shapes=scratch)
            res = pl.pallas_call(wrapped, out_shape=out_shape, grid_spec=grid_spec, name=name,
                                 compiler_params=params, input_output_aliases=aliases or {})(prefetch, *args, *tokens)
        if on_path:
            _Seq.last = res[0] if isinstance(res, (list, tuple)) else res
        return res
    return run


def _sds(shape, dtype):
    return jax.ShapeDtypeStruct(tuple(shape), dtype)


def _sigmoid(v):
    return 1.0 / (1.0 + jnp.exp(-v))


def _normal_cdf_pdf(v):
    a = jnp.abs(v) * 0.7071067811865476
    t = 1.0 / (1.0 + 0.3275911 * a)
    poly = t * (0.254829592 + t * (-0.284496736 + t * (1.421413741 + t * (-1.453152027 + t * 1.061405429))))
    e = jnp.exp(-0.5 * v * v)
    half_erf = 0.5 - 0.5 * poly * e
    return 0.5 + jnp.where(v < 0, -half_erf, half_erf), 0.3989422804014327 * e


def _gelu(v):
    return v * _normal_cdf_pdf(v)[0]


def _fold_rows(val):
    rows, w = val.shape
    return val.reshape(rows // V7X_SUBLANES, V7X_SUBLANES, w).sum(axis=0)


def _rowwise(fn, name, rows_in, vecs_in, rows_out, acc_widths, tm=256):
    t = rows_in[0].shape[0]
    tm = _tile(t, tm, V7X_SUBLANES)
    steps = t // tm
    nr, nv, no, na = len(rows_in), len(vecs_in), len(rows_out), len(acc_widths)

    def body(*refs):
        rin, vin = refs[:nr], refs[nr:nr + nv]
        rout = refs[nr + nv:nr + nv + no]
        aout = refs[nr + nv + no:nr + nv + no + na]
        accs = refs[nr + nv + no + na:]
        i = pl.program_id(0)
        outs, acc_vals = fn(*[r[...] for r in rin], *[v[...] for v in vin])
        for r, o in zip(rout, outs):
            r[...] = o.astype(r.dtype)
        if na:
            @pl.when(i == 0)
            def _():
                for a in accs:
                    a[...] = jnp.zeros_like(a)

            for a, val in zip(accs, acc_vals):
                a[...] += _fold_rows(val)

            @pl.when(i == steps - 1)
            def _():
                for o, a in zip(aout, accs):
                    o[...] = jnp.sum(a[...], axis=0, keepdims=True)

    in_specs = [pl.BlockSpec((tm, r.shape[1]), lambda i: (i, 0)) for r in rows_in]
    in_specs += [pl.BlockSpec(v.shape, functools.partial(lambda nd, i: (0,) * nd, v.ndim)) for v in vecs_in]
    out_specs = [pl.BlockSpec((tm, r.shape[1]), lambda i: (i, 0)) for r in rows_out]
    out_specs += [pl.BlockSpec((1, w), lambda i: (0, 0)) for w in acc_widths]
    out_shape = list(rows_out) + [_sds((1, w), F32) for w in acc_widths]
    scratch = [pltpu.VMEM((V7X_SUBLANES, w), F32) for w in acc_widths]
    res = _call(body, name=name, grid=(steps,), in_specs=in_specs, out_specs=out_specs, out_shape=out_shape,
                scratch=scratch, sem=("arbitrary",) if na else ("parallel",))(*rows_in, *vecs_in)
    return res[:no], res[no:]


def _norm_mod(x, g, scale, shift, name):
    def fn(xv, gv, sc, sh):
        r = lax.rsqrt(jnp.mean(xv * xv, axis=-1, keepdims=True) + NORM_EPS)
        return ((xv * r * gv) * (1.0 + sc) + sh,), ()
    (h,), _ = _rowwise(fn, name, [x], [g, scale, shift], [_sds(x.shape, BF)], [])
    return h


def _final_loss(x, target, g, y, gate, coef, name):
    d = x.shape[1]

    def fn(xv, tv, yv, gv, gt):
        r = lax.rsqrt(jnp.mean(xv * xv, axis=-1, keepdims=True) + NORM_EPS)
        xhat = xv * r
        err = xhat * gv - tv
        dl = err * (1.0 / d)
        dxhat = dl * gv
        dx = r * (dxhat - xhat * jnp.mean(dxhat * xhat, axis=-1, keepdims=True))
        return (dx, (coef * gt) * dx), (err * err, dl * xhat, coef * dx * yv.astype(F32))
    (dx, dy), (sq, dg, dgate) = _rowwise(fn, name, [x, target, y], [g, gate],
                                         [_sds(x.shape, F32), _sds(x.shape, BF)], [d, d, d])
    return sq, dx, dg, dy, dgate


def _gm_act(pre, ln_g, ln_b, name):
    e = pre.shape[1] // 2

    def fn(pv, gv, bv):
        p = pv.astype(F32)
        u = _gelu(p[:, :e])
        v = _gelu(p[:, e:])
        mu = jnp.mean(v, axis=-1, keepdims=True)
        vc = v - mu
        rstd = lax.rsqrt(jnp.mean(vc * vc, axis=-1, keepdims=True) + NORM_EPS)
        return (u, vc * rstd * gv + bv), ()
    t = pre.shape[0]
    (u, vn), _ = _rowwise(fn, name, [pre], [ln_g, ln_b], [_sds((t, e), BF), _sds((t, e), BF)], [])
    return u, vn


def _gm_act_bwd(pre, du, dvn, ln_g, name):
    e = pre.shape[1] // 2

    def fn(pv, duv, dvv, gv):
        p = pv.astype(F32)
        pu, pvv = p[:, :e], p[:, e:]
        cdf_u, pdf_u = _normal_cdf_pdf(pu)
        cdf_v, pdf_v = _normal_cdf_pdf(pvv)
        v = pvv * cdf_v
        mu = jnp.mean(v, axis=-1, keepdims=True)
        vc = v - mu
        rstd = lax.rsqrt(jnp.mean(vc * vc, axis=-1, keepdims=True) + NORM_EPS)
        vhat = vc * rstd
        dvn_f = dvv.astype(F32)
        dvhat = dvn_f * gv
        dv = rstd * (dvhat - jnp.mean(dvhat, axis=-1, keepdims=True)
                     - vhat * jnp.mean(dvhat * vhat, axis=-1, keepdims=True))
        dpu = duv.astype(F32) * (cdf_u + pu * pdf_u)
        dpv = dv * (cdf_v + pvv * pdf_v)
        return (jnp.concatenate([dpu, dpv], axis=1),), (dvn_f * vhat, dvn_f)
    (dpre,), (dg, db) = _rowwise(fn, name, [pre, du, dvn], [ln_g], [_sds(pre.shape, BF)], [e, e], tm=128)
    return dpre, dg, db


def _cv_act(yc, ln_g, ln_b, name):
    def fn(yv, gv, bv):
        mu = jnp.mean(yv, axis=-1, keepdims=True)
        c = yv - mu
        rstd = lax.rsqrt(jnp.mean(c * c, axis=-1, keepdims=True) + NORM_EPS)
        yn = c * rstd * gv + bv
        return (yn * _sigmoid(yn),), ()
    (ys,), _ = _rowwise(fn, name, [yc], [ln_g, ln_b], [_sds(yc.shape, BF)], [])
    return ys


def _cv_act_bwd(dys, yc, ln_g, ln_b, name):
    def fn(dv, yv, gv, bv):
        mu = jnp.mean(yv, axis=-1, keepdims=True)
        c = yv - mu
        rstd = lax.rsqrt(jnp.mean(c * c, axis=-1, keepdims=True) + NORM_EPS)
        yhat = c * rstd
        yn = yhat * gv + bv
        sig = _sigmoid(yn)
        dyn = dv.astype(F32) * (sig * (1.0 + yn * (1.0 - sig)))
        dyhat = dyn * gv
        dyc = rstd * (dyhat - jnp.mean(dyhat, axis=-1, keepdims=True)
                      - yhat * jnp.mean(dyhat * yhat, axis=-1, keepdims=True))
        return (dyc,), (dyn * yhat, dyn, dyc)
    cw = yc.shape[1]
    (dyc,), (dg, db, dbias) = _rowwise(fn, name, [dys, yc], [ln_g, ln_b], [_sds(yc.shape, F32)], [cw, cw, cw])
    return dyc, dg, db, dbias


MM_ROWS = 1024
MM_SEG_ROWS = 256


def _ffn_in(h, w_blk, blk0, name):
    t, d = h.shape
    bn = w_blk.shape[2]
    half = N_DEV // 2
    f = half * bn
    tm = _tile(t, MM_ROWS, V7X_SUBLANES)
    seg_rows = _tile(tm, MM_SEG_ROWS, 2 * V7X_SUBLANES)

    def body(h_ref, wg_ref, wu_ref, dg_ref, du_ref, a_ref):
        for seg in range(tm // seg_rows):
            rows = pl.ds(seg * seg_rows, seg_rows)
            hv = h_ref[rows, :]
            g = jnp.dot(hv, wg_ref[...], preferred_element_type=F32)
            u = jnp.dot(hv, wu_ref[...], preferred_element_type=F32)
            sig = _sigmoid(g)
            sl = g * sig
            dg_ref[rows, :] = (u * (sig * (1.0 + g * (1.0 - sig)))).astype(BF)
            du_ref[rows, :] = sl.astype(BF)
            a_ref[rows, :] = (sl * u).astype(BF)

    out = _sds((t, f), BF)
    tile = pl.BlockSpec((tm, bn), lambda j, i: (i, j))
    return _call(
        body, name=name, grid=(half, t // tm),
        in_specs=[pl.BlockSpec((tm, d), lambda j, i: (i, 0)),
                  pl.BlockSpec((None, d, bn), lambda j, i: (blk0 + j, 0, 0)),
                  pl.BlockSpec((None, d, bn), lambda j, i: (blk0 + half + j, 0, 0))],
        out_specs=[tile, tile, tile], out_shape=[out, out, out], sem=("parallel", "parallel"))(h, w_blk, w_blk)


def _in_proj(h, w_blk, bias, name):
    t, d = h.shape
    bn = w_blk.shape[2]
    tm = _tile(t, 1024, V7X_SUBLANES)

    def body(*refs):
        if bias is None:
            h_ref, w_ref, o_ref = refs
            o_ref[...] = jnp.dot(h_ref[...], w_ref[...], preferred_element_type=F32).astype(BF)
        else:
            h_ref, w_ref, b_ref, o_ref = refs
            o_ref[...] = (jnp.dot(h_ref[...], w_ref[...], preferred_element_type=F32) + b_ref[...]).astype(BF)

    in_specs = [pl.BlockSpec((tm, d), lambda j, i: (i, 0)), pl.BlockSpec((None, d, bn), lambda j, i: (j, 0, 0))]
    args = [h, w_blk]
    if bias is not None:
        in_specs.append(pl.BlockSpec((1, bn), lambda j, i: (0, j)))
        args.append(bias)
    return _call(body, name=name, grid=(N_DEV, t // tm), in_specs=in_specs,
                 out_specs=pl.BlockSpec((tm, bn), lambda j, i: (i, j)), out_shape=_sds((t, N_DEV * bn), BF),
                 sem=("parallel", "parallel"))(*args)


def _out_proj(a, w3, widx, x, gate, bias, coef, name):
    t, k = a.shape
    d = w3.shape[2]
    tm = _tile(t, MM_ROWS, V7X_SUBLANES)
    tn = _tile(d, 512, V7X_LANES)

    def body(*refs):
        if bias is None:
            a_ref, w_ref, x_ref, g_ref, xo_ref, y_ref = refs
            y = jnp.dot(a_ref[...], w_ref[...], preferred_element_type=F32)
        else:
            a_ref, w_ref, x_ref, g_ref, b_ref, xo_ref, y_ref = refs
            y = jnp.dot(a_ref[...], w_ref[...], preferred_element_type=F32) + b_ref[...]
        y_ref[...] = y.astype(BF)
        xo_ref[...] = x_ref[...] + (coef * g_ref[...]) * y

    tile = pl.BlockSpec((tm, tn), lambda j, i: (i, j))
    vec = pl.BlockSpec((1, tn), lambda j, i: (0, j))
    in_specs = [pl.BlockSpec((tm, k), lambda j, i: (i, 0)),
                pl.BlockSpec((None, k, tn), lambda j, i: (widx, 0, j)), tile, vec]
    args = [a, w3, x, gate]
    if bias is not None:
        in_specs.append(vec)
        args.append(bias)
    return _call(body, name=name, grid=(d // tn, t // tm), in_specs=in_specs, out_specs=[tile, tile],
                 out_shape=[_sds((t, d), F32), _sds((t, d), BF)], sem=("parallel", "parallel"))(*args)


def _ffn_da(dy, w3, widx, fg, fu, name):
    t, d = dy.shape
    f = w3.shape[1]
    bn = f // (N_DEV // 2)
    tm = _tile(t, MM_ROWS, V7X_SUBLANES)
    seg_rows = _tile(tm, MM_SEG_ROWS, 2 * V7X_SUBLANES)

    def body(dy_ref, w_ref, fg_ref, fu_ref, dgu_ref):
        for seg in range(tm // seg_rows):
            rows = pl.ds(seg * seg_rows, seg_rows)
            da = lax.dot_general(dy_ref[rows, :], w_ref[...], (((1,), (1,)), ((), ())), preferred_element_type=F32)
            dgu_ref[0, rows, :] = (da * fg_ref[rows, :].astype(F32)).astype(BF)
            dgu_ref[1, rows, :] = (da * fu_ref[rows, :].astype(F32)).astype(BF)

    tile = pl.BlockSpec((tm, bn), lambda j, i: (i, j))
    return _call(
        body, name=name, grid=(f // bn, t // tm),
        in_specs=[pl.BlockSpec((tm, d), lambda j, i: (i, 0)),
                  pl.BlockSpec((None, bn, d), lambda j, i: (widx, j, 0)), tile, tile],
        out_specs=pl.BlockSpec((2, tm, bn), lambda j, i: (0, i, j)),
        out_shape=_sds((2, t, f), BF), sem=("parallel", "parallel"))(dy, w3, fg, fu)


def _mm_nt(dy, w3, widx, name):
    t, k = dy.shape
    n = w3.shape[1]
    tm = _tile(t, MM_ROWS, V7X_SUBLANES)
    tn = _tile(n, 1024, V7X_LANES)

    def body(dy_ref, w_ref, o_ref):
        o_ref[...] = lax.dot_general(dy_ref[...], w_ref[...], (((1,), (1,)), ((), ())),
                                     preferred_element_type=F32).astype(BF)

    return _call(body, name=name, grid=(n // tn, t // tm),
                 in_specs=[pl.BlockSpec((tm, k), lambda j, i: (i, 0)),
                           pl.BlockSpec((None, tn, k), lambda j, i: (widx, j, 0))],
                 out_specs=pl.BlockSpec((tm, tn), lambda j, i: (i, j)), out_shape=_sds((t, n), BF),
                 sem=("parallel", "parallel"))(dy, w3)


NORM_BWD_ROWS = 128


def _dh_norm_bwd(z3, w_blk, x, dxp, g, scale, nxt, name):
    lead, t, _ = z3.shape
    d, bn = w_blk.shape[1], w_blk.shape[2]
    per = N_DEV // lead
    tm = _tile(t, 512, NORM_BWD_ROWS)
    ni = t // tm
    n_in = 6 if nxt is None else 8
    coef, colsum = (None, False) if nxt is None else nxt[2:]
    n_acc = 3 if nxt is None else (5 if colsum else 4)
    n_rows = 1 if nxt is None else 2

    def body(*refs):
        z_ref, w_ref, x_ref, dp_ref, g_ref, sc_ref = refs[:6]
        row_outs = refs[n_in:n_in + n_rows]
        vec_outs = refs[n_in + n_rows:n_in + n_rows + n_acc]
        acc_ref, accs = refs[n_in + n_rows + n_acc], refs[n_in + n_rows + n_acc + 1:]
        i, k = pl.program_id(0), pl.program_id(1)

        @pl.when(k == 0)
        def _():
            acc_ref[...] = jnp.zeros_like(acc_ref)

        @pl.when((i == 0) & (k == 0))
        def _():
            for a in accs:
                a[...] = jnp.zeros_like(a)

        acc_ref[...] += lax.dot_general(z_ref[...], w_ref[...], (((1,), (1,)), ((), ())),
                                        preferred_element_type=F32)

        @pl.when(k == N_DEV - 1)
        def _():
            gv, sc = g_ref[...], sc_ref[...]

            def chunk(ci, carry):
                rows = pl.ds(pl.multiple_of(ci * NORM_BWD_ROWS, NORM_BWD_ROWS), NORM_BWD_ROWS)
                dh, xv = acc_ref[rows, :], x_ref[rows, :]
                r = lax.rsqrt(jnp.mean(xv * xv, axis=-1, keepdims=True) + NORM_EPS)
                xhat = xv * r
                dn = dh * (1.0 + sc)
                dxhat = dn * gv
                dx = r * (dxhat - xhat * jnp.mean(dxhat * xhat, axis=-1, keepdims=True)) + dp_ref[rows, :]
                row_outs[0][rows, :] = dx
                accs[0][...] += _fold_rows(dh * (xhat * gv))
                accs[1][...] += _fold_rows(dh)
                accs[2][...] += _fold_rows(dn * xhat)
                if nxt is not None:
                    y_ref, gate_ref = refs[6], refs[7]
                    dy = (coef * gate_ref[...]) * dx
                    row_outs[1][rows, :] = dy.astype(BF)
                    accs[3][...] += _fold_rows(coef * dx * y_ref[rows, :].astype(F32))
                    if colsum:
                        accs[4][...] += _fold_rows(dy)
                return carry
            lax.fori_loop(0, tm // NORM_BWD_ROWS, chunk, 0)

        @pl.when((i == ni - 1) & (k == N_DEV - 1))
        def _():
            for o, a in zip(vec_outs, accs):
                o[...] = jnp.sum(a[...], axis=0, keepdims=True)

    rows = pl.BlockSpec((tm, d), lambda i, k: (i, 0))
    vec = pl.BlockSpec((1, d), lambda i, k: (0, 0))
    in_specs = [pl.BlockSpec((None, tm, bn), lambda i, k: (k // per, i, k % per)),
                pl.BlockSpec((None, d, bn), lambda i, k: (k, 0, 0)), rows, rows, vec, vec]
    args = [z3, w_blk, x, dxp, g, scale]
    out_specs, out_shape = [rows], [_sds((t, d), F32)]
    if nxt is not None:
        in_specs += [rows, vec]
        args += [nxt[0], nxt[1]]
        out_specs.append(rows)
        out_shape.append(_sds((t, d), BF))
    out_specs += [vec] * n_acc
    out_shape += [_sds((1, d), F32)] * n_acc
    return _call(body, name=name, grid=(ni, N_DEV), in_specs=in_specs, out_specs=out_specs, out_shape=out_shape,
                 scratch=[pltpu.VMEM((tm, d), F32)] + [pltpu.VMEM((V7X_SUBLANES, d), F32)] * n_acc,
                 sem=("arbitrary", "arbitrary"))(*args)


def _mm_tn(a, b3, ta, tb, blocked, name):
    t, ka = a.shape
    lead, _, w = b3.shape
    per = w // tb
    nj = lead * per
    tk = t
    while tk > 512 and 4 * tk * (ta + tb) + 8 * ta * tb > V7X_VMEM_LIMIT * 3 // 4:
        tk //= 2
    tk = _tile(t, tk, V7X_SUBLANES)
    nk = t // tk

    def body(a_ref, b_ref, o_ref, acc_ref):
        k = pl.program_id(2)

        @pl.when(k == 0)
        def _():
            acc_ref[...] = jnp.zeros_like(acc_ref)

        acc_ref[...] += lax.dot_general(a_ref[...], b_ref[...], (((0,), (0,)), ((), ())),
                                        preferred_element_type=F32)

        @pl.when(k == nk - 1)
        def _():
            o_ref[...] = acc_ref[...].astype(BF)

    if blocked:
        out_shape = _sds((nj, ka, tb), BF)
        out_spec = pl.BlockSpec((None, ta, tb), lambda i, j, k: (j, i, 0))
    else:
        out_shape = _sds((1, ka, w), BF)
        out_spec = pl.BlockSpec((None, ta, tb), lambda i, j, k: (0, i, j))
    return _call(body, name=name, grid=(ka // ta, nj, nk),
                 in_specs=[pl.BlockSpec((tk, ta), lambda i, j, k: (k, i)),
                           pl.BlockSpec((None, tk, tb), lambda i, j, k: (j // per, k, j % per))],
                 out_specs=out_spec, out_shape=out_shape, scratch=[pltpu.VMEM((ta, tb), F32)],
                 sem=("parallel", "parallel", "arbitrary"))(a, b3)


def _causal(ws):
    l = ws.shape[0]
    row = lax.broadcasted_iota(jnp.int32, (l, l), 0)
    col = lax.broadcasted_iota(jnp.int32, (l, l), 1)
    return jnp.where(col <= row, ws, 0.0)


def _sgu_fwd(u, vn, ws, bsb, name):
    t, e = u.shape
    hn, l, _ = ws.shape
    dh = e // hn
    nc = t // l

    def body(u_ref, v_ref, ws_ref, bs_ref, s_ref):
        wsc = _causal(ws_ref[...]).astype(BF)
        bias = bs_ref[...]

        def chunk(c, carry):
            rows = pl.ds(pl.multiple_of(c * l, l), l)
            vo = jnp.dot(wsc, v_ref[rows, :], preferred_element_type=F32) + bias
            s_ref[rows, :] = (u_ref[rows, :].astype(F32) * vo).astype(BF)
            return carry
        lax.fori_loop(0, nc, chunk, 0)

    col = pl.BlockSpec((t, dh), lambda h: (0, h))
    return _call(body, name=name, grid=(hn,),
                 in_specs=[col, col, pl.BlockSpec((None, l, l), lambda h: (h, 0, 0)),
                           pl.BlockSpec((None, l, dh), lambda h: (h, 0, 0))],
                 out_specs=col, out_shape=_sds((t, e), BF), sem=("parallel",))(u, vn, ws, bsb)


def _sgu_bwd(ds, u, vn, ws, bsb, name):
    t, e = u.shape
    hn, l, _ = ws.shape
    dh = e // hn
    nc = t // l

    def body(ds_ref, u_ref, v_ref, ws_ref, bs_ref, du_ref, dv_ref, dws_ref, dbs_ref, accw_ref, accb_ref):
        wsc = _causal(ws_ref[...]).astype(BF)
        bias = bs_ref[...]
        accw_ref[...] = jnp.zeros_like(accw_ref)
        accb_ref[...] = jnp.zeros_like(accb_ref)

        def chunk(c, carry):
            rows = pl.ds(pl.multiple_of(c * l, l), l)
            vc = v_ref[rows, :]
            dsv = ds_ref[rows, :].astype(F32)
            vo = jnp.dot(wsc, vc, preferred_element_type=F32) + bias
            du_ref[rows, :] = (dsv * vo).astype(BF)
            dvo = dsv * u_ref[rows, :].astype(F32)
            dvo_b = dvo.astype(BF)
            accb_ref[...] += dvo
            accw_ref[...] += lax.dot_general(dvo_b, vc, (((1,), (1,)), ((), ())), preferred_element_type=F32)
            dv_ref[rows, :] = lax.dot_general(wsc, dvo_b, (((0,), (0,)), ((), ())),
                                              preferred_element_type=F32).astype(BF)
            return carry
        lax.fori_loop(0, nc, chunk, 0)
        dws_ref[...] = _causal(accw_ref[...])
        dbs_ref[...] = jnp.broadcast_to(jnp.sum(accb_ref[...], axis=1, keepdims=True), (l, dh))

    col = pl.BlockSpec((t, dh), lambda h: (0, h))
    return _call(body, name=name, grid=(hn,),
                 in_specs=[col, col, col, pl.BlockSpec((None, l, l), lambda h: (h, 0, 0)),
                           pl.BlockSpec((None, l, dh), lambda h: (h, 0, 0))],
                 out_specs=[col, col, pl.BlockSpec((None, l, l), lambda h: (h, 0, 0)),
                            pl.BlockSpec((None, l, dh), lambda h: (h, 0, 0))],
                 out_shape=[_sds((t, e), BF), _sds((t, e), BF), _sds((hn, l, l), F32), _sds((hn, l, dh), F32)],
                 scratch=[pltpu.VMEM((l, l), F32), pltpu.VMEM((l, dh), F32)],
                 sem=("parallel",))(ds, u, vn, ws, bsb)


AG_AHEAD = 3

CONV_HALO = 32
CONV_ROWS = 64
CONV_LANES = 256


def _shifted_windows(win_ref, sh_ref, rows):
    for b in range(1, V7X_SUBLANES):
        sh_ref[b - 1, 0:rows, :] = win_ref[b:b + rows, :]


def _window_rows(win_ref, sh_ref, shift, r0, rows):
    a, b = divmod(shift, V7X_SUBLANES)
    start = pl.multiple_of(r0 + V7X_SUBLANES * a, V7X_SUBLANES)
    if b == 0:
        return win_ref[pl.ds(start, rows), :]
    return sh_ref[b - 1, pl.ds(start, rows), :]


def _dwconv_fwd(p, dw_w, dw_b, name):
    t, c2 = p.shape
    cw = c2 // 2
    kw = dw_w.shape[0]
    cb = _tile(cw, CONV_LANES, V7X_LANES)
    ncb = cw // cb
    tm = _tile(t, 512, CONV_ROWS)
    off = CONV_HALO - (kw - 1)

    def body(a_ref, g_ref, ap_ref, gp_ref, w_ref, b_ref, o_ref, win_ref, sh_ref):
        i = pl.program_id(1)
        prev = ap_ref[...].astype(F32) * _sigmoid(gp_ref[...].astype(F32))
        win_ref[0:CONV_HALO, :] = jnp.where(i > 0, prev, 0.0)
        win_ref[CONV_HALO:, :] = a_ref[...].astype(F32) * _sigmoid(g_ref[...].astype(F32))
        _shifted_windows(win_ref, sh_ref, tm + CONV_HALO - V7X_SUBLANES)

        def chunk(ci, carry):
            r0 = ci * CONV_ROWS
            acc = jnp.zeros((CONV_ROWS, cb), F32) + b_ref[...]
            for k in range(kw):
                acc = acc + w_ref[k:k + 1, :] * _window_rows(win_ref, sh_ref, off + k, r0, CONV_ROWS)
            o_ref[pl.ds(pl.multiple_of(r0, CONV_ROWS), CONV_ROWS), :] = acc
            return carry
        lax.fori_loop(0, tm // CONV_ROWS, chunk, 0)

    hpt = tm // CONV_HALO
    cur_a = pl.BlockSpec((tm, cb), lambda j, i: (i, j))
    cur_g = pl.BlockSpec((tm, cb), lambda j, i: (i, ncb + j))
    prev_a = pl.BlockSpec((CONV_HALO, cb), lambda j, i: (jnp.maximum(i * hpt - 1, 0), j))
    prev_g = pl.BlockSpec((CONV_HALO, cb), lambda j, i: (jnp.maximum(i * hpt - 1, 0), ncb + j))
    return _call(body, name=name, grid=(ncb, t // tm),
                 in_specs=[cur_a, cur_g, prev_a, prev_g, pl.BlockSpec((kw, cb), lambda j, i: (0, j)),
                           pl.BlockSpec((1, cb), lambda j, i: (0, j))],
                 out_specs=pl.BlockSpec((tm, cb), lambda j, i: (i, j)), out_shape=_sds((t, cw), F32),
                 scratch=[pltpu.VMEM((tm + CONV_HALO, cb), F32),
                          pltpu.VMEM((V7X_SUBLANES - 1, tm + CONV_HALO - V7X_SUBLANES, cb), F32)],
                 sem=("parallel", "parallel"))(p, p, p, p, dw_w, dw_b)


def _dwconv_bwd(dyc, p, dw_w, name):
    t, c2 = p.shape
    cw = c2 // 2
    kw = dw_w.shape[0]
    cb = _tile(cw, CONV_LANES, V7X_LANES)
    ncb = cw // cb
    tm = _tile(t, 512, CONV_ROWS)
    nt = t // tm
    off = CONV_HALO - (kw - 1)
    kpad = -(-kw // V7X_SUBLANES) * V7X_SUBLANES
    sh_rows = tm + CONV_HALO - V7X_SUBLANES

    def body(d_ref, dn_ref, a_ref, g_ref, ap_ref, gp_ref, w_ref,
             dp_ref, dw_ref, dba_ref, dbg_ref, dwin_ref, ywin_ref, dsh_ref, ysh_ref, accw_ref, acca_ref, accg_ref):
        i = pl.program_id(1)

        @pl.when(i == 0)
        def _():
            accw_ref[...] = jnp.zeros_like(accw_ref)
            acca_ref[...] = jnp.zeros_like(acca_ref)
            accg_ref[...] = jnp.zeros_like(accg_ref)

        prev = ap_ref[...].astype(F32) * _sigmoid(gp_ref[...].astype(F32))
        ywin_ref[0:CONV_HALO, :] = jnp.where(i > 0, prev, 0.0)
        ywin_ref[CONV_HALO:, :] = a_ref[...].astype(F32) * _sigmoid(g_ref[...].astype(F32))
        dwin_ref[0:tm, :] = d_ref[...]
        dwin_ref[tm:, :] = jnp.where(i < nt - 1, dn_ref[...], 0.0)
        _shifted_windows(ywin_ref, ysh_ref, sh_rows)
        _shifted_windows(dwin_ref, dsh_ref, sh_rows)

        def chunk(ci, carry):
            r0 = ci * CONV_ROWS
            rows = pl.ds(pl.multiple_of(r0, CONV_ROWS), CONV_ROWS)
            dcur = d_ref[rows, :]
            dyg = jnp.zeros((CONV_ROWS, cb), F32)
            for k in range(kw):
                dyg = dyg + w_ref[k:k + 1, :] * _window_rows(dwin_ref, dsh_ref, kw - 1 - k, r0, CONV_ROWS)
                accw_ref[k] += _fold_rows(dcur * _window_rows(ywin_ref, ysh_ref, off + k, r0, CONV_ROWS))
            av = a_ref[rows, :].astype(F32)
            sig = _sigmoid(g_ref[rows, :].astype(F32))
            da = dyg * sig
            dg = dyg * av * sig * (1.0 - sig)
            dp_ref[0, rows, :] = da.astype(BF)
            dp_ref[1, rows, :] = dg.astype(BF)
            acca_ref[...] += _fold_rows(da)
            accg_ref[...] += _fold_rows(dg)
            return carry
        lax.fori_loop(0, tm // CONV_ROWS, chunk, 0)

        @pl.when(i == nt - 1)
        def _():
            dw_ref[...] = jnp.sum(accw_ref[...], axis=1)
            dba_ref[...] = jnp.sum(acca_ref[...], axis=0, keepdims=True)
            dbg_ref[...] = jnp.sum(accg_ref[...], axis=0, keepdims=True)

    hpt = tm // CONV_HALO
    last_halo = t // CONV_HALO - 1
    tile = pl.BlockSpec((tm, cb), lambda j, i: (i, j))
    cur_g = pl.BlockSpec((tm, cb), lambda j, i: (i, ncb + j))
    nxt = pl.BlockSpec((CONV_HALO, cb), lambda j, i: (jnp.minimum((i + 1) * hpt, last_halo), j))
    prev_a = pl.BlockSpec((CONV_HALO, cb), lambda j, i: (jnp.maximum(i * hpt - 1, 0), j))
    prev_g = pl.BlockSpec((CONV_HALO, cb), lambda j, i: (jnp.maximum(i * hpt - 1, 0), ncb + j))
    vec = pl.BlockSpec((1, cb), lambda j, i: (0, j))
    dp, ddw, dba, dbg = _call(
        body, name=name, grid=(ncb, nt),
        in_specs=[tile, nxt, tile, cur_g, prev_a, prev_g, pl.BlockSpec((kw, cb), lambda j, i: (0, j))],
        out_specs=[pl.BlockSpec((2, tm, cb), lambda j, i: (0, i, j)), pl.BlockSpec((kpad, cb), lambda j, i: (0, j)),
                   vec, vec],
        out_shape=[_sds((2, t, cw), BF), _sds((kpad, cw), F32), _sds((1, cw), F32), _sds((1, cw), F32)],
        scratch=[pltpu.VMEM((tm + CONV_HALO, cb), F32), pltpu.VMEM((tm + CONV_HALO, cb), F32),
                 pltpu.VMEM((V7X_SUBLANES - 1, sh_rows, cb), F32), pltpu.VMEM((V7X_SUBLANES - 1, sh_rows, cb), F32),
                 pltpu.VMEM((kpad, V7X_SUBLANES, cb), F32), pltpu.VMEM((V7X_SUBLANES, cb), F32),
                 pltpu.VMEM((V7X_SUBLANES, cb), F32)],
        sem=("parallel", "arbitrary"))(dyc, dyc, p, p, p, p, dw_w)
    return dp, ddw[:kw], dba, dbg


def _adam_math(g, w, m, v):
    m2 = ADAM_B1 * m + (1.0 - ADAM_B1) * g
    v2 = ADAM_B2 * v + (1.0 - ADAM_B2) * (g * g)
    m_hat = m2 / (1.0 - ADAM_B1 ** ADAM_STEP)
    v_hat = v2 / (1.0 - ADAM_B2 ** ADAM_STEP)
    delta = -ADAM_LR * (m_hat / (jnp.sqrt(v_hat) + ADAM_EPS) + ADAM_WD * w)
    return delta, m2, v2


def _adamw(g_parts, w, m, v, name):
    r, c = w.shape
    tr = _tile(r, 256, V7X_SUBLANES)
    ng = len(g_parts)

    def body(*refs):
        g = refs[0][...].astype(F32)
        for s in refs[1:ng]:
            g = g + s[...].astype(F32)
        w_ref, m_ref, v_ref, go_ref, d_ref, mo_ref, vo_ref = refs[ng:]
        delta, m2, v2 = _adam_math(g, w_ref[...], m_ref[...], v_ref[...])
        go_ref[...] = g
        d_ref[...] = delta
        mo_ref[...] = m2
        vo_ref[...] = v2

    tile = pl.BlockSpec((tr, c), lambda i: (i, 0))
    in_specs = [pl.BlockSpec((None, tr, c), functools.partial(lambda s, i: (s, i, 0), s)) for _, s in g_parts]
    out = _sds((r, c), F32)
    return _call(body, name=name, grid=(r // tr,), in_specs=in_specs + [tile, tile, tile],
                 out_specs=[tile] * 4, out_shape=[out] * 4, sem=("parallel",))(*[a for a, _ in g_parts], w, m, v)


def _adamw_stacked(h, recv, chip, w_st, m_st, v_st, k, prev, name):
    kk, r, c = w_st.shape
    tr = _tile(r, 256, V7X_SUBLANES)
    if prev is None:
        prev = [lax.empty((kk, r, c), F32) for _ in range(4)]

    def body(chip_ref, h_ref, r0_ref, r1_ref, r2_ref, w_ref, m_ref, v_ref, pg, pd, pm, pv,
             go_ref, d_ref, mo_ref, vo_ref):
        g = (h_ref[...].astype(F32) + r0_ref[...].astype(F32)) + (r1_ref[...].astype(F32) + r2_ref[...].astype(F32))
        delta, m2, v2 = _adam_math(g, w_ref[...], m_ref[...], v_ref[...])
        go_ref[...] = g
        d_ref[...] = delta
        mo_ref[...] = m2
        vo_ref[...] = v2

    own = pl.BlockSpec((None, tr, c), lambda i, chip_ref: (chip_ref[0], i, 0))
    rcv = [pl.BlockSpec((None, tr, c), functools.partial(lambda s, i, chip_ref: (s, i, 0), s)) for s in range(3)]
    blk = pl.BlockSpec((None, tr, c), lambda i, chip_ref: (k, i, 0))
    out = _sds((kk, r, c), F32)
    return _call(body, name=name, grid=(r // tr,), in_specs=[own] + rcv + [blk, blk, blk] + [ANY] * 4,
                 out_specs=[blk] * 4, out_shape=[out] * 4, sem=("parallel",), prefetch=chip, on_path=False,
                 aliases={8: 0, 9: 1, 10: 2, 11: 3})(h, recv, recv, recv, w_st, m_st, v_st, *prev)


def _add_sibling(g4, land, core, name):
    n, _, r, c = g4.shape
    tr = _tile(r, 512, V7X_SUBLANES)

    def body(core_ref, a_ref, b_ref, o_ref):
        o_ref[...] = (a_ref[...].astype(F32) + b_ref[...].astype(F32)).astype(BF)

    return _call(body, name=name, grid=(n, r // tr),
                 in_specs=[pl.BlockSpec((None, None, tr, c), lambda p, i, core_ref: (p, core_ref[0], i, 0)),
                           pl.BlockSpec((None, None, tr, c), lambda p, i, core_ref: (p, 0, i, 0))],
                 out_specs=pl.BlockSpec((None, tr, c), lambda p, i, core_ref: (p, i, 0)),
                 out_shape=_sds((n, r, c), BF), sem=("parallel", "parallel"), prefetch=core)(g4, land)


def _cast_to_slot(w, lead, me, name, after=()):
    r, c = w.shape[-2:]
    nl = len(lead)
    tr = _tile(r, 512, 2 * V7X_SUBLANES)

    def body(me_ref, w_ref, o_ref):
        o_ref[...] = w_ref[...].astype(BF)

    return _call(body, name=name, grid=(r // tr,),
                 in_specs=[pl.BlockSpec((None,) * nl + (tr, c), lambda i, me_ref: tuple(lead) + (i, 0))],
                 out_specs=pl.BlockSpec((None, None, tr, c), lambda i, me_ref: (0, me_ref[0], i, 0)),
                 out_shape=_sds((1, N_DEV, r, c), BF), sem=("parallel",), prefetch=me, after=after)(w)


def _ada_fwd(c_pad, ada_w, ada_b, name):
    nl, d, cl = ada_w.shape
    rows = c_pad.shape[0]
    tn = _tile(cl, 256, V7X_LANES)

    def body(c_ref, w_ref, b_ref, o_ref):
        cv = c_ref[...]
        cond = (cv * _sigmoid(cv)).astype(BF)
        o_ref[...] = jnp.dot(cond, w_ref[...].astype(BF), preferred_element_type=F32) + b_ref[...]

    return _call(body, name=name, grid=(nl, cl // tn),
                 in_specs=[pl.BlockSpec((rows, d), lambda l, j: (0, 0)),
                           pl.BlockSpec((None, d, tn), lambda l, j: (l, 0, j)),
                           pl.BlockSpec((None, 1, tn), lambda l, j: (l, 0, j))],
                 out_specs=pl.BlockSpec((None, rows, tn), lambda l, j: (l, 0, j)),
                 out_shape=_sds((nl, rows, cl), F32), sem=("parallel", "parallel"))(c_pad, ada_w, ada_b)


def _ada_bwd(c_pad, dmod, w, m, v, name):
    nl, d, cl = w.shape
    rows = c_pad.shape[0]
    tn = _tile(cl, 256, V7X_LANES)

    def body(c_ref, dm_ref, w_ref, m_ref, v_ref, go_ref, d_ref, mo_ref, vo_ref):
        cv = c_ref[...]
        cond = (cv * _sigmoid(cv)).astype(BF)
        g = lax.dot_general(cond, dm_ref[...].astype(BF), (((0,), (0,)), ((), ())), preferred_element_type=F32)
        delta, m2, v2 = _adam_math(g, w_ref[...], m_ref[...], v_ref[...])
        go_ref[...] = g
        d_ref[...] = delta
        mo_ref[...] = m2
        vo_ref[...] = v2

    tile = pl.BlockSpec((None, d, tn), lambda l, j: (l, 0, j))
    out = _sds((nl, d, cl), F32)
    return _call(body, name=name, grid=(nl, cl // tn),
                 in_specs=[pl.BlockSpec((rows, d), lambda l, j: (0, 0)),
                           pl.BlockSpec((None, rows, tn), lambda l, j: (l, 0, j)), tile, tile, tile],
                 out_specs=[tile] * 4, out_shape=[out] * 4, sem=("parallel", "parallel"))(c_pad, dmod, w, m, v)


def _sum_devices(parts, name):
    n, r, c = parts.shape
    tr = _tile(r, 512, V7X_SUBLANES)

    def body(p_ref, o_ref):
        acc = p_ref[0]
        for k in range(1, n):
            acc = acc + p_ref[k]
        o_ref[...] = acc

    return _call(body, name=name, grid=(r // tr,), in_specs=[pl.BlockSpec((n, tr, c), lambda i: (0, i, 0))],
                 out_specs=pl.BlockSpec((tr, c), lambda i: (i, 0)), out_shape=_sds((r, c), F32),
                 sem=("parallel",))(parts)


def _mesh_pos():
    return lax.axis_index("x"), lax.axis_index("y"), lax.axis_index("c")


def _other_chips(x, y):
    return [(1 - x, y), (x, 1 - y), (1 - x, 1 - y)]


def _all_gather(arrs, name):
    n = len(arrs)

    def body(*refs):
        ins, outs = refs[:n], refs[n:2 * n]
        send_sems, recv_sems, local_sems = refs[2 * n:]
        x, y, c = _mesh_pos()
        me, sibling = (x, y, c), (x, y, 1 - c)
        chips = _other_chips(x, y)

        def slot(a, pos):
            px, py, pc = pos
            return outs[a].at[:, pl.ds(4 * px + 2 * py + pc, 1)]

        def copy(a, k, block, to, src=None):
            return pltpu.make_async_remote_copy(
                src_ref=slot(a, block) if src is None else src, dst_ref=slot(a, block),
                send_sem=send_sems.at[a, k], recv_sem=recv_sems.at[a, k], device_id=to, device_id_type=MESH)

        mine = [pltpu.make_async_copy(ins[a], slot(a, me), local_sems.at[a]) for a in range(n)]
        for cp in mine:
            cp.start()
        first = []
        for a in range(n):
            first.append(copy(a, 0, me, sibling, src=ins[a]))
            first += [copy(a, 1 + j, me, (*chip, c), src=ins[a]) for j, chip in enumerate(chips)]
        for cp in first:
            cp.start()
        passed = []
        for a in range(n):
            for j, chip in enumerate(chips):
                copy(a, 1 + j, (*chip, c), me).wait_recv()
                fwd = copy(a, 4 + j, (*chip, c), sibling)
                fwd.start()
                passed.append(fwd)
        for a in range(n):
            copy(a, 0, sibling, me).wait_recv()
            for j, chip in enumerate(chips):
                copy(a, 4 + j, (*chip, 1 - c), me).wait_recv()
        for cp in first + passed:
            cp.wait_send()
        for cp in mine:
            cp.wait()

    out_shape = [_sds((a.shape[0], N_DEV) + a.shape[2:], a.dtype) for a in arrs]
    return pl.pallas_call(
        body, out_shape=out_shape, in_specs=[ANY] * n, out_specs=[ANY] * n, name=name,
        scratch_shapes=[pltpu.SemaphoreType.DMA((n, N_DEV - 1)), pltpu.SemaphoreType.DMA((n, N_DEV - 1)),
                        pltpu.SemaphoreType.DMA((n,))])(*arrs)


HBM = pl.BlockSpec(memory_space=pltpu.HBM)
SEM = pl.BlockSpec(memory_space=pltpu.SEMAPHORE)


def _hbm(v):
    return pltpu.with_memory_space_constraint(v, pltpu.HBM)


def _comm_call(body, name, bufs, sems_in, sems_out, follow=()):
    after = [] if not sems_in or _Seq.last is None or any(_Seq.last is b for b in bufs) else [_Seq.last]
    after += list(follow)
    nb, ni, na, no = len(bufs), len(sems_in), len(after), len(sems_out)

    def wrapped(*refs):
        body(refs[:nb], refs[nb:nb + ni], refs[nb + ni + na:nb + ni + na + no])
        if no:
            refs[-1][...] = jnp.zeros_like(refs[-1])

    out_shape = [pltpu.SemaphoreType.DMA(s) for s in sems_out] + [pltpu.HBM(b.shape, b.dtype) for b in bufs]
    out_specs = [SEM] * no + [HBM] * nb
    if no:
        out_shape.append(_sds((V7X_SUBLANES, V7X_LANES), F32))
        out_specs.append(pl.BlockSpec(memory_space=pltpu.VMEM))
    res = pl.pallas_call(
        wrapped, name=name, out_shape=out_shape, in_specs=[HBM] * nb + [SEM] * ni + [ANY] * na, out_specs=out_specs,
        input_output_aliases={i: no + i for i in range(nb)},
        compiler_params=pltpu.CompilerParams(has_side_effects=pltpu.SideEffectType.DATAFLOW_SIDE_EFFECTING),
    )(*bufs, *sems_in, *after)
    out_bufs = list(res[no:no + nb])
    if no:
        _Seq.tokens.append(res[-1])
    _Seq.last = out_bufs[0]
    return list(res[:no]), out_bufs


def _remote(src, dst, send_sem, recv_sem, to):
    return pltpu.make_async_remote_copy(src_ref=src, dst_ref=dst, send_sem=send_sem, recv_sem=recv_sem,
                                        device_id=to, device_id_type=MESH)


def _slot(ref, pos):
    px, py, pc = pos
    return ref.at[:, pl.ds(4 * px + 2 * py + pc, 1)]


def _ag_start(bufs, name):
    n = len(bufs)

    def body(b, _, sems):
        send_sib, recv_sib, send_ici, recv_ici = sems
        x, y, c = _mesh_pos()
        for a in range(n):
            mine = _slot(b[a], (x, y, c))
            _remote(mine, mine, send_sib.at[a], recv_sib.at[a], (x, y, 1 - c)).start()
            for j, (px, py) in enumerate(_other_chips(x, y)):
                _remote(mine, mine, send_ici.at[3 * a + j], recv_ici.at[3 * a + j], (px, py, c)).start()

    sems, bufs = _comm_call(body, name, [_hbm(b) for b in bufs], [], [(n,), (n,), (3 * n,), (3 * n,)])
    return dict(bufs=bufs, send_sib=sems[0], recv_sib=sems[1], send_ici=sems[2], recv_ici=sems[3])


def _ag_mid(st, name):
    n = len(st["bufs"])

    def body(b, sems_in, sems):
        (recv_ici,) = sems_in
        send_fwd, recv_fwd = sems
        x, y, c = _mesh_pos()
        for a in range(n):
            for j, (px, py) in enumerate(_other_chips(x, y)):
                blk = _slot(b[a], (px, py, c))
                _remote(blk, blk, send_fwd.at[3 * a + j], recv_ici.at[3 * a + j], (x, y, 1 - c)).wait_recv()
                _remote(blk, blk, send_fwd.at[3 * a + j], recv_fwd.at[3 * a + j], (x, y, 1 - c)).start()

    sems, bufs = _comm_call(body, name, st["bufs"], [st["recv_ici"]], [(3 * n,), (3 * n,)])
    return dict(st, bufs=bufs, send_fwd=sems[0], recv_fwd=sems[1])


def _ag_end(st, name):
    n = len(st["bufs"])

    def body(b, sems_in, _):
        send_sib, recv_sib, send_ici, send_fwd, recv_fwd = sems_in
        x, y, c = _mesh_pos()
        sibling = (x, y, 1 - c)
        for a in range(n):
            mine, sib_blk = _slot(b[a], (x, y, c)), _slot(b[a], sibling)
            _remote(mine, mine, send_sib.at[a], recv_sib.at[a], sibling).wait_send()
            _remote(sib_blk, sib_blk, send_sib.at[a], recv_sib.at[a], sibling).wait_recv()
            for j, (px, py) in enumerate(_other_chips(x, y)):
                blk, sib_got = _slot(b[a], (px, py, c)), _slot(b[a], (px, py, 1 - c))
                _remote(mine, mine, send_ici.at[3 * a + j], recv_sib.at[a], (px, py, c)).wait_send()
                _remote(blk, blk, send_fwd.at[3 * a + j], recv_fwd.at[3 * a + j], sibling).wait_send()
                _remote(sib_got, sib_got, send_fwd.at[3 * a + j], recv_fwd.at[3 * a + j], sibling).wait_recv()

    _, bufs = _comm_call(body, name, st["bufs"],
                         [st[k] for k in ("send_sib", "recv_sib", "send_ici", "send_fwd", "recv_fwd")], [])
    return bufs


def _rs_start(g4s, name):
    n = len(g4s)
    lands = [lax.empty((N_CHIP, 1) + g.shape[2:], g.dtype) for g in g4s]

    def body(b, _, sems):
        send, recv = sems
        x, y, c = _mesh_pos()
        for a in range(n):
            _remote(b[a].at[:, pl.ds(1 - c, 1)], b[n + a], send.at[a], recv.at[a], (x, y, 1 - c)).start()

    sems, bufs = _comm_call(body, name, [_hbm(v) for v in list(g4s) + lands], [], [(n,), (n,)])
    return dict(bufs=bufs, send=sems[0], recv=sems[1])


def _rs_mid(st, name):
    n = len(st["bufs"]) // 2

    def body(b, sems_in, _):
        send, recv = sems_in
        x, y, c = _mesh_pos()
        for a in range(n):
            cp = _remote(b[a].at[:, pl.ds(1 - c, 1)], b[n + a], send.at[a], recv.at[a], (x, y, 1 - c))
            cp.wait_send()
            cp.wait_recv()

    _, bufs = _comm_call(body, name, st["bufs"], [st["send"], st["recv"]], [])
    return bufs[:n], bufs[n:]


def _rs_start2(sums, name):
    n = len(sums)
    lands = [lax.empty((N_CHIP - 1,) + s.shape[1:], s.dtype) for s in sums]

    def body(b, _, sems):
        send, recv = sems
        x, y, c = _mesh_pos()
        for a in range(n):
            for j, (px, py) in enumerate(_other_chips(x, y)):
                _remote(b[a].at[pl.ds(2 * px + py, 1)], b[n + a].at[pl.ds(j, 1)], send.at[3 * a + j], recv.at[3 * a + j],
                        (px, py, c)).start()

    sems, bufs = _comm_call(body, name, [_hbm(v) for v in list(sums) + lands], [], [(3 * n,), (3 * n,)])
    return dict(bufs=bufs, send=sems[0], recv=sems[1])


def _rs_end(st, name, follow=()):
    n = len(st["bufs"]) // 2

    def body(b, sems_in, _):
        send, recv = sems_in
        x, y, c = _mesh_pos()
        for a in range(n):
            for j, (px, py) in enumerate(_other_chips(x, y)):
                cp = _remote(b[a].at[pl.ds(2 * px + py, 1)], b[n + a].at[pl.ds(j, 1)], send.at[3 * a + j], recv.at[3 * a + j],
                             (px, py, c))
                cp.wait_send()
                cp.wait_recv()

    _, bufs = _comm_call(body, name, st["bufs"], [st["send"], st["recv"]], [], follow=follow)
    return bufs[:n], bufs[n:]


def _pack(parts, rows_align=V7X_SUBLANES):
    flat, total = [], 0
    for p in parts:
        v = p.reshape(-1).astype(F32)
        pad = -v.shape[0] % PACK_ALIGN
        flat.append(jnp.pad(v, (0, pad)) if pad else v)
        total += v.shape[0] + pad
    tail = -total % (rows_align * V7X_LANES)
    if tail:
        flat.append(jnp.zeros((tail,), F32))
    return jnp.concatenate(flat).reshape(-1, V7X_LANES)


def _unpack(buf, shapes):
    lead = buf.shape[:-2]
    flat = buf.reshape(lead + (-1,))
    out, pos = [], 0
    for s in shapes:
        size = 1
        for d in s:
            size *= d
        out.append(flat[..., pos:pos + size].reshape(lead + tuple(s)))
        pos += size + (-size % PACK_ALIGN)
    return out


def kernel(x, c, ada_w, ada_b, norm_g, ffn_w_in, ffn_w_out, gm_w_in, gm_ln_g, gm_ln_b, gm_ws, gm_bs, gm_w_out, cv_w_in, cv_b_in, cv_dw_w, cv_dw_b, cv_ln_g, cv_ln_b, cv_w_out, cv_b_out, final_g, loss_target, m_ada_w, m_ada_b, m_norm_g, m_ffn_w_in, m_ffn_w_out, m_gm_w_in, m_gm_ln_g, m_gm_ln_b, m_gm_ws, m_gm_bs, m_gm_w_out, m_cv_w_in, m_cv_b_in, m_cv_dw_w, m_cv_dw_b, m_cv_ln_g, m_cv_ln_b, m_cv_w_out, m_cv_b_out, m_final_g, v_ada_w, v_ada_b, v_norm_g, v_ffn_w_in, v_ffn_w_out, v_gm_w_in, v_gm_ln_g, v_gm_ln_b, v_gm_ws, v_gm_bs, v_gm_w_out, v_cv_w_in, v_cv_b_in, v_cv_dw_w, v_cv_dw_b, v_cv_ln_g, v_cv_ln_b, v_cv_w_out, v_cv_b_out, v_final_g):
    t, d = x.shape[1], x.shape[2]
    depth = ada_w.shape[0]
    assert depth == 2 and ffn_w_in.shape[:2] == (2, 2) and gm_w_in.shape[0] == 1 and cv_w_in.shape[0] == 1
    dl = d // N_DEV
    bn = ffn_w_in.shape[3]
    fl = ffn_w_out.shape[2]
    f = fl * N_DEV
    el = gm_w_in.shape[2]
    e = el * N_DEV // 2
    hn, l = gm_ws.shape[1], gm_ws.shape[2]
    kw = cv_dw_w.shape[1]
    cl = ada_w.shape[2]
    me = 4 * lax.axis_index("x") + 2 * lax.axis_index("y") + lax.axis_index("c")
    me1 = me.astype(jnp.int32).reshape(1)
    chip1 = (2 * lax.axis_index("x") + lax.axis_index("y")).astype(jnp.int32).reshape(1)
    core1 = lax.axis_index("c").astype(jnp.int32).reshape(1)
    _Seq.last, _Seq.tokens = None, []

    xs = x[0]
    tgt = loss_target[0]

    ag_groups = [("win00", [(ffn_w_in, (0, 0))]), ("wout00", [(ffn_w_out, (0, 0))]),
                 ("gm", [(gm_w_in, (0,)), (gm_w_out, (0,))]),
                 ("win01", [(ffn_w_in, (0, 1))]), ("wout01", [(ffn_w_out, (0, 1))]),
                 ("win10", [(ffn_w_in, (1, 0))]), ("wout10", [(ffn_w_out, (1, 0))]),
                 ("cv", [(cv_w_in, (0,)), (cv_w_out, (0,))]),
                 ("win11", [(ffn_w_in, (1, 1))]), ("wout11", [(ffn_w_out, (1, 1))])]
    ag_flight = {}

    ag_slots = {}

    def ag_cast(gi, after=()):
        gname, members = ag_groups[gi]
        ag_slots[gi] = [_cast_to_slot(w, lead, me1, name=f"cast_{gname}_{k}", after=after)
                        for k, (w, lead) in enumerate(members)]

    def ag_start(gi):
        ag_flight[gi] = _ag_start(ag_slots.pop(gi), name=f"ag_start_{ag_groups[gi][0]}")

    def ag_forward(gi):
        if gi in ag_flight and "send_fwd" not in ag_flight[gi]:
            ag_flight[gi] = _ag_mid(ag_flight[gi], name=f"ag_mid_{ag_groups[gi][0]}")

    def ag_take(gi):
        ag_forward(gi)
        bufs = _ag_end(ag_flight.pop(gi), name=f"ag_end_{ag_groups[gi][0]}")
        if gi > 0 and gi + 1 != AG_AHEAD:
            ag_forward(gi + 1)
        if gi + AG_AHEAD < len(ag_groups):
            ag_start(gi + AG_AHEAD)
        return [b[0] for b in bufs]

    small_in = [c, norm_g, cv_b_in, cv_dw_w, cv_dw_b, cv_ln_g, cv_ln_b, cv_b_out]
    pack1 = _pack(small_in)
    (pack1_all,) = _all_gather([pack1[None, None]], name="ag_small")
    parts = _unpack(pack1_all[0], [s.shape for s in small_in])
    c_all = parts[0].reshape(N_DEV, d)
    ng_full = jnp.moveaxis(parts[1], 0, 2).reshape(depth, 3, d)
    cvb_in_full = parts[2].reshape(1, 2 * e)
    dww_full = jnp.moveaxis(parts[3][:, 0], 0, 1).reshape(kw, e)
    dwb_full, cln_g_full, cln_b_full, cvb_out_full = [p.reshape(1, d) for p in parts[4:8]]

    c_pad = jnp.pad(c_all, ((0, 16 - N_DEV), (0, 0)))
    ada_b_loc = lax.dynamic_slice_in_dim(ada_b, me * cl, cl, axis=1).reshape(depth, 1, cl)
    mod_part = _ada_fwd(c_pad, ada_w, ada_b_loc, name="ada_fwd")[:, :N_DEV]
    (mod_all,) = _all_gather([_pack([mod_part])[None, None]], name="ag_mod")
    mod_all = _unpack(mod_all[0], [mod_part.shape])[0]
    mod_mine = lax.dynamic_index_in_dim(mod_all, me, axis=2, keepdims=False)
    mod = jnp.moveaxis(mod_mine, 0, 1).reshape(depth, 3, 3, 1, d)

    for gi in range(len(ag_groups)):
        ag_cast(gi, after=[mod_all])
        if gi < AG_AHEAD:
            ag_start(gi)

    ws = gm_ws[0]
    bsb = jnp.broadcast_to(gm_bs[0][:, :, None], (hn, l, e // hn))
    gm_g, gm_b = gm_ln_g, gm_ln_b

    saved = []
    xcur = xs
    next_group = 0
    for i in range(depth):
        for s in range(3):
            shift, scale, gate = mod[i, s, 0], mod[i, s, 1], mod[i, s, 2]
            g_norm = ng_full[i, s][None]
            tag = f"l{i}s{s}"
            h = _norm_mod(xcur, g_norm, scale, shift, name=f"norm_mod_{tag}")
            if s != 1:
                (w_in_blk,) = ag_take(next_group)
                fg, fu, act = _ffn_in(h, w_in_blk, 0, name=f"ffn_in_{tag}")
                w_out3 = ag_take(next_group + 1)[0].reshape(1, f, d)
                next_group += 2
                xnext, yv = _out_proj(act, w_out3, 0, xcur, gate, None, 0.5, name=f"ffn_out_{tag}")
                saved.append(dict(x=xcur, h=h, fg=fg, fu=fu, act=act, y=yv, w_in=w_in_blk, w_out=w_out3))
            elif i % 2 == 0:
                gm_in_blk, gm_out = ag_take(next_group)
                gm_out3 = gm_out.reshape(1, e, d)
                next_group += 1
                pre = _in_proj(h, gm_in_blk, None, name=f"gm_in_{tag}")
                uu, vn = _gm_act(pre, gm_g, gm_b, name=f"gm_act_{tag}")
                sg = _sgu_fwd(uu, vn, ws, bsb, name=f"sgu_fwd_{tag}")
                xnext, yv = _out_proj(sg, gm_out3, 0, xcur, gate, None, 1.0, name=f"gm_out_{tag}")
                saved.append(dict(x=xcur, h=h, pre=pre, u=uu, vn=vn, sg=sg, y=yv, w_in=gm_in_blk, w_out=gm_out3))
            else:
                cv_in_blk, cv_out = ag_take(next_group)
                cv_out3 = cv_out.reshape(1, e, d)
                next_group += 1
                p = _in_proj(h, cv_in_blk, cvb_in_full, name=f"cv_in_{tag}")
                yc = _dwconv_fwd(p, dww_full, dwb_full, name=f"dwconv_fwd_{tag}")
                ys = _cv_act(yc, cln_g_full, cln_b_full, name=f"cv_act_{tag}")
                xnext, yv = _out_proj(ys, cv_out3, 0, xcur, gate, cvb_out_full, 1.0, name=f"cv_out_{tag}")
                saved.append(dict(x=xcur, h=h, p=p, yc=yc, ys=ys, y=yv, w_in=cv_in_blk, w_out=cv_out3))
            xcur = xnext

    def bwd_head(i, s):
        return saved[3 * i + s]["y"], mod[i, s, 2], 0.5 if s != 1 else 1.0, s == 1 and i % 2 == 1

    sq, dx, d_final_g, dy, dgate = _final_loss(xcur, tgt, final_g[None], *bwd_head(depth - 1, 2)[:3], name="final_loss")
    dbout = None
    loss = lax.psum(0.5 / d * jnp.sum(sq), ("x", "y", "c"))

    dmod = [[[None] * 3 for _ in range(3)] for _ in range(depth)]
    d_norm_g = [[None] * 3 for _ in range(depth)]
    small = {}

    stacked = {
        "ffn_w_in": [a.reshape(4, d, bn) for a in (ffn_w_in, m_ffn_w_in, v_ffn_w_in)],
        "ffn_w_out": [a.reshape(4, fl, d) for a in (ffn_w_out, m_ffn_w_out, v_ffn_w_out)],
        "gm_w_in": [gm_w_in, m_gm_w_in, v_gm_w_in], "gm_w_out": [gm_w_out, m_gm_w_out, v_gm_w_out],
        "cv_w_in": [cv_w_in, m_cv_w_in, v_cv_w_in], "cv_w_out": [cv_w_out, m_cv_w_out, v_cv_w_out],
    }
    res_big = {}

    def rs_sibling(g4s, tag):
        return _rs_start(g4s, name=f"rs_start_{tag}"), tag

    def rs_chips(flight):
        st, tag = flight
        g4s, lands = _rs_mid(st, name=f"rs_mid_{tag}")
        sums = [_add_sibling(g4, land, core1, name=f"rs_add_{tag}_{k}") for k, (g4, land) in enumerate(zip(g4s, lands))]
        return _rs_start2(sums, name=f"rs_start2_{tag}"), tag

    def rs_finish(flight, targets, follow=()):
        st, tag = flight
        sums, recvs = _rs_end(st, name=f"rs_end_{tag}", follow=follow)
        for (pname, k), hsum, recv in zip(targets, sums, recvs):
            w_st, m_st, v_st = stacked[pname]
            res_big[pname] = _adamw_stacked(hsum, recv, chip1, w_st, m_st, v_st, k, res_big.get(pname),
                                            name=f"adamw_{pname}_{k}")

    pending = []
    last_sibling = None
    for i in reversed(range(depth)):
        for s in reversed(range(3)):
            sv = saved[3 * i + s]
            shift, scale, gate = mod[i, s, 0], mod[i, s, 1], mod[i, s, 2]
            g_norm = ng_full[i, s][None]
            tag = f"l{i}s{s}"
            last = i == 0 and s == 0
            if s != 1:
                widx = 2 * i + s // 2
                dgu, act = _ffn_da(dy, sv["w_out"], 0, sv["fg"], sv["fu"], name=f"ffn_da_{tag}"), sv["act"]
                if last:
                    g_in = _mm_tn(sv["h"], dgu, d, bn, True, name=f"ffn_dwin_{tag}").reshape(N_CHIP, 2, d, bn)
                    sib_in = rs_sibling([g_in], f"{tag}_in")
                    g_out = _mm_tn(act, dy[None], bn, d, False, name=f"ffn_dwout_{tag}").reshape(N_CHIP, 2, fl, d)
                    new_flights = [(rs_chips(sib_in), [("ffn_w_in", widx)])]
                    sib, targets = rs_sibling([g_out], f"{tag}_out"), [("ffn_w_out", widx)]
                else:
                    g_out = _mm_tn(act, dy[None], bn, d, False, name=f"ffn_dwout_{tag}").reshape(N_CHIP, 2, fl, d)
                    sib_out = rs_sibling([g_out], f"{tag}_out")
                    g_in = _mm_tn(sv["h"], dgu, d, bn, True, name=f"ffn_dwin_{tag}").reshape(N_CHIP, 2, d, bn)
                    new_flights = [(rs_chips(sib_out), [("ffn_w_out", widx)])]
                    sib, targets = rs_sibling([g_in], f"{tag}_in"), [("ffn_w_in", widx)]
                z3, w_blk = dgu, sv["w_in"]
            elif i % 2 == 0:
                ds = _mm_nt(dy, sv["w_out"], 0, name=f"gm_ds_{tag}")
                g_out = _mm_tn(sv["sg"], dy[None], _tile(e, 1024, V7X_LANES), d, False,
                               name=f"gm_dwout_{tag}").reshape(N_CHIP, 2, dl, d)
                du, dvn, dws, dbs = _sgu_bwd(ds, sv["u"], sv["vn"], ws, bsb, name=f"sgu_bwd_{tag}")
                dpre, dlng, dlnb = _gm_act_bwd(sv["pre"], du, dvn, gm_g, name=f"gm_act_bwd_{tag}")
                small["gm_ln_g"], small["gm_ln_b"] = dlng, dlnb
                small["gm_ws"], small["gm_bs"] = dws, dbs[:, :, 0]
                g_in = _mm_tn(sv["h"], dpre[None], d, el, True, name=f"gm_dwin_{tag}").reshape(N_CHIP, 2, d, el)
                targets, new_flights = [("gm_w_in", 0), ("gm_w_out", 0)], []
                sib = rs_sibling([g_in, g_out], tag)
                z3, w_blk = dpre[None], sv["w_in"]
            else:
                dys = _mm_nt(dy, sv["w_out"], 0, name=f"cv_dys_{tag}")
                g_out = _mm_tn(sv["ys"], dy[None], _tile(e, 1024, V7X_LANES), d, False,
                               name=f"cv_dwout_{tag}").reshape(N_CHIP, 2, dl, d)
                dyc, dlng, dlnb, ddwb = _cv_act_bwd(dys, sv["yc"], cln_g_full, cln_b_full, name=f"cv_act_bwd_{tag}")
                dp, ddww, dba, dbg = _dwconv_bwd(dyc, sv["p"], dww_full, name=f"dwconv_bwd_{tag}")
                small["cv_b_out"], small["cv_ln_g"], small["cv_ln_b"], small["cv_dw_b"] = dbout, dlng, dlnb, ddwb
                small["cv_dw_w"] = ddww
                small["cv_b_in"] = jnp.concatenate([dba, dbg], axis=1)
                g_in = _mm_tn(sv["h"], dp, d, el, True, name=f"cv_dwin_{tag}").reshape(N_CHIP, 2, d, el)
                targets, new_flights = [("cv_w_in", 0), ("cv_w_out", 0)], []
                sib = rs_sibling([g_in, g_out], tag)
                z3, w_blk = dp, sv["w_in"]
            nxt = None if last else bwd_head(*((i, s - 1) if s > 0 else (i - 1, 2)))
            res = _dh_norm_bwd(z3, w_blk, sv["x"], dx, g_norm, scale, nxt, name=f"dh_norm_bwd_{tag}")
            if last:
                (dx, dscale, dshift, dgn), last_sibling = res, (sib, targets)
            else:
                new_flights.append((rs_chips(sib), targets))
                dx, dy_next, dscale, dshift, dgn, dgate_next = res[:6]
                dbout_next = res[6] if nxt[3] else None
            dmod[i][s] = [dshift, dscale, dgate]
            d_norm_g[i][s] = dgn
            if not last:
                dy, dgate, dbout = dy_next, dgate_next, dbout_next
            for flight in pending:
                rs_finish(*flight)
            pending = new_flights
    grad_x = dx[None]

    dmod_mine = jnp.concatenate([v for per_l in dmod for per_s in per_l for v in per_s], axis=1)
    dng_mine = jnp.concatenate([v for per_l in d_norm_g for v in per_l], axis=1)
    small_out = [dmod_mine, dng_mine, small["gm_ln_g"], small["gm_ln_b"], small["gm_ws"], small["gm_bs"],
                 small["cv_b_in"], small["cv_dw_w"], small["cv_dw_b"], small["cv_ln_g"], small["cv_ln_b"],
                 small["cv_b_out"], d_final_g]
    shapes2 = [s.shape for s in small_out]
    (pack2_all,) = _all_gather([_pack(small_out, rows_align=256)[None, None]], name="ag_small_grads")
    _Seq.last = pack2_all
    pending.append((rs_chips(last_sibling[0]), last_sibling[1]))
    summed = _unpack(_sum_devices(pack2_all[0], name="sum_small_grads"), shapes2)
    dmod_all = _unpack(pack2_all[0], shapes2)[0].reshape(N_DEV, depth, 9 * d)

    def my_cols(full, width):
        return lax.dynamic_slice_in_dim(full, me * width, width, axis=full.ndim - 1)

    g_ada_b = summed[0].reshape(depth, 9 * d)
    g_norm_g = my_cols(summed[1].reshape(depth, 3, d), dl)
    g_small = {
        "ada_b": g_ada_b, "norm_g": g_norm_g,
        "gm_ln_g": summed[2], "gm_ln_b": summed[3], "gm_ws": summed[4][None], "gm_bs": summed[5][None],
        "cv_b_in": my_cols(summed[6], el), "cv_dw_w": my_cols(summed[7], dl)[None],
        "cv_dw_b": my_cols(summed[8], dl), "cv_ln_g": my_cols(summed[9], dl), "cv_ln_b": my_cols(summed[10], dl),
        "cv_b_out": my_cols(summed[11], dl), "final_g": summed[12].reshape(d),
    }

    dm_loc = jnp.moveaxis(my_cols(dmod_all, cl), 0, 1)
    dm_loc = jnp.pad(dm_loc, ((0, 0), (0, 16 - N_DEV), (0, 0)))
    res_ada_w = _ada_bwd(c_pad, dm_loc, ada_w, m_ada_w, v_ada_w, name="ada_bwd_adamw")

    def flat2(a):
        return a.reshape(-1, a.shape[-1])

    small_params = {
        "ada_b": (ada_b, m_ada_b, v_ada_b), "norm_g": (norm_g, m_norm_g, v_norm_g),
        "gm_ln_g": (gm_ln_g, m_gm_ln_g, v_gm_ln_g), "gm_ln_b": (gm_ln_b, m_gm_ln_b, v_gm_ln_b),
        "gm_ws": (gm_ws, m_gm_ws, v_gm_ws), "gm_bs": (gm_bs, m_gm_bs, v_gm_bs),
        "cv_b_in": (cv_b_in, m_cv_b_in, v_cv_b_in), "cv_dw_w": (cv_dw_w, m_cv_dw_w, v_cv_dw_w),
        "cv_dw_b": (cv_dw_b, m_cv_dw_b, v_cv_dw_b), "cv_ln_g": (cv_ln_g, m_cv_ln_g, v_cv_ln_g),
        "cv_ln_b": (cv_ln_b, m_cv_ln_b, v_cv_ln_b), "cv_b_out": (cv_b_out, m_cv_b_out, v_cv_b_out),
        "final_g": (final_g, m_final_g, v_final_g),
    }
    res_small = {}
    for key, (w, m, v) in small_params.items():
        g2 = flat2(g_small[key].reshape(w.shape)) if w.ndim > 1 else g_small[key].reshape(1, -1)
        w2, m2, v2 = [flat2(a) if a.ndim > 1 else a.reshape(1, -1) for a in (w, m, v)]
        res_small[key] = [o.reshape(w.shape) for o in _adamw([(g2[None], 0)], w2, m2, v2, name=f"adamw_{key}")]

    for flight in pending[:-1]:
        rs_finish(*flight)
    rs_finish(*pending[-1], follow=[res_ada_w[0]] + [r[0] for r in res_big.values()])

    def big(name, k):
        if name == "ada_w":
            return res_ada_w[k]
        return res_big[name][k].reshape(stacked_shape[name])

    stacked_shape = {"ffn_w_in": ffn_w_in.shape, "ffn_w_out": ffn_w_out.shape, "gm_w_in": gm_w_in.shape,
                     "gm_w_out": gm_w_out.shape, "cv_w_in": cv_w_in.shape, "cv_w_out": cv_w_out.shape}

    order = ["ada_w", "ada_b", "norm_g", "ffn_w_in", "ffn_w_out", "gm_w_in", "gm_ln_g", "gm_ln_b", "gm_ws", "gm_bs",
             "gm_w_out", "cv_w_in", "cv_b_in", "cv_dw_w", "cv_dw_b", "cv_ln_g", "cv_ln_b", "cv_w_out", "cv_b_out",
             "final_g"]
    outs = [loss, grad_x]
    for k in range(4):
        for name in order:
            outs.append(res_small[name][k] if name in res_small else big(name, k))
    return tuple(outs)
```

```python
import functools

import jax
import jax.numpy as jnp
from jax import lax
from jax.experimental import pallas as pl
from jax.experimental.pallas import tpu as pltpu

F32 = jnp.float32
BF = jnp.bfloat16
MESH = pl.DeviceIdType.MESH

N_DEV = 8
N_CHIP = 4
NORM_EPS = 1e-6
ADAM_LR = 0.001
ADAM_B1 = 0.9
ADAM_B2 = 0.999
ADAM_EPS = 1e-08
ADAM_WD = 0.01
ADAM_STEP = 10

V7X_SUBLANES = 8
V7X_LANES = 128
PACK_ALIGN = V7X_SUBLANES * V7X_LANES
V7X_VMEM_LIMIT = 56 * 1024 * 1024


def _tile(n, pref, align):
    if n <= pref:
        return n
    t = pref - pref % align
    while t >= align:
        if n % t == 0:
            return t
        t -= align
    return n


ANY = pl.BlockSpec(memory_space=pl.ANY)


class _Seq:
    last = None
    tokens = []


def _call(body, *, name, grid, in_specs, out_specs, out_shape, scratch=(), sem=None, prefetch=None, aliases=None,
          after=(), on_path=True):
    def run(*args):
        if on_path:
            tokens, _Seq.tokens = _Seq.tokens + list(after), []
        else:
            tokens = list(after)
        lead = 0 if prefetch is None else 1
        n_in, n_tok = lead + len(args), len(tokens)

        def wrapped(*refs):
            body(*refs[:n_in], *refs[n_in + n_tok:])

        specs = list(in_specs) + [ANY] * n_tok
        params = pltpu.CompilerParams(dimension_semantics=sem, vmem_limit_bytes=V7X_VMEM_LIMIT)
        if prefetch is None:
            res = pl.pallas_call(wrapped, out_shape=out_shape, grid=grid, in_specs=specs, out_specs=out_specs,
                                 scratch_shapes=scratch, name=name, compiler_params=params,
                                 input_output_aliases=aliases or {})(*args, *tokens)
        else:
            grid_spec = pltpu.PrefetchScalarGridSpec(num_scalar_prefetch=1, grid=grid, in_specs=specs,
                                                     out_specs=out_specs, scratch_shapes=scratch)
            res = pl.pallas_call(wrapped, out_shape=out_shape, grid_spec=grid_spec, name=name,
                                 compiler_params=params, input_output_aliases=aliases or {})(prefetch, *args, *tokens)
        if on_path:
            _Seq.last = res[0] if isinstance(res, (list, tuple)) else res
        return res
    return run


def _sds(shape, dtype):
    return jax.ShapeDtypeStruct(tuple(shape), dtype)


def _sigmoid(v):
    return 1.0 / (1.0 + jnp.exp(-v))


def _normal_cdf_pdf(v):
    a = jnp.abs(v) * 0.7071067811865476
    t = 1.0 / (1.0 + 0.3275911 * a)
    poly = t * (0.254829592 + t * (-0.284496736 + t * (1.421413741 + t * (-1.453152027 + t * 1.061405429))))
    e = jnp.exp(-0.5 * v * v)
    half_erf = 0.5 - 0.5 * poly * e
    return 0.5 + jnp.where(v < 0, -half_erf, half_erf), 0.3989422804014327 * e


def _gelu(v):
    return v * _normal_cdf_pdf(v)[0]


def _fold_rows(val):
    rows, w = val.shape
    return val.reshape(rows // V7X_SUBLANES, V7X_SUBLANES, w).sum(axis=0)


def _rowwise(fn, name, rows_in, vecs_in, rows_out, acc_widths, tm=256):
    t = rows_in[0].shape[0]
    tm = _tile(t, tm, V7X_SUBLANES)
    steps = t // tm
    nr, nv, no, na = len(rows_in), len(vecs_in), len(rows_out), len(acc_widths)

    def body(*refs):
        rin, vin = refs[:nr], refs[nr:nr + nv]
        rout = refs[nr + nv:nr + nv + no]
        aout = refs[nr + nv + no:nr + nv + no + na]
        accs = refs[nr + nv + no + na:]
        i = pl.program_id(0)
        outs, acc_vals = fn(*[r[...] for r in rin], *[v[...] for v in vin])
        for r, o in zip(rout, outs):
            r[...] = o.astype(r.dtype)
        if na:
            @pl.when(i == 0)
            def _():
                for a in accs:
                    a[...] = jnp.zeros_like(a)

            for a, val in zip(accs, acc_vals):
                a[...] += _fold_rows(val)

            @pl.when(i == steps - 1)
            def _():
                for o, a in zip(aout, accs):
                    o[...] = jnp.sum(a[...], axis=0, keepdims=True)

    in_specs = [pl.BlockSpec((tm, r.shape[1]), lambda i: (i, 0)) for r in rows_in]
    in_specs += [pl.BlockSpec(v.shape, functools.partial(lambda nd, i: (0,) * nd, v.ndim)) for v in vecs_in]
    out_specs = [pl.BlockSpec((tm, r.shape[1]), lambda i: (i, 0)) for r in rows_out]
    out_specs += [pl.BlockSpec((1, w), lambda i: (0, 0)) for w in acc_widths]
    out_shape = list(rows_out) + [_sds((1, w), F32) for w in acc_widths]
    scratch = [pltpu.VMEM((V7X_SUBLANES, w), F32) for w in acc_widths]
    res = _call(body, name=name, grid=(steps,), in_specs=in_specs, out_specs=out_specs, out_shape=out_shape,
                scratch=scratch, sem=("arbitrary",) if na else ("parallel",))(*rows_in, *vecs_in)
    return res[:no], res[no:]


def _norm_mod(x, g, scale, shift, name):
    def fn(xv, gv, sc, sh):
        r = lax.rsqrt(jnp.mean(xv * xv, axis=-1, keepdims=True) + NORM_EPS)
        return ((xv * r * gv) * (1.0 + sc) + sh,), ()
    (h,), _ = _rowwise(fn, name, [x], [g, scale, shift], [_sds(x.shape, BF)], [], tm=512)
    return h


def _final_loss(x, target, g, y, gate, coef, name):
    d = x.shape[1]

    def fn(xv, tv, yv, gv, gt):
        r = lax.rsqrt(jnp.mean(xv * xv, axis=-1, keepdims=True) + NORM_EPS)
        xhat = xv * r
        err = xhat * gv - tv
        dl = err * (1.0 / d)
        dxhat = dl * gv
        dx = r * (dxhat - xhat * jnp.mean(dxhat * xhat, axis=-1, keepdims=True))
        return (dx, (coef * gt) * dx), (err * err, dl * xhat, coef * dx * yv.astype(F32))
    (dx, dy), (sq, dg, dgate) = _rowwise(fn, name, [x, target, y], [g, gate],
                                         [_sds(x.shape, F32), _sds(x.shape, BF)], [d, d, d])
    return sq, dx, dg, dy, dgate


def _gm_act(pre, ln_g, ln_b, name):
    e = pre.shape[1] // 2

    def fn(pv, gv, bv):
        p = pv.astype(F32)
        u = _gelu(p[:, :e])
        v = _gelu(p[:, e:])
        mu = jnp.mean(v, axis=-1, keepdims=True)
        vc = v - mu
        rstd = lax.rsqrt(jnp.mean(vc * vc, axis=-1, keepdims=True) + NORM_EPS)
        return (u, vc * rstd * gv + bv), ()
    t = pre.shape[0]
    (u, vn), _ = _rowwise(fn, name, [pre], [ln_g, ln_b], [_sds((t, e), BF), _sds((t, e), BF)], [])
    return u, vn


def _gm_act_bwd(pre, du, dvn, ln_g, name):
    e = pre.shape[1] // 2

    def fn(pv, duv, dvv, gv):
        p = pv.astype(F32)
        pu, pvv = p[:, :e], p[:, e:]
        cdf_u, pdf_u = _normal_cdf_pdf(pu)
        cdf_v, pdf_v = _normal_cdf_pdf(pvv)
        v = pvv * cdf_v
        mu = jnp.mean(v, axis=-1, keepdims=True)
        vc = v - mu
        rstd = lax.rsqrt(jnp.mean(vc * vc, axis=-1, keepdims=True) + NORM_EPS)
        vhat = vc * rstd
        dvn_f = dvv.astype(F32)
        dvhat = dvn_f * gv
        dv = rstd * (dvhat - jnp.mean(dvhat, axis=-1, keepdims=True)
                     - vhat * jnp.mean(dvhat * vhat, axis=-1, keepdims=True))
        dpu = duv.astype(F32) * (cdf_u + pu * pdf_u)
        dpv = dv * (cdf_v + pvv * pdf_v)
        return (jnp.concatenate([dpu, dpv], axis=1),), (dvn_f * vhat, dvn_f)
    (dpre,), (dg, db) = _rowwise(fn, name, [pre, du, dvn], [ln_g], [_sds(pre.shape, BF)], [e, e], tm=128)
    return dpre, dg, db


def _cv_act(yc, ln_g, ln_b, name):
    def fn(yv, gv, bv):
        mu = jnp.mean(yv, axis=-1, keepdims=True)
        c = yv - mu
        rstd = lax.rsqrt(jnp.mean(c * c, axis=-1, keepdims=True) + NORM_EPS)
        yn = c * rstd * gv + bv
        return (yn * _sigmoid(yn),), ()
    (ys,), _ = _rowwise(fn, name, [yc], [ln_g, ln_b], [_sds(yc.shape, BF)], [])
    return ys


def _cv_act_bwd(dys, yc, ln_g, ln_b, name):
    def fn(dv, yv, gv, bv):
        mu = jnp.mean(yv, axis=-1, keepdims=True)
        c = yv - mu
        rstd = lax.rsqrt(jnp.mean(c * c, axis=-1, keepdims=True) + NORM_EPS)
        yhat = c * rstd
        yn = yhat * gv + bv
        sig = _sigmoid(yn)
        dyn = dv.astype(F32) * (sig * (1.0 + yn * (1.0 - sig)))
        dyhat = dyn * gv
        dyc = rstd * (dyhat - jnp.mean(dyhat, axis=-1, keepdims=True)
                      - yhat * jnp.mean(dyhat * yhat, axis=-1, keepdims=True))
        return (dyc,), (dyn * yhat, dyn, dyc)
    cw = yc.shape[1]
    (dyc,), (dg, db, dbias) = _rowwise(fn, name, [dys, yc], [ln_g, ln_b], [_sds(yc.shape, F32)], [cw, cw, cw])
    return dyc, dg, db, dbias


MM_ROWS = 1024
MM_SEG_ROWS = 256


def _ffn_in(h, w_blk, blk0, name):
    t, d = h.shape
    bn = w_blk.shape[2]
    half = N_DEV // 2
    f = half * bn
    tm = _tile(t, MM_ROWS, V7X_SUBLANES)
    seg_rows = _tile(tm, MM_SEG_ROWS, 2 * V7X_SUBLANES)

    def body(h_ref, wg_ref, wu_ref, dg_ref, du_ref, a_ref):
        for seg in range(tm // seg_rows):
            rows = pl.ds(seg * seg_rows, seg_rows)
            hv = h_ref[rows, :]
            g = jnp.dot(hv, wg_ref[...], preferred_element_type=F32)
            u = jnp.dot(hv, wu_ref[...], preferred_element_type=F32)
            sig = _sigmoid(g)
            sl = g * sig
            dg_ref[rows, :] = (u * (sig * (1.0 + g * (1.0 - sig)))).astype(BF)
            du_ref[rows, :] = sl.astype(BF)
            a_ref[rows, :] = (sl * u).astype(BF)

    out = _sds((t, f), BF)
    tile = pl.BlockSpec((tm, bn), lambda j, i: (i, j))
    return _call(
        body, name=name, grid=(half, t // tm),
        in_specs=[pl.BlockSpec((tm, d), lambda j, i: (i, 0)),
                  pl.BlockSpec((None, d, bn), lambda j, i: (blk0 + j, 0, 0)),
                  pl.BlockSpec((None, d, bn), lambda j, i: (blk0 + half + j, 0, 0))],
        out_specs=[tile, tile, tile], out_shape=[out, out, out], sem=("parallel", "parallel"))(h, w_blk, w_blk)


def _in_proj(h, w_blk, bias, name):
    t, d = h.shape
    bn = w_blk.shape[2]
    tm = _tile(t, 1024, V7X_SUBLANES)

    def body(*refs):
        if bias is None:
            h_ref, w_ref, o_ref = refs
            o_ref[...] = jnp.dot(h_ref[...], w_ref[...], preferred_element_type=F32).astype(BF)
        else:
            h_ref, w_ref, b_ref, o_ref = refs
            o_ref[...] = (jnp.dot(h_ref[...], w_ref[...], preferred_element_type=F32) + b_ref[...]).astype(BF)

    in_specs = [pl.BlockSpec((tm, d), lambda j, i: (i, 0)), pl.BlockSpec((None, d, bn), lambda j, i: (j, 0, 0))]
    args = [h, w_blk]
    if bias is not None:
        in_specs.append(pl.BlockSpec((1, bn), lambda j, i: (0, j)))
        args.append(bias)
    return _call(body, name=name, grid=(N_DEV, t // tm), in_specs=in_specs,
                 out_specs=pl.BlockSpec((tm, bn), lambda j, i: (i, j)), out_shape=_sds((t, N_DEV * bn), BF),
                 sem=("parallel", "parallel"))(*args)


def _out_proj(a, w3, widx, x, gate, bias, coef, name):
    t, k = a.shape
    d = w3.shape[2]
    tm = _tile(t, MM_ROWS, V7X_SUBLANES)
    tn = _tile(d, 512, V7X_LANES)

    def body(*refs):
        if bias is None:
            a_ref, w_ref, x_ref, g_ref, xo_ref, y_ref = refs
            y = jnp.dot(a_ref[...], w_ref[...], preferred_element_type=F32)
        else:
            a_ref, w_ref, x_ref, g_ref, b_ref, xo_ref, y_ref = refs
            y = jnp.dot(a_ref[...], w_ref[...], preferred_element_type=F32) + b_ref[...]
        y_ref[...] = y.astype(BF)
        xo_ref[...] = x_ref[...] + (coef * g_ref[...]) * y

    tile = pl.BlockSpec((tm, tn), lambda j, i: (i, j))
    vec = pl.BlockSpec((1, tn), lambda j, i: (0, j))
    in_specs = [pl.BlockSpec((tm, k), lambda j, i: (i, 0)),
                pl.BlockSpec((None, k, tn), lambda j, i: (widx, 0, j)), tile, vec]
    args = [a, w3, x, gate]
    if bias is not None:
        in_specs.append(vec)
        args.append(bias)
    return _call(body, name=name, grid=(d // tn, t // tm), in_specs=in_specs, out_specs=[tile, tile],
                 out_shape=[_sds((t, d), F32), _sds((t, d), BF)], sem=("parallel", "parallel"))(*args)


def _ffn_da(dy, w3, widx, fg, fu, name):
    t, d = dy.shape
    f = w3.shape[1]
    bn = f // (N_DEV // 2)
    tm = _tile(t, MM_ROWS, V7X_SUBLANES)
    seg_rows = _tile(tm, MM_SEG_ROWS, 2 * V7X_SUBLANES)

    def body(dy_ref, w_ref, fg_ref, fu_ref, dgu_ref):
        for seg in range(tm // seg_rows):
            rows = pl.ds(seg * seg_rows, seg_rows)
            da = lax.dot_general(dy_ref[rows, :], w_ref[...], (((1,), (1,)), ((), ())), preferred_element_type=F32)
            dgu_ref[0, rows, :] = (da * fg_ref[rows, :].astype(F32)).astype(BF)
            dgu_ref[1, rows, :] = (da * fu_ref[rows, :].astype(F32)).astype(BF)

    tile = pl.BlockSpec((tm, bn), lambda j, i: (i, j))
    return _call(
        body, name=name, grid=(f // bn, t // tm),
        in_specs=[pl.BlockSpec((tm, d), lambda j, i: (i, 0)),
                  pl.BlockSpec((None, bn, d), lambda j, i: (widx, j, 0)), tile, tile],
        out_specs=pl.BlockSpec((2, tm, bn), lambda j, i: (0, i, j)),
        out_shape=_sds((2, t, f), BF), sem=("parallel", "parallel"))(dy, w3, fg, fu)


def _mm_nt(dy, w3, widx, name):
    t, k = dy.shape
    n = w3.shape[1]
    tm = _tile(t, MM_ROWS, V7X_SUBLANES)
    tn = _tile(n, 1024, V7X_LANES)

    def body(dy_ref, w_ref, o_ref):
        o_ref[...] = lax.dot_general(dy_ref[...], w_ref[...], (((1,), (1,)), ((), ())),
                                     preferred_element_type=F32).astype(BF)

    return _call(body, name=name, grid=(n // tn, t // tm),
                 in_specs=[pl.BlockSpec((tm, k), lambda j, i: (i, 0)),
                           pl.BlockSpec((None, tn, k), lambda j, i: (widx, j, 0))],
                 out_specs=pl.BlockSpec((tm, tn), lambda j, i: (i, j)), out_shape=_sds((t, n), BF),
                 sem=("parallel", "parallel"))(dy, w3)


NORM_BWD_ROWS = 128


def _dh_norm_bwd(z3, w_blk, x, dxp, g, scale, nxt, name):
    lead, t, _ = z3.shape
    d, bn = w_blk.shape[1], w_blk.shape[2]
    per = N_DEV // lead
    tm = _tile(t, 512, NORM_BWD_ROWS)
    ni = t // tm
    n_in = 6 if nxt is None else 8
    coef, colsum = (None, False) if nxt is None else nxt[2:]
    n_acc = 3 if nxt is None else (5 if colsum else 4)
    n_rows = 1 if nxt is None else 2

    def body(*refs):
        z_ref, w_ref, x_ref, dp_ref, g_ref, sc_ref = refs[:6]
        row_outs = refs[n_in:n_in + n_rows]
        vec_outs = refs[n_in + n_rows:n_in + n_rows + n_acc]
        acc_ref, accs = refs[n_in + n_rows + n_acc], refs[n_in + n_rows + n_acc + 1:]
        i, k = pl.program_id(0), pl.program_id(1)

        @pl.when(k == 0)
        def _():
            acc_ref[...] = jnp.zeros_like(acc_ref)

        @pl.when((i == 0) & (k == 0))
        def _():
            for a in accs:
                a[...] = jnp.zeros_like(a)

        acc_ref[...] += lax.dot_general(z_ref[...], w_ref[...], (((1,), (1,)), ((), ())),
                                        preferred_element_type=F32)

        @pl.when(k == N_DEV - 1)
        def _():
            gv, sc = g_ref[...], sc_ref[...]

            def chunk(ci, carry):
                rows = pl.ds(pl.multiple_of(ci * NORM_BWD_ROWS, NORM_BWD_ROWS), NORM_BWD_ROWS)
                dh, xv = acc_ref[rows, :], x_ref[rows, :]
                r = lax.rsqrt(jnp.mean(xv * xv, axis=-1, keepdims=True) + NORM_EPS)
                xhat = xv * r
                dn = dh * (1.0 + sc)
                dxhat = dn * gv
                dx = r * (dxhat - xhat * jnp.mean(dxhat * xhat, axis=-1, keepdims=True)) + dp_ref[rows, :]
                row_outs[0][rows, :] = dx
                accs[0][...] += _fold_rows(dh * (xhat * gv))
                accs[1][...] += _fold_rows(dh)
                accs[2][...] += _fold_rows(dn * xhat)
                if nxt is not None:
                    y_ref, gate_ref = refs[6], refs[7]
                    dy = (coef * gate_ref[...]) * dx
                    row_outs[1][rows, :] = dy.astype(BF)
                    accs[3][...] += _fold_rows(coef * dx * y_ref[rows, :].astype(F32))
                    if colsum:
                        accs[4][...] += _fold_rows(dy)
                return carry
            lax.fori_loop(0, tm // NORM_BWD_ROWS, chunk, 0)

        @pl.when((i == ni - 1) & (k == N_DEV - 1))
        def _():
            for o, a in zip(vec_outs, accs):
                o[...] = jnp.sum(a[...], axis=0, keepdims=True)

    rows = pl.BlockSpec((tm, d), lambda i, k: (i, 0))
    vec = pl.BlockSpec((1, d), lambda i, k: (0, 0))
    in_specs = [pl.BlockSpec((None, tm, bn), lambda i, k: (k // per, i, k % per)),
                pl.BlockSpec((None, d, bn), lambda i, k: (k, 0, 0)), rows, rows, vec, vec]
    args = [z3, w_blk, x, dxp, g, scale]
    out_specs, out_shape = [rows], [_sds((t, d), F32)]
    if nxt is not None:
        in_specs += [rows, vec]
        args += [nxt[0], nxt[1]]
        out_specs.append(rows)
        out_shape.append(_sds((t, d), BF))
    out_specs += [vec] * n_acc
    out_shape += [_sds((1, d), F32)] * n_acc
    return _call(body, name=name, grid=(ni, N_DEV), in_specs=in_specs, out_specs=out_specs, out_shape=out_shape,
                 scratch=[pltpu.VMEM((tm, d), F32)] + [pltpu.VMEM((V7X_SUBLANES, d), F32)] * n_acc,
                 sem=("arbitrary", "arbitrary"))(*args)


def _mm_tn(a, b3, ta, tb, blocked, name):
    t, ka = a.shape
    lead, _, w = b3.shape
    per = w // tb
    nj = lead * per
    tk = t
    while tk > 512 and 4 * tk * (ta + tb) + 8 * ta * tb > V7X_VMEM_LIMIT * 3 // 4:
        tk //= 2
    tk = _tile(t, tk, V7X_SUBLANES)
    nk = t // tk

    def body(a_ref, b_ref, o_ref, acc_ref):
        k = pl.program_id(2)

        @pl.when(k == 0)
        def _():
            acc_ref[...] = jnp.zeros_like(acc_ref)

        acc_ref[...] += lax.dot_general(a_ref[...], b_ref[...], (((0,), (0,)), ((), ())),
                                        preferred_element_type=F32)

        @pl.when(k == nk - 1)
        def _():
            o_ref[...] = acc_ref[...].astype(BF)

    if blocked:
        out_shape = _sds((nj, ka, tb), BF)
        out_spec = pl.BlockSpec((None, ta, tb), lambda i, j, k: (j, i, 0))
    else:
        out_shape = _sds((1, ka, w), BF)
        out_spec = pl.BlockSpec((None, ta, tb), lambda i, j, k: (0, i, j))
    return _call(body, name=name, grid=(ka // ta, nj, nk),
                 in_specs=[pl.BlockSpec((tk, ta), lambda i, j, k: (k, i)),
                           pl.BlockSpec((None, tk, tb), lambda i, j, k: (j // per, k, j % per))],
                 out_specs=out_spec, out_shape=out_shape, scratch=[pltpu.VMEM((ta, tb), F32)],
                 sem=("parallel", "parallel", "arbitrary"))(a, b3)


def _causal(ws):
    l = ws.shape[0]
    row = lax.broadcasted_iota(jnp.int32, (l, l), 0)
    col = lax.broadcasted_iota(jnp.int32, (l, l), 1)
    return jnp.where(col <= row, ws, 0.0)


def _sgu_fwd(u, vn, ws, bsb, name):
    t, e = u.shape
    hn, l, _ = ws.shape
    dh = e // hn
    nc = t // l

    def body(u_ref, v_ref, ws_ref, bs_ref, s_ref):
        wsc = _causal(ws_ref[...]).astype(BF)
        bias = bs_ref[...]

        def chunk(c, carry):
            rows = pl.ds(pl.multiple_of(c * l, l), l)
            vo = jnp.dot(wsc, v_ref[rows, :], preferred_element_type=F32) + bias
            s_ref[rows, :] = (u_ref[rows, :].astype(F32) * vo).astype(BF)
            return carry
        lax.fori_loop(0, nc, chunk, 0)

    col = pl.BlockSpec((t, dh), lambda h: (0, h))
    return _call(body, name=name, grid=(hn,),
                 in_specs=[col, col, pl.BlockSpec((None, l, l), lambda h: (h, 0, 0)),
                           pl.BlockSpec((None, l, dh), lambda h: (h, 0, 0))],
                 out_specs=col, out_shape=_sds((t, e), BF), sem=("parallel",))(u, vn, ws, bsb)


def _sgu_bwd(ds, u, vn, ws, bsb, name):
    t, e = u.shape
    hn, l, _ = ws.shape
    dh = e // hn
    nc = t // l

    def body(ds_ref, u_ref, v_ref, ws_ref, bs_ref, du_ref, dv_ref, dws_ref, dbs_ref, accw_ref, accb_ref):
        wsc = _causal(ws_ref[...]).astype(BF)
        bias = bs_ref[...]
        accw_ref[...] = jnp.zeros_like(accw_ref)
        accb_ref[...] = jnp.zeros_like(accb_ref)

        def chunk(c, carry):
            rows = pl.ds(pl.multiple_of(c * l, l), l)
            vc = v_ref[rows, :]
            dsv = ds_ref[rows, :].astype(F32)
            vo = jnp.dot(wsc, vc, preferred_element_type=F32) + bias
            du_ref[rows, :] = (dsv * vo).astype(BF)
            dvo = dsv * u_ref[rows, :].astype(F32)
            dvo_b = dvo.astype(BF)
            accb_ref[...] += dvo
            accw_ref[...] += lax.dot_general(dvo_b, vc, (((1,), (1,)), ((), ())), preferred_element_type=F32)
            dv_ref[rows, :] = lax.dot_general(wsc, dvo_b, (((0,), (0,)), ((), ())),
                                              preferred_element_type=F32).astype(BF)
            return carry
        lax.fori_loop(0, nc, chunk, 0)
        dws_ref[...] = _causal(accw_ref[...])
        dbs_ref[...] = jnp.broadcast_to(jnp.sum(accb_ref[...], axis=1, keepdims=True), (l, dh))

    col = pl.BlockSpec((t, dh), lambda h: (0, h))
    return _call(body, name=name, grid=(hn,),
                 in_specs=[col, col, col, pl.BlockSpec((None, l, l), lambda h: (h, 0, 0)),
                           pl.BlockSpec((None, l, dh), lambda h: (h, 0, 0))],
                 out_specs=[col, col, pl.BlockSpec((None, l, l), lambda h: (h, 0, 0)),
                            pl.BlockSpec((None, l, dh), lambda h: (h, 0, 0))],
                 out_shape=[_sds((t, e), BF), _sds((t, e), BF), _sds((hn, l, l), F32), _sds((hn, l, dh), F32)],
                 scratch=[pltpu.VMEM((l, l), F32), pltpu.VMEM((l, dh), F32)],
                 sem=("parallel",))(ds, u, vn, ws, bsb)


AG_AHEAD = 3

CONV_HALO = 32
CONV_ROWS = 64
CONV_LANES = 256


def _shifted_windows(win_ref, sh_ref, rows):
    for b in range(1, V7X_SUBLANES):
        sh_ref[b - 1, 0:rows, :] = win_ref[b:b + rows, :]


def _window_rows(win_ref, sh_ref, shift, r0, rows):
    a, b = divmod(shift, V7X_SUBLANES)
    start = pl.multiple_of(r0 + V7X_SUBLANES * a, V7X_SUBLANES)
    if b == 0:
        return win_ref[pl.ds(start, rows), :]
    return sh_ref[b - 1, pl.ds(start, rows), :]


def _dwconv_fwd(p, dw_w, dw_b, name):
    t, c2 = p.shape
    cw = c2 // 2
    kw = dw_w.shape[0]
    cb = _tile(cw, CONV_LANES, V7X_LANES)
    ncb = cw // cb
    tm = _tile(t, 512, CONV_ROWS)
    off = CONV_HALO - (kw - 1)

    def body(a_ref, g_ref, ap_ref, gp_ref, w_ref, b_ref, o_ref, win_ref, sh_ref):
        i = pl.program_id(1)
        prev = ap_ref[...].astype(F32) * _sigmoid(gp_ref[...].astype(F32))
        win_ref[0:CONV_HALO, :] = jnp.where(i > 0, prev, 0.0)
        win_ref[CONV_HALO:, :] = a_ref[...].astype(F32) * _sigmoid(g_ref[...].astype(F32))
        _shifted_windows(win_ref, sh_ref, tm + CONV_HALO - V7X_SUBLANES)

        def chunk(ci, carry):
            r0 = ci * CONV_ROWS
            acc = jnp.zeros((CONV_ROWS, cb), F32) + b_ref[...]
            for k in range(kw):
                acc = acc + w_ref[k:k + 1, :] * _window_rows(win_ref, sh_ref, off + k, r0, CONV_ROWS)
            o_ref[pl.ds(pl.multiple_of(r0, CONV_ROWS), CONV_ROWS), :] = acc
            return carry
        lax.fori_loop(0, tm // CONV_ROWS, chunk, 0)

    hpt = tm // CONV_HALO
    cur_a = pl.BlockSpec((tm, cb), lambda j, i: (i, j))
    cur_g = pl.BlockSpec((tm, cb), lambda j, i: (i, ncb + j))
    prev_a = pl.BlockSpec((CONV_HALO, cb), lambda j, i: (jnp.maximum(i * hpt - 1, 0), j))
    prev_g = pl.BlockSpec((CONV_HALO, cb), lambda j, i: (jnp.maximum(i * hpt - 1, 0), ncb + j))
    return _call(body, name=name, grid=(ncb, t // tm),
                 in_specs=[cur_a, cur_g, prev_a, prev_g, pl.BlockSpec((kw, cb), lambda j, i: (0, j)),
                           pl.BlockSpec((1, cb), lambda j, i: (0, j))],
                 out_specs=pl.BlockSpec((tm, cb), lambda j, i: (i, j)), out_shape=_sds((t, cw), F32),
                 scratch=[pltpu.VMEM((tm + CONV_HALO, cb), F32),
                          pltpu.VMEM((V7X_SUBLANES - 1, tm + CONV_HALO - V7X_SUBLANES, cb), F32)],
                 sem=("parallel", "parallel"))(p, p, p, p, dw_w, dw_b)


def _dwconv_bwd(dyc, p, dw_w, name):
    t, c2 = p.shape
    cw = c2 // 2
    kw = dw_w.shape[0]
    cb = _tile(cw, CONV_LANES, V7X_LANES)
    ncb = cw // cb
    tm = _tile(t, 512, CONV_ROWS)
    nt = t // tm
    off = CONV_HALO - (kw - 1)
    kpad = -(-kw // V7X_SUBLANES) * V7X_SUBLANES
    sh_rows = tm + CONV_HALO - V7X_SUBLANES

    def body(d_ref, dn_ref, a_ref, g_ref, ap_ref, gp_ref, w_ref,
             dp_ref, dw_ref, dba_ref, dbg_ref, dwin_ref, ywin_ref, dsh_ref, ysh_ref, accw_ref, acca_ref, accg_ref):
        i = pl.program_id(1)

        @pl.when(i == 0)
        def _():
            accw_ref[...] = jnp.zeros_like(accw_ref)
            acca_ref[...] = jnp.zeros_like(acca_ref)
            accg_ref[...] = jnp.zeros_like(accg_ref)

        prev = ap_ref[...].astype(F32) * _sigmoid(gp_ref[...].astype(F32))
        ywin_ref[0:CONV_HALO, :] = jnp.where(i > 0, prev, 0.0)
        ywin_ref[CONV_HALO:, :] = a_ref[...].astype(F32) * _sigmoid(g_ref[...].astype(F32))
        dwin_ref[0:tm, :] = d_ref[...]
        dwin_ref[tm:, :] = jnp.where(i < nt - 1, dn_ref[...], 0.0)
        _shifted_windows(ywin_ref, ysh_ref, sh_rows)
        _shifted_windows(dwin_ref, dsh_ref, sh_rows)

        def chunk(ci, carry):
            r0 = ci * CONV_ROWS
            rows = pl.ds(pl.multiple_of(r0, CONV_ROWS), CONV_ROWS)
            dcur = d_ref[rows, :]
            dyg = jnp.zeros((CONV_ROWS, cb), F32)
            for k in range(kw):
                dyg = dyg + w_ref[k:k + 1, :] * _window_rows(dwin_ref, dsh_ref, kw - 1 - k, r0, CONV_ROWS)
                accw_ref[k] += _fold_rows(dcur * _window_rows(ywin_ref, ysh_ref, off + k, r0, CONV_ROWS))
            av = a_ref[rows, :].astype(F32)
            sig = _sigmoid(g_ref[rows, :].astype(F32))
            da = dyg * sig
            dg = dyg * av * sig * (1.0 - sig)
            dp_ref[0, rows, :] = da.astype(BF)
            dp_ref[1, rows, :] = dg.astype(BF)
            acca_ref[...] += _fold_rows(da)
            accg_ref[...] += _fold_rows(dg)
            return carry
        lax.fori_loop(0, tm // CONV_ROWS, chunk, 0)

        @pl.when(i == nt - 1)
        def _():
            dw_ref[...] = jnp.sum(accw_ref[...], axis=1)
            dba_ref[...] = jnp.sum(acca_ref[...], axis=0, keepdims=True)
            dbg_ref[...] = jnp.sum(accg_ref[...], axis=0, keepdims=True)

    hpt = tm // CONV_HALO
    last_halo = t // CONV_HALO - 1
    tile = pl.BlockSpec((tm, cb), lambda j, i: (i, j))
    cur_g = pl.BlockSpec((tm, cb), lambda j, i: (i, ncb + j))
    nxt = pl.BlockSpec((CONV_HALO, cb), lambda j, i: (jnp.minimum((i + 1) * hpt, last_halo), j))
    prev_a = pl.BlockSpec((CONV_HALO, cb), lambda j, i: (jnp.maximum(i * hpt - 1, 0), j))
    prev_g = pl.BlockSpec((CONV_HALO, cb), lambda j, i: (jnp.maximum(i * hpt - 1, 0), ncb + j))
    vec = pl.BlockSpec((1, cb), lambda j, i: (0, j))
    dp, ddw, dba, dbg = _call(
        body, name=name, grid=(ncb, nt),
        in_specs=[tile, nxt, tile, cur_g, prev_a, prev_g, pl.BlockSpec((kw, cb), lambda j, i: (0, j))],
        out_specs=[pl.BlockSpec((2, tm, cb), lambda j, i: (0, i, j)), pl.BlockSpec((kpad, cb), lambda j, i: (0, j)),
                   vec, vec],
        out_shape=[_sds((2, t, cw), BF), _sds((kpad, cw), F32), _sds((1, cw), F32), _sds((1, cw), F32)],
        scratch=[pltpu.VMEM((tm + CONV_HALO, cb), F32), pltpu.VMEM((tm + CONV_HALO, cb), F32),
                 pltpu.VMEM((V7X_SUBLANES - 1, sh_rows, cb), F32), pltpu.VMEM((V7X_SUBLANES - 1, sh_rows, cb), F32),
                 pltpu.VMEM((kpad, V7X_SUBLANES, cb), F32), pltpu.VMEM((V7X_SUBLANES, cb), F32),
                 pltpu.VMEM((V7X_SUBLANES, cb), F32)],
        sem=("parallel", "arbitrary"))(dyc, dyc, p, p, p, p, dw_w)
    return dp, ddw[:kw], dba, dbg


def _adam_math(g, w, m, v):
    m2 = ADAM_B1 * m + (1.0 - ADAM_B1) * g
    v2 = ADAM_B2 * v + (1.0 - ADAM_B2) * (g * g)
    m_hat = m2 / (1.0 - ADAM_B1 ** ADAM_STEP)
    v_hat = v2 / (1.0 - ADAM_B2 ** ADAM_STEP)
    delta = -ADAM_LR * (m_hat / (jnp.sqrt(v_hat) + ADAM_EPS) + ADAM_WD * w)
    return delta, m2, v2


def _adamw(g_parts, w, m, v, name):
    r, c = w.shape
    tr = _tile(r, 256, V7X_SUBLANES)
    ng = len(g_parts)

    def body(*refs):
        g = refs[0][...].astype(F32)
        for s in refs[1:ng]:
            g = g + s[...].astype(F32)
        w_ref, m_ref, v_ref, go_ref, d_ref, mo_ref, vo_ref = refs[ng:]
        delta, m2, v2 = _adam_math(g, w_ref[...], m_ref[...], v_ref[...])
        go_ref[...] = g
        d_ref[...] = delta
        mo_ref[...] = m2
        vo_ref[...] = v2

    tile = pl.BlockSpec((tr, c), lambda i: (i, 0))
    in_specs = [pl.BlockSpec((None, tr, c), functools.partial(lambda s, i: (s, i, 0), s)) for _, s in g_parts]
    out = _sds((r, c), F32)
    return _call(body, name=name, grid=(r // tr,), in_specs=in_specs + [tile, tile, tile],
                 out_specs=[tile] * 4, out_shape=[out] * 4, sem=("parallel",))(*[a for a, _ in g_parts], w, m, v)


def _adamw_stacked(h, recv, chip, w_st, m_st, v_st, k, prev, name):
    kk, r, c = w_st.shape
    tr = _tile(r, 256, V7X_SUBLANES)
    if prev is None:
        prev = [lax.empty((kk, r, c), F32) for _ in range(4)]

    def body(chip_ref, h_ref, r0_ref, r1_ref, r2_ref, w_ref, m_ref, v_ref, pg, pd, pm, pv,
             go_ref, d_ref, mo_ref, vo_ref):
        g = (h_ref[...].astype(F32) + r0_ref[...].astype(F32)) + (r1_ref[...].astype(F32) + r2_ref[...].astype(F32))
        delta, m2, v2 = _adam_math(g, w_ref[...], m_ref[...], v_ref[...])
        go_ref[...] = g
        d_ref[...] = delta
        mo_ref[...] = m2
        vo_ref[...] = v2

    own = pl.BlockSpec((None, tr, c), lambda i, chip_ref: (chip_ref[0], i, 0))
    rcv = [pl.BlockSpec((None, tr, c), functools.partial(lambda s, i, chip_ref: (s, i, 0), s)) for s in range(3)]
    blk = pl.BlockSpec((None, tr, c), lambda i, chip_ref: (k, i, 0))
    out = _sds((kk, r, c), F32)
    return _call(body, name=name, grid=(r // tr,), in_specs=[own] + rcv + [blk, blk, blk] + [ANY] * 4,
                 out_specs=[blk] * 4, out_shape=[out] * 4, sem=("parallel",), prefetch=chip, on_path=False,
                 aliases={8: 0, 9: 1, 10: 2, 11: 3})(h, recv, recv, recv, w_st, m_st, v_st, *prev)


def _add_sibling(g4, land, core, name):
    n, _, r, c = g4.shape
    tr = _tile(r, 512, V7X_SUBLANES)

    def body(core_ref, a_ref, b_ref, o_ref):
        o_ref[...] = (a_ref[...].astype(F32) + b_ref[...].astype(F32)).astype(BF)

    return _call(body, name=name, grid=(n, r // tr),
                 in_specs=[pl.BlockSpec((None, None, tr, c), lambda p, i, core_ref: (p, core_ref[0], i, 0)),
                           pl.BlockSpec((None, None, tr, c), lambda p, i, core_ref: (p, 0, i, 0))],
                 out_specs=pl.BlockSpec((None, tr, c), lambda p, i, core_ref: (p, i, 0)),
                 out_shape=_sds((n, r, c), BF), sem=("parallel", "parallel"), prefetch=core)(g4, land)


def _cast_to_slot(w, lead, me, name, after=(), dtype=BF):
    r, c = w.shape[-2:]
    nl = len(lead)
    tr = _tile(r, 512, 2 * V7X_SUBLANES)

    def body(me_ref, w_ref, o_ref):
        o_ref[...] = w_ref[...].astype(dtype)

    return _call(body, name=name, grid=(r // tr,),
                 in_specs=[pl.BlockSpec((None,) * nl + (tr, c), lambda i, me_ref: tuple(lead) + (i, 0))],
                 out_specs=pl.BlockSpec((None, None, tr, c), lambda i, me_ref: (0, me_ref[0], i, 0)),
                 out_shape=_sds((1, N_DEV, r, c), dtype), sem=("parallel",), prefetch=me, after=after)(w)


def _ada_fwd(c_pad, ada_w, ada_b, name):
    nl, d, cl = ada_w.shape
    rows = c_pad.shape[0]
    tn = _tile(cl, 256, V7X_LANES)

    def body(c_ref, w_ref, b_ref, o_ref):
        cv = c_ref[...]
        cond = (cv * _sigmoid(cv)).astype(BF)
        o_ref[...] = jnp.dot(cond, w_ref[...].astype(BF), preferred_element_type=F32) + b_ref[...]

    return _call(body, name=name, grid=(nl, cl // tn),
                 in_specs=[pl.BlockSpec((rows, d), lambda l, j: (0, 0)),
                           pl.BlockSpec((None, d, tn), lambda l, j: (l, 0, j)),
                           pl.BlockSpec((None, 1, tn), lambda l, j: (l, 0, j))],
                 out_specs=pl.BlockSpec((None, rows, tn), lambda l, j: (l, 0, j)),
                 out_shape=_sds((nl, rows, cl), F32), sem=("parallel", "parallel"))(c_pad, ada_w, ada_b)


def _ada_bwd(c_pad, dmod, w, m, v, name):
    nl, d, cl = w.shape
    rows = c_pad.shape[0]
    tn = _tile(cl, 256, V7X_LANES)

    def body(c_ref, dm_ref, w_ref, m_ref, v_ref, go_ref, d_ref, mo_ref, vo_ref):
        cv = c_ref[...]
        cond = (cv * _sigmoid(cv)).astype(BF)
        g = lax.dot_general(cond, dm_ref[...].astype(BF), (((0,), (0,)), ((), ())), preferred_element_type=F32)
        delta, m2, v2 = _adam_math(g, w_ref[...], m_ref[...], v_ref[...])
        go_ref[...] = g
        d_ref[...] = delta
        mo_ref[...] = m2
        vo_ref[...] = v2

    tile = pl.BlockSpec((None, d, tn), lambda l, j: (l, 0, j))
    out = _sds((nl, d, cl), F32)
    return _call(body, name=name, grid=(nl, cl // tn),
                 in_specs=[pl.BlockSpec((rows, d), lambda l, j: (0, 0)),
                           pl.BlockSpec((None, rows, tn), lambda l, j: (l, 0, j)), tile, tile, tile],
                 out_specs=[tile] * 4, out_shape=[out] * 4, sem=("parallel", "parallel"))(c_pad, dmod, w, m, v)


def _sum_devices(parts, name):
    n, r, c = parts.shape
    tr = _tile(r, 512, V7X_SUBLANES)

    def body(p_ref, o_ref):
        acc = p_ref[0]
        for k in range(1, n):
            acc = acc + p_ref[k]
        o_ref[...] = acc

    return _call(body, name=name, grid=(r // tr,), in_specs=[pl.BlockSpec((n, tr, c), lambda i: (0, i, 0))],
                 out_specs=pl.BlockSpec((tr, c), lambda i: (i, 0)), out_shape=_sds((r, c), F32),
                 sem=("parallel",))(parts)


def _mesh_pos():
    return lax.axis_index("x"), lax.axis_index("y"), lax.axis_index("c")


def _other_chips(x, y):
    return [(1 - x, y), (x, 1 - y), (1 - x, 1 - y)]


def _all_gather(arrs, name):
    n = len(arrs)

    def body(*refs):
        ins, outs = refs[:n], refs[n:2 * n]
        send_sems, recv_sems, local_sems = refs[2 * n:]
        x, y, c = _mesh_pos()
        me, sibling = (x, y, c), (x, y, 1 - c)
        chips = _other_chips(x, y)

        def slot(a, pos):
            px, py, pc = pos
            return outs[a].at[:, pl.ds(4 * px + 2 * py + pc, 1)]

        def copy(a, k, block, to, src=None):
            return pltpu.make_async_remote_copy(
                src_ref=slot(a, block) if src is None else src, dst_ref=slot(a, block),
                send_sem=send_sems.at[a, k], recv_sem=recv_sems.at[a, k], device_id=to, device_id_type=MESH)

        mine = [pltpu.make_async_copy(ins[a], slot(a, me), local_sems.at[a]) for a in range(n)]
        for cp in mine:
            cp.start()
        first = []
        for a in range(n):
            first.append(copy(a, 0, me, sibling, src=ins[a]))
            first += [copy(a, 1 + j, me, (*chip, c), src=ins[a]) for j, chip in enumerate(chips)]
        for cp in first:
            cp.start()
        passed = []
        for a in range(n):
            for j, chip in enumerate(chips):
                copy(a, 1 + j, (*chip, c), me).wait_recv()
                fwd = copy(a, 4 + j, (*chip, c), sibling)
                fwd.start()
                passed.append(fwd)
        for a in range(n):
            copy(a, 0, sibling, me).wait_recv()
            for j, chip in enumerate(chips):
                copy(a, 4 + j, (*chip, 1 - c), me).wait_recv()
        for cp in first + passed:
            cp.wait_send()
        for cp in mine:
            cp.wait()

    out_shape = [_sds((a.shape[0], N_DEV) + a.shape[2:], a.dtype) for a in arrs]
    return pl.pallas_call(
        body, out_shape=out_shape, in_specs=[ANY] * n, out_specs=[ANY] * n, name=name,
        scratch_shapes=[pltpu.SemaphoreType.DMA((n, N_DEV - 1)), pltpu.SemaphoreType.DMA((n, N_DEV - 1)),
                        pltpu.SemaphoreType.DMA((n,))])(*arrs)


HBM = pl.BlockSpec(memory_space=pltpu.HBM)
SEM = pl.BlockSpec(memory_space=pltpu.SEMAPHORE)


def _hbm(v):
    return pltpu.with_memory_space_constraint(v, pltpu.HBM)


def _comm_call(body, name, bufs, sems_in, sems_out, follow=()):
    after = [] if not sems_in or _Seq.last is None or any(_Seq.last is b for b in bufs) else [_Seq.last]
    after += list(follow)
    nb, ni, na, no = len(bufs), len(sems_in), len(after), len(sems_out)

    def wrapped(*refs):
        body(refs[:nb], refs[nb:nb + ni], refs[nb + ni + na:nb + ni + na + no])
        if no:
            refs[-1][...] = jnp.zeros_like(refs[-1])

    out_shape = [pltpu.SemaphoreType.DMA(s) for s in sems_out] + [pltpu.HBM(b.shape, b.dtype) for b in bufs]
    out_specs = [SEM] * no + [HBM] * nb
    if no:
        out_shape.append(_sds((V7X_SUBLANES, V7X_LANES), F32))
        out_specs.append(pl.BlockSpec(memory_space=pltpu.VMEM))
    res = pl.pallas_call(
        wrapped, name=name, out_shape=out_shape, in_specs=[HBM] * nb + [SEM] * ni + [ANY] * na, out_specs=out_specs,
        input_output_aliases={i: no + i for i in range(nb)},
        compiler_params=pltpu.CompilerParams(has_side_effects=pltpu.SideEffectType.DATAFLOW_SIDE_EFFECTING),
    )(*bufs, *sems_in, *after)
    out_bufs = list(res[no:no + nb])
    if no:
        _Seq.tokens.append(res[-1])
    _Seq.last = out_bufs[0]
    return list(res[:no]), out_bufs


def _remote(src, dst, send_sem, recv_sem, to):
    return pltpu.make_async_remote_copy(src_ref=src, dst_ref=dst, send_sem=send_sem, recv_sem=recv_sem,
                                        device_id=to, device_id_type=MESH)


def _slot(ref, pos):
    px, py, pc = pos
    return ref.at[:, pl.ds(4 * px + 2 * py + pc, 1)]


def _ag_start(bufs, name):
    n = len(bufs)

    def body(b, _, sems):
        send_sib, recv_sib, send_ici, recv_ici = sems
        x, y, c = _mesh_pos()
        for a in range(n):
            mine = _slot(b[a], (x, y, c))
            _remote(mine, mine, send_sib.at[a], recv_sib.at[a], (x, y, 1 - c)).start()
            for j, (px, py) in enumerate(_other_chips(x, y)):
                _remote(mine, mine, send_ici.at[3 * a + j], recv_ici.at[3 * a + j], (px, py, c)).start()

    sems, bufs = _comm_call(body, name, [_hbm(b) for b in bufs], [], [(n,), (n,), (3 * n,), (3 * n,)])
    return dict(bufs=bufs, send_sib=sems[0], recv_sib=sems[1], send_ici=sems[2], recv_ici=sems[3])


def _ag_mid(st, name):
    n = len(st["bufs"])

    def body(b, sems_in, sems):
        (recv_ici,) = sems_in
        send_fwd, recv_fwd = sems
        x, y, c = _mesh_pos()
        for a in range(n):
            for j, (px, py) in enumerate(_other_chips(x, y)):
                blk = _slot(b[a], (px, py, c))
                _remote(blk, blk, send_fwd.at[3 * a + j], recv_ici.at[3 * a + j], (x, y, 1 - c)).wait_recv()
                _remote(blk, blk, send_fwd.at[3 * a + j], recv_fwd.at[3 * a + j], (x, y, 1 - c)).start()

    sems, bufs = _comm_call(body, name, st["bufs"], [st["recv_ici"]], [(3 * n,), (3 * n,)])
    return dict(st, bufs=bufs, send_fwd=sems[0], recv_fwd=sems[1])


def _ag_end(st, name):
    n = len(st["bufs"])

    def body(b, sems_in, _):
        send_sib, recv_sib, send_ici, send_fwd, recv_fwd = sems_in
        x, y, c = _mesh_pos()
        sibling = (x, y, 1 - c)
        for a in range(n):
            mine, sib_blk = _slot(b[a], (x, y, c)), _slot(b[a], sibling)
            _remote(mine, mine, send_sib.at[a], recv_sib.at[a], sibling).wait_send()
            _remote(sib_blk, sib_blk, send_sib.at[a], recv_sib.at[a], sibling).wait_recv()
            for j, (px, py) in enumerate(_other_chips(x, y)):
                blk, sib_got = _slot(b[a], (px, py, c)), _slot(b[a], (px, py, 1 - c))
                _remote(mine, mine, send_ici.at[3 * a + j], recv_sib.at[a], (px, py, c)).wait_send()
                _remote(blk, blk, send_fwd.at[3 * a + j], recv_fwd.at[3 * a + j], sibling).wait_send()
                _remote(sib_got, sib_got, send_fwd.at[3 * a + j], recv_fwd.at[3 * a + j], sibling).wait_recv()

    _, bufs = _comm_call(body, name, st["bufs"],
                         [st[k] for k in ("send_sib", "recv_sib", "send_ici", "send_fwd", "recv_fwd")], [])
    return bufs


def _rs_start(g4s, name):
    n = len(g4s)
    lands = [lax.empty((N_CHIP, 1) + g.shape[2:], g.dtype) for g in g4s]

    def body(b, _, sems):
        send, recv = sems
        x, y, c = _mesh_pos()
        for a in range(n):
            _remote(b[a].at[:, pl.ds(1 - c, 1)], b[n + a], send.at[a], recv.at[a], (x, y, 1 - c)).start()

    sems, bufs = _comm_call(body, name, [_hbm(v) for v in list(g4s) + lands], [], [(n,), (n,)])
    return dict(bufs=bufs, send=sems[0], recv=sems[1])


def _rs_mid(st, name):
    n = len(st["bufs"]) // 2

    def body(b, sems_in, _):
        send, recv = sems_in
        x, y, c = _mesh_pos()
        for a in range(n):
            cp = _remote(b[a].at[:, pl.ds(1 - c, 1)], b[n + a], send.at[a], recv.at[a], (x, y, 1 - c))
            cp.wait_send()
            cp.wait_recv()

    _, bufs = _comm_call(body, name, st["bufs"], [st["send"], st["recv"]], [])
    return bufs[:n], bufs[n:]


def _rs_start2(sums, name):
    n = len(sums)
    lands = [lax.empty((N_CHIP - 1,) + s.shape[1:], s.dtype) for s in sums]

    def body(b, _, sems):
        send, recv = sems
        x, y, c = _mesh_pos()
        for a in range(n):
            for j, (px, py) in enumerate(_other_chips(x, y)):
                _remote(b[a].at[pl.ds(2 * px + py, 1)], b[n + a].at[pl.ds(j, 1)], send.at[3 * a + j], recv.at[3 * a + j],
                        (px, py, c)).start()

    sems, bufs = _comm_call(body, name, [_hbm(v) for v in list(sums) + lands], [], [(3 * n,), (3 * n,)])
    return dict(bufs=bufs, send=sems[0], recv=sems[1])


def _rs_end(st, name, follow=()):
    n = len(st["bufs"]) // 2

    def body(b, sems_in, _):
        send, recv = sems_in
        x, y, c = _mesh_pos()
        for a in range(n):
            for j, (px, py) in enumerate(_other_chips(x, y)):
                cp = _remote(b[a].at[pl.ds(2 * px + py, 1)], b[n + a].at[pl.ds(j, 1)], send.at[3 * a + j], recv.at[3 * a + j],
                             (px, py, c))
                cp.wait_send()
                cp.wait_recv()

    _, bufs = _comm_call(body, name, st["bufs"], [st["send"], st["recv"]], [], follow=follow)
    return bufs[:n], bufs[n:]


def _pack(parts, rows_align=V7X_SUBLANES):
    flat, total = [], 0
    for p in parts:
        v = p.reshape(-1).astype(F32)
        pad = -v.shape[0] % PACK_ALIGN
        flat.append(jnp.pad(v, (0, pad)) if pad else v)
        total += v.shape[0] + pad
    tail = -total % (rows_align * V7X_LANES)
    if tail:
        flat.append(jnp.zeros((tail,), F32))
    return jnp.concatenate(flat).reshape(-1, V7X_LANES)


def _unpack(buf, shapes):
    lead = buf.shape[:-2]
    flat = buf.reshape(lead + (-1,))
    out, pos = [], 0
    for s in shapes:
        size = 1
        for d in s:
            size *= d
        out.append(flat[..., pos:pos + size].reshape(lead + tuple(s)))
        pos += size + (-size % PACK_ALIGN)
    return out


def kernel(x, c, ada_w, ada_b, norm_g, ffn_w_in, ffn_w_out, gm_w_in, gm_ln_g, gm_ln_b, gm_ws, gm_bs, gm_w_out, cv_w_in, cv_b_in, cv_dw_w, cv_dw_b, cv_ln_g, cv_ln_b, cv_w_out, cv_b_out, final_g, loss_target, m_ada_w, m_ada_b, m_norm_g, m_ffn_w_in, m_ffn_w_out, m_gm_w_in, m_gm_ln_g, m_gm_ln_b, m_gm_ws, m_gm_bs, m_gm_w_out, m_cv_w_in, m_cv_b_in, m_cv_dw_w, m_cv_dw_b, m_cv_ln_g, m_cv_ln_b, m_cv_w_out, m_cv_b_out, m_final_g, v_ada_w, v_ada_b, v_norm_g, v_ffn_w_in, v_ffn_w_out, v_gm_w_in, v_gm_ln_g, v_gm_ln_b, v_gm_ws, v_gm_bs, v_gm_w_out, v_cv_w_in, v_cv_b_in, v_cv_dw_w, v_cv_dw_b, v_cv_ln_g, v_cv_ln_b, v_cv_w_out, v_cv_b_out, v_final_g):
    t, d = x.shape[1], x.shape[2]
    depth = ada_w.shape[0]
    assert depth == 2 and ffn_w_in.shape[:2] == (2, 2) and gm_w_in.shape[0] == 1 and cv_w_in.shape[0] == 1
    dl = d // N_DEV
    bn = ffn_w_in.shape[3]
    fl = ffn_w_out.shape[2]
    f = fl * N_DEV
    el = gm_w_in.shape[2]
    e = el * N_DEV // 2
    hn, l = gm_ws.shape[1], gm_ws.shape[2]
    kw = cv_dw_w.shape[1]
    cl = ada_w.shape[2]
    me = 4 * lax.axis_index("x") + 2 * lax.axis_index("y") + lax.axis_index("c")
    me1 = me.astype(jnp.int32).reshape(1)
    chip1 = (2 * lax.axis_index("x") + lax.axis_index("y")).astype(jnp.int32).reshape(1)
    core1 = lax.axis_index("c").astype(jnp.int32).reshape(1)
    _Seq.last, _Seq.tokens = None, []

    xs = x[0]
    tgt = loss_target[0]

    ag_groups = [("win00", [(ffn_w_in, (0, 0))]), ("wout00", [(ffn_w_out, (0, 0))]),
                 ("gm", [(gm_w_in, (0,)), (gm_w_out, (0,))]),
                 ("win01", [(ffn_w_in, (0, 1))]), ("wout01", [(ffn_w_out, (0, 1))]),
                 ("win10", [(ffn_w_in, (1, 0))]), ("wout10", [(ffn_w_out, (1, 0))]),
                 ("cv", [(cv_w_in, (0,)), (cv_w_out, (0,))]),
                 ("win11", [(ffn_w_in, (1, 1))]), ("wout11", [(ffn_w_out, (1, 1))])]
    ag_flight = {}

    ag_slots = {}

    def ag_cast(gi, after=()):
        gname, members = ag_groups[gi]
        ag_slots[gi] = [_cast_to_slot(w, lead, me1, name=f"cast_{gname}_{k}", after=after)
                        for k, (w, lead) in enumerate(members)]

    def ag_start(gi):
        ag_flight[gi] = _ag_start(ag_slots.pop(gi), name=f"ag_start_{ag_groups[gi][0]}")

    def ag_forward(gi):
        if gi in ag_flight and "send_fwd" not in ag_flight[gi]:
            ag_flight[gi] = _ag_mid(ag_flight[gi], name=f"ag_mid_{ag_groups[gi][0]}")

    def ag_take(gi):
        ag_forward(gi)
        bufs = _ag_end(ag_flight.pop(gi), name=f"ag_end_{ag_groups[gi][0]}")
        if gi > 0 and gi + 1 != AG_AHEAD:
            ag_forward(gi + 1)
        if gi + AG_AHEAD < len(ag_groups):
            ag_start(gi + AG_AHEAD)
        return [b[0] for b in bufs]

    small_in = [c, norm_g, cv_b_in, cv_dw_w, cv_dw_b, cv_ln_g, cv_ln_b, cv_b_out]
    pack1 = _pack(small_in)
    (pack1_all,) = _all_gather([pack1[None, None]], name="ag_small")
    parts = _unpack(pack1_all[0], [s.shape for s in small_in])
    c_all = parts[0].reshape(N_DEV, d)
    ng_full = jnp.moveaxis(parts[1], 0, 2).reshape(depth, 3, d)
    cvb_in_full = parts[2].reshape(1, 2 * e)
    dww_full = jnp.moveaxis(parts[3][:, 0], 0, 1).reshape(kw, e)
    dwb_full, cln_g_full, cln_b_full, cvb_out_full = [p.reshape(1, d) for p in parts[4:8]]

    c_pad = jnp.pad(c_all, ((0, 16 - N_DEV), (0, 0)))
    ada_b_loc = lax.dynamic_slice_in_dim(ada_b, me * cl, cl, axis=1).reshape(depth, 1, cl)
    mod_part = _ada_fwd(c_pad, ada_w, ada_b_loc, name="ada_fwd")[:, :N_DEV]
    (mod_all,) = _all_gather([_pack([mod_part])[None, None]], name="ag_mod")
    mod_all = _unpack(mod_all[0], [mod_part.shape])[0]
    mod_mine = lax.dynamic_index_in_dim(mod_all, me, axis=2, keepdims=False)
    mod = jnp.moveaxis(mod_mine, 0, 1).reshape(depth, 3, 3, 1, d)

    for gi in range(len(ag_groups)):
        ag_cast(gi, after=[mod_all])
        if gi < AG_AHEAD:
            ag_start(gi)

    ws = gm_ws[0]
    bsb = jnp.broadcast_to(gm_bs[0][:, :, None], (hn, l, e // hn))
    gm_g, gm_b = gm_ln_g, gm_ln_b

    saved = []
    xcur = xs
    next_group = 0
    for i in range(depth):
        for s in range(3):
            shift, scale, gate = mod[i, s, 0], mod[i, s, 1], mod[i, s, 2]
            g_norm = ng_full[i, s][None]
            tag = f"l{i}s{s}"
            h = _norm_mod(xcur, g_norm, scale, shift, name=f"norm_mod_{tag}")
            if s != 1:
                (w_in_blk,) = ag_take(next_group)
                fg, fu, act = _ffn_in(h, w_in_blk, 0, name=f"ffn_in_{tag}")
                w_out3 = ag_take(next_group + 1)[0].reshape(1, f, d)
                next_group += 2
                xnext, yv = _out_proj(act, w_out3, 0, xcur, gate, None, 0.5, name=f"ffn_out_{tag}")
                saved.append(dict(x=xcur, h=h, fg=fg, fu=fu, act=act, y=yv, w_in=w_in_blk, w_out=w_out3))
            elif i % 2 == 0:
                gm_in_blk, gm_out = ag_take(next_group)
                gm_out3 = gm_out.reshape(1, e, d)
                next_group += 1
                pre = _in_proj(h, gm_in_blk, None, name=f"gm_in_{tag}")
                uu, vn = _gm_act(pre, gm_g, gm_b, name=f"gm_act_{tag}")
                sg = _sgu_fwd(uu, vn, ws, bsb, name=f"sgu_fwd_{tag}")
                xnext, yv = _out_proj(sg, gm_out3, 0, xcur, gate, None, 1.0, name=f"gm_out_{tag}")
                saved.append(dict(x=xcur, h=h, pre=pre, u=uu, vn=vn, sg=sg, y=yv, w_in=gm_in_blk, w_out=gm_out3))
            else:
                cv_in_blk, cv_out = ag_take(next_group)
                cv_out3 = cv_out.reshape(1, e, d)
                next_group += 1
                p = _in_proj(h, cv_in_blk, cvb_in_full, name=f"cv_in_{tag}")
                yc = _dwconv_fwd(p, dww_full, dwb_full, name=f"dwconv_fwd_{tag}")
                ys = _cv_act(yc, cln_g_full, cln_b_full, name=f"cv_act_{tag}")
                xnext, yv = _out_proj(ys, cv_out3, 0, xcur, gate, cvb_out_full, 1.0, name=f"cv_out_{tag}")
                saved.append(dict(x=xcur, h=h, p=p, yc=yc, ys=ys, y=yv, w_in=cv_in_blk, w_out=cv_out3))
            xcur = xnext

    def bwd_head(i, s):
        return saved[3 * i + s]["y"], mod[i, s, 2], 0.5 if s != 1 else 1.0, s == 1 and i % 2 == 1

    sq, dx, d_final_g, dy, dgate = _final_loss(xcur, tgt, final_g[None], *bwd_head(depth - 1, 2)[:3], name="final_loss")
    dbout = None
    loss = lax.psum(0.5 / d * jnp.sum(sq), ("x", "y", "c"))

    dmod = [[[None] * 3 for _ in range(3)] for _ in range(depth)]
    d_norm_g = [[None] * 3 for _ in range(depth)]
    small = {}

    stacked = {
        "ffn_w_in": [a.reshape(4, d, bn) for a in (ffn_w_in, m_ffn_w_in, v_ffn_w_in)],
        "ffn_w_out": [a.reshape(4, fl, d) for a in (ffn_w_out, m_ffn_w_out, v_ffn_w_out)],
        "gm_w_in": [gm_w_in, m_gm_w_in, v_gm_w_in], "gm_w_out": [gm_w_out, m_gm_w_out, v_gm_w_out],
        "cv_w_in": [cv_w_in, m_cv_w_in, v_cv_w_in], "cv_w_out": [cv_w_out, m_cv_w_out, v_cv_w_out],
    }
    res_big = {}

    def rs_sibling(g4s, tag):
        return _rs_start(g4s, name=f"rs_start_{tag}"), tag

    def rs_chips(flight):
        st, tag = flight
        g4s, lands = _rs_mid(st, name=f"rs_mid_{tag}")
        sums = [_add_sibling(g4, land, core1, name=f"rs_add_{tag}_{k}") for k, (g4, land) in enumerate(zip(g4s, lands))]
        return _rs_start2(sums, name=f"rs_start2_{tag}"), tag

    def rs_finish(flight, targets, follow=()):
        st, tag = flight
        sums, recvs = _rs_end(st, name=f"rs_end_{tag}", follow=follow)
        for (pname, k), hsum, recv in zip(targets, sums, recvs):
            w_st, m_st, v_st = stacked[pname]
            res_big[pname] = _adamw_stacked(hsum, recv, chip1, w_st, m_st, v_st, k, res_big.get(pname),
                                            name=f"adamw_{pname}_{k}")

    pending = []
    last_sibling = None
    for i in reversed(range(depth)):
        for s in reversed(range(3)):
            sv = saved[3 * i + s]
            shift, scale, gate = mod[i, s, 0], mod[i, s, 1], mod[i, s, 2]
            g_norm = ng_full[i, s][None]
            tag = f"l{i}s{s}"
            last = i == 0 and s == 0
            if s != 1:
                widx = 2 * i + s // 2
                dgu, act = _ffn_da(dy, sv["w_out"], 0, sv["fg"], sv["fu"], name=f"ffn_da_{tag}"), sv["act"]
                if last:
                    g_in = _mm_tn(sv["h"], dgu, d, bn, True, name=f"ffn_dwin_{tag}").reshape(N_CHIP, 2, d, bn)
                    sib_in = rs_sibling([g_in], f"{tag}_in")
                    g_out = _mm_tn(act, dy[None], bn, d, False, name=f"ffn_dwout_{tag}").reshape(N_CHIP, 2, fl, d)
                    new_flights = [(rs_chips(sib_in), [("ffn_w_in", widx)])]
                    sib, targets = rs_sibling([g_out], f"{tag}_out"), [("ffn_w_out", widx)]
                else:
                    g_out = _mm_tn(act, dy[None], bn, d, False, name=f"ffn_dwout_{tag}").reshape(N_CHIP, 2, fl, d)
                    sib_out = rs_sibling([g_out], f"{tag}_out")
                    g_in = _mm_tn(sv["h"], dgu, d, bn, True, name=f"ffn_dwin_{tag}").reshape(N_CHIP, 2, d, bn)
                    new_flights = [(rs_chips(sib_out), [("ffn_w_out", widx)])]
                    sib, targets = rs_sibling([g_in], f"{tag}_in"), [("ffn_w_in", widx)]
                z3, w_blk = dgu, sv["w_in"]
            elif i % 2 == 0:
                ds = _mm_nt(dy, sv["w_out"], 0, name=f"gm_ds_{tag}")
                g_out = _mm_tn(sv["sg"], dy[None], _tile(e, 1024, V7X_LANES), d, False,
                               name=f"gm_dwout_{tag}").reshape(N_CHIP, 2, dl, d)
                du, dvn, dws, dbs = _sgu_bwd(ds, sv["u"], sv["vn"], ws, bsb, name=f"sgu_bwd_{tag}")
                dpre, dlng, dlnb = _gm_act_bwd(sv["pre"], du, dvn, gm_g, name=f"gm_act_bwd_{tag}")
                gm_small = [dlng, dlnb, dws, dbs[:, :, 0]]
                gm_slot = _cast_to_slot(_pack(gm_small, rows_align=256), (), me1, name="slot_gm_grads", dtype=F32)
                gm_flight = _ag_start([gm_slot], name="ag_start_gm_grads")
                g_in = _mm_tn(sv["h"], dpre[None], d, el, True, name=f"gm_dwin_{tag}").reshape(N_CHIP, 2, d, el)
                targets, new_flights = [("gm_w_in", 0), ("gm_w_out", 0)], []
                sib = rs_sibling([g_in, g_out], tag)
                z3, w_blk = dpre[None], sv["w_in"]
            else:
                dys = _mm_nt(dy, sv["w_out"], 0, name=f"cv_dys_{tag}")
                g_out = _mm_tn(sv["ys"], dy[None], _tile(e, 1024, V7X_LANES), d, False,
                               name=f"cv_dwout_{tag}").reshape(N_CHIP, 2, dl, d)
                dyc, dlng, dlnb, ddwb = _cv_act_bwd(dys, sv["yc"], cln_g_full, cln_b_full, name=f"cv_act_bwd_{tag}")
                dp, ddww, dba, dbg = _dwconv_bwd(dyc, sv["p"], dww_full, name=f"dwconv_bwd_{tag}")
                small["cv_b_out"], small["cv_ln_g"], small["cv_ln_b"], small["cv_dw_b"] = dbout, dlng, dlnb, ddwb
                small["cv_dw_w"] = ddww
                small["cv_b_in"] = jnp.concatenate([dba, dbg], axis=1)
                g_in = _mm_tn(sv["h"], dp, d, el, True, name=f"cv_dwin_{tag}").reshape(N_CHIP, 2, d, el)
                targets, new_flights = [("cv_w_in", 0), ("cv_w_out", 0)], []
                sib = rs_sibling([g_in, g_out], tag)
                z3, w_blk = dp, sv["w_in"]
            nxt = None if last else bwd_head(*((i, s - 1) if s > 0 else (i - 1, 2)))
            res = _dh_norm_bwd(z3, w_blk, sv["x"], dx, g_norm, scale, nxt, name=f"dh_norm_bwd_{tag}")
            if last:
                (dx, dscale, dshift, dgn), last_sibling = res, (sib, targets)
            else:
                new_flights.append((rs_chips(sib), targets))
                dx, dy_next, dscale, dshift, dgn, dgate_next = res[:6]
                dbout_next = res[6] if nxt[3] else None
            dmod[i][s] = [dshift, dscale, dgate]
            d_norm_g[i][s] = dgn
            if not last:
                dy, dgate, dbout = dy_next, dgate_next, dbout_next
            for flight in pending:
                rs_finish(*flight)
            pending = new_flights
    grad_x = dx[None]

    dmod_mine = jnp.concatenate([v for per_l in dmod for per_s in per_l for v in per_s], axis=1)
    dng_mine = jnp.concatenate([v for per_l in d_norm_g for v in per_l], axis=1)
    small_out = [dmod_mine, dng_mine, small["cv_b_in"], small["cv_dw_w"], small["cv_dw_b"], small["cv_ln_g"],
                 small["cv_ln_b"], small["cv_b_out"], d_final_g]
    shapes2 = [s.shape for s in small_out]
    (pack2_all,) = _all_gather([_pack(small_out, rows_align=256)[None, None]], name="ag_small_grads")
    _Seq.last = pack2_all
    pending.append((rs_chips(last_sibling[0]), last_sibling[1]))
    summed = _unpack(_sum_devices(pack2_all[0], name="sum_small_grads"), shapes2)
    dmod_all = _unpack(pack2_all[0], shapes2)[0].reshape(N_DEV, depth, 9 * d)
    (gm_all,) = _ag_end(_ag_mid(gm_flight, name="ag_mid_gm_grads"), name="ag_end_gm_grads")
    gm_sum = _unpack(_sum_devices(gm_all[0], name="sum_gm_grads"), [s.shape for s in gm_small])

    def my_cols(full, width):
        return lax.dynamic_slice_in_dim(full, me * width, width, axis=full.ndim - 1)

    g_ada_b = summed[0].reshape(depth, 9 * d)
    g_norm_g = my_cols(summed[1].reshape(depth, 3, d), dl)
    g_small = {
        "ada_b": g_ada_b, "norm_g": g_norm_g,
        "gm_ln_g": gm_sum[0], "gm_ln_b": gm_sum[1], "gm_ws": gm_sum[2][None], "gm_bs": gm_sum[3][None],
        "cv_b_in": my_cols(summed[2], el), "cv_dw_w": my_cols(summed[3], dl)[None],
        "cv_dw_b": my_cols(summed[4], dl), "cv_ln_g": my_cols(summed[5], dl), "cv_ln_b": my_cols(summed[6], dl),
        "cv_b_out": my_cols(summed[7], dl), "final_g": summed[8].reshape(d),
    }

    dm_loc = jnp.moveaxis(my_cols(dmod_all, cl), 0, 1)
    dm_loc = jnp.pad(dm_loc, ((0, 0), (0, 16 - N_DEV), (0, 0)))
    res_ada_w = _ada_bwd(c_pad, dm_loc, ada_w, m_ada_w, v_ada_w, name="ada_bwd_adamw")

    def flat2(a):
        return a.reshape(-1, a.shape[-1])

    small_params = {
        "ada_b": (ada_b, m_ada_b, v_ada_b), "norm_g": (norm_g, m_norm_g, v_norm_g),
        "gm_ln_g": (gm_ln_g, m_gm_ln_g, v_gm_ln_g), "gm_ln_b": (gm_ln_b, m_gm_ln_b, v_gm_ln_b),
        "gm_ws": (gm_ws, m_gm_ws, v_gm_ws), "gm_bs": (gm_bs, m_gm_bs, v_gm_bs),
        "cv_b_in": (cv_b_in, m_cv_b_in, v_cv_b_in), "cv_dw_w": (cv_dw_w, m_cv_dw_w, v_cv_dw_w),
        "cv_dw_b": (cv_dw_b, m_cv_dw_b, v_cv_dw_b), "cv_ln_g": (cv_ln_g, m_cv_ln_g, v_cv_ln_g),
        "cv_ln_b": (cv_ln_b, m_cv_ln_b, v_cv_ln_b), "cv_b_out": (cv_b_out, m_cv_b_out, v_cv_b_out),
        "final_g": (final_g, m_final_g, v_final_g),
    }
    res_small = {}
    for key, (w, m, v) in small_params.items():
        g2 = flat2(g_small[key].reshape(w.shape)) if w.ndim > 1 else g_small[key].reshape(1, -1)
        w2, m2, v2 = [flat2(a) if a.ndim > 1 else a.reshape(1, -1) for a in (w, m, v)]
        res_small[key] = [o.reshape(w.shape) for o in _adamw([(g2[None], 0)], w2, m2, v2, name=f"adamw_{key}")]

    for flight in pending[:-1]:
        rs_finish(*flight)
    rs_finish(*pending[-1], follow=[res_ada_w[0]] + [r[0] for r in res_big.values()])

    def big(name, k):
        if name == "ada_w":
            return res_ada_w[k]
        return res_big[name][k].reshape(stacked_shape[name])

    stacked_shape = {"ffn_w_in": ffn_w_in.shape, "ffn_w_out": ffn_w_out.shape, "gm_w_in": gm_w_in.shape,
                     "gm_w_out": gm_w_out.shape, "cv_w_in": cv_w_in.shape, "cv_w_out": cv_w_out.shape}

    order = ["ada_w", "ada_b", "norm_g", "ffn_w_in", "ffn_w_out", "gm_w_in", "gm_ln_g", "gm_ln_b", "gm_ws", "gm_bs",
             "gm_w_out", "cv_w_in", "cv_b_in", "cv_dw_w", "cv_dw_b", "cv_ln_g", "cv_ln_b", "cv_w_out", "cv_b_out",
             "final_g"]
    outs = [loss, grad_x]
    for k in range(4):
        for name in order:
            outs.append(res_small[name][k] if name in res_small else big(name, k))
    return tuple(outs)
```

```python
import functools

import jax
import jax.numpy as jnp
from jax import lax
from jax.experimental import pallas as pl
from jax.experimental.pallas import tpu as pltpu

F32 = jnp.float32
BF = jnp.bfloat16
MESH = pl.DeviceIdType.MESH

N_DEV = 8
N_CHIP = 4
NORM_EPS = 1e-6
ADAM_LR = 0.001
ADAM_B1 = 0.9
ADAM_B2 = 0.999
ADAM_EPS = 1e-08
ADAM_WD = 0.01
ADAM_STEP = 10

V7X_SUBLANES = 8
V7X_LANES = 128
PACK_ALIGN = V7X_SUBLANES * V7X_LANES
V7X_VMEM_LIMIT = 56 * 1024 * 1024


def _tile(n, pref, align):
    if n <= pref:
        return n
    t = pref - pref % align
    while t >= align:
        if n % t == 0:
            return t
        t -= align
    return n


ANY = pl.BlockSpec(memory_space=pl.ANY)


class _Seq:
    last = None
    tokens = []


def _call(body, *, name, grid, in_specs, out_specs, out_shape, scratch=(), sem=None, prefetch=None, aliases=None,
          after=(), on_path=True):
    def run(*args):
        if on_path:
            tokens, _Seq.tokens = _Seq.tokens + list(after), []
        else:
            tokens = list(after)
        lead = 0 if prefetch is None else 1
        n_in, n_tok = lead + len(args), len(tokens)

        def wrapped(*refs):
            body(*refs[:n_in], *refs[n_in + n_tok:])

        specs = list(in_specs) + [ANY] * n_tok
        params = pltpu.CompilerParams(dimension_semantics=sem, vmem_limit_bytes=V7X_VMEM_LIMIT)
        if prefetch is None:
            res = pl.pallas_call(wrapped, out_shape=out_shape, grid=grid, in_specs=specs, out_specs=out_specs,
                                 scratch_shapes=scratch, name=name, compiler_params=params,
                                 input_output_aliases=aliases or {})(*args, *tokens)
        else:
            grid_spec = pltpu.PrefetchScalarGridSpec(num_scalar_prefetch=1, grid=grid, in_specs=specs,
                                                     out_specs=out_specs, scratch_shapes=scratch)
            res = pl.pallas_call(wrapped, out_shape=out_shape, grid_spec=grid_spec, name=name,
                                 compiler_params=params, input_output_aliases=aliases or {})(prefetch, *args, *tokens)
        if on_path:
            _Seq.last = res[0] if isinstance(res, (list, tuple)) else res
        return res
    return run


def _sds(shape, dtype):
    return jax.ShapeDtypeStruct(tuple(shape), dtype)


def _sigmoid(v):
    return 1.0 / (1.0 + jnp.exp(-v))


def _normal_cdf_pdf(v):
    a = jnp.abs(v) * 0.7071067811865476
    t = 1.0 / (1.0 + 0.3275911 * a)
    poly = t * (0.254829592 + t * (-0.284496736 + t * (1.421413741 + t * (-1.453152027 + t * 1.061405429))))
    e = jnp.exp(-0.5 * v * v)
    half_erf = 0.5 - 0.5 * poly * e
    return 0.5 + jnp.where(v < 0, -half_erf, half_erf), 0.3989422804014327 * e


def _gelu(v):
    return v * _normal_cdf_pdf(v)[0]


def _fold_rows(val):
    rows, w = val.shape
    return val.reshape(rows // V7X_SUBLANES, V7X_SUBLANES, w).sum(axis=0)


def _rowwise(fn, name, rows_in, vecs_in, rows_out, acc_widths, tm=256):
    t = rows_in[0].shape[0]
    tm = _tile(t, tm, V7X_SUBLANES)
    steps = t // tm
    nr, nv, no, na = len(rows_in), len(vecs_in), len(rows_out), len(acc_widths)

    def body(*refs):
        rin, vin = refs[:nr], refs[nr:nr + nv]
        rout = refs[nr + nv:nr + nv + no]
        aout = refs[nr + nv + no:nr + nv + no + na]
        accs = refs[nr + nv + no + na:]
        i = pl.program_id(0)
        outs, acc_vals = fn(*[r[...] for r in rin], *[v[...] for v in vin])
        for r, o in zip(rout, outs):
            r[...] = o.astype(r.dtype)
        if na:
            @pl.when(i == 0)
            def _():
                for a in accs:
                    a[...] = jnp.zeros_like(a)

            for a, val in zip(accs, acc_vals):
                a[...] += _fold_rows(val)

            @pl.when(i == steps - 1)
            def _():
                for o, a in zip(aout, accs):
                    o[...] = jnp.sum(a[...], axis=0, keepdims=True)

    in_specs = [pl.BlockSpec((tm, r.shape[1]), lambda i: (i, 0)) for r in rows_in]
    in_specs += [pl.BlockSpec(v.shape, functools.partial(lambda nd, i: (0,) * nd, v.ndim)) for v in vecs_in]
    out_specs = [pl.BlockSpec((tm, r.shape[1]), lambda i: (i, 0)) for r in rows_out]
    out_specs += [pl.BlockSpec((1, w), lambda i: (0, 0)) for w in acc_widths]
    out_shape = list(rows_out) + [_sds((1, w), F32) for w in acc_widths]
    scratch = [pltpu.VMEM((V7X_SUBLANES, w), F32) for w in acc_widths]
    res = _call(body, name=name, grid=(steps,), in_specs=in_specs, out_specs=out_specs, out_shape=out_shape,
                scratch=scratch, sem=("arbitrary",) if na else ("parallel",))(*rows_in, *vecs_in)
    return res[:no], res[no:]


def _norm_mod(x, g, scale, shift, name):
    def fn(xv, gv, sc, sh):
        r = lax.rsqrt(jnp.mean(xv * xv, axis=-1, keepdims=True) + NORM_EPS)
        return ((xv * r * gv) * (1.0 + sc) + sh,), ()
    (h,), _ = _rowwise(fn, name, [x], [g, scale, shift], [_sds(x.shape, BF)], [], tm=512)
    return h


def _final_loss(x, target, g, y, gate, coef, name):
    d = x.shape[1]

    def fn(xv, tv, yv, gv, gt):
        r = lax.rsqrt(jnp.mean(xv * xv, axis=-1, keepdims=True) + NORM_EPS)
        xhat = xv * r
        err = xhat * gv - tv
        dl = err * (1.0 / d)
        dxhat = dl * gv
        dx = r * (dxhat - xhat * jnp.mean(dxhat * xhat, axis=-1, keepdims=True))
        return (dx, (coef * gt) * dx), (err * err, dl * xhat, coef * dx * yv.astype(F32))
    (dx, dy), (sq, dg, dgate) = _rowwise(fn, name, [x, target, y], [g, gate],
                                         [_sds(x.shape, F32), _sds(x.shape, BF)], [d, d, d])
    return sq, dx, dg, dy, dgate


def _gm_act(pre, ln_g, ln_b, name):
    e = pre.shape[1] // 2

    def fn(pv, gv, bv):
        p = pv.astype(F32)
        u = _gelu(p[:, :e])
        v = _gelu(p[:, e:])
        mu = jnp.mean(v, axis=-1, keepdims=True)
        vc = v - mu
        rstd = lax.rsqrt(jnp.mean(vc * vc, axis=-1, keepdims=True) + NORM_EPS)
        return (u, vc * rstd * gv + bv), ()
    t = pre.shape[0]
    (u, vn), _ = _rowwise(fn, name, [pre], [ln_g, ln_b], [_sds((t, e), BF), _sds((t, e), BF)], [])
    return u, vn


def _gm_act_bwd(pre, du, dvn, ln_g, name):
    e = pre.shape[1] // 2

    def fn(pv, duv, dvv, gv):
        p = pv.astype(F32)
        pu, pvv = p[:, :e], p[:, e:]
        cdf_u, pdf_u = _normal_cdf_pdf(pu)
        cdf_v, pdf_v = _normal_cdf_pdf(pvv)
        v = pvv * cdf_v
        mu = jnp.mean(v, axis=-1, keepdims=True)
        vc = v - mu
        rstd = lax.rsqrt(jnp.mean(vc * vc, axis=-1, keepdims=True) + NORM_EPS)
        vhat = vc * rstd
        dvn_f = dvv.astype(F32)
        dvhat = dvn_f * gv
        dv = rstd * (dvhat - jnp.mean(dvhat, axis=-1, keepdims=True)
                     - vhat * jnp.mean(dvhat * vhat, axis=-1, keepdims=True))
        dpu = duv.astype(F32) * (cdf_u + pu * pdf_u)
        dpv = dv * (cdf_v + pvv * pdf_v)
        return (jnp.concatenate([dpu, dpv], axis=1),), (dvn_f * vhat, dvn_f)
    (dpre,), (dg, db) = _rowwise(fn, name, [pre, du, dvn], [ln_g], [_sds(pre.shape, BF)], [e, e], tm=128)
    return dpre, dg, db


def _cv_act(yc, ln_g, ln_b, name):
    def fn(yv, gv, bv):
        mu = jnp.mean(yv, axis=-1, keepdims=True)
        c = yv - mu
        rstd = lax.rsqrt(jnp.mean(c * c, axis=-1, keepdims=True) + NORM_EPS)
        yn = c * rstd * gv + bv
        return (yn * _sigmoid(yn),), ()
    (ys,), _ = _rowwise(fn, name, [yc], [ln_g, ln_b], [_sds(yc.shape, BF)], [])
    return ys


def _cv_act_bwd(dys, yc, ln_g, ln_b, name):
    def fn(dv, yv, gv, bv):
        mu = jnp.mean(yv, axis=-1, keepdims=True)
        c = yv - mu
        rstd = lax.rsqrt(jnp.mean(c * c, axis=-1, keepdims=True) + NORM_EPS)
        yhat = c * rstd
        yn = yhat * gv + bv
        sig = _sigmoid(yn)
        dyn = dv.astype(F32) * (sig * (1.0 + yn * (1.0 - sig)))
        dyhat = dyn * gv
        dyc = rstd * (dyhat - jnp.mean(dyhat, axis=-1, keepdims=True)
                      - yhat * jnp.mean(dyhat * yhat, axis=-1, keepdims=True))
        return (dyc,), (dyn * yhat, dyn, dyc)
    cw = yc.shape[1]
    (dyc,), (dg, db, dbias) = _rowwise(fn, name, [dys, yc], [ln_g, ln_b], [_sds(yc.shape, F32)], [cw, cw, cw])
    return dyc, dg, db, dbias


MM_ROWS = 1024
MM_SEG_ROWS = 256


def _ffn_in(h, w_blk, blk0, name):
    t, d = h.shape
    bn = w_blk.shape[2]
    half = N_DEV // 2
    f = half * bn
    tm = _tile(t, MM_ROWS, V7X_SUBLANES)
    seg_rows = _tile(tm, MM_SEG_ROWS, 2 * V7X_SUBLANES)

    def body(h_ref, wg_ref, wu_ref, dg_ref, du_ref, a_ref):
        for seg in range(tm // seg_rows):
            rows = pl.ds(seg * seg_rows, seg_rows)
            hv = h_ref[rows, :]
            g = jnp.dot(hv, wg_ref[...], preferred_element_type=F32)
            u = jnp.dot(hv, wu_ref[...], preferred_element_type=F32)
            sig = _sigmoid(g)
            sl = g * sig
            dg_ref[rows, :] = (u * (sig * (1.0 + g * (1.0 - sig)))).astype(BF)
            du_ref[rows, :] = sl.astype(BF)
            a_ref[rows, :] = (sl * u).astype(BF)

    out = _sds((t, f), BF)
    tile = pl.BlockSpec((tm, bn), lambda j, i: (i, j))
    return _call(
        body, name=name, grid=(half, t // tm),
        in_specs=[pl.BlockSpec((tm, d), lambda j, i: (i, 0)),
                  pl.BlockSpec((None, d, bn), lambda j, i: (blk0 + j, 0, 0)),
                  pl.BlockSpec((None, d, bn), lambda j, i: (blk0 + half + j, 0, 0))],
        out_specs=[tile, tile, tile], out_shape=[out, out, out], sem=("parallel", "parallel"))(h, w_blk, w_blk)


def _in_proj(h, w_blk, bias, name):
    t, d = h.shape
    bn = w_blk.shape[2]
    tm = _tile(t, 1024, V7X_SUBLANES)

    def body(*refs):
        if bias is None:
            h_ref, w_ref, o_ref = refs
            o_ref[...] = jnp.dot(h_ref[...], w_ref[...], preferred_element_type=F32).astype(BF)
        else:
            h_ref, w_ref, b_ref, o_ref = refs
            o_ref[...] = (jnp.dot(h_ref[...], w_ref[...], preferred_element_type=F32) + b_ref[...]).astype(BF)

    in_specs = [pl.BlockSpec((tm, d), lambda j, i: (i, 0)), pl.BlockSpec((None, d, bn), lambda j, i: (j, 0, 0))]
    args = [h, w_blk]
    if bias is not None:
        in_specs.append(pl.BlockSpec((1, bn), lambda j, i: (0, j)))
        args.append(bias)
    return _call(body, name=name, grid=(N_DEV, t // tm), in_specs=in_specs,
                 out_specs=pl.BlockSpec((tm, bn), lambda j, i: (i, j)), out_shape=_sds((t, N_DEV * bn), BF),
                 sem=("parallel", "parallel"))(*args)


def _out_proj(a, w3, widx, x, gate, bias, coef, name):
    t, k = a.shape
    d = w3.shape[2]
    tm = _tile(t, MM_ROWS, V7X_SUBLANES)
    tn = _tile(d, 512, V7X_LANES)

    def body(*refs):
        if bias is None:
            a_ref, w_ref, x_ref, g_ref, xo_ref, y_ref = refs
            y = jnp.dot(a_ref[...], w_ref[...], preferred_element_type=F32)
        else:
            a_ref, w_ref, x_ref, g_ref, b_ref, xo_ref, y_ref = refs
            y = jnp.dot(a_ref[...], w_ref[...], preferred_element_type=F32) + b_ref[...]
        y_ref[...] = y.astype(BF)
        xo_ref[...] = x_ref[...] + (coef * g_ref[...]) * y

    tile = pl.BlockSpec((tm, tn), lambda j, i: (i, j))
    vec = pl.BlockSpec((1, tn), lambda j, i: (0, j))
    in_specs = [pl.BlockSpec((tm, k), lambda j, i: (i, 0)),
                pl.BlockSpec((None, k, tn), lambda j, i: (widx, 0, j)), tile, vec]
    args = [a, w3, x, gate]
    if bias is not None:
        in_specs.append(vec)
        args.append(bias)
    return _call(body, name=name, grid=(d // tn, t // tm), in_specs=in_specs, out_specs=[tile, tile],
                 out_shape=[_sds((t, d), F32), _sds((t, d), BF)], sem=("parallel", "parallel"))(*args)


def _ffn_da(dy, w3, widx, fg, fu, name):
    t, d = dy.shape
    f = w3.shape[1]
    bn = f // (N_DEV // 2)
    tm = _tile(t, MM_ROWS, V7X_SUBLANES)
    seg_rows = _tile(tm, MM_SEG_ROWS, 2 * V7X_SUBLANES)

    def body(dy_ref, w_ref, fg_ref, fu_ref, dgu_ref):
        for seg in range(tm // seg_rows):
            rows = pl.ds(seg * seg_rows, seg_rows)
            da = lax.dot_general(dy_ref[rows, :], w_ref[...], (((1,), (1,)), ((), ())), preferred_element_type=F32)
            dgu_ref[0, rows, :] = (da * fg_ref[rows, :].astype(F32)).astype(BF)
            dgu_ref[1, rows, :] = (da * fu_ref[rows, :].astype(F32)).astype(BF)

    tile = pl.BlockSpec((tm, bn), lambda j, i: (i, j))
    return _call(
        body, name=name, grid=(f // bn, t // tm),
        in_specs=[pl.BlockSpec((tm, d), lambda j, i: (i, 0)),
                  pl.BlockSpec((None, bn, d), lambda j, i: (widx, j, 0)), tile, tile],
        out_specs=pl.BlockSpec((2, tm, bn), lambda j, i: (0, i, j)),
        out_shape=_sds((2, t, f), BF), sem=("parallel", "parallel"))(dy, w3, fg, fu)


def _mm_nt(dy, w3, widx, name):
    t, k = dy.shape
    n = w3.shape[1]
    tm = _tile(t, MM_ROWS, V7X_SUBLANES)
    tn = _tile(n, 1024, V7X_LANES)

    def body(dy_ref, w_ref, o_ref):
        o_ref[...] = lax.dot_general(dy_ref[...], w_ref[...], (((1,), (1,)), ((), ())),
                                     preferred_element_type=F32).astype(BF)

    return _call(body, name=name, grid=(n // tn, t // tm),
                 in_specs=[pl.BlockSpec((tm, k), lambda j, i: (i, 0)),
                           pl.BlockSpec((None, tn, k), lambda j, i: (widx, j, 0))],
                 out_specs=pl.BlockSpec((tm, tn), lambda j, i: (i, j)), out_shape=_sds((t, n), BF),
                 sem=("parallel", "parallel"))(dy, w3)


NORM_BWD_ROWS = 128


def _dh_norm_bwd(z3, w_blk, x, dxp, g, scale, nxt, name):
    lead, t, _ = z3.shape
    d, bn = w_blk.shape[1], w_blk.shape[2]
    per = N_DEV // lead
    tm = _tile(t, 512, NORM_BWD_ROWS)
    ni = t // tm
    n_in = 6 if nxt is None else 8
    coef, colsum = (None, False) if nxt is None else nxt[2:]
    n_acc = 3 if nxt is None else (5 if colsum else 4)
    n_rows = 1 if nxt is None else 2

    def body(*refs):
        z_ref, w_ref, x_ref, dp_ref, g_ref, sc_ref = refs[:6]
        row_outs = refs[n_in:n_in + n_rows]
        vec_outs = refs[n_in + n_rows:n_in + n_rows + n_acc]
        acc_ref, accs = refs[n_in + n_rows + n_acc], refs[n_in + n_rows + n_acc + 1:]
        i, k = pl.program_id(0), pl.program_id(1)

        @pl.when(k == 0)
        def _():
            acc_ref[...] = jnp.zeros_like(acc_ref)

        @pl.when((i == 0) & (k == 0))
        def _():
            for a in accs:
                a[...] = jnp.zeros_like(a)

        acc_ref[...] += lax.dot_general(z_ref[...], w_ref[...], (((1,), (1,)), ((), ())),
                                        preferred_element_type=F32)

        @pl.when(k == N_DEV - 1)
        def _():
            gv, sc = g_ref[...], sc_ref[...]

            def chunk(ci, carry):
                rows = pl.ds(pl.multiple_of(ci * NORM_BWD_ROWS, NORM_BWD_ROWS), NORM_BWD_ROWS)
                dh, xv = acc_ref[rows, :], x_ref[rows, :]
                r = lax.rsqrt(jnp.mean(xv * xv, axis=-1, keepdims=True) + NORM_EPS)
                xhat = xv * r
                dn = dh * (1.0 + sc)
                dxhat = dn * gv
                dx = r * (dxhat - xhat * jnp.mean(dxhat * xhat, axis=-1, keepdims=True)) + dp_ref[rows, :]
                row_outs[0][rows, :] = dx
                accs[0][...] += _fold_rows(dh * (xhat * gv))
                accs[1][...] += _fold_rows(dh)
                accs[2][...] += _fold_rows(dn * xhat)
                if nxt is not None:
                    y_ref, gate_ref = refs[6], refs[7]
                    dy = (coef * gate_ref[...]) * dx
                    row_outs[1][rows, :] = dy.astype(BF)
                    accs[3][...] += _fold_rows(coef * dx * y_ref[rows, :].astype(F32))
                    if colsum:
                        accs[4][...] += _fold_rows(dy)
                return carry
            lax.fori_loop(0, tm // NORM_BWD_ROWS, chunk, 0)

        @pl.when((i == ni - 1) & (k == N_DEV - 1))
        def _():
            for o, a in zip(vec_outs, accs):
                o[...] = jnp.sum(a[...], axis=0, keepdims=True)

    rows = pl.BlockSpec((tm, d), lambda i, k: (i, 0))
    vec = pl.BlockSpec((1, d), lambda i, k: (0, 0))
    in_specs = [pl.BlockSpec((None, tm, bn), lambda i, k: (k // per, i, k % per)),
                pl.BlockSpec((None, d, bn), lambda i, k: (k, 0, 0)), rows, rows, vec, vec]
    args = [z3, w_blk, x, dxp, g, scale]
    out_specs, out_shape = [rows], [_sds((t, d), F32)]
    if nxt is not None:
        in_specs += [rows, vec]
        args += [nxt[0], nxt[1]]
        out_specs.append(rows)
        out_shape.append(_sds((t, d), BF))
    out_specs += [vec] * n_acc
    out_shape += [_sds((1, d), F32)] * n_acc
    return _call(body, name=name, grid=(ni, N_DEV), in_specs=in_specs, out_specs=out_specs, out_shape=out_shape,
                 scratch=[pltpu.VMEM((tm, d), F32)] + [pltpu.VMEM((V7X_SUBLANES, d), F32)] * n_acc,
                 sem=("arbitrary", "arbitrary"))(*args)


def _mm_tn(a, b3, ta, tb, blocked, name):
    t, ka = a.shape
    lead, _, w = b3.shape
    per = w // tb
    nj = lead * per
    tk = t
    while tk > 512 and 4 * tk * (ta + tb) + 8 * ta * tb > V7X_VMEM_LIMIT * 3 // 4:
        tk //= 2
    tk = _tile(t, tk, V7X_SUBLANES)
    nk = t // tk

    def body(a_ref, b_ref, o_ref, acc_ref):
        k = pl.program_id(2)

        @pl.when(k == 0)
        def _():
            acc_ref[...] = jnp.zeros_like(acc_ref)

        acc_ref[...] += lax.dot_general(a_ref[...], b_ref[...], (((0,), (0,)), ((), ())),
                                        preferred_element_type=F32)

        @pl.when(k == nk - 1)
        def _():
            o_ref[...] = acc_ref[...].astype(BF)

    if blocked:
        out_shape = _sds((nj, ka, tb), BF)
        out_spec = pl.BlockSpec((None, ta, tb), lambda i, j, k: (j, i, 0))
    else:
        out_shape = _sds((1, ka, w), BF)
        out_spec = pl.BlockSpec((None, ta, tb), lambda i, j, k: (0, i, j))
    return _call(body, name=name, grid=(ka // ta, nj, nk),
                 in_specs=[pl.BlockSpec((tk, ta), lambda i, j, k: (k, i)),
                           pl.BlockSpec((None, tk, tb), lambda i, j, k: (j // per, k, j % per))],
                 out_specs=out_spec, out_shape=out_shape, scratch=[pltpu.VMEM((ta, tb), F32)],
                 sem=("parallel", "parallel", "arbitrary"))(a, b3)


def _causal(ws):
    l = ws.shape[0]
    row = lax.broadcasted_iota(jnp.int32, (l, l), 0)
    col = lax.broadcasted_iota(jnp.int32, (l, l), 1)
    return jnp.where(col <= row, ws, 0.0)


def _sgu_fwd(u, vn, ws, bsb, name):
    t, e = u.shape
    hn, l, _ = ws.shape
    dh = e // hn
    nc = t // l

    def body(u_ref, v_ref, ws_ref, bs_ref, s_ref):
        wsc = _causal(ws_ref[...]).astype(BF)
        bias = bs_ref[...]

        def chunk(c, carry):
            rows = pl.ds(pl.multiple_of(c * l, l), l)
            vo = jnp.dot(wsc, v_ref[rows, :], preferred_element_type=F32) + bias
            s_ref[rows, :] = (u_ref[rows, :].astype(F32) * vo).astype(BF)
            return carry
        lax.fori_loop(0, nc, chunk, 0)

    col = pl.BlockSpec((t, dh), lambda h: (0, h))
    return _call(body, name=name, grid=(hn,),
                 in_specs=[col, col, pl.BlockSpec((None, l, l), lambda h: (h, 0, 0)),
                           pl.BlockSpec((None, l, dh), lambda h: (h, 0, 0))],
                 out_specs=col, out_shape=_sds((t, e), BF), sem=("parallel",))(u, vn, ws, bsb)


def _sgu_bwd(ds, u, vn, ws, bsb, name):
    t, e = u.shape
    hn, l, _ = ws.shape
    dh = e // hn
    nc = t // l

    def body(ds_ref, u_ref, v_ref, ws_ref, bs_ref, du_ref, dv_ref, dws_ref, dbs_ref, accw_ref, accb_ref):
        wsc = _causal(ws_ref[...]).astype(BF)
        bias = bs_ref[...]
        accw_ref[...] = jnp.zeros_like(accw_ref)
        accb_ref[...] = jnp.zeros_like(accb_ref)

        def chunk(c, carry):
            rows = pl.ds(pl.multiple_of(c * l, l), l)
            vc = v_ref[rows, :]
            dsv = ds_ref[rows, :].astype(F32)
            vo = jnp.dot(wsc, vc, preferred_element_type=F32) + bias
            du_ref[rows, :] = (dsv * vo).astype(BF)
            dvo = dsv * u_ref[rows, :].astype(F32)
            dvo_b = dvo.astype(BF)
            accb_ref[...] += dvo
            accw_ref[...] += lax.dot_general(dvo_b, vc, (((1,), (1,)), ((), ())), preferred_element_type=F32)
            dv_ref[rows, :] = lax.dot_general(wsc, dvo_b, (((0,), (0,)), ((), ())),
                                              preferred_element_type=F32).astype(BF)
            return carry
        lax.fori_loop(0, nc, chunk, 0)
        dws_ref[...] = _causal(accw_ref[...])
        dbs_ref[...] = jnp.broadcast_to(jnp.sum(accb_ref[...], axis=1, keepdims=True), (l, dh))

    col = pl.BlockSpec((t, dh), lambda h: (0, h))
    return _call(body, name=name, grid=(hn,),
                 in_specs=[col, col, col, pl.BlockSpec((None, l, l), lambda h: (h, 0, 0)),
                           pl.BlockSpec((None, l, dh), lambda h: (h, 0, 0))],
                 out_specs=[col, col, pl.BlockSpec((None, l, l), lambda h: (h, 0, 0)),
                            pl.BlockSpec((None, l, dh), lambda h: (h, 0, 0))],
                 out_shape=[_sds((t, e), BF), _sds((t, e), BF), _sds((hn, l, l), F32), _sds((hn, l, dh), F32)],
                 scratch=[pltpu.VMEM((l, l), F32), pltpu.VMEM((l, dh), F32)],
                 sem=("parallel",))(ds, u, vn, ws, bsb)


AG_AHEAD = 3

CONV_HALO = 32
CONV_ROWS = 64
CONV_LANES = 256


def _shifted_windows(win_ref, sh_ref, rows):
    for b in range(1, V7X_SUBLANES):
        sh_ref[b - 1, 0:rows, :] = win_ref[b:b + rows, :]


def _window_rows(win_ref, sh_ref, shift, r0, rows):
    a, b = divmod(shift, V7X_SUBLANES)
    start = pl.multiple_of(r0 + V7X_SUBLANES * a, V7X_SUBLANES)
    if b == 0:
        return win_ref[pl.ds(start, rows), :]
    return sh_ref[b - 1, pl.ds(start, rows), :]


def _dwconv_fwd(p, dw_w, dw_b, name):
    t, c2 = p.shape
    cw = c2 // 2
    kw = dw_w.shape[0]
    cb = _tile(cw, CONV_LANES, V7X_LANES)
    ncb = cw // cb
    tm = _tile(t, 512, CONV_ROWS)
    off = CONV_HALO - (kw - 1)

    def body(a_ref, g_ref, ap_ref, gp_ref, w_ref, b_ref, o_ref, win_ref, sh_ref):
        i = pl.program_id(1)
        prev = ap_ref[...].astype(F32) * _sigmoid(gp_ref[...].astype(F32))
        win_ref[0:CONV_HALO, :] = jnp.where(i > 0, prev, 0.0)
        win_ref[CONV_HALO:, :] = a_ref[...].astype(F32) * _sigmoid(g_ref[...].astype(F32))
        _shifted_windows(win_ref, sh_ref, tm + CONV_HALO - V7X_SUBLANES)

        def chunk(ci, carry):
            r0 = ci * CONV_ROWS
            acc = jnp.zeros((CONV_ROWS, cb), F32) + b_ref[...]
            for k in range(kw):
                acc = acc + w_ref[k:k + 1, :] * _window_rows(win_ref, sh_ref, off + k, r0, CONV_ROWS)
            o_ref[pl.ds(pl.multiple_of(r0, CONV_ROWS), CONV_ROWS), :] = acc
            return carry
        lax.fori_loop(0, tm // CONV_ROWS, chunk, 0)

    hpt = tm // CONV_HALO
    cur_a = pl.BlockSpec((tm, cb), lambda j, i: (i, j))
    cur_g = pl.BlockSpec((tm, cb), lambda j, i: (i, ncb + j))
    prev_a = pl.BlockSpec((CONV_HALO, cb), lambda j, i: (jnp.maximum(i * hpt - 1, 0), j))
    prev_g = pl.BlockSpec((CONV_HALO, cb), lambda j, i: (jnp.maximum(i * hpt - 1, 0), ncb + j))
    return _call(body, name=name, grid=(ncb, t // tm),
                 in_specs=[cur_a, cur_g, prev_a, prev_g, pl.BlockSpec((kw, cb), lambda j, i: (0, j)),
                           pl.BlockSpec((1, cb), lambda j, i: (0, j))],
                 out_specs=pl.BlockSpec((tm, cb), lambda j, i: (i, j)), out_shape=_sds((t, cw), F32),
                 scratch=[pltpu.VMEM((tm + CONV_HALO, cb), F32),
                          pltpu.VMEM((V7X_SUBLANES - 1, tm + CONV_HALO - V7X_SUBLANES, cb), F32)],
                 sem=("parallel", "parallel"))(p, p, p, p, dw_w, dw_b)


def _dwconv_bwd(dyc, p, dw_w, name):
    t, c2 = p.shape
    cw = c2 // 2
    kw = dw_w.shape[0]
    cb = _tile(cw, CONV_LANES, V7X_LANES)
    ncb = cw // cb
    tm = _tile(t, 512, CONV_ROWS)
    nt = t // tm
    off = CONV_HALO - (kw - 1)
    kpad = -(-kw // V7X_SUBLANES) * V7X_SUBLANES
    sh_rows = tm + CONV_HALO - V7X_SUBLANES

    def body(d_ref, dn_ref, a_ref, g_ref, ap_ref, gp_ref, w_ref,
             dp_ref, dw_ref, dba_ref, dbg_ref, dwin_ref, ywin_ref, dsh_ref, ysh_ref, accw_ref, acca_ref, accg_ref):
        i = pl.program_id(1)

        @pl.when(i == 0)
        def _():
            accw_ref[...] = jnp.zeros_like(accw_ref)
            acca_ref[...] = jnp.zeros_like(acca_ref)
            accg_ref[...] = jnp.zeros_like(accg_ref)

        prev = ap_ref[...].astype(F32) * _sigmoid(gp_ref[...].astype(F32))
        ywin_ref[0:CONV_HALO, :] = jnp.where(i > 0, prev, 0.0)
        ywin_ref[CONV_HALO:, :] = a_ref[...].astype(F32) * _sigmoid(g_ref[...].astype(F32))
        dwin_ref[0:tm, :] = d_ref[...]
        dwin_ref[tm:, :] = jnp.where(i < nt - 1, dn_ref[...], 0.0)
        _shifted_windows(ywin_ref, ysh_ref, sh_rows)
        _shifted_windows(dwin_ref, dsh_ref, sh_rows)

        def chunk(ci, carry):
            r0 = ci * CONV_ROWS
            rows = pl.ds(pl.multiple_of(r0, CONV_ROWS), CONV_ROWS)
            dcur = d_ref[rows, :]
            dyg = jnp.zeros((CONV_ROWS, cb), F32)
            for k in range(kw):
                dyg = dyg + w_ref[k:k + 1, :] * _window_rows(dwin_ref, dsh_ref, kw - 1 - k, r0, CONV_ROWS)
                accw_ref[k] += _fold_rows(dcur * _window_rows(ywin_ref, ysh_ref, off + k, r0, CONV_ROWS))
            av = a_ref[rows, :].astype(F32)
            sig = _sigmoid(g_ref[rows, :].astype(F32))
            da = dyg * sig
            dg = dyg * av * sig * (1.0 - sig)
            dp_ref[0, rows, :] = da.astype(BF)
            dp_ref[1, rows, :] = dg.astype(BF)
            acca_ref[...] += _fold_rows(da)
            accg_ref[...] += _fold_rows(dg)
            return carry
        lax.fori_loop(0, tm // CONV_ROWS, chunk, 0)

        @pl.when(i == nt - 1)
        def _():
            dw_ref[...] = jnp.sum(accw_ref[...], axis=1)
            dba_ref[...] = jnp.sum(acca_ref[...], axis=0, keepdims=True)
            dbg_ref[...] = jnp.sum(accg_ref[...], axis=0, keepdims=True)

    hpt = tm // CONV_HALO
    last_halo = t // CONV_HALO - 1
    tile = pl.BlockSpec((tm, cb), lambda j, i: (i, j))
    cur_g = pl.BlockSpec((tm, cb), lambda j, i: (i, ncb + j))
    nxt = pl.BlockSpec((CONV_HALO, cb), lambda j, i: (jnp.minimum((i + 1) * hpt, last_halo), j))
    prev_a = pl.BlockSpec((CONV_HALO, cb), lambda j, i: (jnp.maximum(i * hpt - 1, 0), j))
    prev_g = pl.BlockSpec((CONV_HALO, cb), lambda j, i: (jnp.maximum(i * hpt - 1, 0), ncb + j))
    vec = pl.BlockSpec((1, cb), lambda j, i: (0, j))
    dp, ddw, dba, dbg = _call(
        body, name=name, grid=(ncb, nt),
        in_specs=[tile, nxt, tile, cur_g, prev_a, prev_g, pl.BlockSpec((kw, cb), lambda j, i: (0, j))],
        out_specs=[pl.BlockSpec((2, tm, cb), lambda j, i: (0, i, j)), pl.BlockSpec((kpad, cb), lambda j, i: (0, j)),
                   vec, vec],
        out_shape=[_sds((2, t, cw), BF), _sds((kpad, cw), F32), _sds((1, cw), F32), _sds((1, cw), F32)],
        scratch=[pltpu.VMEM((tm + CONV_HALO, cb), F32), pltpu.VMEM((tm + CONV_HALO, cb), F32),
                 pltpu.VMEM((V7X_SUBLANES - 1, sh_rows, cb), F32), pltpu.VMEM((V7X_SUBLANES - 1, sh_rows, cb), F32),
                 pltpu.VMEM((kpad, V7X_SUBLANES, cb), F32), pltpu.VMEM((V7X_SUBLANES, cb), F32),
                 pltpu.VMEM((V7X_SUBLANES, cb), F32)],
        sem=("parallel", "arbitrary"))(dyc, dyc, p, p, p, p, dw_w)
    return dp, ddw[:kw], dba, dbg


def _adam_math(g, w, m, v):
    m2 = ADAM_B1 * m + (1.0 - ADAM_B1) * g
    v2 = ADAM_B2 * v + (1.0 - ADAM_B2) * (g * g)
    m_hat = m2 / (1.0 - ADAM_B1 ** ADAM_STEP)
    v_hat = v2 / (1.0 - ADAM_B2 ** ADAM_STEP)
    delta = -ADAM_LR * (m_hat / (jnp.sqrt(v_hat) + ADAM_EPS) + ADAM_WD * w)
    return delta, m2, v2


def _adamw(g_parts, w, m, v, name):
    r, c = w.shape
    tr = _tile(r, 256, V7X_SUBLANES)
    ng = len(g_parts)

    def body(*refs):
        g = refs[0][...].astype(F32)
        for s in refs[1:ng]:
            g = g + s[...].astype(F32)
        w_ref, m_ref, v_ref, go_ref, d_ref, mo_ref, vo_ref = refs[ng:]
        delta, m2, v2 = _adam_math(g, w_ref[...], m_ref[...], v_ref[...])
        go_ref[...] = g
        d_ref[...] = delta
        mo_ref[...] = m2
        vo_ref[...] = v2

    tile = pl.BlockSpec((tr, c), lambda i: (i, 0))
    in_specs = [pl.BlockSpec((None, tr, c), functools.partial(lambda s, i: (s, i, 0), s)) for _, s in g_parts]
    out = _sds((r, c), F32)
    return _call(body, name=name, grid=(r // tr,), in_specs=in_specs + [tile, tile, tile],
                 out_specs=[tile] * 4, out_shape=[out] * 4, sem=("parallel",))(*[a for a, _ in g_parts], w, m, v)


def _adamw_stacked(h, recv, chip, w_st, m_st, v_st, k, prev, name):
    kk, r, c = w_st.shape
    tr = _tile(r, 256, V7X_SUBLANES)
    if prev is None:
        prev = [lax.empty((kk, r, c), F32) for _ in range(4)]

    def body(chip_ref, h_ref, r0_ref, r1_ref, r2_ref, w_ref, m_ref, v_ref, pg, pd, pm, pv,
             go_ref, d_ref, mo_ref, vo_ref):
        g = (h_ref[...].astype(F32) + r0_ref[...].astype(F32)) + (r1_ref[...].astype(F32) + r2_ref[...].astype(F32))
        delta, m2, v2 = _adam_math(g, w_ref[...], m_ref[...], v_ref[...])
        go_ref[...] = g
        d_ref[...] = delta
        mo_ref[...] = m2
        vo_ref[...] = v2

    own = pl.BlockSpec((None, tr, c), lambda i, chip_ref: (chip_ref[0], i, 0))
    rcv = [pl.BlockSpec((None, tr, c), functools.partial(lambda s, i, chip_ref: (s, i, 0), s)) for s in range(3)]
    blk = pl.BlockSpec((None, tr, c), lambda i, chip_ref: (k, i, 0))
    out = _sds((kk, r, c), F32)
    return _call(body, name=name, grid=(r // tr,), in_specs=[own] + rcv + [blk, blk, blk] + [ANY] * 4,
                 out_specs=[blk] * 4, out_shape=[out] * 4, sem=("parallel",), prefetch=chip, on_path=False,
                 aliases={8: 0, 9: 1, 10: 2, 11: 3})(h, recv, recv, recv, w_st, m_st, v_st, *prev)


def _add_sibling(g4, land, core, name):
    n, _, r, c = g4.shape
    tr = _tile(r, 1024, 2 * V7X_SUBLANES)

    def body(core_ref, a_ref, b_ref, o_ref):
        o_ref[...] = (a_ref[...].astype(F32) + b_ref[...].astype(F32)).astype(BF)

    return _call(body, name=name, grid=(n, r // tr),
                 in_specs=[pl.BlockSpec((None, None, tr, c), lambda p, i, core_ref: (p, core_ref[0], i, 0)),
                           pl.BlockSpec((None, None, tr, c), lambda p, i, core_ref: (p, 0, i, 0))],
                 out_specs=pl.BlockSpec((None, tr, c), lambda p, i, core_ref: (p, i, 0)),
                 out_shape=_sds((n, r, c), BF), sem=("parallel", "parallel"), prefetch=core)(g4, land)


def _cast_to_slot(w, lead, me, name, after=(), dtype=BF):
    r, c = w.shape[-2:]
    nl = len(lead)
    tr = _tile(r, 1024, 2 * V7X_SUBLANES)

    def body(me_ref, w_ref, o_ref):
        o_ref[...] = w_ref[...].astype(dtype)

    return _call(body, name=name, grid=(r // tr,),
                 in_specs=[pl.BlockSpec((None,) * nl + (tr, c), lambda i, me_ref: tuple(lead) + (i, 0))],
                 out_specs=pl.BlockSpec((None, None, tr, c), lambda i, me_ref: (0, me_ref[0], i, 0)),
                 out_shape=_sds((1, N_DEV, r, c), dtype), sem=("parallel",), prefetch=me, after=after)(w)


def _ada_fwd(c_pad, ada_w, ada_b, name):
    nl, d, cl = ada_w.shape
    rows = c_pad.shape[0]
    tn = _tile(cl, 256, V7X_LANES)

    def body(c_ref, w_ref, b_ref, o_ref):
        cv = c_ref[...]
        cond = (cv * _sigmoid(cv)).astype(BF)
        o_ref[...] = jnp.dot(cond, w_ref[...].astype(BF), preferred_element_type=F32) + b_ref[...]

    return _call(body, name=name, grid=(nl, cl // tn),
                 in_specs=[pl.BlockSpec((rows, d), lambda l, j: (0, 0)),
                           pl.BlockSpec((None, d, tn), lambda l, j: (l, 0, j)),
                           pl.BlockSpec((None, 1, tn), lambda l, j: (l, 0, j))],
                 out_specs=pl.BlockSpec((None, rows, tn), lambda l, j: (l, 0, j)),
                 out_shape=_sds((nl, rows, cl), F32), sem=("parallel", "parallel"))(c_pad, ada_w, ada_b)


def _ada_bwd(c_pad, dmod, w, m, v, name):
    nl, d, cl = w.shape
    rows = c_pad.shape[0]
    tn = _tile(cl, 256, V7X_LANES)

    def body(c_ref, dm_ref, w_ref, m_ref, v_ref, go_ref, d_ref, mo_ref, vo_ref):
        cv = c_ref[...]
        cond = (cv * _sigmoid(cv)).astype(BF)
        g = lax.dot_general(cond, dm_ref[...].astype(BF), (((0,), (0,)), ((), ())), preferred_element_type=F32)
        delta, m2, v2 = _adam_math(g, w_ref[...], m_ref[...], v_ref[...])
        go_ref[...] = g
        d_ref[...] = delta
        mo_ref[...] = m2
        vo_ref[...] = v2

    tile = pl.BlockSpec((None, d, tn), lambda l, j: (l, 0, j))
    out = _sds((nl, d, cl), F32)
    return _call(body, name=name, grid=(nl, cl // tn),
                 in_specs=[pl.BlockSpec((rows, d), lambda l, j: (0, 0)),
                           pl.BlockSpec((None, rows, tn), lambda l, j: (l, 0, j)), tile, tile, tile],
                 out_specs=[tile] * 4, out_shape=[out] * 4, sem=("parallel", "parallel"))(c_pad, dmod, w, m, v)


def _sum_devices(parts, name):
    n, r, c = parts.shape
    tr = _tile(r, 512, V7X_SUBLANES)

    def body(p_ref, o_ref):
        acc = p_ref[0]
        for k in range(1, n):
            acc = acc + p_ref[k]
        o_ref[...] = acc

    return _call(body, name=name, grid=(r // tr,), in_specs=[pl.BlockSpec((n, tr, c), lambda i: (0, i, 0))],
                 out_specs=pl.BlockSpec((tr, c), lambda i: (i, 0)), out_shape=_sds((r, c), F32),
                 sem=("parallel",))(parts)


def _mesh_pos():
    return lax.axis_index("x"), lax.axis_index("y"), lax.axis_index("c")


def _other_chips(x, y):
    return [(1 - x, y), (x, 1 - y), (1 - x, 1 - y)]


def _all_gather(arrs, name):
    n = len(arrs)

    def body(*refs):
        ins, outs = refs[:n], refs[n:2 * n]
        send_sems, recv_sems, local_sems = refs[2 * n:]
        x, y, c = _mesh_pos()
        me, sibling = (x, y, c), (x, y, 1 - c)
        chips = _other_chips(x, y)

        def slot(a, pos):
            px, py, pc = pos
            return outs[a].at[:, pl.ds(4 * px + 2 * py + pc, 1)]

        def copy(a, k, block, to, src=None):
            return pltpu.make_async_remote_copy(
                src_ref=slot(a, block) if src is None else src, dst_ref=slot(a, block),
                send_sem=send_sems.at[a, k], recv_sem=recv_sems.at[a, k], device_id=to, device_id_type=MESH)

        mine = [pltpu.make_async_copy(ins[a], slot(a, me), local_sems.at[a]) for a in range(n)]
        for cp in mine:
            cp.start()
        first = []
        for a in range(n):
            first.append(copy(a, 0, me, sibling, src=ins[a]))
            first += [copy(a, 1 + j, me, (*chip, c), src=ins[a]) for j, chip in enumerate(chips)]
        for cp in first:
            cp.start()
        passed = []
        for a in range(n):
            for j, chip in enumerate(chips):
                copy(a, 1 + j, (*chip, c), me).wait_recv()
                fwd = copy(a, 4 + j, (*chip, c), sibling)
                fwd.start()
                passed.append(fwd)
        for a in range(n):
            copy(a, 0, sibling, me).wait_recv()
            for j, chip in enumerate(chips):
                copy(a, 4 + j, (*chip, 1 - c), me).wait_recv()
        for cp in first + passed:
            cp.wait_send()
        for cp in mine:
            cp.wait()

    out_shape = [_sds((a.shape[0], N_DEV) + a.shape[2:], a.dtype) for a in arrs]
    return pl.pallas_call(
        body, out_shape=out_shape, in_specs=[ANY] * n, out_specs=[ANY] * n, name=name,
        scratch_shapes=[pltpu.SemaphoreType.DMA((n, N_DEV - 1)), pltpu.SemaphoreType.DMA((n, N_DEV - 1)),
                        pltpu.SemaphoreType.DMA((n,))])(*arrs)


HBM = pl.BlockSpec(memory_space=pltpu.HBM)
SEM = pl.BlockSpec(memory_space=pltpu.SEMAPHORE)


def _hbm(v):
    return pltpu.with_memory_space_constraint(v, pltpu.HBM)


def _comm_call(body, name, bufs, sems_in, sems_out, follow=()):
    after = [] if not sems_in or _Seq.last is None or any(_Seq.last is b for b in bufs) else [_Seq.last]
    after += list(follow)
    nb, ni, na, no = len(bufs), len(sems_in), len(after), len(sems_out)

    def wrapped(*refs):
        body(refs[:nb], refs[nb:nb + ni], refs[nb + ni + na:nb + ni + na + no])
        if no:
            refs[-1][...] = jnp.zeros_like(refs[-1])

    out_shape = [pltpu.SemaphoreType.DMA(s) for s in sems_out] + [pltpu.HBM(b.shape, b.dtype) for b in bufs]
    out_specs = [SEM] * no + [HBM] * nb
    if no:
        out_shape.append(_sds((V7X_SUBLANES, V7X_LANES), F32))
        out_specs.append(pl.BlockSpec(memory_space=pltpu.VMEM))
    res = pl.pallas_call(
        wrapped, name=name, out_shape=out_shape, in_specs=[HBM] * nb + [SEM] * ni + [ANY] * na, out_specs=out_specs,
        input_output_aliases={i: no + i for i in range(nb)},
        compiler_params=pltpu.CompilerParams(has_side_effects=pltpu.SideEffectType.DATAFLOW_SIDE_EFFECTING),
    )(*bufs, *sems_in, *after)
    out_bufs = list(res[no:no + nb])
    if no:
        _Seq.tokens.append(res[-1])
    _Seq.last = out_bufs[0]
    return list(res[:no]), out_bufs


def _remote(src, dst, send_sem, recv_sem, to):
    return pltpu.make_async_remote_copy(src_ref=src, dst_ref=dst, send_sem=send_sem, recv_sem=recv_sem,
                                        device_id=to, device_id_type=MESH)


def _slot(ref, pos):
    px, py, pc = pos
    return ref.at[:, pl.ds(4 * px + 2 * py + pc, 1)]


def _ag_start(bufs, name):
    n = len(bufs)

    def body(b, _, sems):
        send_sib, recv_sib, send_ici, recv_ici = sems
        x, y, c = _mesh_pos()
        for a in range(n):
            mine = _slot(b[a], (x, y, c))
            _remote(mine, mine, send_sib.at[a], recv_sib.at[a], (x, y, 1 - c)).start()
            for j, (px, py) in enumerate(_other_chips(x, y)):
                _remote(mine, mine, send_ici.at[3 * a + j], recv_ici.at[3 * a + j], (px, py, c)).start()

    sems, bufs = _comm_call(body, name, [_hbm(b) for b in bufs], [], [(n,), (n,), (3 * n,), (3 * n,)])
    return dict(bufs=bufs, send_sib=sems[0], recv_sib=sems[1], send_ici=sems[2], recv_ici=sems[3])


def _ag_mid(st, name):
    n = len(st["bufs"])

    def body(b, sems_in, sems):
        (recv_ici,) = sems_in
        send_fwd, recv_fwd = sems
        x, y, c = _mesh_pos()
        for a in range(n):
            for j, (px, py) in enumerate(_other_chips(x, y)):
                blk = _slot(b[a], (px, py, c))
                _remote(blk, blk, send_fwd.at[3 * a + j], recv_ici.at[3 * a + j], (x, y, 1 - c)).wait_recv()
                _remote(blk, blk, send_fwd.at[3 * a + j], recv_fwd.at[3 * a + j], (x, y, 1 - c)).start()

    sems, bufs = _comm_call(body, name, st["bufs"], [st["recv_ici"]], [(3 * n,), (3 * n,)])
    return dict(st, bufs=bufs, send_fwd=sems[0], recv_fwd=sems[1])


def _ag_end(st, name):
    n = len(st["bufs"])

    def body(b, sems_in, _):
        send_sib, recv_sib, send_ici, send_fwd, recv_fwd = sems_in
        x, y, c = _mesh_pos()
        sibling = (x, y, 1 - c)
        for a in range(n):
            mine, sib_blk = _slot(b[a], (x, y, c)), _slot(b[a], sibling)
            _remote(mine, mine, send_sib.at[a], recv_sib.at[a], sibling).wait_send()
            _remote(sib_blk, sib_blk, send_sib.at[a], recv_sib.at[a], sibling).wait_recv()
            for j, (px, py) in enumerate(_other_chips(x, y)):
                blk, sib_got = _slot(b[a], (px, py, c)), _slot(b[a], (px, py, 1 - c))
                _remote(mine, mine, send_ici.at[3 * a + j], recv_sib.at[a], (px, py, c)).wait_send()
                _remote(blk, blk, send_fwd.at[3 * a + j], recv_fwd.at[3 * a + j], sibling).wait_send()
                _remote(sib_got, sib_got, send_fwd.at[3 * a + j], recv_fwd.at[3 * a + j], sibling).wait_recv()

    _, bufs = _comm_call(body, name, st["bufs"],
                         [st[k] for k in ("send_sib", "recv_sib", "send_ici", "send_fwd", "recv_fwd")], [])
    return bufs


def _rs_start(g4s, name):
    n = len(g4s)
    lands = [lax.empty((N_CHIP, 1) + g.shape[2:], g.dtype) for g in g4s]

    def body(b, _, sems):
        send, recv = sems
        x, y, c = _mesh_pos()
        for a in range(n):
            _remote(b[a].at[:, pl.ds(1 - c, 1)], b[n + a], send.at[a], recv.at[a], (x, y, 1 - c)).start()

    sems, bufs = _comm_call(body, name, [_hbm(v) for v in list(g4s) + lands], [], [(n,), (n,)])
    return dict(bufs=bufs, send=sems[0], recv=sems[1])


def _rs_mid(st, name):
    n = len(st["bufs"]) // 2

    def body(b, sems_in, _):
        send, recv = sems_in
        x, y, c = _mesh_pos()
        for a in range(n):
            cp = _remote(b[a].at[:, pl.ds(1 - c, 1)], b[n + a], send.at[a], recv.at[a], (x, y, 1 - c))
            cp.wait_send()
            cp.wait_recv()

    _, bufs = _comm_call(body, name, st["bufs"], [st["send"], st["recv"]], [])
    return bufs[:n], bufs[n:]


def _rs_start2(sums, name):
    n = len(sums)
    lands = [lax.empty((N_CHIP - 1,) + s.shape[1:], s.dtype) for s in sums]

    def body(b, _, sems):
        send, recv = sems
        x, y, c = _mesh_pos()
        for a in range(n):
            for j, (px, py) in enumerate(_other_chips(x, y)):
                _remote(b[a].at[pl.ds(2 * px + py, 1)], b[n + a].at[pl.ds(j, 1)], send.at[3 * a + j], recv.at[3 * a + j],
                        (px, py, c)).start()

    sems, bufs = _comm_call(body, name, [_hbm(v) for v in list(sums) + lands], [], [(3 * n,), (3 * n,)])
    return dict(bufs=bufs, send=sems[0], recv=sems[1])


def _rs_end(st, name, follow=()):
    n = len(st["bufs"]) // 2

    def body(b, sems_in, _):
        send, recv = sems_in
        x, y, c = _mesh_pos()
        for a in range(n):
            for j, (px, py) in enumerate(_other_chips(x, y)):
                cp = _remote(b[a].at[pl.ds(2 * px + py, 1)], b[n + a].at[pl.ds(j, 1)], send.at[3 * a + j], recv.at[3 * a + j],
                             (px, py, c))
                cp.wait_send()
                cp.wait_recv()

    _, bufs = _comm_call(body, name, st["bufs"], [st["send"], st["recv"]], [], follow=follow)
    return bufs[:n], bufs[n:]


def _pack(parts, rows_align=V7X_SUBLANES):
    flat, total = [], 0
    for p in parts:
        v = p.reshape(-1).astype(F32)
        pad = -v.shape[0] % PACK_ALIGN
        flat.append(jnp.pad(v, (0, pad)) if pad else v)
        total += v.shape[0] + pad
    tail = -total % (rows_align * V7X_LANES)
    if tail:
        flat.append(jnp.zeros((tail,), F32))
    return jnp.concatenate(flat).reshape(-1, V7X_LANES)


def _unpack(buf, shapes):
    lead = buf.shape[:-2]
    flat = buf.reshape(lead + (-1,))
    out, pos = [], 0
    for s in shapes:
        size = 1
        for d in s:
            size *= d
        out.append(flat[..., pos:pos + size].reshape(lead + tuple(s)))
        pos += size + (-size % PACK_ALIGN)
    return out


def kernel(x, c, ada_w, ada_b, norm_g, ffn_w_in, ffn_w_out, gm_w_in, gm_ln_g, gm_ln_b, gm_ws, gm_bs, gm_w_out, cv_w_in, cv_b_in, cv_dw_w, cv_dw_b, cv_ln_g, cv_ln_b, cv_w_out, cv_b_out, final_g, loss_target, m_ada_w, m_ada_b, m_norm_g, m_ffn_w_in, m_ffn_w_out, m_gm_w_in, m_gm_ln_g, m_gm_ln_b, m_gm_ws, m_gm_bs, m_gm_w_out, m_cv_w_in, m_cv_b_in, m_cv_dw_w, m_cv_dw_b, m_cv_ln_g, m_cv_ln_b, m_cv_w_out, m_cv_b_out, m_final_g, v_ada_w, v_ada_b, v_norm_g, v_ffn_w_in, v_ffn_w_out, v_gm_w_in, v_gm_ln_g, v_gm_ln_b, v_gm_ws, v_gm_bs, v_gm_w_out, v_cv_w_in, v_cv_b_in, v_cv_dw_w, v_cv_dw_b, v_cv_ln_g, v_cv_ln_b, v_cv_w_out, v_cv_b_out, v_final_g):
    t, d = x.shape[1], x.shape[2]
    depth = ada_w.shape[0]
    assert depth == 2 and ffn_w_in.shape[:2] == (2, 2) and gm_w_in.shape[0] == 1 and cv_w_in.shape[0] == 1
    dl = d // N_DEV
    bn = ffn_w_in.shape[3]
    fl = ffn_w_out.shape[2]
    f = fl * N_DEV
    el = gm_w_in.shape[2]
    e = el * N_DEV // 2
    hn, l = gm_ws.shape[1], gm_ws.shape[2]
    kw = cv_dw_w.shape[1]
    cl = ada_w.shape[2]
    me = 4 * lax.axis_index("x") + 2 * lax.axis_index("y") + lax.axis_index("c")
    me1 = me.astype(jnp.int32).reshape(1)
    chip1 = (2 * lax.axis_index("x") + lax.axis_index("y")).astype(jnp.int32).reshape(1)
    core1 = lax.axis_index("c").astype(jnp.int32).reshape(1)
    _Seq.last, _Seq.tokens = None, []

    xs = x[0]
    tgt = loss_target[0]

    ag_groups = [("win00", [(ffn_w_in, (0, 0))]), ("wout00", [(ffn_w_out, (0, 0))]),
                 ("gm", [(gm_w_in, (0,)), (gm_w_out, (0,))]),
                 ("win01", [(ffn_w_in, (0, 1))]), ("wout01", [(ffn_w_out, (0, 1))]),
                 ("win10", [(ffn_w_in, (1, 0))]), ("wout10", [(ffn_w_out, (1, 0))]),
                 ("cv", [(cv_w_in, (0,)), (cv_w_out, (0,))]),
                 ("win11", [(ffn_w_in, (1, 1))]), ("wout11", [(ffn_w_out, (1, 1))])]
    ag_flight = {}

    ag_slots = {}

    def ag_cast(gi, after=()):
        gname, members = ag_groups[gi]
        ag_slots[gi] = [_cast_to_slot(w, lead, me1, name=f"cast_{gname}_{k}", after=after)
                        for k, (w, lead) in enumerate(members)]

    def ag_start(gi):
        ag_flight[gi] = _ag_start(ag_slots.pop(gi), name=f"ag_start_{ag_groups[gi][0]}")

    def ag_forward(gi):
        if gi in ag_flight and "send_fwd" not in ag_flight[gi]:
            ag_flight[gi] = _ag_mid(ag_flight[gi], name=f"ag_mid_{ag_groups[gi][0]}")

    def ag_take(gi):
        ag_forward(gi)
        bufs = _ag_end(ag_flight.pop(gi), name=f"ag_end_{ag_groups[gi][0]}")
        if gi > 0 and gi + 1 != AG_AHEAD:
            ag_forward(gi + 1)
        if gi + AG_AHEAD < len(ag_groups):
            ag_start(gi + AG_AHEAD)
        return [b[0] for b in bufs]

    small_in = [c, norm_g, cv_b_in, cv_dw_w, cv_dw_b, cv_ln_g, cv_ln_b, cv_b_out]
    pack1 = _pack(small_in)
    (pack1_all,) = _all_gather([pack1[None, None]], name="ag_small")
    parts = _unpack(pack1_all[0], [s.shape for s in small_in])
    c_all = parts[0].reshape(N_DEV, d)
    ng_full = jnp.moveaxis(parts[1], 0, 2).reshape(depth, 3, d)
    cvb_in_full = parts[2].reshape(1, 2 * e)
    dww_full = jnp.moveaxis(parts[3][:, 0], 0, 1).reshape(kw, e)
    dwb_full, cln_g_full, cln_b_full, cvb_out_full = [p.reshape(1, d) for p in parts[4:8]]

    c_pad = jnp.pad(c_all, ((0, 16 - N_DEV), (0, 0)))
    ada_b_loc = lax.dynamic_slice_in_dim(ada_b, me * cl, cl, axis=1).reshape(depth, 1, cl)
    mod_part = _ada_fwd(c_pad, ada_w, ada_b_loc, name="ada_fwd")[:, :N_DEV]
    (mod_all,) = _all_gather([_pack([mod_part])[None, None]], name="ag_mod")
    mod_all = _unpack(mod_all[0], [mod_part.shape])[0]
    mod_mine = lax.dynamic_index_in_dim(mod_all, me, axis=2, keepdims=False)
    mod = jnp.moveaxis(mod_mine, 0, 1).reshape(depth, 3, 3, 1, d)

    for gi in range(len(ag_groups)):
        ag_cast(gi, after=[mod_all])
        if gi < AG_AHEAD:
            ag_start(gi)

    ws = gm_ws[0]
    bsb = jnp.broadcast_to(gm_bs[0][:, :, None], (hn, l, e // hn))
    gm_g, gm_b = gm_ln_g, gm_ln_b

    saved = []
    xcur = xs
    next_group = 0
    for i in range(depth):
        for s in range(3):
            shift, scale, gate = mod[i, s, 0], mod[i, s, 1], mod[i, s, 2]
            g_norm = ng_full[i, s][None]
            tag = f"l{i}s{s}"
            h = _norm_mod(xcur, g_norm, scale, shift, name=f"norm_mod_{tag}")
            if s != 1:
                (w_in_blk,) = ag_take(next_group)
                fg, fu, act = _ffn_in(h, w_in_blk, 0, name=f"ffn_in_{tag}")
                w_out3 = ag_take(next_group + 1)[0].reshape(1, f, d)
                next_group += 2
                xnext, yv = _out_proj(act, w_out3, 0, xcur, gate, None, 0.5, name=f"ffn_out_{tag}")
                saved.append(dict(x=xcur, h=h, fg=fg, fu=fu, act=act, y=yv, w_in=w_in_blk, w_out=w_out3))
            elif i % 2 == 0:
                gm_in_blk, gm_out = ag_take(next_group)
                gm_out3 = gm_out.reshape(1, e, d)
                next_group += 1
                pre = _in_proj(h, gm_in_blk, None, name=f"gm_in_{tag}")
                uu, vn = _gm_act(pre, gm_g, gm_b, name=f"gm_act_{tag}")
                sg = _sgu_fwd(uu, vn, ws, bsb, name=f"sgu_fwd_{tag}")
                xnext, yv = _out_proj(sg, gm_out3, 0, xcur, gate, None, 1.0, name=f"gm_out_{tag}")
                saved.append(dict(x=xcur, h=h, pre=pre, u=uu, vn=vn, sg=sg, y=yv, w_in=gm_in_blk, w_out=gm_out3))
            else:
                cv_in_blk, cv_out = ag_take(next_group)
                cv_out3 = cv_out.reshape(1, e, d)
                next_group += 1
                p = _in_proj(h, cv_in_blk, cvb_in_full, name=f"cv_in_{tag}")
                yc = _dwconv_fwd(p, dww_full, dwb_full, name=f"dwconv_fwd_{tag}")
                ys = _cv_act(yc, cln_g_full, cln_b_full, name=f"cv_act_{tag}")
                xnext, yv = _out_proj(ys, cv_out3, 0, xcur, gate, cvb_out_full, 1.0, name=f"cv_out_{tag}")
                saved.append(dict(x=xcur, h=h, p=p, yc=yc, ys=ys, y=yv, w_in=cv_in_blk, w_out=cv_out3))
            xcur = xnext

    def bwd_head(i, s):
        return saved[3 * i + s]["y"], mod[i, s, 2], 0.5 if s != 1 else 1.0, s == 1 and i % 2 == 1

    sq, dx, d_final_g, dy, dgate = _final_loss(xcur, tgt, final_g[None], *bwd_head(depth - 1, 2)[:3], name="final_loss")
    dbout = None
    loss = lax.psum(0.5 / d * jnp.sum(sq), ("x", "y", "c"))

    dmod = [[[None] * 3 for _ in range(3)] for _ in range(depth)]
    d_norm_g = [[None] * 3 for _ in range(depth)]
    small = {}

    stacked = {
        "ffn_w_in": [a.reshape(4, d, bn) for a in (ffn_w_in, m_ffn_w_in, v_ffn_w_in)],
        "ffn_w_out": [a.reshape(4, fl, d) for a in (ffn_w_out, m_ffn_w_out, v_ffn_w_out)],
        "gm_w_in": [gm_w_in, m_gm_w_in, v_gm_w_in], "gm_w_out": [gm_w_out, m_gm_w_out, v_gm_w_out],
        "cv_w_in": [cv_w_in, m_cv_w_in, v_cv_w_in], "cv_w_out": [cv_w_out, m_cv_w_out, v_cv_w_out],
    }
    res_big = {}

    def rs_sibling(g4s, tag):
        return _rs_start(g4s, name=f"rs_start_{tag}"), tag

    def rs_chips(flight):
        st, tag = flight
        g4s, lands = _rs_mid(st, name=f"rs_mid_{tag}")
        sums = [_add_sibling(g4, land, core1, name=f"rs_add_{tag}_{k}") for k, (g4, land) in enumerate(zip(g4s, lands))]
        return _rs_start2(sums, name=f"rs_start2_{tag}"), tag

    def rs_finish(flight, targets, follow=()):
        st, tag = flight
        sums, recvs = _rs_end(st, name=f"rs_end_{tag}", follow=follow)
        for (pname, k), hsum, recv in zip(targets, sums, recvs):
            w_st, m_st, v_st = stacked[pname]
            res_big[pname] = _adamw_stacked(hsum, recv, chip1, w_st, m_st, v_st, k, res_big.get(pname),
                                            name=f"adamw_{pname}_{k}")

    pending = []
    last_sibling = None
    for i in reversed(range(depth)):
        for s in reversed(range(3)):
            sv = saved[3 * i + s]
            shift, scale, gate = mod[i, s, 0], mod[i, s, 1], mod[i, s, 2]
            g_norm = ng_full[i, s][None]
            tag = f"l{i}s{s}"
            last = i == 0 and s == 0
            if s != 1:
                widx = 2 * i + s // 2
                dgu, act = _ffn_da(dy, sv["w_out"], 0, sv["fg"], sv["fu"], name=f"ffn_da_{tag}"), sv["act"]
                if last:
                    g_in = _mm_tn(sv["h"], dgu, d, bn, True, name=f"ffn_dwin_{tag}").reshape(N_CHIP, 2, d, bn)
                    sib_in = rs_sibling([g_in], f"{tag}_in")
                    g_out = _mm_tn(act, dy[None], bn, d, False, name=f"ffn_dwout_{tag}").reshape(N_CHIP, 2, fl, d)
                    new_flights = [(rs_chips(sib_in), [("ffn_w_in", widx)])]
                    sib, targets = rs_sibling([g_out], f"{tag}_out"), [("ffn_w_out", widx)]
                else:
                    g_out = _mm_tn(act, dy[None], bn, d, False, name=f"ffn_dwout_{tag}").reshape(N_CHIP, 2, fl, d)
                    sib_out = rs_sibling([g_out], f"{tag}_out")
                    g_in = _mm_tn(sv["h"], dgu, d, bn, True, name=f"ffn_dwin_{tag}").reshape(N_CHIP, 2, d, bn)
                    new_flights = [(rs_chips(sib_out), [("ffn_w_out", widx)])]
                    sib, targets = rs_sibling([g_in], f"{tag}_in"), [("ffn_w_in", widx)]
                z3, w_blk = dgu, sv["w_in"]
            elif i % 2 == 0:
                ds = _mm_nt(dy, sv["w_out"], 0, name=f"gm_ds_{tag}")
                g_out = _mm_tn(sv["sg"], dy[None], _tile(e, 1024, V7X_LANES), d, False,
                               name=f"gm_dwout_{tag}").reshape(N_CHIP, 2, dl, d)
                du, dvn, dws, dbs = _sgu_bwd(ds, sv["u"], sv["vn"], ws, bsb, name=f"sgu_bwd_{tag}")
                dpre, dlng, dlnb = _gm_act_bwd(sv["pre"], du, dvn, gm_g, name=f"gm_act_bwd_{tag}")
                gm_small = [dlng, dlnb, dws, dbs[:, :, 0]]
                gm_slot = _cast_to_slot(_pack(gm_small, rows_align=256), (), me1, name="slot_gm_grads", dtype=F32)
                gm_flight = _ag_start([gm_slot], name="ag_start_gm_grads")
                g_in = _mm_tn(sv["h"], dpre[None], d, el, True, name=f"gm_dwin_{tag}").reshape(N_CHIP, 2, d, el)
                targets, new_flights = [("gm_w_in", 0), ("gm_w_out", 0)], []
                sib = rs_sibling([g_in, g_out], tag)
                z3, w_blk = dpre[None], sv["w_in"]
            else:
                dys = _mm_nt(dy, sv["w_out"], 0, name=f"cv_dys_{tag}")
                g_out = _mm_tn(sv["ys"], dy[None], _tile(e, 1024, V7X_LANES), d, False,
                               name=f"cv_dwout_{tag}").reshape(N_CHIP, 2, dl, d)
                dyc, dlng, dlnb, ddwb = _cv_act_bwd(dys, sv["yc"], cln_g_full, cln_b_full, name=f"cv_act_bwd_{tag}")
                dp, ddww, dba, dbg = _dwconv_bwd(dyc, sv["p"], dww_full, name=f"dwconv_bwd_{tag}")
                small["cv_b_out"], small["cv_ln_g"], small["cv_ln_b"], small["cv_dw_b"] = dbout, dlng, dlnb, ddwb
                small["cv_dw_w"] = ddww
                small["cv_b_in"] = jnp.concatenate([dba, dbg], axis=1)
                g_in = _mm_tn(sv["h"], dp, d, el, True, name=f"cv_dwin_{tag}").reshape(N_CHIP, 2, d, el)
                targets, new_flights = [("cv_w_in", 0), ("cv_w_out", 0)], []
                sib = rs_sibling([g_in, g_out], tag)
                z3, w_blk = dp, sv["w_in"]
            nxt = None if last else bwd_head(*((i, s - 1) if s > 0 else (i - 1, 2)))
            res = _dh_norm_bwd(z3, w_blk, sv["x"], dx, g_norm, scale, nxt, name=f"dh_norm_bwd_{tag}")
            if last:
                (dx, dscale, dshift, dgn), last_sibling = res, (sib, targets)
            else:
                new_flights.append((rs_chips(sib), targets))
                dx, dy_next, dscale, dshift, dgn, dgate_next = res[:6]
                dbout_next = res[6] if nxt[3] else None
            dmod[i][s] = [dshift, dscale, dgate]
            d_norm_g[i][s] = dgn
            if not last:
                dy, dgate, dbout = dy_next, dgate_next, dbout_next
            for flight in pending:
                rs_finish(*flight)
            pending = new_flights
    grad_x = dx[None]

    dmod_mine = jnp.concatenate([v for per_l in dmod for per_s in per_l for v in per_s], axis=1)
    dng_mine = jnp.concatenate([v for per_l in d_norm_g for v in per_l], axis=1)
    small_out = [dmod_mine, dng_mine, small["cv_b_in"], small["cv_dw_w"], small["cv_dw_b"], small["cv_ln_g"],
                 small["cv_ln_b"], small["cv_b_out"], d_final_g]
    shapes2 = [s.shape for s in small_out]
    (pack2_all,) = _all_gather([_pack(small_out, rows_align=256)[None, None]], name="ag_small_grads")
    _Seq.last = pack2_all
    pending.append((rs_chips(last_sibling[0]), last_sibling[1]))
    summed = _unpack(_sum_devices(pack2_all[0], name="sum_small_grads"), shapes2)
    dmod_all = _unpack(pack2_all[0], shapes2)[0].reshape(N_DEV, depth, 9 * d)
    (gm_all,) = _ag_end(_ag_mid(gm_flight, name="ag_mid_gm_grads"), name="ag_end_gm_grads")
    gm_sum = _unpack(_sum_devices(gm_all[0], name="sum_gm_grads"), [s.shape for s in gm_small])

    def my_cols(full, width):
        return lax.dynamic_slice_in_dim(full, me * width, width, axis=full.ndim - 1)

    g_ada_b = summed[0].reshape(depth, 9 * d)
    g_norm_g = my_cols(summed[1].reshape(depth, 3, d), dl)
    g_small = {
        "ada_b": g_ada_b, "norm_g": g_norm_g,
        "gm_ln_g": gm_sum[0], "gm_ln_b": gm_sum[1], "gm_ws": gm_sum[2][None], "gm_bs": gm_sum[3][None],
        "cv_b_in": my_cols(summed[2], el), "cv_dw_w": my_cols(summed[3], dl)[None],
        "cv_dw_b": my_cols(summed[4], dl), "cv_ln_g": my_cols(summed[5], dl), "cv_ln_b": my_cols(summed[6], dl),
        "cv_b_out": my_cols(summed[7], dl), "final_g": summed[8].reshape(d),
    }

    dm_loc = jnp.moveaxis(my_cols(dmod_all, cl), 0, 1)
    dm_loc = jnp.pad(dm_loc, ((0, 0), (0, 16 - N_DEV), (0, 0)))
    res_ada_w = _ada_bwd(c_pad, dm_loc, ada_w, m_ada_w, v_ada_w, name="ada_bwd_adamw")

    def flat2(a):
        return a.reshape(-1, a.shape[-1])

    small_params = {
        "ada_b": (ada_b, m_ada_b, v_ada_b), "norm_g": (norm_g, m_norm_g, v_norm_g),
        "gm_ln_g": (gm_ln_g, m_gm_ln_g, v_gm_ln_g), "gm_ln_b": (gm_ln_b, m_gm_ln_b, v_gm_ln_b),
        "gm_ws": (gm_ws, m_gm_ws, v_gm_ws), "gm_bs": (gm_bs, m_gm_bs, v_gm_bs),
        "cv_b_in": (cv_b_in, m_cv_b_in, v_cv_b_in), "cv_dw_w": (cv_dw_w, m_cv_dw_w, v_cv_dw_w),
        "cv_dw_b": (cv_dw_b, m_cv_dw_b, v_cv_dw_b), "cv_ln_g": (cv_ln_g, m_cv_ln_g, v_cv_ln_g),
        "cv_ln_b": (cv_ln_b, m_cv_ln_b, v_cv_ln_b), "cv_b_out": (cv_b_out, m_cv_b_out, v_cv_b_out),
        "final_g": (final_g, m_final_g, v_final_g),
    }
    res_small = {}
    for key, (w, m, v) in small_params.items():
        g2 = flat2(g_small[key].reshape(w.shape)) if w.ndim > 1 else g_small[key].reshape(1, -1)
        w2, m2, v2 = [flat2(a) if a.ndim > 1 else a.reshape(1, -1) for a in (w, m, v)]
        res_small[key] = [o.reshape(w.shape) for o in _adamw([(g2[None], 0)], w2, m2, v2, name=f"adamw_{key}")]

    for flight in pending[:-1]:
        rs_finish(*flight)
    rs_finish(*pending[-1], follow=[res_ada_w[0]] + [r[0] for r in res_big.values()])

    def big(name, k):
        if name == "ada_w":
            return res_ada_w[k]
        return res_big[name][k].reshape(stacked_shape[name])

    stacked_shape = {"ffn_w_in": ffn_w_in.shape, "ffn_w_out": ffn_w_out.shape, "gm_w_in": gm_w_in.shape,
                     "gm_w_out": gm_w_out.shape, "cv_w_in": cv_w_in.shape, "cv_w_out": cv_w_out.shape}

    order = ["ada_w", "ada_b", "norm_g", "ffn_w_in", "ffn_w_out", "gm_w_in", "gm_ln_g", "gm_ln_b", "gm_ws", "gm_bs",
             "gm_w_out", "cv_w_in", "cv_b_in", "cv_dw_w", "cv_dw_b", "cv_ln_g", "cv_ln_b", "cv_w_out", "cv_b_out",
             "final_g"]
    outs = [loss, grad_x]
    for k in range(4):
        for name in order:
            outs.append(res_small[name][k] if name in res_small else big(name, k))
    return tuple(outs)
```

```python
import functools

import jax
import jax.numpy as jnp
from jax import lax
from jax.experimental import pallas as pl
from jax.experimental.pallas import tpu as pltpu

F32 = jnp.float32
BF = jnp.bfloat16
MESH = pl.DeviceIdType.MESH

N_DEV = 8
N_CHIP = 4
NORM_EPS = 1e-6
ADAM_LR = 0.001
ADAM_B1 = 0.9
ADAM_B2 = 0.999
ADAM_EPS = 1e-08
ADAM_WD = 0.01
ADAM_STEP = 10

V7X_SUBLANES = 8
V7X_LANES = 128
PACK_ALIGN = V7X_SUBLANES * V7X_LANES
V7X_VMEM_LIMIT = 56 * 1024 * 1024


def _tile(n, pref, align):
    if n <= pref:
        return n
    t = pref - pref % align
    while t >= align:
        if n % t == 0:
            return t
        t -= align
    return n


ANY = pl.BlockSpec(memory_space=pl.ANY)


class _Seq:
    last = None
    tokens = []


def _call(body, *, name, grid, in_specs, out_specs, out_shape, scratch=(), sem=None, prefetch=None, aliases=None,
          after=(), on_path=True):
    def run(*args):
        if on_path:
            tokens, _Seq.tokens = _Seq.tokens + list(after), []
        else:
            tokens = list(after)
        lead = 0 if prefetch is None else 1
        n_in, n_tok = lead + len(args), len(tokens)

        def wrapped(*refs):
            body(*refs[:n_in], *refs[n_in + n_tok:])

        specs = list(in_specs) + [ANY] * n_tok
        params = pltpu.CompilerParams(dimension_semantics=sem, vmem_limit_bytes=V7X_VMEM_LIMIT)
        if prefetch is None:
            res = pl.pallas_call(wrapped, out_shape=out_shape, grid=grid, in_specs=specs, out_specs=out_specs,
                                 scratch_shapes=scratch, name=name, compiler_params=params,
                                 input_output_aliases=aliases or {})(*args, *tokens)
        else:
            grid_spec = pltpu.PrefetchScalarGridSpec(num_scalar_prefetch=1, grid=grid, in_specs=specs,
                                                     out_specs=out_specs, scratch_shapes=scratch)
            res = pl.pallas_call(wrapped, out_shape=out_shape, grid_spec=grid_spec, name=name,
                                 compiler_params=params, input_output_aliases=aliases or {})(prefetch, *args, *tokens)
        if on_path:
            _Seq.last = res[0] if isinstance(res, (list, tuple)) else res
        return res
    return run


def _sds(shape, dtype):
    return jax.ShapeDtypeStruct(tuple(shape), dtype)


def _sigmoid(v):
    return 1.0 / (1.0 + jnp.exp(-v))


def _normal_cdf_pdf(v):
    a = jnp.abs(v) * 0.7071067811865476
    t = 1.0 / (1.0 + 0.3275911 * a)
    poly = t * (0.254829592 + t * (-0.284496736 + t * (1.421413741 + t * (-1.453152027 + t * 1.061405429))))
    e = jnp.exp(-0.5 * v * v)
    half_erf = 0.5 - 0.5 * poly * e
    return 0.5 + jnp.where(v < 0, -half_erf, half_erf), 0.3989422804014327 * e


def _gelu(v):
    return v * _normal_cdf_pdf(v)[0]


def _fold_rows(val):
    rows, w = val.shape
    return val.reshape(rows // V7X_SUBLANES, V7X_SUBLANES, w).sum(axis=0)


def _rowwise(fn, name, rows_in, vecs_in, rows_out, acc_widths, tm=256):
    t = rows_in[0].shape[0]
    tm = _tile(t, tm, V7X_SUBLANES)
    steps = t // tm
    nr, nv, no, na = len(rows_in), len(vecs_in), len(rows_out), len(acc_widths)

    def body(*refs):
        rin, vin = refs[:nr], refs[nr:nr + nv]
        rout = refs[nr + nv:nr + nv + no]
        aout = refs[nr + nv + no:nr + nv + no + na]
        accs = refs[nr + nv + no + na:]
        i = pl.program_id(0)
        outs, acc_vals = fn(*[r[...] for r in rin], *[v[...] for v in vin])
        for r, o in zip(rout, outs):
            r[...] = o.astype(r.dtype)
        if na:
            @pl.when(i == 0)
            def _():
                for a in accs:
                    a[...] = jnp.zeros_like(a)

            for a, val in zip(accs, acc_vals):
                a[...] += _fold_rows(val)

            @pl.when(i == steps - 1)
            def _():
                for o, a in zip(aout, accs):
                    o[...] = jnp.sum(a[...], axis=0, keepdims=True)

    in_specs = [pl.BlockSpec((tm, r.shape[1]), lambda i: (i, 0)) for r in rows_in]
    in_specs += [pl.BlockSpec(v.shape, functools.partial(lambda nd, i: (0,) * nd, v.ndim)) for v in vecs_in]
    out_specs = [pl.BlockSpec((tm, r.shape[1]), lambda i: (i, 0)) for r in rows_out]
    out_specs += [pl.BlockSpec((1, w), lambda i: (0, 0)) for w in acc_widths]
    out_shape = list(rows_out) + [_sds((1, w), F32) for w in acc_widths]
    scratch = [pltpu.VMEM((V7X_SUBLANES, w), F32) for w in acc_widths]
    res = _call(body, name=name, grid=(steps,), in_specs=in_specs, out_specs=out_specs, out_shape=out_shape,
                scratch=scratch, sem=("arbitrary",) if na else ("parallel",))(*rows_in, *vecs_in)
    return res[:no], res[no:]


def _norm_mod(x, g, scale, shift, name):
    def fn(xv, gv, sc, sh):
        r = lax.rsqrt(jnp.mean(xv * xv, axis=-1, keepdims=True) + NORM_EPS)
        return ((xv * r * gv) * (1.0 + sc) + sh,), ()
    (h,), _ = _rowwise(fn, name, [x], [g, scale, shift], [_sds(x.shape, BF)], [], tm=512)
    return h


def _final_loss(x, target, g, y, gate, coef, name):
    d = x.shape[1]

    def fn(xv, tv, yv, gv, gt):
        r = lax.rsqrt(jnp.mean(xv * xv, axis=-1, keepdims=True) + NORM_EPS)
        xhat = xv * r
        err = xhat * gv - tv
        dl = err * (1.0 / d)
        dxhat = dl * gv
        dx = r * (dxhat - xhat * jnp.mean(dxhat * xhat, axis=-1, keepdims=True))
        return (dx, (coef * gt) * dx), (err * err, dl * xhat, coef * dx * yv.astype(F32))
    (dx, dy), (sq, dg, dgate) = _rowwise(fn, name, [x, target, y], [g, gate],
                                         [_sds(x.shape, F32), _sds(x.shape, BF)], [d, d, d])
    return sq, dx, dg, dy, dgate


def _gm_act(pre, ln_g, ln_b, name):
    e = pre.shape[1] // 2

    def fn(pv, gv, bv):
        p = pv.astype(F32)
        u = _gelu(p[:, :e])
        v = _gelu(p[:, e:])
        mu = jnp.mean(v, axis=-1, keepdims=True)
        vc = v - mu
        rstd = lax.rsqrt(jnp.mean(vc * vc, axis=-1, keepdims=True) + NORM_EPS)
        return (u, vc * rstd * gv + bv), ()
    t = pre.shape[0]
    (u, vn), _ = _rowwise(fn, name, [pre], [ln_g, ln_b], [_sds((t, e), BF), _sds((t, e), BF)], [])
    return u, vn


def _gm_act_bwd(pre, du, dvn, ln_g, name):
    e = pre.shape[1] // 2

    def fn(pv, duv, dvv, gv):
        p = pv.astype(F32)
        pu, pvv = p[:, :e], p[:, e:]
        cdf_u, pdf_u = _normal_cdf_pdf(pu)
        cdf_v, pdf_v = _normal_cdf_pdf(pvv)
        v = pvv * cdf_v
        mu = jnp.mean(v, axis=-1, keepdims=True)
        vc = v - mu
        rstd = lax.rsqrt(jnp.mean(vc * vc, axis=-1, keepdims=True) + NORM_EPS)
        vhat = vc * rstd
        dvn_f = dvv.astype(F32)
        dvhat = dvn_f * gv
        dv = rstd * (dvhat - jnp.mean(dvhat, axis=-1, keepdims=True)
                     - vhat * jnp.mean(dvhat * vhat, axis=-1, keepdims=True))
        dpu = duv.astype(F32) * (cdf_u + pu * pdf_u)
        dpv = dv * (cdf_v + pvv * pdf_v)
        return (jnp.concatenate([dpu, dpv], axis=1),), (dvn_f * vhat, dvn_f)
    (dpre,), (dg, db) = _rowwise(fn, name, [pre, du, dvn], [ln_g], [_sds(pre.shape, BF)], [e, e], tm=128)
    return dpre, dg, db


def _cv_act(yc, ln_g, ln_b, name):
    def fn(yv, gv, bv):
        mu = jnp.mean(yv, axis=-1, keepdims=True)
        c = yv - mu
        rstd = lax.rsqrt(jnp.mean(c * c, axis=-1, keepdims=True) + NORM_EPS)
        yn = c * rstd * gv + bv
        return (yn * _sigmoid(yn),), ()
    (ys,), _ = _rowwise(fn, name, [yc], [ln_g, ln_b], [_sds(yc.shape, BF)], [])
    return ys


def _cv_act_bwd(dys, yc, ln_g, ln_b, name):
    def fn(dv, yv, gv, bv):
        mu = jnp.mean(yv, axis=-1, keepdims=True)
        c = yv - mu
        rstd = lax.rsqrt(jnp.mean(c * c, axis=-1, keepdims=True) + NORM_EPS)
        yhat = c * rstd
        yn = yhat * gv + bv
        sig = _sigmoid(yn)
        dyn = dv.astype(F32) * (sig * (1.0 + yn * (1.0 - sig)))
        dyhat = dyn * gv
        dyc = rstd * (dyhat - jnp.mean(dyhat, axis=-1, keepdims=True)
                      - yhat * jnp.mean(dyhat * yhat, axis=-1, keepdims=True))
        return (dyc,), (dyn * yhat, dyn, dyc)
    cw = yc.shape[1]
    (dyc,), (dg, db, dbias) = _rowwise(fn, name, [dys, yc], [ln_g, ln_b], [_sds(yc.shape, F32)], [cw, cw, cw])
    return dyc, dg, db, dbias


MM_ROWS = 1024
MM_SEG_ROWS = 256


def _ffn_in(h, w_blk, blk0, name):
    t, d = h.shape
    bn = w_blk.shape[2]
    half = N_DEV // 2
    f = half * bn
    tm = _tile(t, MM_ROWS, V7X_SUBLANES)
    seg_rows = _tile(tm, MM_SEG_ROWS, 2 * V7X_SUBLANES)

    def body(h_ref, wg_ref, wu_ref, dg_ref, du_ref, a_ref):
        for seg in range(tm // seg_rows):
            rows = pl.ds(seg * seg_rows, seg_rows)
            hv = h_ref[rows, :]
            g = jnp.dot(hv, wg_ref[...], preferred_element_type=F32)
            u = jnp.dot(hv, wu_ref[...], preferred_element_type=F32)
            sig = _sigmoid(g)
            sl = g * sig
            dg_ref[rows, :] = (u * (sig * (1.0 + g * (1.0 - sig)))).astype(BF)
            du_ref[rows, :] = sl.astype(BF)
            a_ref[rows, :] = (sl * u).astype(BF)

    out = _sds((t, f), BF)
    tile = pl.BlockSpec((tm, bn), lambda j, i: (i, j))
    return _call(
        body, name=name, grid=(half, t // tm),
        in_specs=[pl.BlockSpec((tm, d), lambda j, i: (i, 0)),
                  pl.BlockSpec((None, d, bn), lambda j, i: (blk0 + j, 0, 0)),
                  pl.BlockSpec((None, d, bn), lambda j, i: (blk0 + half + j, 0, 0))],
        out_specs=[tile, tile, tile], out_shape=[out, out, out], sem=("parallel", "parallel"))(h, w_blk, w_blk)


def _in_proj(h, w_blk, bias, name):
    t, d = h.shape
    bn = w_blk.shape[2]
    tm = _tile(t, 1024, V7X_SUBLANES)

    def body(*refs):
        if bias is None:
            h_ref, w_ref, o_ref = refs
            o_ref[...] = jnp.dot(h_ref[...], w_ref[...], preferred_element_type=F32).astype(BF)
        else:
            h_ref, w_ref, b_ref, o_ref = refs
            o_ref[...] = (jnp.dot(h_ref[...], w_ref[...], preferred_element_type=F32) + b_ref[...]).astype(BF)

    in_specs = [pl.BlockSpec((tm, d), lambda j, i: (i, 0)), pl.BlockSpec((None, d, bn), lambda j, i: (j, 0, 0))]
    args = [h, w_blk]
    if bias is not None:
        in_specs.append(pl.BlockSpec((1, bn), lambda j, i: (0, j)))
        args.append(bias)
    return _call(body, name=name, grid=(N_DEV, t // tm), in_specs=in_specs,
                 out_specs=pl.BlockSpec((tm, bn), lambda j, i: (i, j)), out_shape=_sds((t, N_DEV * bn), BF),
                 sem=("parallel", "parallel"))(*args)


def _out_proj(a, w3, widx, x, gate, bias, coef, name):
    t, k = a.shape
    d = w3.shape[2]
    tm = _tile(t, MM_ROWS, V7X_SUBLANES)
    tn = _tile(d, 512, V7X_LANES)

    def body(*refs):
        if bias is None:
            a_ref, w_ref, x_ref, g_ref, xo_ref, y_ref = refs
            y = jnp.dot(a_ref[...], w_ref[...], preferred_element_type=F32)
        else:
            a_ref, w_ref, x_ref, g_ref, b_ref, xo_ref, y_ref = refs
            y = jnp.dot(a_ref[...], w_ref[...], preferred_element_type=F32) + b_ref[...]
        y_ref[...] = y.astype(BF)
        xo_ref[...] = x_ref[...] + (coef * g_ref[...]) * y

    tile = pl.BlockSpec((tm, tn), lambda j, i: (i, j))
    vec = pl.BlockSpec((1, tn), lambda j, i: (0, j))
    in_specs = [pl.BlockSpec((tm, k), lambda j, i: (i, 0)),
                pl.BlockSpec((None, k, tn), lambda j, i: (widx, 0, j)), tile, vec]
    args = [a, w3, x, gate]
    if bias is not None:
        in_specs.append(vec)
        args.append(bias)
    return _call(body, name=name, grid=(d // tn, t // tm), in_specs=in_specs, out_specs=[tile, tile],
                 out_shape=[_sds((t, d), F32), _sds((t, d), BF)], sem=("parallel", "parallel"))(*args)


def _ffn_da(dy, w3, widx, fg, fu, name):
    t, d = dy.shape
    f = w3.shape[1]
    bn = f // (N_DEV // 2)
    tm = _tile(t, MM_ROWS, V7X_SUBLANES)
    seg_rows = _tile(tm, MM_SEG_ROWS, 2 * V7X_SUBLANES)

    def body(dy_ref, w_ref, fg_ref, fu_ref, dgu_ref):
        for seg in range(tm // seg_rows):
            rows = pl.ds(seg * seg_rows, seg_rows)
            da = lax.dot_general(dy_ref[rows, :], w_ref[...], (((1,), (1,)), ((), ())), preferred_element_type=F32)
            dgu_ref[0, rows, :] = (da * fg_ref[rows, :].astype(F32)).astype(BF)
            dgu_ref[1, rows, :] = (da * fu_ref[rows, :].astype(F32)).astype(BF)

    tile = pl.BlockSpec((tm, bn), lambda j, i: (i, j))
    return _call(
        body, name=name, grid=(f // bn, t // tm),
        in_specs=[pl.BlockSpec((tm, d), lambda j, i: (i, 0)),
                  pl.BlockSpec((None, bn, d), lambda j, i: (widx, j, 0)), tile, tile],
        out_specs=pl.BlockSpec((2, tm, bn), lambda j, i: (0, i, j)),
        out_shape=_sds((2, t, f), BF), sem=("parallel", "parallel"))(dy, w3, fg, fu)


def _mm_nt(dy, w3, widx, name):
    t, k = dy.shape
    n = w3.shape[1]
    tm = _tile(t, MM_ROWS, V7X_SUBLANES)
    tn = _tile(n, 1024, V7X_LANES)

    def body(dy_ref, w_ref, o_ref):
        o_ref[...] = lax.dot_general(dy_ref[...], w_ref[...], (((1,), (1,)), ((), ())),
                                     preferred_element_type=F32).astype(BF)

    return _call(body, name=name, grid=(n // tn, t // tm),
                 in_specs=[pl.BlockSpec((tm, k), lambda j, i: (i, 0)),
                           pl.BlockSpec((None, tn, k), lambda j, i: (widx, j, 0))],
                 out_specs=pl.BlockSpec((tm, tn), lambda j, i: (i, j)), out_shape=_sds((t, n), BF),
                 sem=("parallel", "parallel"))(dy, w3)


NORM_BWD_ROWS = 128


def _dh_norm_bwd(z3, w_blk, x, dxp, g, scale, nxt, name):
    lead, t, _ = z3.shape
    d, bn = w_blk.shape[1], w_blk.shape[2]
    per = N_DEV // lead
    tm = _tile(t, 512, NORM_BWD_ROWS)
    ni = t // tm
    n_in = 6 if nxt is None else 8
    coef, colsum = (None, False) if nxt is None else nxt[2:]
    n_acc = 3 if nxt is None else (5 if colsum else 4)
    n_rows = 1 if nxt is None else 2

    def body(*refs):
        z_ref, w_ref, x_ref, dp_ref, g_ref, sc_ref = refs[:6]
        row_outs = refs[n_in:n_in + n_rows]
        vec_outs = refs[n_in + n_rows:n_in + n_rows + n_acc]
        acc_ref, accs = refs[n_in + n_rows + n_acc], refs[n_in + n_rows + n_acc + 1:]
        i, k = pl.program_id(0), pl.program_id(1)

        @pl.when(k == 0)
        def _():
            acc_ref[...] = jnp.zeros_like(acc_ref)

        @pl.when((i == 0) & (k == 0))
        def _():
            for a in accs:
                a[...] = jnp.zeros_like(a)

        acc_ref[...] += lax.dot_general(z_ref[...], w_ref[...], (((1,), (1,)), ((), ())),
                                        preferred_element_type=F32)

        @pl.when(k == N_DEV - 1)
        def _():
            gv, sc = g_ref[...], sc_ref[...]

            def chunk(ci, carry):
                rows = pl.ds(pl.multiple_of(ci * NORM_BWD_ROWS, NORM_BWD_ROWS), NORM_BWD_ROWS)
                dh, xv = acc_ref[rows, :], x_ref[rows, :]
                r = lax.rsqrt(jnp.mean(xv * xv, axis=-1, keepdims=True) + NORM_EPS)
                xhat = xv * r
                dn = dh * (1.0 + sc)
                dxhat = dn * gv
                dx = r * (dxhat - xhat * jnp.mean(dxhat * xhat, axis=-1, keepdims=True)) + dp_ref[rows, :]
                row_outs[0][rows, :] = dx
                accs[0][...] += _fold_rows(dh * (xhat * gv))
                accs[1][...] += _fold_rows(dh)
                accs[2][...] += _fold_rows(dn * xhat)
                if nxt is not None:
                    y_ref, gate_ref = refs[6], refs[7]
                    dy = (coef * gate_ref[...]) * dx
                    row_outs[1][rows, :] = dy.astype(BF)
                    accs[3][...] += _fold_rows(coef * dx * y_ref[rows, :].astype(F32))
                    if colsum:
                        accs[4][...] += _fold_rows(dy)
                return carry
            lax.fori_loop(0, tm // NORM_BWD_ROWS, chunk, 0)

        @pl.when((i == ni - 1) & (k == N_DEV - 1))
        def _():
            for o, a in zip(vec_outs, accs):
                o[...] = jnp.sum(a[...], axis=0, keepdims=True)

    rows = pl.BlockSpec((tm, d), lambda i, k: (i, 0))
    vec = pl.BlockSpec((1, d), lambda i, k: (0, 0))
    in_specs = [pl.BlockSpec((None, tm, bn), lambda i, k: (k // per, i, k % per)),
                pl.BlockSpec((None, d, bn), lambda i, k: (k, 0, 0)), rows, rows, vec, vec]
    args = [z3, w_blk, x, dxp, g, scale]
    out_specs, out_shape = [rows], [_sds((t, d), F32)]
    if nxt is not None:
        in_specs += [rows, vec]
        args += [nxt[0], nxt[1]]
        out_specs.append(rows)
        out_shape.append(_sds((t, d), BF))
    out_specs += [vec] * n_acc
    out_shape += [_sds((1, d), F32)] * n_acc
    return _call(body, name=name, grid=(ni, N_DEV), in_specs=in_specs, out_specs=out_specs, out_shape=out_shape,
                 scratch=[pltpu.VMEM((tm, d), F32)] + [pltpu.VMEM((V7X_SUBLANES, d), F32)] * n_acc,
                 sem=("arbitrary", "arbitrary"))(*args)


def _mm_tn(a, b3, ta, tb, blocked, name):
    t, ka = a.shape
    lead, _, w = b3.shape
    per = w // tb
    nj = lead * per
    tk = t
    while tk > 512 and 4 * tk * (ta + tb) + 8 * ta * tb > V7X_VMEM_LIMIT * 3 // 4:
        tk //= 2
    tk = _tile(t, tk, V7X_SUBLANES)
    nk = t // tk

    def body(a_ref, b_ref, o_ref, acc_ref):
        k = pl.program_id(2)

        @pl.when(k == 0)
        def _():
            acc_ref[...] = jnp.zeros_like(acc_ref)

        acc_ref[...] += lax.dot_general(a_ref[...], b_ref[...], (((0,), (0,)), ((), ())),
                                        preferred_element_type=F32)

        @pl.when(k == nk - 1)
        def _():
            o_ref[...] = acc_ref[...].astype(BF)

    if blocked:
        out_shape = _sds((nj, ka, tb), BF)
        out_spec = pl.BlockSpec((None, ta, tb), lambda i, j, k: (j, i, 0))
    else:
        out_shape = _sds((1, ka, w), BF)
        out_spec = pl.BlockSpec((None, ta, tb), lambda i, j, k: (0, i, j))
    return _call(body, name=name, grid=(ka // ta, nj, nk),
                 in_specs=[pl.BlockSpec((tk, ta), lambda i, j, k: (k, i)),
                           pl.BlockSpec((None, tk, tb), lambda i, j, k: (j // per, k, j % per))],
                 out_specs=out_spec, out_shape=out_shape, scratch=[pltpu.VMEM((ta, tb), F32)],
                 sem=("parallel", "parallel", "arbitrary"))(a, b3)


def _causal(ws):
    l = ws.shape[0]
    row = lax.broadcasted_iota(jnp.int32, (l, l), 0)
    col = lax.broadcasted_iota(jnp.int32, (l, l), 1)
    return jnp.where(col <= row, ws, 0.0)


def _sgu_fwd(u, vn, ws, bsb, name):
    t, e = u.shape
    hn, l, _ = ws.shape
    dh = e // hn
    nc = t // l

    def body(u_ref, v_ref, ws_ref, bs_ref, s_ref):
        wsc = _causal(ws_ref[...]).astype(BF)
        bias = bs_ref[...]

        def chunk(c, carry):
            rows = pl.ds(pl.multiple_of(c * l, l), l)
            vo = jnp.dot(wsc, v_ref[rows, :], preferred_element_type=F32) + bias
            s_ref[rows, :] = (u_ref[rows, :].astype(F32) * vo).astype(BF)
            return carry
        lax.fori_loop(0, nc, chunk, 0)

    col = pl.BlockSpec((t, dh), lambda h: (0, h))
    return _call(body, name=name, grid=(hn,),
                 in_specs=[col, col, pl.BlockSpec((None, l, l), lambda h: (h, 0, 0)),
                           pl.BlockSpec((None, l, dh), lambda h: (h, 0, 0))],
                 out_specs=col, out_shape=_sds((t, e), BF), sem=("parallel",))(u, vn, ws, bsb)


def _sgu_bwd(ds, u, vn, ws, bsb, name):
    t, e = u.shape
    hn, l, _ = ws.shape
    dh = e // hn
    nc = t // l

    def body(ds_ref, u_ref, v_ref, ws_ref, bs_ref, du_ref, dv_ref, dws_ref, dbs_ref, accw_ref, accb_ref):
        wsc = _causal(ws_ref[...]).astype(BF)
        bias = bs_ref[...]
        accw_ref[...] = jnp.zeros_like(accw_ref)
        accb_ref[...] = jnp.zeros_like(accb_ref)

        def chunk(c, carry):
            rows = pl.ds(pl.multiple_of(c * l, l), l)
            vc = v_ref[rows, :]
            dsv = ds_ref[rows, :].astype(F32)
            vo = jnp.dot(wsc, vc, preferred_element_type=F32) + bias
            du_ref[rows, :] = (dsv * vo).astype(BF)
            dvo = dsv * u_ref[rows, :].astype(F32)
            dvo_b = dvo.astype(BF)
            accb_ref[...] += dvo
            accw_ref[...] += lax.dot_general(dvo_b, vc, (((1,), (1,)), ((), ())), preferred_element_type=F32)
            dv_ref[rows, :] = lax.dot_general(wsc, dvo_b, (((0,), (0,)), ((), ())),
                                              preferred_element_type=F32).astype(BF)
            return carry
        lax.fori_loop(0, nc, chunk, 0)
        dws_ref[...] = _causal(accw_ref[...])
        dbs_ref[...] = jnp.broadcast_to(jnp.sum(accb_ref[...], axis=1, keepdims=True), (l, dh))

    col = pl.BlockSpec((t, dh), lambda h: (0, h))
    return _call(body, name=name, grid=(hn,),
                 in_specs=[col, col, col, pl.BlockSpec((None, l, l), lambda h: (h, 0, 0)),
                           pl.BlockSpec((None, l, dh), lambda h: (h, 0, 0))],
                 out_specs=[col, col, pl.BlockSpec((None, l, l), lambda h: (h, 0, 0)),
                            pl.BlockSpec((None, l, dh), lambda h: (h, 0, 0))],
                 out_shape=[_sds((t, e), BF), _sds((t, e), BF), _sds((hn, l, l), F32), _sds((hn, l, dh), F32)],
                 scratch=[pltpu.VMEM((l, l), F32), pltpu.VMEM((l, dh), F32)],
                 sem=("parallel",))(ds, u, vn, ws, bsb)


AG_AHEAD = 3

CONV_HALO = 32
CONV_ROWS = 64
CONV_LANES = 256


def _shifted_windows(win_ref, sh_ref, rows):
    for b in range(1, V7X_SUBLANES):
        sh_ref[b - 1, 0:rows, :] = win_ref[b:b + rows, :]


def _window_rows(win_ref, sh_ref, shift, r0, rows):
    a, b = divmod(shift, V7X_SUBLANES)
    start = pl.multiple_of(r0 + V7X_SUBLANES * a, V7X_SUBLANES)
    if b == 0:
        return win_ref[pl.ds(start, rows), :]
    return sh_ref[b - 1, pl.ds(start, rows), :]


def _dwconv_fwd(p, dw_w, dw_b, name):
    t, c2 = p.shape
    cw = c2 // 2
    kw = dw_w.shape[0]
    cb = _tile(cw, CONV_LANES, V7X_LANES)
    ncb = cw // cb
    tm = _tile(t, 512, CONV_ROWS)
    off = CONV_HALO - (kw - 1)

    def body(a_ref, g_ref, ap_ref, gp_ref, w_ref, b_ref, o_ref, win_ref, sh_ref):
        i = pl.program_id(1)
        prev = ap_ref[...].astype(F32) * _sigmoid(gp_ref[...].astype(F32))
        win_ref[0:CONV_HALO, :] = jnp.where(i > 0, prev, 0.0)
        win_ref[CONV_HALO:, :] = a_ref[...].astype(F32) * _sigmoid(g_ref[...].astype(F32))
        _shifted_windows(win_ref, sh_ref, tm + CONV_HALO - V7X_SUBLANES)

        def chunk(ci, carry):
            r0 = ci * CONV_ROWS
            acc = jnp.zeros((CONV_ROWS, cb), F32) + b_ref[...]
            for k in range(kw):
                acc = acc + w_ref[k:k + 1, :] * _window_rows(win_ref, sh_ref, off + k, r0, CONV_ROWS)
            o_ref[pl.ds(pl.multiple_of(r0, CONV_ROWS), CONV_ROWS), :] = acc
            return carry
        lax.fori_loop(0, tm // CONV_ROWS, chunk, 0)

    hpt = tm // CONV_HALO
    cur_a = pl.BlockSpec((tm, cb), lambda j, i: (i, j))
    cur_g = pl.BlockSpec((tm, cb), lambda j, i: (i, ncb + j))
    prev_a = pl.BlockSpec((CONV_HALO, cb), lambda j, i: (jnp.maximum(i * hpt - 1, 0), j))
    prev_g = pl.BlockSpec((CONV_HALO, cb), lambda j, i: (jnp.maximum(i * hpt - 1, 0), ncb + j))
    return _call(body, name=name, grid=(ncb, t // tm),
                 in_specs=[cur_a, cur_g, prev_a, prev_g, pl.BlockSpec((kw, cb), lambda j, i: (0, j)),
                           pl.BlockSpec((1, cb), lambda j, i: (0, j))],
                 out_specs=pl.BlockSpec((tm, cb), lambda j, i: (i, j)), out_shape=_sds((t, cw), F32),
                 scratch=[pltpu.VMEM((tm + CONV_HALO, cb), F32),
                          pltpu.VMEM((V7X_SUBLANES - 1, tm + CONV_HALO - V7X_SUBLANES, cb), F32)],
                 sem=("parallel", "parallel"))(p, p, p, p, dw_w, dw_b)


def _dwconv_bwd(dyc, p, dw_w, name):
    t, c2 = p.shape
    cw = c2 // 2
    kw = dw_w.shape[0]
    cb = _tile(cw, CONV_LANES, V7X_LANES)
    ncb = cw // cb
    tm = _tile(t, 512, CONV_ROWS)
    nt = t // tm
    off = CONV_HALO - (kw - 1)
    kpad = -(-kw // V7X_SUBLANES) * V7X_SUBLANES
    sh_rows = tm + CONV_HALO - V7X_SUBLANES

    def body(d_ref, dn_ref, a_ref, g_ref, ap_ref, gp_ref, w_ref,
             dp_ref, dw_ref, dba_ref, dbg_ref, dwin_ref, ywin_ref, dsh_ref, ysh_ref, accw_ref, acca_ref, accg_ref):
        i = pl.program_id(1)

        @pl.when(i == 0)
        def _():
            accw_ref[...] = jnp.zeros_like(accw_ref)
            acca_ref[...] = jnp.zeros_like(acca_ref)
            accg_ref[...] = jnp.zeros_like(accg_ref)

        prev = ap_ref[...].astype(F32) * _sigmoid(gp_ref[...].astype(F32))
        ywin_ref[0:CONV_HALO, :] = jnp.where(i > 0, prev, 0.0)
        ywin_ref[CONV_HALO:, :] = a_ref[...].astype(F32) * _sigmoid(g_ref[...].astype(F32))
        dwin_ref[0:tm, :] = d_ref[...]
        dwin_ref[tm:, :] = jnp.where(i < nt - 1, dn_ref[...], 0.0)
        _shifted_windows(ywin_ref, ysh_ref, sh_rows)
        _shifted_windows(dwin_ref, dsh_ref, sh_rows)

        def chunk(ci, carry):
            r0 = ci * CONV_ROWS
            rows = pl.ds(pl.multiple_of(r0, CONV_ROWS), CONV_ROWS)
            dcur = d_ref[rows, :]
            dyg = jnp.zeros((CONV_ROWS, cb), F32)
            for k in range(kw):
                dyg = dyg + w_ref[k:k + 1, :] * _window_rows(dwin_ref, dsh_ref, kw - 1 - k, r0, CONV_ROWS)
                accw_ref[k] += _fold_rows(dcur * _window_rows(ywin_ref, ysh_ref, off + k, r0, CONV_ROWS))
            av = a_ref[rows, :].astype(F32)
            sig = _sigmoid(g_ref[rows, :].astype(F32))
            da = dyg * sig
            dg = dyg * av * sig * (1.0 - sig)
            dp_ref[0, rows, :] = da.astype(BF)
            dp_ref[1, rows, :] = dg.astype(BF)
            acca_ref[...] += _fold_rows(da)
            accg_ref[...] += _fold_rows(dg)
            return carry
        lax.fori_loop(0, tm // CONV_ROWS, chunk, 0)

        @pl.when(i == nt - 1)
        def _():
            dw_ref[...] = jnp.sum(accw_ref[...], axis=1)
            dba_ref[...] = jnp.sum(acca_ref[...], axis=0, keepdims=True)
            dbg_ref[...] = jnp.sum(accg_ref[...], axis=0, keepdims=True)

    hpt = tm // CONV_HALO
    last_halo = t // CONV_HALO - 1
    tile = pl.BlockSpec((tm, cb), lambda j, i: (i, j))
    cur_g = pl.BlockSpec((tm, cb), lambda j, i: (i, ncb + j))
    nxt = pl.BlockSpec((CONV_HALO, cb), lambda j, i: (jnp.minimum((i + 1) * hpt, last_halo), j))
    prev_a = pl.BlockSpec((CONV_HALO, cb), lambda j, i: (jnp.maximum(i * hpt - 1, 0), j))
    prev_g = pl.BlockSpec((CONV_HALO, cb), lambda j, i: (jnp.maximum(i * hpt - 1, 0), ncb + j))
    vec = pl.BlockSpec((1, cb), lambda j, i: (0, j))
    dp, ddw, dba, dbg = _call(
        body, name=name, grid=(ncb, nt),
        in_specs=[tile, nxt, tile, cur_g, prev_a, prev_g, pl.BlockSpec((kw, cb), lambda j, i: (0, j))],
        out_specs=[pl.BlockSpec((2, tm, cb), lambda j, i: (0, i, j)), pl.BlockSpec((kpad, cb), lambda j, i: (0, j)),
                   vec, vec],
        out_shape=[_sds((2, t, cw), BF), _sds((kpad, cw), F32), _sds((1, cw), F32), _sds((1, cw), F32)],
        scratch=[pltpu.VMEM((tm + CONV_HALO, cb), F32), pltpu.VMEM((tm + CONV_HALO, cb), F32),
                 pltpu.VMEM((V7X_SUBLANES - 1, sh_rows, cb), F32), pltpu.VMEM((V7X_SUBLANES - 1, sh_rows, cb), F32),
                 pltpu.VMEM((kpad, V7X_SUBLANES, cb), F32), pltpu.VMEM((V7X_SUBLANES, cb), F32),
                 pltpu.VMEM((V7X_SUBLANES, cb), F32)],
        sem=("parallel", "arbitrary"))(dyc, dyc, p, p, p, p, dw_w)
    return dp, ddw[:kw], dba, dbg


def _adam_math(g, w, m, v):
    m2 = ADAM_B1 * m + (1.0 - ADAM_B1) * g
    v2 = ADAM_B2 * v + (1.0 - ADAM_B2) * (g * g)
    m_hat = m2 / (1.0 - ADAM_B1 ** ADAM_STEP)
    v_hat = v2 / (1.0 - ADAM_B2 ** ADAM_STEP)
    delta = -ADAM_LR * (m_hat / (jnp.sqrt(v_hat) + ADAM_EPS) + ADAM_WD * w)
    return delta, m2, v2


def _adamw(g_parts, w, m, v, name):
    r, c = w.shape
    tr = _tile(r, 256, V7X_SUBLANES)
    ng = len(g_parts)

    def body(*refs):
        g = refs[0][...].astype(F32)
        for s in refs[1:ng]:
            g = g + s[...].astype(F32)
        w_ref, m_ref, v_ref, go_ref, d_ref, mo_ref, vo_ref = refs[ng:]
        delta, m2, v2 = _adam_math(g, w_ref[...], m_ref[...], v_ref[...])
        go_ref[...] = g
        d_ref[...] = delta
        mo_ref[...] = m2
        vo_ref[...] = v2

    tile = pl.BlockSpec((tr, c), lambda i: (i, 0))
    in_specs = [pl.BlockSpec((None, tr, c), functools.partial(lambda s, i: (s, i, 0), s)) for _, s in g_parts]
    out = _sds((r, c), F32)
    return _call(body, name=name, grid=(r // tr,), in_specs=in_specs + [tile, tile, tile],
                 out_specs=[tile] * 4, out_shape=[out] * 4, sem=("parallel",))(*[a for a, _ in g_parts], w, m, v)


def _adamw_stacked(h, recv, chip, w_st, m_st, v_st, k, prev, name):
    kk, r, c = w_st.shape
    tr = _tile(r, 256, V7X_SUBLANES)
    if prev is None:
        prev = [lax.empty((kk, r, c), F32) for _ in range(4)]

    def body(chip_ref, h_ref, r0_ref, r1_ref, r2_ref, w_ref, m_ref, v_ref, pg, pd, pm, pv,
             go_ref, d_ref, mo_ref, vo_ref):
        g = (h_ref[...].astype(F32) + r0_ref[...].astype(F32)) + (r1_ref[...].astype(F32) + r2_ref[...].astype(F32))
        delta, m2, v2 = _adam_math(g, w_ref[...], m_ref[...], v_ref[...])
        go_ref[...] = g
        d_ref[...] = delta
        mo_ref[...] = m2
        vo_ref[...] = v2

    own = pl.BlockSpec((None, tr, c), lambda i, chip_ref: (chip_ref[0], i, 0))
    rcv = [pl.BlockSpec((None, tr, c), functools.partial(lambda s, i, chip_ref: (s, i, 0), s)) for s in range(3)]
    blk = pl.BlockSpec((None, tr, c), lambda i, chip_ref: (k, i, 0))
    out = _sds((kk, r, c), F32)
    return _call(body, name=name, grid=(r // tr,), in_specs=[own] + rcv + [blk, blk, blk] + [ANY] * 4,
                 out_specs=[blk] * 4, out_shape=[out] * 4, sem=("parallel",), prefetch=chip, on_path=False,
                 aliases={8: 0, 9: 1, 10: 2, 11: 3})(h, recv, recv, recv, w_st, m_st, v_st, *prev)


def _add_sibling(g4, land, core, name):
    n, _, r, c = g4.shape
    tr = _tile(r, 1024, 2 * V7X_SUBLANES)

    def body(core_ref, a_ref, b_ref, o_ref):
        o_ref[...] = (a_ref[...].astype(F32) + b_ref[...].astype(F32)).astype(BF)

    return _call(body, name=name, grid=(n, r // tr),
                 in_specs=[pl.BlockSpec((None, None, tr, c), lambda p, i, core_ref: (p, core_ref[0], i, 0)),
                           pl.BlockSpec((None, None, tr, c), lambda p, i, core_ref: (p, 0, i, 0))],
                 out_specs=pl.BlockSpec((None, tr, c), lambda p, i, core_ref: (p, i, 0)),
                 out_shape=_sds((n, r, c), BF), sem=("parallel", "parallel"), prefetch=core)(g4, land)


def _cast_to_slot(w, lead, me, name, after=(), dtype=BF):
    r, c = w.shape[-2:]
    nl = len(lead)
    tr = _tile(r, 1024, 2 * V7X_SUBLANES)

    def body(me_ref, w_ref, o_ref):
        o_ref[...] = w_ref[...].astype(dtype)

    return _call(body, name=name, grid=(r // tr,),
                 in_specs=[pl.BlockSpec((None,) * nl + (tr, c), lambda i, me_ref: tuple(lead) + (i, 0))],
                 out_specs=pl.BlockSpec((None, None, tr, c), lambda i, me_ref: (0, me_ref[0], i, 0)),
                 out_shape=_sds((1, N_DEV, r, c), dtype), sem=("parallel",), prefetch=me, after=after)(w)


def _ada_fwd(c_pad, ada_w, ada_b, name):
    nl, d, cl = ada_w.shape
    rows = c_pad.shape[0]
    tn = _tile(cl, 256, V7X_LANES)

    def body(c_ref, w_ref, b_ref, o_ref):
        cv = c_ref[...]
        cond = (cv * _sigmoid(cv)).astype(BF)
        o_ref[...] = jnp.dot(cond, w_ref[...].astype(BF), preferred_element_type=F32) + b_ref[...]

    return _call(body, name=name, grid=(nl, cl // tn),
                 in_specs=[pl.BlockSpec((rows, d), lambda l, j: (0, 0)),
                           pl.BlockSpec((None, d, tn), lambda l, j: (l, 0, j)),
                           pl.BlockSpec((None, 1, tn), lambda l, j: (l, 0, j))],
                 out_specs=pl.BlockSpec((None, rows, tn), lambda l, j: (l, 0, j)),
                 out_shape=_sds((nl, rows, cl), F32), sem=("parallel", "parallel"))(c_pad, ada_w, ada_b)


def _ada_bwd(c_pad, dmod, w, m, v, name):
    nl, d, cl = w.shape
    rows = c_pad.shape[0]
    tn = _tile(cl, 256, V7X_LANES)

    def body(c_ref, dm_ref, w_ref, m_ref, v_ref, go_ref, d_ref, mo_ref, vo_ref):
        cv = c_ref[...]
        cond = (cv * _sigmoid(cv)).astype(BF)
        g = lax.dot_general(cond, dm_ref[...].astype(BF), (((0,), (0,)), ((), ())), preferred_element_type=F32)
        delta, m2, v2 = _adam_math(g, w_ref[...], m_ref[...], v_ref[...])
        go_ref[...] = g
        d_ref[...] = delta
        mo_ref[...] = m2
        vo_ref[...] = v2

    tile = pl.BlockSpec((None, d, tn), lambda l, j: (l, 0, j))
    out = _sds((nl, d, cl), F32)
    return _call(body, name=name, grid=(nl, cl // tn),
                 in_specs=[pl.BlockSpec((rows, d), lambda l, j: (0, 0)),
                           pl.BlockSpec((None, rows, tn), lambda l, j: (l, 0, j)), tile, tile, tile],
                 out_specs=[tile] * 4, out_shape=[out] * 4, sem=("parallel", "parallel"))(c_pad, dmod, w, m, v)


def _sum_devices(parts, name):
    n, r, c = parts.shape
    tr = _tile(r, 512, V7X_SUBLANES)

    def body(p_ref, o_ref):
        acc = p_ref[0]
        for k in range(1, n):
            acc = acc + p_ref[k]
        o_ref[...] = acc

    return _call(body, name=name, grid=(r // tr,), in_specs=[pl.BlockSpec((n, tr, c), lambda i: (0, i, 0))],
                 out_specs=pl.BlockSpec((tr, c), lambda i: (i, 0)), out_shape=_sds((r, c), F32),
                 sem=("parallel",))(parts)


def _mesh_pos():
    return lax.axis_index("x"), lax.axis_index("y"), lax.axis_index("c")


def _other_chips(x, y):
    return [(1 - x, y), (x, 1 - y), (1 - x, 1 - y)]


def _all_gather(arrs, name):
    n = len(arrs)

    def body(*refs):
        ins, outs = refs[:n], refs[n:2 * n]
        send_sems, recv_sems, local_sems = refs[2 * n:]
        x, y, c = _mesh_pos()
        me, sibling = (x, y, c), (x, y, 1 - c)
        chips = _other_chips(x, y)

        def slot(a, pos):
            px, py, pc = pos
            return outs[a].at[:, pl.ds(4 * px + 2 * py + pc, 1)]

        def copy(a, k, block, to, src=None):
            return pltpu.make_async_remote_copy(
                src_ref=slot(a, block) if src is None else src, dst_ref=slot(a, block),
                send_sem=send_sems.at[a, k], recv_sem=recv_sems.at[a, k], device_id=to, device_id_type=MESH)

        mine = [pltpu.make_async_copy(ins[a], slot(a, me), local_sems.at[a]) for a in range(n)]
        for cp in mine:
            cp.start()
        first = []
        for a in range(n):
            first.append(copy(a, 0, me, sibling, src=ins[a]))
            first += [copy(a, 1 + j, me, (*chip, c), src=ins[a]) for j, chip in enumerate(chips)]
        for cp in first:
            cp.start()
        passed = []
        for a in range(n):
            for j, chip in enumerate(chips):
                copy(a, 1 + j, (*chip, c), me).wait_recv()
                fwd = copy(a, 4 + j, (*chip, c), sibling)
                fwd.start()
                passed.append(fwd)
        for a in range(n):
            copy(a, 0, sibling, me).wait_recv()
            for j, chip in enumerate(chips):
                copy(a, 4 + j, (*chip, 1 - c), me).wait_recv()
        for cp in first + passed:
            cp.wait_send()
        for cp in mine:
            cp.wait()

    out_shape = [_sds((a.shape[0], N_DEV) + a.shape[2:], a.dtype) for a in arrs]
    return pl.pallas_call(
        body, out_shape=out_shape, in_specs=[ANY] * n, out_specs=[ANY] * n, name=name,
        scratch_shapes=[pltpu.SemaphoreType.DMA((n, N_DEV - 1)), pltpu.SemaphoreType.DMA((n, N_DEV - 1)),
                        pltpu.SemaphoreType.DMA((n,))])(*arrs)


HBM = pl.BlockSpec(memory_space=pltpu.HBM)
SEM = pl.BlockSpec(memory_space=pltpu.SEMAPHORE)


def _hbm(v):
    return pltpu.with_memory_space_constraint(v, pltpu.HBM)


def _comm_call(body, name, bufs, sems_in, sems_out, follow=()):
    after = [] if not sems_in or _Seq.last is None or any(_Seq.last is b for b in bufs) else [_Seq.last]
    after += list(follow)
    nb, ni, na, no = len(bufs), len(sems_in), len(after), len(sems_out)

    def wrapped(*refs):
        body(refs[:nb], refs[nb:nb + ni], refs[nb + ni + na:nb + ni + na + no])
        if no:
            refs[-1][...] = jnp.zeros_like(refs[-1])

    out_shape = [pltpu.SemaphoreType.DMA(s) for s in sems_out] + [pltpu.HBM(b.shape, b.dtype) for b in bufs]
    out_specs = [SEM] * no + [HBM] * nb
    if no:
        out_shape.append(_sds((V7X_SUBLANES, V7X_LANES), F32))
        out_specs.append(pl.BlockSpec(memory_space=pltpu.VMEM))
    res = pl.pallas_call(
        wrapped, name=name, out_shape=out_shape, in_specs=[HBM] * nb + [SEM] * ni + [ANY] * na, out_specs=out_specs,
        input_output_aliases={i: no + i for i in range(nb)},
        compiler_params=pltpu.CompilerParams(has_side_effects=pltpu.SideEffectType.DATAFLOW_SIDE_EFFECTING),
    )(*bufs, *sems_in, *after)
    out_bufs = list(res[no:no + nb])
    if no:
        _Seq.tokens.append(res[-1])
    _Seq.last = out_bufs[0]
    return list(res[:no]), out_bufs


def _remote(src, dst, send_sem, recv_sem, to):
    return pltpu.make_async_remote_copy(src_ref=src, dst_ref=dst, send_sem=send_sem, recv_sem=recv_sem,
                                        device_id=to, device_id_type=MESH)


def _slot(ref, pos):
    px, py, pc = pos
    return ref.at[:, pl.ds(4 * px + 2 * py + pc, 1)]


def _ag_start(bufs, name):
    n = len(bufs)

    def body(b, _, sems):
        send_sib, recv_sib, send_ici, recv_ici = sems
        x, y, c = _mesh_pos()
        for a in range(n):
            mine = _slot(b[a], (x, y, c))
            _remote(mine, mine, send_sib.at[a], recv_sib.at[a], (x, y, 1 - c)).start()
            for j, (px, py) in enumerate(_other_chips(x, y)):
                _remote(mine, mine, send_ici.at[3 * a + j], recv_ici.at[3 * a + j], (px, py, c)).start()

    sems, bufs = _comm_call(body, name, [_hbm(b) for b in bufs], [], [(n,), (n,), (3 * n,), (3 * n,)])
    return dict(bufs=bufs, send_sib=sems[0], recv_sib=sems[1], send_ici=sems[2], recv_ici=sems[3])


def _ag_mid(st, name):
    n = len(st["bufs"])

    def body(b, sems_in, sems):
        (recv_ici,) = sems_in
        send_fwd, recv_fwd = sems
        x, y, c = _mesh_pos()
        for a in range(n):
            for j, (px, py) in enumerate(_other_chips(x, y)):
                blk = _slot(b[a], (px, py, c))
                _remote(blk, blk, send_fwd.at[3 * a + j], recv_ici.at[3 * a + j], (x, y, 1 - c)).wait_recv()
                _remote(blk, blk, send_fwd.at[3 * a + j], recv_fwd.at[3 * a + j], (x, y, 1 - c)).start()

    sems, bufs = _comm_call(body, name, st["bufs"], [st["recv_ici"]], [(3 * n,), (3 * n,)])
    return dict(st, bufs=bufs, send_fwd=sems[0], recv_fwd=sems[1])


def _ag_end(st, name):
    n = len(st["bufs"])

    def body(b, sems_in, _):
        send_sib, recv_sib, send_ici, send_fwd, recv_fwd = sems_in
        x, y, c = _mesh_pos()
        sibling = (x, y, 1 - c)
        for a in range(n):
            mine, sib_blk = _slot(b[a], (x, y, c)), _slot(b[a], sibling)
            _remote(mine, mine, send_sib.at[a], recv_sib.at[a], sibling).wait_send()
            _remote(sib_blk, sib_blk, send_sib.at[a], recv_sib.at[a], sibling).wait_recv()
            for j, (px, py) in enumerate(_other_chips(x, y)):
                blk, sib_got = _slot(b[a], (px, py, c)), _slot(b[a], (px, py, 1 - c))
                _remote(mine, mine, send_ici.at[3 * a + j], recv_sib.at[a], (px, py, c)).wait_send()
                _remote(blk, blk, send_fwd.at[3 * a + j], recv_fwd.at[3 * a + j], sibling).wait_send()
                _remote(sib_got, sib_got, send_fwd.at[3 * a + j], recv_fwd.at[3 * a + j], sibling).wait_recv()

    _, bufs = _comm_call(body, name, st["bufs"],
                         [st[k] for k in ("send_sib", "recv_sib", "send_ici", "send_fwd", "recv_fwd")], [])
    return bufs


def _ag_step(end_st, mid_st, start_bufs, name):
    e_bufs = end_st["bufs"]
    m_bufs = mid_st["bufs"] if mid_st is not None else []
    s_bufs = [_hbm(b) for b in start_bufs] if start_bufs is not None else []
    ne, nm, ns = len(e_bufs), len(m_bufs), len(s_bufs)
    sems_in = [end_st[k] for k in ("send_sib", "recv_sib", "send_ici", "send_fwd", "recv_fwd")]
    sems_out = []
    if nm:
        sems_in.append(mid_st["recv_ici"])
        sems_out += [(3 * nm,), (3 * nm,)]
    if ns:
        sems_out += [(ns,), (ns,), (3 * ns,), (3 * ns,)]

    def body(b, si, so):
        x, y, c = _mesh_pos()
        sibling = (x, y, 1 - c)
        chips = _other_chips(x, y)
        send_sib, recv_sib, send_ici, send_fwd, recv_fwd = si[:5]
        for a in range(ne):
            mine, sib_blk = _slot(b[a], (x, y, c)), _slot(b[a], sibling)
            _remote(mine, mine, send_sib.at[a], recv_sib.at[a], sibling).wait_send()
            _remote(sib_blk, sib_blk, send_sib.at[a], recv_sib.at[a], sibling).wait_recv()
            for j, (px, py) in enumerate(chips):
                blk, sib_got = _slot(b[a], (px, py, c)), _slot(b[a], (px, py, 1 - c))
                _remote(mine, mine, send_ici.at[3 * a + j], recv_sib.at[a], (px, py, c)).wait_send()
                _remote(blk, blk, send_fwd.at[3 * a + j], recv_fwd.at[3 * a + j], sibling).wait_send()
                _remote(sib_got, sib_got, send_fwd.at[3 * a + j], recv_fwd.at[3 * a + j], sibling).wait_recv()
        if nm:
            recv_next, send_f, recv_f = si[5], so[0], so[1]
            for a in range(nm):
                for j, (px, py) in enumerate(chips):
                    blk = _slot(b[ne + a], (px, py, c))
                    _remote(blk, blk, send_f.at[3 * a + j], recv_next.at[3 * a + j], sibling).wait_recv()
                    _remote(blk, blk, send_f.at[3 * a + j], recv_f.at[3 * a + j], sibling).start()
        if ns:
            s_sib, r_sib, s_ici, r_ici = so[-4:]
            for a in range(ns):
                mine = _slot(b[ne + nm + a], (x, y, c))
                _remote(mine, mine, s_sib.at[a], r_sib.at[a], sibling).start()
                for j, (px, py) in enumerate(chips):
                    _remote(mine, mine, s_ici.at[3 * a + j], r_ici.at[3 * a + j], (px, py, c)).start()

    sems, bufs = _comm_call(body, name, list(e_bufs) + list(m_bufs) + s_bufs, sems_in, sems_out)
    fwd = dict(mid_st, bufs=bufs[ne:ne + nm], send_fwd=sems[0], recv_fwd=sems[1]) if nm else None
    started = None
    if ns:
        started = dict(bufs=bufs[ne + nm:], send_sib=sems[-4], recv_sib=sems[-3], send_ici=sems[-2], recv_ici=sems[-1])
    return bufs[:ne], fwd, started


def _rs_start(g4s, name):
    n = len(g4s)
    lands = [lax.empty((N_CHIP, 1) + g.shape[2:], g.dtype) for g in g4s]

    def body(b, _, sems):
        send, recv = sems
        x, y, c = _mesh_pos()
        for a in range(n):
            _remote(b[a].at[:, pl.ds(1 - c, 1)], b[n + a], send.at[a], recv.at[a], (x, y, 1 - c)).start()

    sems, bufs = _comm_call(body, name, [_hbm(v) for v in list(g4s) + lands], [], [(n,), (n,)])
    return dict(bufs=bufs, send=sems[0], recv=sems[1])


def _rs_mid(st, name):
    n = len(st["bufs"]) // 2

    def body(b, sems_in, _):
        send, recv = sems_in
        x, y, c = _mesh_pos()
        for a in range(n):
            cp = _remote(b[a].at[:, pl.ds(1 - c, 1)], b[n + a], send.at[a], recv.at[a], (x, y, 1 - c))
            cp.wait_send()
            cp.wait_recv()

    _, bufs = _comm_call(body, name, st["bufs"], [st["send"], st["recv"]], [])
    return bufs[:n], bufs[n:]


def _rs_start2(sums, name):
    n = len(sums)
    lands = [lax.empty((N_CHIP - 1,) + s.shape[1:], s.dtype) for s in sums]

    def body(b, _, sems):
        send, recv = sems
        x, y, c = _mesh_pos()
        for a in range(n):
            for j, (px, py) in enumerate(_other_chips(x, y)):
                _remote(b[a].at[pl.ds(2 * px + py, 1)], b[n + a].at[pl.ds(j, 1)], send.at[3 * a + j], recv.at[3 * a + j],
                        (px, py, c)).start()

    sems, bufs = _comm_call(body, name, [_hbm(v) for v in list(sums) + lands], [], [(3 * n,), (3 * n,)])
    return dict(bufs=bufs, send=sems[0], recv=sems[1])


def _rs_end(st, name, follow=()):
    n = len(st["bufs"]) // 2

    def body(b, sems_in, _):
        send, recv = sems_in
        x, y, c = _mesh_pos()
        for a in range(n):
            for j, (px, py) in enumerate(_other_chips(x, y)):
                cp = _remote(b[a].at[pl.ds(2 * px + py, 1)], b[n + a].at[pl.ds(j, 1)], send.at[3 * a + j], recv.at[3 * a + j],
                             (px, py, c))
                cp.wait_send()
                cp.wait_recv()

    _, bufs = _comm_call(body, name, st["bufs"], [st["send"], st["recv"]], [], follow=follow)
    return bufs[:n], bufs[n:]


def _pack(parts, rows_align=V7X_SUBLANES):
    flat, total = [], 0
    for p in parts:
        v = p.reshape(-1).astype(F32)
        pad = -v.shape[0] % PACK_ALIGN
        flat.append(jnp.pad(v, (0, pad)) if pad else v)
        total += v.shape[0] + pad
    tail = -total % (rows_align * V7X_LANES)
    if tail:
        flat.append(jnp.zeros((tail,), F32))
    return jnp.concatenate(flat).reshape(-1, V7X_LANES)


def _unpack(buf, shapes):
    lead = buf.shape[:-2]
    flat = buf.reshape(lead + (-1,))
    out, pos = [], 0
    for s in shapes:
        size = 1
        for d in s:
            size *= d
        out.append(flat[..., pos:pos + size].reshape(lead + tuple(s)))
        pos += size + (-size % PACK_ALIGN)
    return out


def kernel(x, c, ada_w, ada_b, norm_g, ffn_w_in, ffn_w_out, gm_w_in, gm_ln_g, gm_ln_b, gm_ws, gm_bs, gm_w_out, cv_w_in, cv_b_in, cv_dw_w, cv_dw_b, cv_ln_g, cv_ln_b, cv_w_out, cv_b_out, final_g, loss_target, m_ada_w, m_ada_b, m_norm_g, m_ffn_w_in, m_ffn_w_out, m_gm_w_in, m_gm_ln_g, m_gm_ln_b, m_gm_ws, m_gm_bs, m_gm_w_out, m_cv_w_in, m_cv_b_in, m_cv_dw_w, m_cv_dw_b, m_cv_ln_g, m_cv_ln_b, m_cv_w_out, m_cv_b_out, m_final_g, v_ada_w, v_ada_b, v_norm_g, v_ffn_w_in, v_ffn_w_out, v_gm_w_in, v_gm_ln_g, v_gm_ln_b, v_gm_ws, v_gm_bs, v_gm_w_out, v_cv_w_in, v_cv_b_in, v_cv_dw_w, v_cv_dw_b, v_cv_ln_g, v_cv_ln_b, v_cv_w_out, v_cv_b_out, v_final_g):
    t, d = x.shape[1], x.shape[2]
    depth = ada_w.shape[0]
    assert depth == 2 and ffn_w_in.shape[:2] == (2, 2) and gm_w_in.shape[0] == 1 and cv_w_in.shape[0] == 1
    dl = d // N_DEV
    bn = ffn_w_in.shape[3]
    fl = ffn_w_out.shape[2]
    f = fl * N_DEV
    el = gm_w_in.shape[2]
    e = el * N_DEV // 2
    hn, l = gm_ws.shape[1], gm_ws.shape[2]
    kw = cv_dw_w.shape[1]
    cl = ada_w.shape[2]
    me = 4 * lax.axis_index("x") + 2 * lax.axis_index("y") + lax.axis_index("c")
    me1 = me.astype(jnp.int32).reshape(1)
    chip1 = (2 * lax.axis_index("x") + lax.axis_index("y")).astype(jnp.int32).reshape(1)
    core1 = lax.axis_index("c").astype(jnp.int32).reshape(1)
    _Seq.last, _Seq.tokens = None, []

    xs = x[0]
    tgt = loss_target[0]

    ag_groups = [("win00", [(ffn_w_in, (0, 0))]), ("wout00", [(ffn_w_out, (0, 0))]),
                 ("gm", [(gm_w_in, (0,)), (gm_w_out, (0,))]),
                 ("win01", [(ffn_w_in, (0, 1))]), ("wout01", [(ffn_w_out, (0, 1))]),
                 ("win10", [(ffn_w_in, (1, 0))]), ("wout10", [(ffn_w_out, (1, 0))]),
                 ("cv", [(cv_w_in, (0,)), (cv_w_out, (0,))]),
                 ("win11", [(ffn_w_in, (1, 1))]), ("wout11", [(ffn_w_out, (1, 1))])]
    ag_flight = {}

    ag_slots = {}

    def ag_cast(gi, after=()):
        gname, members = ag_groups[gi]
        ag_slots[gi] = [_cast_to_slot(w, lead, me1, name=f"cast_{gname}_{k}", after=after)
                        for k, (w, lead) in enumerate(members)]

    def ag_start(gi):
        ag_flight[gi] = _ag_start(ag_slots.pop(gi), name=f"ag_start_{ag_groups[gi][0]}")

    def ag_forward(gi):
        if gi in ag_flight and "send_fwd" not in ag_flight[gi]:
            ag_flight[gi] = _ag_mid(ag_flight[gi], name=f"ag_mid_{ag_groups[gi][0]}")

    def ag_take(gi):
        ag_forward(gi)
        nxt = gi + 1 if gi > 0 and gi + 1 != AG_AHEAD and "send_fwd" not in ag_flight.get(gi + 1, {"send_fwd": 0}) else None
        new = gi + AG_AHEAD if gi + AG_AHEAD < len(ag_groups) else None
        bufs, fwd, started = _ag_step(ag_flight.pop(gi), None if nxt is None else ag_flight[nxt],
                                      None if new is None else ag_slots.pop(new), name=f"ag_step_{ag_groups[gi][0]}")
        if nxt is not None:
            ag_flight[nxt] = fwd
        if new is not None:
            ag_flight[new] = started
        return [b[0] for b in bufs]

    small_in = [c, norm_g, cv_b_in, cv_dw_w, cv_dw_b, cv_ln_g, cv_ln_b, cv_b_out]
    pack1 = _pack(small_in)
    (pack1_all,) = _all_gather([pack1[None, None]], name="ag_small")
    parts = _unpack(pack1_all[0], [s.shape for s in small_in])
    c_all = parts[0].reshape(N_DEV, d)
    ng_full = jnp.moveaxis(parts[1], 0, 2).reshape(depth, 3, d)
    cvb_in_full = parts[2].reshape(1, 2 * e)
    dww_full = jnp.moveaxis(parts[3][:, 0], 0, 1).reshape(kw, e)
    dwb_full, cln_g_full, cln_b_full, cvb_out_full = [p.reshape(1, d) for p in parts[4:8]]

    c_pad = jnp.pad(c_all, ((0, 16 - N_DEV), (0, 0)))
    ada_b_loc = lax.dynamic_slice_in_dim(ada_b, me * cl, cl, axis=1).reshape(depth, 1, cl)
    mod_part = _ada_fwd(c_pad, ada_w, ada_b_loc, name="ada_fwd")[:, :N_DEV]
    (mod_all,) = _all_gather([_pack([mod_part])[None, None]], name="ag_mod")
    mod_all = _unpack(mod_all[0], [mod_part.shape])[0]
    mod_mine = lax.dynamic_index_in_dim(mod_all, me, axis=2, keepdims=False)
    mod = jnp.moveaxis(mod_mine, 0, 1).reshape(depth, 3, 3, 1, d)

    for gi in range(len(ag_groups)):
        ag_cast(gi, after=[mod_all])
        if gi < AG_AHEAD:
            ag_start(gi)

    ws = gm_ws[0]
    bsb = jnp.broadcast_to(gm_bs[0][:, :, None], (hn, l, e // hn))
    gm_g, gm_b = gm_ln_g, gm_ln_b

    saved = []
    xcur = xs
    next_group = 0
    for i in range(depth):
        for s in range(3):
            shift, scale, gate = mod[i, s, 0], mod[i, s, 1], mod[i, s, 2]
            g_norm = ng_full[i, s][None]
            tag = f"l{i}s{s}"
            h = _norm_mod(xcur, g_norm, scale, shift, name=f"norm_mod_{tag}")
            if s != 1:
                (w_in_blk,) = ag_take(next_group)
                fg, fu, act = _ffn_in(h, w_in_blk, 0, name=f"ffn_in_{tag}")
                w_out3 = ag_take(next_group + 1)[0].reshape(1, f, d)
                next_group += 2
                xnext, yv = _out_proj(act, w_out3, 0, xcur, gate, None, 0.5, name=f"ffn_out_{tag}")
                saved.append(dict(x=xcur, h=h, fg=fg, fu=fu, act=act, y=yv, w_in=w_in_blk, w_out=w_out3))
            elif i % 2 == 0:
                gm_in_blk, gm_out = ag_take(next_group)
                gm_out3 = gm_out.reshape(1, e, d)
                next_group += 1
                pre = _in_proj(h, gm_in_blk, None, name=f"gm_in_{tag}")
                uu, vn = _gm_act(pre, gm_g, gm_b, name=f"gm_act_{tag}")
                sg = _sgu_fwd(uu, vn, ws, bsb, name=f"sgu_fwd_{tag}")
                xnext, yv = _out_proj(sg, gm_out3, 0, xcur, gate, None, 1.0, name=f"gm_out_{tag}")
                saved.append(dict(x=xcur, h=h, pre=pre, u=uu, vn=vn, sg=sg, y=yv, w_in=gm_in_blk, w_out=gm_out3))
            else:
                cv_in_blk, cv_out = ag_take(next_group)
                cv_out3 = cv_out.reshape(1, e, d)
                next_group += 1
                p = _in_proj(h, cv_in_blk, cvb_in_full, name=f"cv_in_{tag}")
                yc = _dwconv_fwd(p, dww_full, dwb_full, name=f"dwconv_fwd_{tag}")
                ys = _cv_act(yc, cln_g_full, cln_b_full, name=f"cv_act_{tag}")
                xnext, yv = _out_proj(ys, cv_out3, 0, xcur, gate, cvb_out_full, 1.0, name=f"cv_out_{tag}")
                saved.append(dict(x=xcur, h=h, p=p, yc=yc, ys=ys, y=yv, w_in=cv_in_blk, w_out=cv_out3))
            xcur = xnext

    def bwd_head(i, s):
        return saved[3 * i + s]["y"], mod[i, s, 2], 0.5 if s != 1 else 1.0, s == 1 and i % 2 == 1

    sq, dx, d_final_g, dy, dgate = _final_loss(xcur, tgt, final_g[None], *bwd_head(depth - 1, 2)[:3], name="final_loss")
    dbout = None
    loss = lax.psum(0.5 / d * jnp.sum(sq), ("x", "y", "c"))

    dmod = [[[None] * 3 for _ in range(3)] for _ in range(depth)]
    d_norm_g = [[None] * 3 for _ in range(depth)]
    small = {}

    stacked = {
        "ffn_w_in": [a.reshape(4, d, bn) for a in (ffn_w_in, m_ffn_w_in, v_ffn_w_in)],
        "ffn_w_out": [a.reshape(4, fl, d) for a in (ffn_w_out, m_ffn_w_out, v_ffn_w_out)],
        "gm_w_in": [gm_w_in, m_gm_w_in, v_gm_w_in], "gm_w_out": [gm_w_out, m_gm_w_out, v_gm_w_out],
        "cv_w_in": [cv_w_in, m_cv_w_in, v_cv_w_in], "cv_w_out": [cv_w_out, m_cv_w_out, v_cv_w_out],
    }
    res_big = {}

    def rs_sibling(g4s, tag):
        return _rs_start(g4s, name=f"rs_start_{tag}"), tag

    def rs_chips(flight):
        st, tag = flight
        g4s, lands = _rs_mid(st, name=f"rs_mid_{tag}")
        sums = [_add_sibling(g4, land, core1, name=f"rs_add_{tag}_{k}") for k, (g4, land) in enumerate(zip(g4s, lands))]
        return _rs_start2(sums, name=f"rs_start2_{tag}"), tag

    def rs_finish(flight, targets, follow=()):
        st, tag = flight
        sums, recvs = _rs_end(st, name=f"rs_end_{tag}", follow=follow)
        for (pname, k), hsum, recv in zip(targets, sums, recvs):
            w_st, m_st, v_st = stacked[pname]
            res_big[pname] = _adamw_stacked(hsum, recv, chip1, w_st, m_st, v_st, k, res_big.get(pname),
                                            name=f"adamw_{pname}_{k}")

    pending = []
    last_sibling = None
    for i in reversed(range(depth)):
        for s in reversed(range(3)):
            sv = saved[3 * i + s]
            shift, scale, gate = mod[i, s, 0], mod[i, s, 1], mod[i, s, 2]
            g_norm = ng_full[i, s][None]
            tag = f"l{i}s{s}"
            last = i == 0 and s == 0
            if s != 1:
                widx = 2 * i + s // 2
                dgu, act = _ffn_da(dy, sv["w_out"], 0, sv["fg"], sv["fu"], name=f"ffn_da_{tag}"), sv["act"]
                if last:
                    g_in = _mm_tn(sv["h"], dgu, d, bn, True, name=f"ffn_dwin_{tag}").reshape(N_CHIP, 2, d, bn)
                    sib_in = rs_sibling([g_in], f"{tag}_in")
                    g_out = _mm_tn(act, dy[None], bn, d, False, name=f"ffn_dwout_{tag}").reshape(N_CHIP, 2, fl, d)
                    new_flights = [(rs_chips(sib_in), [("ffn_w_in", widx)])]
                    sib, targets = rs_sibling([g_out], f"{tag}_out"), [("ffn_w_out", widx)]
                else:
                    g_out = _mm_tn(act, dy[None], bn, d, False, name=f"ffn_dwout_{tag}").reshape(N_CHIP, 2, fl, d)
                    sib_out = rs_sibling([g_out], f"{tag}_out")
                    g_in = _mm_tn(sv["h"], dgu, d, bn, True, name=f"ffn_dwin_{tag}").reshape(N_CHIP, 2, d, bn)
                    new_flights = [(rs_chips(sib_out), [("ffn_w_out", widx)])]
                    sib, targets = rs_sibling([g_in], f"{tag}_in"), [("ffn_w_in", widx)]
                z3, w_blk = dgu, sv["w_in"]
            elif i % 2 == 0:
                ds = _mm_nt(dy, sv["w_out"], 0, name=f"gm_ds_{tag}")
                g_out = _mm_tn(sv["sg"], dy[None], _tile(e, 1024, V7X_LANES), d, False,
                               name=f"gm_dwout_{tag}").reshape(N_CHIP, 2, dl, d)
                du, dvn, dws, dbs = _sgu_bwd(ds, sv["u"], sv["vn"], ws, bsb, name=f"sgu_bwd_{tag}")
                dpre, dlng, dlnb = _gm_act_bwd(sv["pre"], du, dvn, gm_g, name=f"gm_act_bwd_{tag}")
                gm_small = [dlng, dlnb, dws, dbs[:, :, 0]]
                gm_slot = _cast_to_slot(_pack(gm_small, rows_align=256), (), me1, name="slot_gm_grads", dtype=F32)
                gm_flight = _ag_start([gm_slot], name="ag_start_gm_grads")
                g_in = _mm_tn(sv["h"], dpre[None], d, el, True, name=f"gm_dwin_{tag}").reshape(N_CHIP, 2, d, el)
                targets, new_flights = [("gm_w_in", 0), ("gm_w_out", 0)], []
                sib = rs_sibling([g_in, g_out], tag)
                z3, w_blk = dpre[None], sv["w_in"]
            else:
                dys = _mm_nt(dy, sv["w_out"], 0, name=f"cv_dys_{tag}")
                g_out = _mm_tn(sv["ys"], dy[None], _tile(e, 1024, V7X_LANES), d, False,
                               name=f"cv_dwout_{tag}").reshape(N_CHIP, 2, dl, d)
                dyc, dlng, dlnb, ddwb = _cv_act_bwd(dys, sv["yc"], cln_g_full, cln_b_full, name=f"cv_act_bwd_{tag}")
                dp, ddww, dba, dbg = _dwconv_bwd(dyc, sv["p"], dww_full, name=f"dwconv_bwd_{tag}")
                small["cv_b_out"], small["cv_ln_g"], small["cv_ln_b"], small["cv_dw_b"] = dbout, dlng, dlnb, ddwb
                small["cv_dw_w"] = ddww
                small["cv_b_in"] = jnp.concatenate([dba, dbg], axis=1)
                g_in = _mm_tn(sv["h"], dp, d, el, True, name=f"cv_dwin_{tag}").reshape(N_CHIP, 2, d, el)
                targets, new_flights = [("cv_w_in", 0), ("cv_w_out", 0)], []
                sib = rs_sibling([g_in, g_out], tag)
                z3, w_blk = dp, sv["w_in"]
            nxt = None if last else bwd_head(*((i, s - 1) if s > 0 else (i - 1, 2)))
            res = _dh_norm_bwd(z3, w_blk, sv["x"], dx, g_norm, scale, nxt, name=f"dh_norm_bwd_{tag}")
            if last:
                (dx, dscale, dshift, dgn), last_sibling = res, (sib, targets)
            else:
                new_flights.append((rs_chips(sib), targets))
                dx, dy_next, dscale, dshift, dgn, dgate_next = res[:6]
                dbout_next = res[6] if nxt[3] else None
            dmod[i][s] = [dshift, dscale, dgate]
            d_norm_g[i][s] = dgn
            if not last:
                dy, dgate, dbout = dy_next, dgate_next, dbout_next
            for flight in pending:
                rs_finish(*flight)
            pending = new_flights
    grad_x = dx[None]

    dmod_mine = jnp.concatenate([v for per_l in dmod for per_s in per_l for v in per_s], axis=1)
    dng_mine = jnp.concatenate([v for per_l in d_norm_g for v in per_l], axis=1)
    small_out = [dmod_mine, dng_mine, small["cv_b_in"], small["cv_dw_w"], small["cv_dw_b"], small["cv_ln_g"],
                 small["cv_ln_b"], small["cv_b_out"], d_final_g]
    shapes2 = [s.shape for s in small_out]
    (pack2_all,) = _all_gather([_pack(small_out, rows_align=256)[None, None]], name="ag_small_grads")
    _Seq.last = pack2_all
    pending.append((rs_chips(last_sibling[0]), last_sibling[1]))
    summed = _unpack(_sum_devices(pack2_all[0], name="sum_small_grads"), shapes2)
    dmod_all = _unpack(pack2_all[0], shapes2)[0].reshape(N_DEV, depth, 9 * d)
    (gm_all,) = _ag_end(_ag_mid(gm_flight, name="ag_mid_gm_grads"), name="ag_end_gm_grads")
    gm_sum = _unpack(_sum_devices(gm_all[0], name="sum_gm_grads"), [s.shape for s in gm_small])

    def my_cols(full, width):
        return lax.dynamic_slice_in_dim(full, me * width, width, axis=full.ndim - 1)

    g_ada_b = summed[0].reshape(depth, 9 * d)
    g_norm_g = my_cols(summed[1].reshape(depth, 3, d), dl)
    g_small = {
        "ada_b": g_ada_b, "norm_g": g_norm_g,
        "gm_ln_g": gm_sum[0], "gm_ln_b": gm_sum[1], "gm_ws": gm_sum[2][None], "gm_bs": gm_sum[3][None],
        "cv_b_in": my_cols(summed[2], el), "cv_dw_w": my_cols(summed[3], dl)[None],
        "cv_dw_b": my_cols(summed[4], dl), "cv_ln_g": my_cols(summed[5], dl), "cv_ln_b": my_cols(summed[6], dl),
        "cv_b_out": my_cols(summed[7], dl), "final_g": summed[8].reshape(d),
    }

    dm_loc = jnp.moveaxis(my_cols(dmod_all, cl), 0, 1)
    dm_loc = jnp.pad(dm_loc, ((0, 0), (0, 16 - N_DEV), (0, 0)))
    res_ada_w = _ada_bwd(c_pad, dm_loc, ada_w, m_ada_w, v_ada_w, name="ada_bwd_adamw")

    def flat2(a):
        return a.reshape(-1, a.shape[-1])

    small_params = {
        "ada_b": (ada_b, m_ada_b, v_ada_b), "norm_g": (norm_g, m_norm_g, v_norm_g),
        "gm_ln_g": (gm_ln_g, m_gm_ln_g, v_gm_ln_g), "gm_ln_b": (gm_ln_b, m_gm_ln_b, v_gm_ln_b),
        "gm_ws": (gm_ws, m_gm_ws, v_gm_ws), "gm_bs": (gm_bs, m_gm_bs, v_gm_bs),
        "cv_b_in": (cv_b_in, m_cv_b_in, v_cv_b_in), "cv_dw_w": (cv_dw_w, m_cv_dw_w, v_cv_dw_w),
        "cv_dw_b": (cv_dw_b, m_cv_dw_b, v_cv_dw_b), "cv_ln_g": (cv_ln_g, m_cv_ln_g, v_cv_ln_g),
        "cv_ln_b": (cv_ln_b, m_cv_ln_b, v_cv_ln_b), "cv_b_out": (cv_b_out, m_cv_b_out, v_cv_b_out),
        "final_g": (final_g, m_final_g, v_final_g),
    }
    res_small = {}
    for key, (w, m, v) in small_params.items():
        g2 = flat2(g_small[key].reshape(w.shape)) if w.ndim > 1 else g_small[key].reshape(1, -1)
        w2, m2, v2 = [flat2(a) if a.ndim > 1 else a.reshape(1, -1) for a in (w, m, v)]
        res_small[key] = [o.reshape(w.shape) for o in _adamw([(g2[None], 0)], w2, m2, v2, name=f"adamw_{key}")]

    for flight in pending[:-1]:
        rs_finish(*flight)
    rs_finish(*pending[-1], follow=[res_ada_w[0]] + [r[0] for r in res_big.values()])

    def big(name, k):
        if name == "ada_w":
            return res_ada_w[k]
        return res_big[name][k].reshape(stacked_shape[name])

    stacked_shape = {"ffn_w_in": ffn_w_in.shape, "ffn_w_out": ffn_w_out.shape, "gm_w_in": gm_w_in.shape,
                     "gm_w_out": gm_w_out.shape, "cv_w_in": cv_w_in.shape, "cv_w_out": cv_w_out.shape}

    order = ["ada_w", "ada_b", "norm_g", "ffn_w_in", "ffn_w_out", "gm_w_in", "gm_ln_g", "gm_ln_b", "gm_ws", "gm_bs",
             "gm_w_out", "cv_w_in", "cv_b_in", "cv_dw_w", "cv_dw_b", "cv_ln_g", "cv_ln_b", "cv_w_out", "cv_b_out",
             "final_g"]
    outs = [loss, grad_x]
    for k in range(4):
        for name in order:
            outs.append(res_small[name][k] if name in res_small else big(name, k))
    return tuple(outs)
```
